```python
import jax, jax.numpy as jnp
from jax import lax
import numpy as np

D_MODEL = 1024
BATCH = 8
SEQ = 8192
DEPTH = 4

D_MIX = D_MODEL
D_CONF = D_MIX // 4
D_ATT = D_MIX // 4
D_SC = D_MIX // 4
D_POOL = D_MIX - D_CONF - D_ATT - D_SC
HEAD_DIM = 64
N_ATT_HEADS = D_ATT // HEAD_DIM
CONF_KERNEL = 31
SC_KERNEL = 3
POOL_WINDOWS = (2, 4, 8, 16)
N_POOL_GROUPS = len(POOL_WINDOWS)
POOL_GROUP_DIM = D_POOL // N_POOL_GROUPS
D_FF = 4 * D_MODEL
D_PLE = 256
Q_BLOCK = 128
EPS = 1e-6
SPLITS = (2 * D_CONF, D_ATT, D_ATT, D_ATT, N_ATT_HEADS, D_SC, D_SC, D_SC, D_POOL)
D_IN = sum(SPLITS)
SPLIT_IDX = tuple(int(s) for s in np.cumsum(SPLITS)[:-1])

kernel_name = "hybrid_parallel_groups_fox_conv_pool"


def rms_norm(x, g):
    x32 = x.astype(jnp.float32)
    y = x32 * lax.rsqrt(jnp.mean(x32 * x32, axis=-1, keepdims=True) + EPS)
    return (y * g.astype(jnp.float32)).astype(x.dtype)


def layer_norm(x, g, b):
    x32 = x.astype(jnp.float32)
    mu = jnp.mean(x32, axis=-1, keepdims=True)
    xc = x32 - mu
    y = xc * lax.rsqrt(jnp.mean(xc * xc, axis=-1, keepdims=True) + EPS)
    return (y * g.astype(jnp.float32) + b.astype(jnp.float32)).astype(x.dtype)


def causal_depthwise_conv(u, w):
    k = w.shape[0]
    return lax.conv_general_dilated(
        u, w[:, None, :].astype(u.dtype), window_strides=(1,), padding=((k - 1, 0),),
        dimension_numbers=("NWC", "WIO", "NWC"), feature_group_count=u.shape[-1])


def conformer_conv(ab, w_dw, ln_g, ln_b, w_pw):
    a, b = jnp.split(ab, 2, axis=-1)
    u = a * jax.nn.sigmoid(b)
    u = causal_depthwise_conv(u, w_dw)
    u = jax.nn.silu(layer_norm(u, ln_g, ln_b))
    return u @ w_pw


def forgetting_attention(q, k, v, f_logit):
    b, s, _ = q.shape
    q = q.reshape(b, s, N_ATT_HEADS, HEAD_DIM).transpose(0, 2, 1, 3)
    k = k.reshape(b, s, N_ATT_HEADS, HEAD_DIM).transpose(0, 2, 1, 3)
    v = v.reshape(b, s, N_ATT_HEADS, HEAD_DIM).transpose(0, 2, 1, 3)
    log_f = jax.nn.log_sigmoid(f_logit.astype(jnp.float32))
    c = jnp.cumsum(log_f, axis=1).transpose(0, 2, 1)
    nb = s // Q_BLOCK
    qb = q.reshape(b, N_ATT_HEADS, nb, Q_BLOCK, HEAD_DIM).transpose(2, 0, 1, 3, 4)
    cb = c.reshape(b, N_ATT_HEADS, nb, Q_BLOCK).transpose(2, 0, 1, 3)
    pos = jnp.arange(s, dtype=jnp.int32)
    posb = pos.reshape(nb, Q_BLOCK)
    k32 = k.astype(jnp.float32)
    scale = HEAD_DIM ** -0.5

    def block(args):
        qi, ci, pi = args
        logits = jnp.einsum("bhqd,bhkd->bhqk", qi.astype(jnp.float32), k32) * scale
        logits = logits + ci[..., None] - c[:, :, None, :]
        mask = pi[:, None] >= pos[None, :]
        logits = jnp.where(mask, logits, -jnp.inf)
        probs = jax.nn.softmax(logits, axis=-1)
        return jnp.einsum("bhqk,bhkd->bhqd", probs.astype(v.dtype), v)

    o = lax.map(block, (qb, cb, posb))
    return o.transpose(1, 0, 3, 2, 4).reshape(b, s, D_ATT)


def short_conv_mixer(h, bg, cg, w_sc):
    return bg * causal_depthwise_conv(cg * h, w_sc)


def multiscale_pool(v, w_pool, scale):
    b, s, _ = v.shape
    v32 = v.astype(jnp.float32)
    count = jnp.arange(1, s + 1, dtype=jnp.float32)[None, :, None]
    groups = jnp.split(v32, N_POOL_GROUPS, axis=-1)
    outs = []
    for g, w in zip(groups, POOL_WINDOWS):
        csum = jnp.cumsum(g, axis=1)
        lag = jnp.pad(csum, ((0, 0), (w, 0), (0, 0)))[:, :s]
        mean = (csum - lag) / jnp.minimum(count, w)
        outs.append(mean - g)
    d = jnp.stack(outs, axis=2).astype(v.dtype)
    d = jnp.einsum("bsgc,gcd->bsgd", d, w_pool).reshape(b, s, D_POOL)
    return d * scale


def _fwd_setup_inputs(seed: int = 0) -> dict:
    key = jax.random.key(seed)
    ks = jax.random.split(key, 24)
    f32 = jnp.float32
    L = DEPTH

    def nrm(k, shape, fan_in):
        return jax.random.normal(k, shape, f32) * (fan_in ** -0.5)

    def gain(k, shape):
        return 1.0 + 0.05 * jax.random.normal(k, shape, f32)

    return {
        "x": jax.random.normal(ks[0], (BATCH, SEQ, D_MODEL), f32),
        "p": jax.random.normal(ks[1], (DEPTH, BATCH, SEQ, D_PLE), f32),
        "g_mix_pre": gain(ks[2], (L, D_MODEL)),
        "w_in": nrm(ks[3], (L, D_MODEL, D_IN), D_MODEL),
        "b_forget": 0.1 * jax.random.normal(ks[4], (L, N_ATT_HEADS), f32),
        "w_conf_dw": nrm(ks[5], (L, CONF_KERNEL, D_CONF), CONF_KERNEL),
        "conf_ln_g": gain(ks[6], (L, D_CONF)),
        "conf_ln_b": 0.02 * jax.random.normal(ks[7], (L, D_CONF), f32),
        "w_conf_pw": nrm(ks[8], (L, D_CONF, D_CONF), D_CONF),
        "w_sc": nrm(ks[9], (L, SC_KERNEL, D_SC), SC_KERNEL),
        "w_pool": nrm(ks[10], (L, N_POOL_GROUPS, POOL_GROUP_DIM, POOL_GROUP_DIM), POOL_GROUP_DIM),
        "pool_scale": gain(ks[11], (L, D_POOL)),
        "w_out": nrm(ks[12], (L, D_MIX, D_MODEL), D_MIX),
        "g_mix_post": gain(ks[13], (L, D_MODEL)),
        "g_mlp_pre": gain(ks[14], (L, D_MODEL)),
        "w_up": nrm(ks[15], (L, D_MODEL, D_FF), D_MODEL),
        "w_down": nrm(ks[16], (L, D_FF, D_MODEL), D_FF),
        "g_mlp_post": gain(ks[17], (L, D_MODEL)),
        "g_ple_pre": gain(ks[18], (L, D_MODEL)),
        "w_ple_gate": nrm(ks[19], (L, D_MODEL, D_MODEL), D_MODEL),
        "w_ple_proj": nrm(ks[20], (L, D_PLE, D_MODEL), D_PLE),
        "g_ple_post": gain(ks[21], (L, D_MODEL)),
    }


def _fwd_reference(x, p, g_mix_pre, w_in, b_forget, w_conf_dw, conf_ln_g, conf_ln_b, w_conf_pw,
              w_sc, w_pool, pool_scale, w_out, g_mix_post, g_mlp_pre, w_up, w_down,
              g_mlp_post, g_ple_pre, w_ple_gate, w_ple_proj, g_ple_post):
    h = x
    for i in range(DEPTH):
        xn = rms_norm(h, g_mix_pre[i])
        z = xn @ w_in[i]
        conf_ab, q, k, v, f_logit, sc_h, sc_b, sc_c, pool_v = jnp.split(z, SPLIT_IDX, axis=-1)
        y_conf = conformer_conv(conf_ab, w_conf_dw[i], conf_ln_g[i], conf_ln_b[i], w_conf_pw[i])
        y_att = forgetting_attention(q, k, v, f_logit + b_forget[i])
        y_sc = short_conv_mixer(sc_h, sc_b, sc_c, w_sc[i])
        y_pool = multiscale_pool(pool_v, w_pool[i], pool_scale[i])
        mix = jnp.concatenate([y_conf, y_att, y_sc, y_pool], axis=-1) @ w_out[i]
        h = h + rms_norm(mix, g_mix_post[i])
        hn = rms_norm(h, g_mlp_pre[i])
        ff = jnp.square(jax.nn.relu(hn @ w_up[i])) @ w_down[i]
        h = h + rms_norm(ff, g_mlp_post[i])
        gate = jax.nn.sigmoid(rms_norm(h, g_ple_pre[i]) @ w_ple_gate[i])
        e = (p[i] @ w_ple_proj[i]) * gate
        h = h + rms_norm(e, g_ple_post[i])
    return h


import jax as _jax
import jax.numpy as _jnp

TWIN_FORMAT = 'train_step'
FWD_PARAMS = ['x', 'p', 'g_mix_pre', 'w_in', 'b_forget', 'w_conf_dw', 'conf_ln_g', 'conf_ln_b', 'w_conf_pw', 'w_sc', 'w_pool', 'pool_scale', 'w_out', 'g_mix_post', 'g_mlp_pre', 'w_up', 'w_down', 'g_mlp_post', 'g_ple_pre', 'w_ple_gate', 'w_ple_proj', 'g_ple_post']
TWIN_WEIGHTS = ['g_mix_pre', 'w_in', 'b_forget', 'w_conf_dw', 'conf_ln_g', 'conf_ln_b', 'w_conf_pw', 'w_sc', 'w_pool', 'pool_scale', 'w_out', 'g_mix_post', 'g_mlp_pre', 'w_up', 'w_down', 'g_mlp_post', 'g_ple_pre', 'w_ple_gate', 'w_ple_proj', 'g_ple_post']
TWIN_DIFF_INPUT = 'x'
TWIN_INPUTS = ['x', 'p', 'g_mix_pre', 'w_in', 'b_forget', 'w_conf_dw', 'conf_ln_g', 'conf_ln_b', 'w_conf_pw', 'w_sc', 'w_pool', 'pool_scale', 'w_out', 'g_mix_post', 'g_mlp_pre', 'w_up', 'w_down', 'g_mlp_post', 'g_ple_pre', 'w_ple_gate', 'w_ple_proj', 'g_ple_post', 'loss_target', 'm_g_mix_pre', 'm_w_in', 'm_b_forget', 'm_w_conf_dw', 'm_conf_ln_g', 'm_conf_ln_b', 'm_w_conf_pw', 'm_w_sc', 'm_w_pool', 'm_pool_scale', 'm_w_out', 'm_g_mix_post', 'm_g_mlp_pre', 'm_w_up', 'm_w_down', 'm_g_mlp_post', 'm_g_ple_pre', 'm_w_ple_gate', 'm_w_ple_proj', 'm_g_ple_post', 'v_g_mix_pre', 'v_w_in', 'v_b_forget', 'v_w_conf_dw', 'v_conf_ln_g', 'v_conf_ln_b', 'v_w_conf_pw', 'v_w_sc', 'v_w_pool', 'v_pool_scale', 'v_w_out', 'v_g_mix_post', 'v_g_mlp_pre', 'v_w_up', 'v_w_down', 'v_g_mlp_post', 'v_g_ple_pre', 'v_w_ple_gate', 'v_w_ple_proj', 'v_g_ple_post']
TWIN_OUTPUTS = ['loss', 'grad_x', 'grad_g_mix_pre', 'grad_w_in', 'grad_b_forget', 'grad_w_conf_dw', 'grad_conf_ln_g', 'grad_conf_ln_b', 'grad_w_conf_pw', 'grad_w_sc', 'grad_w_pool', 'grad_pool_scale', 'grad_w_out', 'grad_g_mix_post', 'grad_g_mlp_pre', 'grad_w_up', 'grad_w_down', 'grad_g_mlp_post', 'grad_g_ple_pre', 'grad_w_ple_gate', 'grad_w_ple_proj', 'grad_g_ple_post', 'delta_g_mix_pre', 'delta_w_in', 'delta_b_forget', 'delta_w_conf_dw', 'delta_conf_ln_g', 'delta_conf_ln_b', 'delta_w_conf_pw', 'delta_w_sc', 'delta_w_pool', 'delta_pool_scale', 'delta_w_out', 'delta_g_mix_post', 'delta_g_mlp_pre', 'delta_w_up', 'delta_w_down', 'delta_g_mlp_post', 'delta_g_ple_pre', 'delta_w_ple_gate', 'delta_w_ple_proj', 'delta_g_ple_post', 'new_m_g_mix_pre', 'new_m_w_in', 'new_m_b_forget', 'new_m_w_conf_dw', 'new_m_conf_ln_g', 'new_m_conf_ln_b', 'new_m_w_conf_pw', 'new_m_w_sc', 'new_m_w_pool', 'new_m_pool_scale', 'new_m_w_out', 'new_m_g_mix_post', 'new_m_g_mlp_pre', 'new_m_w_up', 'new_m_w_down', 'new_m_g_mlp_post', 'new_m_g_ple_pre', 'new_m_w_ple_gate', 'new_m_w_ple_proj', 'new_m_g_ple_post', 'new_v_g_mix_pre', 'new_v_w_in', 'new_v_b_forget', 'new_v_w_conf_dw', 'new_v_conf_ln_g', 'new_v_conf_ln_b', 'new_v_w_conf_pw', 'new_v_w_sc', 'new_v_w_pool', 'new_v_pool_scale', 'new_v_w_out', 'new_v_g_mix_post', 'new_v_g_mlp_pre', 'new_v_w_up', 'new_v_w_down', 'new_v_g_mlp_post', 'new_v_g_ple_pre', 'new_v_w_ple_gate', 'new_v_w_ple_proj', 'new_v_g_ple_post']
TWIN_LEAF_KINDS = {'loss': 'loss', 'grad_x': 'grad_x', 'grad_g_mix_pre': 'grad_w', 'grad_w_in': 'grad_w', 'grad_b_forget': 'grad_w', 'grad_w_conf_dw': 'grad_w', 'grad_conf_ln_g': 'grad_w', 'grad_conf_ln_b': 'grad_w', 'grad_w_conf_pw': 'grad_w', 'grad_w_sc': 'grad_w', 'grad_w_pool': 'grad_w', 'grad_pool_scale': 'grad_w', 'grad_w_out': 'grad_w', 'grad_g_mix_post': 'grad_w', 'grad_g_mlp_pre': 'grad_w', 'grad_w_up': 'grad_w', 'grad_w_down': 'grad_w', 'grad_g_mlp_post': 'grad_w', 'grad_g_ple_pre': 'grad_w', 'grad_w_ple_gate': 'grad_w', 'grad_w_ple_proj': 'grad_w', 'grad_g_ple_post': 'grad_w', 'delta_g_mix_pre': 'delta_w', 'delta_w_in': 'delta_w', 'delta_b_forget': 'delta_w', 'delta_w_conf_dw': 'delta_w', 'delta_conf_ln_g': 'delta_w', 'delta_conf_ln_b': 'delta_w', 'delta_w_conf_pw': 'delta_w', 'delta_w_sc': 'delta_w', 'delta_w_pool': 'delta_w', 'delta_pool_scale': 'delta_w', 'delta_w_out': 'delta_w', 'delta_g_mix_post': 'delta_w', 'delta_g_mlp_pre': 'delta_w', 'delta_w_up': 'delta_w', 'delta_w_down': 'delta_w', 'delta_g_mlp_post': 'delta_w', 'delta_g_ple_pre': 'delta_w', 'delta_w_ple_gate': 'delta_w', 'delta_w_ple_proj': 'delta_w', 'delta_g_ple_post': 'delta_w', 'new_m_g_mix_pre': 'new_m', 'new_m_w_in': 'new_m', 'new_m_b_forget': 'new_m', 'new_m_w_conf_dw': 'new_m', 'new_m_conf_ln_g': 'new_m', 'new_m_conf_ln_b': 'new_m', 'new_m_w_conf_pw': 'new_m', 'new_m_w_sc': 'new_m', 'new_m_w_pool': 'new_m', 'new_m_pool_scale': 'new_m', 'new_m_w_out': 'new_m', 'new_m_g_mix_post': 'new_m', 'new_m_g_mlp_pre': 'new_m', 'new_m_w_up': 'new_m', 'new_m_w_down': 'new_m', 'new_m_g_mlp_post': 'new_m', 'new_m_g_ple_pre': 'new_m', 'new_m_w_ple_gate': 'new_m', 'new_m_w_ple_proj': 'new_m', 'new_m_g_ple_post': 'new_m', 'new_v_g_mix_pre': 'new_v', 'new_v_w_in': 'new_v', 'new_v_b_forget': 'new_v', 'new_v_w_conf_dw': 'new_v', 'new_v_conf_ln_g': 'new_v', 'new_v_conf_ln_b': 'new_v', 'new_v_w_conf_pw': 'new_v', 'new_v_w_sc': 'new_v', 'new_v_w_pool': 'new_v', 'new_v_pool_scale': 'new_v', 'new_v_w_out': 'new_v', 'new_v_g_mix_post': 'new_v', 'new_v_g_mlp_pre': 'new_v', 'new_v_w_up': 'new_v', 'new_v_w_down': 'new_v', 'new_v_g_mlp_post': 'new_v', 'new_v_g_ple_pre': 'new_v', 'new_v_w_ple_gate': 'new_v', 'new_v_w_ple_proj': 'new_v', 'new_v_g_ple_post': 'new_v'}


def _forward(args):
    return _fwd_reference(*[args[k] for k in FWD_PARAMS])


def _output_shape():
    def fwd():
        inp = _fwd_setup_inputs(0)
        return _fwd_reference(*[inp[k] for k in FWD_PARAMS])
    out = _jax.eval_shape(fwd)
    return out.shape, out.dtype

N_MICROBATCH = 1
ADAM_LR = 0.001
ADAM_B1 = 0.9
ADAM_B2 = 0.999
ADAM_EPS = 1e-08
ADAM_WD = 0.01
ADAM_STEP = 10
PER_EXAMPLE_BATCH_AXIS = {'x': 0, 'p': 1, 'loss_target': 0}
SHARED_INPUTS = []
_WEIGHT_DTYPES = {'g_mix_pre': _jnp.float32, 'w_in': _jnp.float32, 'b_forget': _jnp.float32, 'w_conf_dw': _jnp.float32, 'conf_ln_g': _jnp.float32, 'conf_ln_b': _jnp.float32, 'w_conf_pw': _jnp.float32, 'w_sc': _jnp.float32, 'w_pool': _jnp.float32, 'pool_scale': _jnp.float32, 'w_out': _jnp.float32, 'g_mix_post': _jnp.float32, 'g_mlp_pre': _jnp.float32, 'w_up': _jnp.float32, 'w_down': _jnp.float32, 'g_mlp_post': _jnp.float32, 'g_ple_pre': _jnp.float32, 'w_ple_gate': _jnp.float32, 'w_ple_proj': _jnp.float32, 'g_ple_post': _jnp.float32}
MOMENT_SCALE = {'g_mix_pre': 7.826954e+00, 'w_in': 5.275573e+00, 'b_forget': 5.148317e+00, 'w_conf_dw': 8.513590e+00, 'conf_ln_g': 3.121299e+01, 'conf_ln_b': 4.466708e+01, 'w_conf_pw': 2.054254e+01, 'w_sc': 2.538778e+00, 'w_pool': 5.032833e+00, 'pool_scale': 5.495952e+00, 'w_out': 1.241838e+01, 'g_mix_post': 6.532620e+01, 'g_mlp_pre': 8.644204e+00, 'w_up': 4.224815e+00, 'w_down': 2.629633e+01, 'g_mlp_post': 7.247647e+01, 'g_ple_pre': 6.537848e-01, 'w_ple_gate': 6.383647e-01, 'w_ple_proj': 1.826034e+00, 'g_ple_post': 6.507335e+01}


def _to_microbatches(a, axis):
    t = _jnp.moveaxis(a, axis, 0)
    t = t.reshape((N_MICROBATCH, t.shape[0] // N_MICROBATCH) + t.shape[1:])
    return _jnp.moveaxis(t, 1, axis + 1)


def setup_inputs(seed: int = 0) -> dict:
    inp = _fwd_setup_inputs(seed)
    key = _jax.random.fold_in(_jax.random.key(seed), 7919)
    shape, _ = _output_shape()
    out = dict(inp)
    out["loss_target"] = _jax.random.normal(_jax.random.fold_in(key, 0), shape, _jnp.float32)
    for i, name in enumerate(TWIN_WEIGHTS):
        w = inp[name].astype(_jnp.float32)
        if MOMENT_SCALE is None:
            s = _jnp.sqrt(_jnp.mean(_jnp.square(w)) + 1e-30)
        else:
            s = MOMENT_SCALE[name]
        km, kv = _jax.random.split(_jax.random.fold_in(key, i + 1))
        out[name] = w
        out["m_" + name] = s * _jax.random.normal(km, w.shape, _jnp.float32)
        out["v_" + name] = (s * s) * _jax.random.uniform(kv, w.shape, _jnp.float32, 0.5, 1.5)
    if N_MICROBATCH > 1:
        for name, axis in PER_EXAMPLE_BATCH_AXIS.items():
            out[name] = _to_microbatches(out[name], axis)
    return {'x': out['x'], 'p': out['p'], 'g_mix_pre': out['g_mix_pre'], 'w_in': out['w_in'], 'b_forget': out['b_forget'], 'w_conf_dw': out['w_conf_dw'], 'conf_ln_g': out['conf_ln_g'], 'conf_ln_b': out['conf_ln_b'], 'w_conf_pw': out['w_conf_pw'], 'w_sc': out['w_sc'], 'w_pool': out['w_pool'], 'pool_scale': out['pool_scale'], 'w_out': out['w_out'], 'g_mix_post': out['g_mix_post'], 'g_mlp_pre': out['g_mlp_pre'], 'w_up': out['w_up'], 'w_down': out['w_down'], 'g_mlp_post': out['g_mlp_post'], 'g_ple_pre': out['g_ple_pre'], 'w_ple_gate': out['w_ple_gate'], 'w_ple_proj': out['w_ple_proj'], 'g_ple_post': out['g_ple_post'], 'loss_target': out['loss_target'], 'm_g_mix_pre': out['m_g_mix_pre'], 'm_w_in': out['m_w_in'], 'm_b_forget': out['m_b_forget'], 'm_w_conf_dw': out['m_w_conf_dw'], 'm_conf_ln_g': out['m_conf_ln_g'], 'm_conf_ln_b': out['m_conf_ln_b'], 'm_w_conf_pw': out['m_w_conf_pw'], 'm_w_sc': out['m_w_sc'], 'm_w_pool': out['m_w_pool'], 'm_pool_scale': out['m_pool_scale'], 'm_w_out': out['m_w_out'], 'm_g_mix_post': out['m_g_mix_post'], 'm_g_mlp_pre': out['m_g_mlp_pre'], 'm_w_up': out['m_w_up'], 'm_w_down': out['m_w_down'], 'm_g_mlp_post': out['m_g_mlp_post'], 'm_g_ple_pre': out['m_g_ple_pre'], 'm_w_ple_gate': out['m_w_ple_gate'], 'm_w_ple_proj': out['m_w_ple_proj'], 'm_g_ple_post': out['m_g_ple_post'], 'v_g_mix_pre': out['v_g_mix_pre'], 'v_w_in': out['v_w_in'], 'v_b_forget': out['v_b_forget'], 'v_w_conf_dw': out['v_w_conf_dw'], 'v_conf_ln_g': out['v_conf_ln_g'], 'v_conf_ln_b': out['v_conf_ln_b'], 'v_w_conf_pw': out['v_w_conf_pw'], 'v_w_sc': out['v_w_sc'], 'v_w_pool': out['v_w_pool'], 'v_pool_scale': out['v_pool_scale'], 'v_w_out': out['v_w_out'], 'v_g_mix_post': out['v_g_mix_post'], 'v_g_mlp_pre': out['v_g_mlp_pre'], 'v_w_up': out['v_w_up'], 'v_w_down': out['v_w_down'], 'v_g_mlp_post': out['v_g_mlp_post'], 'v_g_ple_pre': out['v_g_ple_pre'], 'v_w_ple_gate': out['v_w_ple_gate'], 'v_w_ple_proj': out['v_w_ple_proj'], 'v_g_ple_post': out['v_g_ple_post']}


def _loss(weights, diff, rest, loss_target):
    with _jax.named_scope("forward"):
        args = {**rest, TWIN_DIFF_INPUT: diff, **{k: w.astype(_WEIGHT_DTYPES[k]) for k, w in weights.items()}}
        y = _forward(args)
    with _jax.named_scope("loss_head"):
        err = _jnp.square(y.astype(_jnp.float32) - loss_target)
        return 0.5 * _jnp.sum(_jnp.mean(err, axis=-1)) if err.ndim else 0.5 * err


def _adamw(w, g, m, v):
    m = ADAM_B1 * m + (1.0 - ADAM_B1) * g
    v = ADAM_B2 * v + (1.0 - ADAM_B2) * _jnp.square(g)
    m_hat = m / (1.0 - ADAM_B1 ** ADAM_STEP)
    v_hat = v / (1.0 - ADAM_B2 ** ADAM_STEP)
    delta = -ADAM_LR * (m_hat / (_jnp.sqrt(v_hat) + ADAM_EPS) + ADAM_WD * w)
    return delta, m, v


def reference(x, p, g_mix_pre, w_in, b_forget, w_conf_dw, conf_ln_g, conf_ln_b, w_conf_pw, w_sc, w_pool, pool_scale, w_out, g_mix_post, g_mlp_pre, w_up, w_down, g_mlp_post, g_ple_pre, w_ple_gate, w_ple_proj, g_ple_post, loss_target, m_g_mix_pre, m_w_in, m_b_forget, m_w_conf_dw, m_conf_ln_g, m_conf_ln_b, m_w_conf_pw, m_w_sc, m_w_pool, m_pool_scale, m_w_out, m_g_mix_post, m_g_mlp_pre, m_w_up, m_w_down, m_g_mlp_post, m_g_ple_pre, m_w_ple_gate, m_w_ple_proj, m_g_ple_post, v_g_mix_pre, v_w_in, v_b_forget, v_w_conf_dw, v_conf_ln_g, v_conf_ln_b, v_w_conf_pw, v_w_sc, v_w_pool, v_pool_scale, v_w_out, v_g_mix_post, v_g_mlp_pre, v_w_up, v_w_down, v_g_mlp_post, v_g_ple_pre, v_w_ple_gate, v_w_ple_proj, v_g_ple_post):
    given = dict(x=x, p=p, g_mix_pre=g_mix_pre, w_in=w_in, b_forget=b_forget, w_conf_dw=w_conf_dw, conf_ln_g=conf_ln_g, conf_ln_b=conf_ln_b, w_conf_pw=w_conf_pw, w_sc=w_sc, w_pool=w_pool, pool_scale=pool_scale, w_out=w_out, g_mix_post=g_mix_post, g_mlp_pre=g_mlp_pre, w_up=w_up, w_down=w_down, g_mlp_post=g_mlp_post, g_ple_pre=g_ple_pre, w_ple_gate=w_ple_gate, w_ple_proj=w_ple_proj, g_ple_post=g_ple_post, loss_target=loss_target, m_g_mix_pre=m_g_mix_pre, m_w_in=m_w_in, m_b_forget=m_b_forget, m_w_conf_dw=m_w_conf_dw, m_conf_ln_g=m_conf_ln_g, m_conf_ln_b=m_conf_ln_b, m_w_conf_pw=m_w_conf_pw, m_w_sc=m_w_sc, m_w_pool=m_w_pool, m_pool_scale=m_pool_scale, m_w_out=m_w_out, m_g_mix_post=m_g_mix_post, m_g_mlp_pre=m_g_mlp_pre, m_w_up=m_w_up, m_w_down=m_w_down, m_g_mlp_post=m_g_mlp_post, m_g_ple_pre=m_g_ple_pre, m_w_ple_gate=m_w_ple_gate, m_w_ple_proj=m_w_ple_proj, m_g_ple_post=m_g_ple_post, v_g_mix_pre=v_g_mix_pre, v_w_in=v_w_in, v_b_forget=v_b_forget, v_w_conf_dw=v_w_conf_dw, v_conf_ln_g=v_conf_ln_g, v_conf_ln_b=v_conf_ln_b, v_w_conf_pw=v_w_conf_pw, v_w_sc=v_w_sc, v_w_pool=v_w_pool, v_pool_scale=v_pool_scale, v_w_out=v_w_out, v_g_mix_post=v_g_mix_post, v_g_mlp_pre=v_g_mlp_pre, v_w_up=v_w_up, v_w_down=v_w_down, v_g_mlp_post=v_g_mlp_post, v_g_ple_pre=v_g_ple_pre, v_w_ple_gate=v_w_ple_gate, v_w_ple_proj=v_w_ple_proj, v_g_ple_post=v_g_ple_post)
    weights = {n: given[n] for n in TWIN_WEIGHTS}
    shared = {n: given[n] for n in SHARED_INPUTS}
    per_example = {n: given[n] for n in ['x', 'p']}
    grad_fn = _jax.value_and_grad(_loss, argnums=(0, 1))

    def one_microbatch(ex, loss_target):
        ex = dict(ex)
        diff = ex.pop(TWIN_DIFF_INPUT)
        return grad_fn(weights, diff, {**shared, **ex}, loss_target)

    if N_MICROBATCH == 1:
        loss, (grad_w, grad_x) = one_microbatch(per_example, given["loss_target"])
    else:
        def body(carry, xs):
            loss_sum, grad_sum = carry
            l_k, (gw_k, gx_k) = one_microbatch(xs[0], xs[1])
            with _jax.named_scope("update"):
                return (loss_sum + l_k, _jax.tree.map(_jnp.add, grad_sum, gw_k)), gx_k

        init = (_jnp.zeros((), _jnp.float32), _jax.tree.map(_jnp.zeros_like, weights))
        (loss, grad_w), grad_x = _jax.lax.scan(body, init, (per_example, given["loss_target"]))
    with _jax.named_scope("update"):
        delta_w, new_m, new_v = {}, {}, {}
        for n in TWIN_WEIGHTS:
            delta_w[n], new_m[n], new_v[n] = _adamw(weights[n], grad_w[n], given["m_" + n], given["v_" + n])
    return (loss, grad_x, *[grad_w[n] for n in TWIN_WEIGHTS], *[delta_w[n] for n in TWIN_WEIGHTS],
            *[new_m[n] for n in TWIN_WEIGHTS], *[new_v[n] for n in TWIN_WEIGHTS])
```

```python
import jax
import jax.numpy as jnp
from jax import lax
from jax.experimental import pallas as pl
from jax.experimental.pallas import tpu as pltpu

F32, BF16 = jnp.float32, jnp.bfloat16

D = 1024
DG = 256
N_HEADS = 4
HEAD_DIM = 64
CONF_K = 31
SC_K = 3
POOL_WINDOWS = (2, 4, 8, 16)
D_FF = 4096
D_PLE = 256
N_LAYERS = 4
N_DEV = 8
EPS = 1e-6
SCALE = HEAD_DIM ** -0.5
W_MAIN = 2304
F_LO, F_HI = 1280, 1284

ADAM_LR, ADAM_B1, ADAM_B2, ADAM_EPS, ADAM_WD, ADAM_STEP = 0.001, 0.9, 0.999, 1e-08, 0.01, 10

TB = 512
HALO = 32
LANES = 128
FF_BLK = D_FF // N_DEV
VMEM_LIMIT = 56 * 1024 * 1024

NT_DIMS = (((1,), (1,)), ((), ()))
TN_DIMS = (((0,), (0,)), ((), ()))
MESH_ID = pl.DeviceIdType.MESH


def _pcall(body, **kw):
    return pl.pallas_call(body, **kw)


def _cparams(n_axes):
    return pltpu.CompilerParams(dimension_semantics=("arbitrary",) * n_axes, vmem_limit_bytes=VMEM_LIMIT)


def _sds(shape, dtype):
    return jax.ShapeDtypeStruct(shape, dtype)


def _tok(width, tb=TB):
    return pl.BlockSpec((tb, width), lambda i: (i, 0))


def _tokcol(width, col):
    return pl.BlockSpec((TB, width), lambda i: (i, col))


def _full(shape):
    zeros = (0,) * len(shape)
    return pl.BlockSpec(shape, lambda *_: zeros)


def _halo_prev(width, col=0):
    return pl.BlockSpec((HALO, width), lambda i: (jnp.maximum(i * (TB // HALO) - 1, 0), col))


def _halo_next(width, n_rows, col=0):
    last = n_rows // HALO - 1
    return pl.BlockSpec((HALO, width), lambda i: (jnp.minimum((i + 1) * (TB // HALO), last), col))


def _dot(a, b):
    return jnp.dot(a, b, preferred_element_type=F32)


def _dot_nt(a, b):
    return lax.dot_general(a, b, NT_DIMS, preferred_element_type=F32)


def _dot_tn(a, b):
    return lax.dot_general(a, b, TN_DIMS, preferred_element_type=F32)


def _dot_exact(a, b):
    return jnp.dot(a, b, precision=lax.Precision.HIGHEST, preferred_element_type=F32)


def _rms(x, g):
    r = lax.rsqrt(jnp.mean(x * x, axis=-1, keepdims=True) + EPS)
    return x * r * g


def _rms_bwd(x, g, dy):
    r = lax.rsqrt(jnp.mean(x * x, axis=-1, keepdims=True) + EPS)
    n = x * r
    dg = jnp.sum(dy * n, axis=0, keepdims=True)
    dn = dy * g
    dx = r * (dn - n * jnp.mean(dn * n, axis=-1, keepdims=True))
    return dx, dg, n * g


def _sigmoid(x):
    return jax.nn.sigmoid(x)


def _log_sigmoid(x):
    return jnp.minimum(x, 0.0) - jnp.log(1.0 + jnp.exp(-jnp.abs(x)))


def _lane_group_select(lane, v2, v4, v8, v16):
    return jnp.where(lane < 64, v2, jnp.where(lane < 128, v4, jnp.where(lane < 192, v8, v16)))


def _pool_counts(t0, rows):
    lane = lax.broadcasted_iota(jnp.int32, (rows, DG), 1)
    t = lax.broadcasted_iota(jnp.int32, (rows, DG), 0) + t0
    win = _lane_group_select(lane, 2, 4, 8, 16)
    return jnp.minimum(t + 1, win).astype(F32), lane


def _mixin_fwd(h, g, win, wf):
    t_len = h.shape[0]

    def body(h_ref, g_ref, win_ref, wf_ref, zc_ref, qkv_ref, fl_ref):
        xn = _rms(h_ref[...], g_ref[...]).astype(BF16)
        z = _dot(xn, win_ref[...].reshape(D, W_MAIN))
        zc_ref[:, 0:512] = z[:, 0:512]
        zc_ref[:, 512:1536] = z[:, 1280:2304]
        qkv_ref[...] = z[:, 512:1280].astype(BF16)
        fl_ref[...] = _dot(xn, wf_ref[...])

    return _pcall(
        body, name="mixin_fwd", grid=(t_len // TB,),
        in_specs=[_tok(D), _full((1, D)), _full((N_DEV, D // N_DEV, W_MAIN)), _full((D, LANES))],
        out_specs=[_tok(1536), _tok(768), _tok(LANES)],
        out_shape=[_sds((t_len, 1536), F32), _sds((t_len, 768), BF16), _sds((t_len, LANES), F32)],
        compiler_params=_cparams(1),
    )(h, g, win, wf)


def _transpose_lanes8(x):
    eye = (lax.broadcasted_iota(jnp.int32, (8, LANES), 0) == lax.broadcasted_iota(jnp.int32, (8, LANES), 1)).astype(F32)
    return lax.dot_general(eye, x, NT_DIMS, precision=lax.Precision.HIGHEST, preferred_element_type=F32)


def _cumsum_fwd(fl, b_row):
    t_len = fl.shape[0]

    def body(fl_ref, b_ref, c_ref, ct_ref, carry):
        @pl.when(pl.program_id(0) == 0)
        def _():
            carry[...] = jnp.zeros_like(carry)

        r = lax.broadcasted_iota(jnp.int32, (TB, TB), 0)
        s = lax.broadcasted_iota(jnp.int32, (TB, TB), 1)
        lf = _log_sigmoid(fl_ref[...] + b_ref[...])
        c = _dot_exact((r >= s).astype(F32), lf) + carry[0:1, :]
        c_ref[...] = c
        ct_ref[...] = _transpose_lanes8(c)
        carry[...] += jnp.sum(lf, axis=0, keepdims=True)

    return _pcall(
        body, name="cumsum_fwd", grid=(t_len // TB,),
        in_specs=[_tok(LANES), _full((1, LANES))],
        out_specs=[_tok(LANES), pl.BlockSpec((8, TB), lambda i: (0, i))],
        out_shape=[_sds((t_len, LANES), F32), _sds((8, t_len), F32)],
        scratch_shapes=[pltpu.VMEM((8, LANES), F32)],
        compiler_params=_cparams(1),
    )(fl, b_row)


def _layer_norm_parts(y, g, b):
    mu = jnp.mean(y, axis=-1, keepdims=True)
    yc = y - mu
    r = lax.rsqrt(jnp.mean(yc * yc, axis=-1, keepdims=True) + EPS)
    n = yc * r
    return n, r, n * g + b


def _pool_window_sums(p0, p1, p2, p3):
    e = HALO + TB
    p1[8:e, :] = p0[pl.ds(8, e - 8), :] + p0[pl.ds(7, e - 8), :]
    p2[16:e, :] = p1[pl.ds(16, e - 16), :] + p1[pl.ds(14, e - 16), :]
    p3[24:e, :] = p2[pl.ds(24, e - 24), :] + p2[pl.ds(20, e - 24), :]
    s16 = p3[pl.ds(HALO, TB), :] + p3[pl.ds(HALO - 8, TB), :]
    return p1[pl.ds(HALO, TB), :], p2[pl.ds(HALO, TB), :], p3[pl.ds(HALO, TB), :], s16


def _branch_fwd(zc, w_dw, ln_g, ln_b, w_pw, w_sc, w_pool, pool_scale):
    t_len = zc.shape[0]
    e = HALO + TB

    def body(z_ref, zh_ref, dw_ref, g_ref, b_ref, pw_ref, sc_ref, pool_ref, ps_ref, cat_ref, u_s, ch_s, p0, p1, p2, p3):
        i = pl.program_id(0)
        hm = (i > 0).astype(F32)
        u_s[0:HALO, :] = zh_ref[:, 0:256] * _sigmoid(zh_ref[:, 256:512]) * hm
        u_s[HALO:e, :] = z_ref[:, 0:256] * _sigmoid(z_ref[:, 256:512])
        y = jnp.zeros((TB, DG), F32)
        for k in range(CONF_K):
            y = y + dw_ref[k:k + 1, :] * u_s[pl.ds(HALO - (CONF_K - 1) + k, TB), :]
        _, _, yn = _layer_norm_parts(y, g_ref[...], b_ref[...])
        s = yn * _sigmoid(yn)
        cat_ref[:, 0:256] = _dot(s.astype(BF16), pw_ref[...])
        ch_s[0:HALO, :] = zh_ref[:, 1024:1280] * zh_ref[:, 512:768] * hm
        ch_s[HALO:e, :] = z_ref[:, 1024:1280] * z_ref[:, 512:768]
        cv = jnp.zeros((TB, DG), F32)
        for k in range(SC_K):
            cv = cv + sc_ref[k:k + 1, :] * ch_s[pl.ds(HALO - (SC_K - 1) + k, TB), :]
        cat_ref[:, 256:512] = z_ref[:, 768:1024] * cv
        p0[0:HALO, :] = zh_ref[:, 1280:1536] * hm
        p0[HALO:e, :] = z_ref[:, 1280:1536]
        s2, s4, s8, s16 = _pool_window_sums(p0, p1, p2, p3)
        cnt, lane = _pool_counts(i * TB, TB)
        dlt = _lane_group_select(lane, s2, s4, s8, s16) / cnt - z_ref[:, 1280:1536]
        cat_ref[:, 512:768] = _dot(dlt.astype(BF16), pool_ref[...]) * ps_ref[...]

    scr = [pltpu.VMEM((e, DG), F32) for _ in range(6)]
    return _pcall(
        body, name="branch_fwd", grid=(t_len // TB,),
        in_specs=[_tok(1536), _halo_prev(1536), _full((32, DG)), _full((1, DG)), _full((1, DG)), _full((DG, DG)),
                  _full((8, DG)), _full((DG, DG)), _full((1, DG))],
        out_specs=_tok(768), out_shape=_sds((t_len, 768), F32), scratch_shapes=scr, compiler_params=_cparams(1),
    )(zc, zc, w_dw, ln_g, ln_b, w_pw, w_sc, w_pool, pool_scale)


def _head_masks(rows):
    lane = lax.broadcasted_iota(jnp.int32, (rows, LANES), 1)
    return lane, (lane < HEAD_DIM, lane >= HEAD_DIM)


def _keep_lanes(mask, x):
    return jnp.where(mask, x.astype(F32), 0.0).astype(BF16)


def _attn_fwd(qkv, c, ct):
    t_len = qkv.shape[0]

    def body(q_ref, k_ref, v_ref, c_ref, ct_ref, o_ref, lse_ref, lset_ref):
        i = pl.program_id(0)
        lane, halves = _head_masks(TB)
        crow = c_ref[...]
        causal = lax.broadcasted_iota(jnp.int32, (TB, TB), 0) >= lax.broadcasted_iota(jnp.int32, (TB, TB), 1)
        lse_out = jnp.zeros((TB, LANES), F32)
        for g in range(2):
            cols = slice(g * LANES, (g + 1) * LANES)
            qg = q_ref[:, cols]
            outs = []
            for hh in range(2):
                h = 2 * g + hh
                qm = _keep_lanes(halves[hh], qg)
                cq = jnp.sum(jnp.where(lane == h, crow, 0.0), axis=1, keepdims=True)

                def block(j, carry, masked, qm=qm, cq=cq, h=h, cols=cols):
                    m, l, acc = carry
                    off = pl.multiple_of(j * TB, TB)
                    kj = k_ref[pl.ds(off, TB), cols]
                    vj = v_ref[pl.ds(off, TB), cols]
                    s = _dot_nt(qm, kj) * SCALE + (cq - ct_ref[h:h + 1, pl.ds(off, TB)])
                    if masked:
                        s = jnp.where(causal, s, -jnp.inf)
                    m_new = jnp.maximum(m, jnp.max(s, axis=1, keepdims=True))
                    alpha = jnp.exp(m - m_new)
                    p = jnp.exp(s - m_new)
                    l = alpha * l + jnp.sum(p, axis=1, keepdims=True)
                    acc = alpha * acc + _dot(p.astype(BF16), vj)
                    return m_new, l, acc

                init = (jnp.full((TB, 1), -jnp.inf, F32), jnp.zeros((TB, 1), F32), jnp.zeros((TB, LANES), F32))
                carry = lax.fori_loop(0, i, lambda j, cr, block=block: block(j, cr, False), init)
                m, l, acc = block(i, carry, True)
                outs.append(acc / l)
                lse_out = jnp.where(lane == h, m + jnp.log(l), lse_out)
            o_ref[:, cols] = jnp.where(halves[0], outs[0], outs[1])
        lse_ref[...] = lse_out
        lset_ref[...] = _transpose_lanes8(lse_out)

    return _pcall(
        body, name="attn_fwd", grid=(t_len // TB,),
        in_specs=[_tokcol(DG, 0), pl.BlockSpec((t_len, DG), lambda i: (0, 1)), pl.BlockSpec((t_len, DG), lambda i: (0, 2)),
                  _tok(LANES), _full((8, t_len))],
        out_specs=[_tok(DG), _tok(LANES), pl.BlockSpec((8, TB), lambda i: (0, i))],
        out_shape=[_sds((t_len, DG), F32), _sds((t_len, LANES), F32), _sds((8, t_len), F32)],
        compiler_params=_cparams(1),
    )(qkv, qkv, qkv, c, ct)


def _mixout_fwd(h, cat3, o, w_rows, g):
    t_len = h.shape[0]

    def body(h_ref, cat_ref, o_ref, w_ref, g_ref, mix_ref, h1_ref):
        w = w_ref[...].reshape(D, D)
        mix = (_dot(cat_ref[:, 0:256].astype(BF16), w[0:256]) + _dot(o_ref[...].astype(BF16), w[256:512])
               + _dot(cat_ref[:, 256:768].astype(BF16), w[512:1024]))
        mix_ref[...] = mix
        h1_ref[...] = h_ref[...] + _rms(mix, g_ref[...])

    return _pcall(
        body, name="mixout_fwd", grid=(t_len // TB,),
        in_specs=[_tok(D), _tok(768), _tok(DG), pl.BlockSpec((N_DEV, 128, D), lambda i: (0, 4, 0)), _full((1, D))],
        out_specs=[_tok(D), _tok(D)], out_shape=[_sds((t_len, D), F32)] * 2, compiler_params=_cparams(1),
    )(h, cat3, o, w_rows, g)


def _mlp_fwd(h, g_pre, w_up, w_rows, g_post):
    t_len = h.shape[0]

    def body(h_ref, g1_ref, up_ref, dn_ref, g2_ref, u_ref, ff_ref, h2_ref, hn_s, acc_s):
        j = pl.program_id(1)

        @pl.when(j == 0)
        def _():
            hn_s[...] = _rms(h_ref[...], g1_ref[...]).astype(BF16)
            acc_s[...] = jnp.zeros_like(acc_s)

        u = _dot(hn_s[...], up_ref[...])
        u_ref[...] = u
        r = jnp.maximum(u, 0.0)
        acc_s[...] += _dot((r * r).astype(BF16), dn_ref[...])

        @pl.when(j == N_DEV - 1)
        def _():
            ff = acc_s[...]
            ff_ref[...] = ff
            h2_ref[...] = h_ref[...] + _rms(ff, g2_ref[...])

    tok2 = pl.BlockSpec((TB, D), lambda i, j: (i, 0))
    vec2 = pl.BlockSpec((1, D), lambda i, j: (0, 0))
    return _pcall(
        body, name="mlp_fwd", grid=(t_len // TB, N_DEV),
        in_specs=[tok2, vec2, pl.BlockSpec((None, D, FF_BLK), lambda i, j: (j, 0, 0)),
                  pl.BlockSpec((None, FF_BLK, D), lambda i, j: (j, 0, 0)), vec2],
        out_specs=[pl.BlockSpec((TB, FF_BLK), lambda i, j: (i, j)), tok2, tok2],
        out_shape=[_sds((t_len, D_FF), F32), _sds((t_len, D), F32), _sds((t_len, D), F32)],
        scratch_shapes=[pltpu.VMEM((TB, D), BF16), pltpu.VMEM((TB, D), F32)], compiler_params=_cparams(2),
    )(h, g_pre, w_up, w_rows, g_post)


def _ple_fwd(h, p, g_pre, w_rows, w_proj, g_post):
    t_len = h.shape[0]

    def body(h_ref, p_ref, g1_ref, wg_ref, wp_ref, g2_ref, pp_ref, gate_ref, h3_ref):
        hn = _rms(h_ref[...], g1_ref[...]).astype(BF16)
        gate = _sigmoid(_dot(hn, wg_ref[...].reshape(D, D)))
        pp = _dot(p_ref[...].astype(BF16), wp_ref[...])
        pp_ref[...] = pp
        gate_ref[...] = gate
        h3_ref[...] = h_ref[...] + _rms(pp * gate, g2_ref[...])

    return _pcall(
        body, name="ple_fwd", grid=(t_len // TB,),
        in_specs=[_tok(D), _tok(D_PLE), _full((1, D)), pl.BlockSpec((N_DEV, 128, D), lambda i: (0, 5, 0)),
                  _full((D_PLE, D)), _full((1, D))],
        out_specs=[_tok(D)] * 3, out_shape=[_sds((t_len, D), F32)] * 3, compiler_params=_cparams(1),
    )(h, p, g_pre, w_rows, w_proj, g_post)


def _loss_bwd(h, target):
    t_len = h.shape[0]

    def body(h_ref, t_ref, dh_ref, loss_ref):
        @pl.when(pl.program_id(0) == 0)
        def _():
            loss_ref[...] = jnp.zeros_like(loss_ref)

        d = h_ref[...] - t_ref[...]
        dh_ref[...] = d * (1.0 / D)
        loss_ref[...] += 0.5 * jnp.sum(jnp.mean(d * d, axis=-1, keepdims=True), axis=0, keepdims=True)

    return _pcall(
        body, name="loss_bwd", grid=(t_len // TB,), in_specs=[_tok(D), _tok(D)],
        out_specs=[_tok(D), _full((8, LANES))], out_shape=[_sds((t_len, D), F32), _sds((8, LANES), F32)],
        compiler_params=_cparams(1),
    )(h, target)


def _acc_init(refs):
    @pl.when(pl.program_id(0) == 0)
    def _():
        for r in refs:
            r[...] = jnp.zeros_like(r)


def _ple_bwd(dh3, h2, pp, gate, g_post, g_pre, w_rows):
    t_len = dh3.shape[0]

    def body(dh_ref, h_ref, pp_ref, gate_ref, g2_ref, g1_ref, wg_ref, dh2_ref, dpp_ref, dpre_ref, hn_ref, dg2_ref, dg1_ref):
        _acc_init([dg2_ref, dg1_ref])
        dh = dh_ref[...]
        pp, gate = pp_ref[...], gate_ref[...]
        de, dg2, _ = _rms_bwd(pp * gate, g2_ref[...], dh)
        dg2_ref[...] += dg2
        dpp_ref[...] = (de * gate).astype(BF16)
        dpre = (de * pp * gate * (1.0 - gate)).astype(BF16)
        dpre_ref[...] = dpre
        dhn = _dot_nt(dpre, wg_ref[...].reshape(D, D))
        dx, dg1, hn = _rms_bwd(h_ref[...], g1_ref[...], dhn)
        dg1_ref[...] += dg1
        hn_ref[...] = hn.astype(BF16)
        dh2_ref[...] = dh + dx

    return _pcall(
        body, name="ple_bwd", grid=(t_len // TB,),
        in_specs=[_tok(D)] * 4 + [_full((1, D)), _full((1, D)), pl.BlockSpec((N_DEV, 128, D), lambda i: (0, 5, 0))],
        out_specs=[_tok(D)] * 4 + [_full((1, D))] * 2,
        out_shape=[_sds((t_len, D), F32)] + [_sds((t_len, D), BF16)] * 3 + [_sds((1, D), F32)] * 2,
        compiler_params=_cparams(1),
    )(dh3, h2, pp, gate, g_post, g_pre, w_rows)


def _mlp_bwd(dh2, h1, u, ff, g_post, g_pre, w_up, w_rows):
    t_len = dh2.shape[0]

    def body(dh_ref, h_ref, u_ref, ff_ref, g2_ref, g1_ref, up_ref, dn_ref,
             dh1_ref, a2_ref, du_ref, dff_ref, hn_ref, dg2_ref, dg1_ref, dff_s, acc_s):
        i, j = pl.program_id(0), pl.program_id(1)

        @pl.when((i == 0) & (j == 0))
        def _():
            dg2_ref[...] = jnp.zeros_like(dg2_ref)
            dg1_ref[...] = jnp.zeros_like(dg1_ref)

        @pl.when(j == 0)
        def _():
            dff, dg2, _ = _rms_bwd(ff_ref[...], g2_ref[...], dh_ref[...])
            dg2_ref[...] += dg2
            dff_s[...] = dff.astype(BF16)
            dff_ref[...] = dff.astype(BF16)
            acc_s[...] = jnp.zeros_like(acc_s)

        r = jnp.maximum(u_ref[...], 0.0)
        a2_ref[...] = (r * r).astype(BF16)
        du = (_dot_nt(dff_s[...], dn_ref[...]) * (2.0 * r)).astype(BF16)
        du_ref[...] = du
        acc_s[...] += _dot_nt(du, up_ref[...])

        @pl.when(j == N_DEV - 1)
        def _():
            dx, dg1, hn = _rms_bwd(h_ref[...], g1_ref[...], acc_s[...])
            dg1_ref[...] += dg1
            hn_ref[...] = hn.astype(BF16)
            dh1_ref[...] = dh_ref[...] + dx

    tok2 = pl.BlockSpec((TB, D), lambda i, j: (i, 0))
    vec2 = pl.BlockSpec((1, D), lambda i, j: (0, 0))
    blk2 = pl.BlockSpec((TB, FF_BLK), lambda i, j: (i, j))
    return _pcall(
        body, name="mlp_bwd", grid=(t_len // TB, N_DEV),
        in_specs=[tok2, tok2, blk2, tok2, vec2, vec2, pl.BlockSpec((None, D, FF_BLK), lambda i, j: (j, 0, 0)),
                  pl.BlockSpec((None, FF_BLK, D), lambda i, j: (j, 0, 0))],
        out_specs=[tok2, blk2, blk2, tok2, tok2, vec2, vec2],
        out_shape=[_sds((t_len, D), F32), _sds((t_len, D_FF), BF16), _sds((t_len, D_FF), BF16), _sds((t_len, D), BF16),
                   _sds((t_len, D), BF16), _sds((1, D), F32), _sds((1, D), F32)],
        scratch_shapes=[pltpu.VMEM((TB, D), BF16), pltpu.VMEM((TB, D), F32)], compiler_params=_cparams(2),
    )(dh2, h1, u, ff, g_post, g_pre, w_up, w_rows)


def _mixout_bwd(dh1, mix, cat3, o, g, w_rows):
    t_len = dh1.shape[0]

    def body(dh_ref, mix_ref, cat_ref, o_ref, g_ref, w_ref, dcat_ref, dmix_ref, catb_ref, dg_ref):
        _acc_init([dg_ref])
        dmix, dg, _ = _rms_bwd(mix_ref[...], g_ref[...], dh_ref[...])
        dg_ref[...] += dg
        dmix = dmix.astype(BF16)
        dmix_ref[...] = dmix
        dcat_ref[...] = _dot_nt(dmix, w_ref[...].reshape(D, D))
        catb_ref[:, 0:256] = cat_ref[:, 0:256].astype(BF16)
        catb_ref[:, 256:512] = o_ref[...].astype(BF16)
        catb_ref[:, 512:1024] = cat_ref[:, 256:768].astype(BF16)

    return _pcall(
        body, name="mixout_bwd", grid=(t_len // TB,),
        in_specs=[_tok(D), _tok(D), _tok(768), _tok(DG), _full((1, D)), pl.BlockSpec((N_DEV, 128, D), lambda i: (0, 4, 0))],
        out_specs=[_tok(D), _tok(D), _tok(D), _full((1, D))],
        out_shape=[_sds((t_len, D), F32), _sds((t_len, D), BF16), _sds((t_len, D), BF16), _sds((1, D), F32)],
        compiler_params=_cparams(1),
    )(dh1, mix, cat3, o, g, w_rows)


def _attn_bwd_dsum(qkv, dcat, c, ct, lset):
    t_len = qkv.shape[0]

    def body(q_ref, do_ref, k_ref, v_ref, c_ref, ct_ref, lset_ref, dt_ref):
        i = pl.program_id(0)
        lane, halves = _head_masks(TB)
        causal_t = lax.broadcasted_iota(jnp.int32, (TB, TB), 1) >= lax.broadcasted_iota(jnp.int32, (TB, TB), 0)
        sub = lax.broadcasted_iota(jnp.int32, (8, TB), 0)
        out = jnp.zeros((8, TB), F32)
        for g in range(2):
            cols = slice(g * LANES, (g + 1) * LANES)
            qi = q_ref[:, cols]
            doi = do_ref[:, cols].astype(BF16)
            for hh in range(2):
                h = 2 * g + hh
                dom = _keep_lanes(halves[hh], doi)
                cq = ct_ref[h:h + 1, :]
                lse_row = lset_ref[h:h + 1, :]

                def block(j, acc, masked, qi=qi, dom=dom, cq=cq, lse_row=lse_row, h=h, hh=hh, cols=cols):
                    off = pl.multiple_of(j * TB, TB)
                    km = _keep_lanes(halves[hh], k_ref[pl.ds(off, TB), cols])
                    ck = jnp.sum(jnp.where(lane == h, c_ref[pl.ds(off, TB), :], 0.0), axis=1, keepdims=True)
                    st = _dot_nt(km, qi) * SCALE + (cq - ck)
                    if masked:
                        st = jnp.where(causal_t, st, -jnp.inf)
                    pt = jnp.exp(st - lse_row)
                    return acc + jnp.sum(pt * _dot_nt(v_ref[pl.ds(off, TB), cols], dom), axis=0, keepdims=True)

                acc = lax.fori_loop(0, i, lambda j, cr, block=block: block(j, cr, False), jnp.zeros((1, TB), F32))
                out = jnp.where(sub == h, block(i, acc, True), out)
        dt_ref[...] = out

    row8 = pl.BlockSpec((8, TB), lambda i: (0, i))
    return _pcall(
        body, name="attn_bwd_dsum", grid=(t_len // TB,),
        in_specs=[_tokcol(DG, 0), _tokcol(DG, 1), pl.BlockSpec((t_len, DG), lambda i: (0, 1)),
                  pl.BlockSpec((t_len, DG), lambda i: (0, 2)), _full((t_len, LANES)), row8, row8],
        out_specs=row8, out_shape=_sds((8, t_len), F32), compiler_params=_cparams(1),
    )(qkv, dcat, qkv, qkv, c, ct, lset)


def _attn_bwd_dq(qkv, c, ct, o, dcat, lse):
    t_len = qkv.shape[0]

    def body(q_ref, k_ref, v_ref, c_ref, ct_ref, o_ref, do_ref, lse_ref, dq_ref, dob_ref):
        i = pl.program_id(0)
        lane, halves = _head_masks(TB)
        crow, lse = c_ref[...], lse_ref[...]
        causal = lax.broadcasted_iota(jnp.int32, (TB, TB), 0) >= lax.broadcasted_iota(jnp.int32, (TB, TB), 1)
        dob_ref[...] = do_ref[...].astype(BF16)
        for g in range(2):
            cols = slice(g * LANES, (g + 1) * LANES)
            qg, dog = q_ref[:, cols], do_ref[:, cols].astype(BF16).astype(F32)
            prod = dog * o_ref[:, cols]
            outs = []
            for hh in range(2):
                h = 2 * g + hh
                qm = _keep_lanes(halves[hh], qg)
                dom = jnp.where(halves[hh], dog, 0.0).astype(BF16)
                dsum = jnp.sum(jnp.where(halves[hh], prod, 0.0), axis=1, keepdims=True)
                cq = jnp.sum(jnp.where(lane == h, crow, 0.0), axis=1, keepdims=True)
                lse_h = jnp.sum(jnp.where(lane == h, lse, 0.0), axis=1, keepdims=True)

                def block(j, dq, masked, qm=qm, dom=dom, dsum=dsum, cq=cq, lse_h=lse_h, h=h, cols=cols):
                    off = pl.multiple_of(j * TB, TB)
                    kj = k_ref[pl.ds(off, TB), cols]
                    vj = v_ref[pl.ds(off, TB), cols]
                    s = _dot_nt(qm, kj) * SCALE + (cq - ct_ref[h:h + 1, pl.ds(off, TB)])
                    if masked:
                        s = jnp.where(causal, s, -jnp.inf)
                    p = jnp.exp(s - lse_h)
                    ds = p * (_dot_nt(dom, vj) - dsum)
                    return dq + _dot(ds.astype(BF16), kj)

                dq = lax.fori_loop(0, i, lambda j, cr, block=block: block(j, cr, False), jnp.zeros((TB, LANES), F32))
                outs.append(block(i, dq, True))
            dq_ref[:, cols] = jnp.where(halves[0], outs[0], outs[1]) * SCALE

    return _pcall(
        body, name="attn_bwd_dq", grid=(t_len // TB,),
        in_specs=[_tokcol(DG, 0), pl.BlockSpec((t_len, DG), lambda i: (0, 1)), pl.BlockSpec((t_len, DG), lambda i: (0, 2)),
                  _tok(LANES), _full((8, t_len)), _tok(DG), _tokcol(DG, 1), _tok(LANES)],
        out_specs=[_tok(DG), _tok(DG)], out_shape=[_sds((t_len, DG), F32), _sds((t_len, DG), BF16)],
        compiler_params=_cparams(1),
    )(qkv, qkv, qkv, c, ct, o, dcat, lse)


def _attn_bwd_dkv(qkv, dob, c, ct, lset, dt):
    t_len = qkv.shape[0]
    n_q = t_len // TB

    def body(q_ref, dob_ref, k_ref, v_ref, c_ref, ct_ref, lset_ref, dt_ref, dk_ref, dv_ref, dc_ref):
        j = pl.program_id(0)
        lane, halves = _head_masks(TB)
        crow = c_ref[...]
        causal_t = lax.broadcasted_iota(jnp.int32, (TB, TB), 1) >= lax.broadcasted_iota(jnp.int32, (TB, TB), 0)
        dc_out = jnp.zeros((TB, LANES), F32)
        for g in range(2):
            cols = slice(g * LANES, (g + 1) * LANES)
            kg, vg = k_ref[:, cols], v_ref[:, cols]
            dk_g = jnp.zeros((TB, LANES), F32)
            dv_g = jnp.zeros((TB, LANES), F32)
            for hh in range(2):
                h = 2 * g + hh
                km = _keep_lanes(halves[hh], kg)
                ck = jnp.sum(jnp.where(lane == h, crow, 0.0), axis=1, keepdims=True)

                def block(i, carry, masked, km=km, ck=ck, vg=vg, h=h, hh=hh, cols=cols):
                    dk, dv, dcs = carry
                    off = pl.multiple_of(i * TB, TB)
                    qi = q_ref[pl.ds(off, TB), cols]
                    doi = dob_ref[pl.ds(off, TB), cols]
                    dom = _keep_lanes(halves[hh], doi)
                    qmi = _keep_lanes(halves[hh], qi)
                    st = _dot_nt(km, qi) * SCALE + (ct_ref[h:h + 1, pl.ds(off, TB)] - ck)
                    if masked:
                        st = jnp.where(causal_t, st, -jnp.inf)
                    pt = jnp.exp(st - lset_ref[h:h + 1, pl.ds(off, TB)])
                    dv = dv + _dot(pt.astype(BF16), dom)
                    dst = pt * (_dot_nt(vg, dom) - dt_ref[h:h + 1, pl.ds(off, TB)])
                    dk = dk + _dot(dst.astype(BF16), qmi)
                    dcs = dcs + jnp.sum(dst, axis=1, keepdims=True)
                    return dk, dv, dcs

                init = (jnp.zeros((TB, LANES), F32), jnp.zeros((TB, LANES), F32), jnp.zeros((TB, 1), F32))
                carry = block(j, init, True)
                dk, dv, dcs = lax.fori_loop(j + 1, n_q, lambda i, cr, block=block: block(i, cr, False), carry)
                dk_g, dv_g = dk_g + dk, dv_g + dv
                dc_out = jnp.where(lane == h, -dcs, dc_out)
            dk_ref[:, cols] = dk_g * SCALE
            dv_ref[:, cols] = dv_g
        dc_ref[...] = dc_out

    return _pcall(
        body, name="attn_bwd_dkv", grid=(n_q,),
        in_specs=[pl.BlockSpec((t_len, DG), lambda i: (0, 0)), _full((t_len, DG)), _tokcol(DG, 1), _tokcol(DG, 2),
                  _tok(LANES), _full((8, t_len)), _full((8, t_len)), _full((8, t_len))],
        out_specs=[_tok(DG), _tok(DG), _tok(LANES)],
        out_shape=[_sds((t_len, DG), F32), _sds((t_len, DG), F32), _sds((t_len, LANES), F32)],
        compiler_params=_cparams(1),
    )(qkv, dob, qkv, qkv, c, ct, lset, dt)


def _forget_bwd(dc, fl, b_row):
    t_len = dc.shape[0]
    n_t = t_len // TB
    rev = pl.BlockSpec((TB, LANES), lambda i: (n_t - 1 - i, 0))

    def body(dc_ref, fl_ref, b_ref, dfl_ref, db_ref, carry):
        @pl.when(pl.program_id(0) == 0)
        def _():
            carry[...] = jnp.zeros_like(carry)
            db_ref[...] = jnp.zeros_like(db_ref)

        r = lax.broadcasted_iota(jnp.int32, (TB, TB), 0)
        s = lax.broadcasted_iota(jnp.int32, (TB, TB), 1)
        dc = dc_ref[...]
        dl = _dot_exact((r <= s).astype(F32), dc) + carry[0:1, :]
        carry[...] += jnp.sum(dc, axis=0, keepdims=True)
        dfl = dl * _sigmoid(-(fl_ref[...] + b_ref[...]))
        dfl_ref[...] = dfl
        db_ref[...] += jnp.sum(dfl, axis=0, keepdims=True)

    return _pcall(
        body, name="forget_bwd", grid=(n_t,), in_specs=[rev, rev, _full((1, LANES))],
        out_specs=[rev, _full((1, LANES))], out_shape=[_sds((t_len, LANES), F32), _sds((1, LANES), F32)],
        scratch_shapes=[pltpu.VMEM((8, LANES), F32)], compiler_params=_cparams(1),
    )(dc, fl, b_row)


def _branch_bwd(zc, dcat, dq, dk, dv, w_dw, ln_g, ln_b, w_pw, w_sc, w_pool, pool_scale):
    t_len = zc.shape[0]
    n_t = t_len // TB
    e2 = HALO + TB + HALO
    e1 = TB + HALO

    def body(z_ref, zp_ref, zn_ref, dcf_ref, dcfn_ref, dsp_ref, dspn_ref, dq_ref, dk_ref, dv_ref,
             dw_ref, g_ref, b_ref, pw_ref, sc_ref, pool_ref, ps_ref,
             dz_ref, ddw_ref, dg_ref, db_ref, dpw_ref, dsc_ref, dpool_ref, dps_ref,
             u_s, dy_s, ch_s, dcv_s, p0, p1, p2, p3, g0, g1, g2, g3):
        i = pl.program_id(0)
        _acc_init([ddw_ref, dg_ref, db_ref, dpw_ref, dsc_ref, dpool_ref, dps_ref])
        hm = (i > 0).astype(F32)
        nm = (i < n_t - 1).astype(F32)

        sig_b = _sigmoid(z_ref[:, 256:512])
        a = z_ref[:, 0:256]
        u_s[0:HALO, :] = zp_ref[:, 0:256] * _sigmoid(zp_ref[:, 256:512]) * hm
        u_s[HALO:HALO + TB, :] = a * sig_b
        u_s[HALO + TB:e2, :] = zn_ref[:, 0:256] * _sigmoid(zn_ref[:, 256:512])
        y = jnp.zeros((e1, DG), F32)
        for k in range(CONF_K):
            y = y + dw_ref[k:k + 1, :] * u_s[pl.ds(HALO - (CONF_K - 1) + k, e1), :]
        n, r, yn = _layer_norm_parts(y, g_ref[...], b_ref[...])
        sg = _sigmoid(yn)
        dyc = jnp.concatenate([dcf_ref[...], dcfn_ref[...] * nm], axis=0)
        ds = _dot_nt(dyc.astype(BF16), pw_ref[...])
        dyn = ds * sg * (1.0 + yn * (1.0 - sg))
        dg_ref[...] += jnp.sum((dyn * n)[0:TB], axis=0, keepdims=True)
        db_ref[...] += jnp.sum(dyn[0:TB], axis=0, keepdims=True)
        dn = dyn * g_ref[...]
        dyv = r * (dn - jnp.mean(dn, axis=-1, keepdims=True) - n * jnp.mean(dn * n, axis=-1, keepdims=True))
        dpw_ref[...] += _dot_tn((yn * sg)[0:TB].astype(BF16), dcf_ref[...].astype(BF16))
        dy_s[...] = dyv
        du = jnp.zeros((TB, DG), F32)
        dyv_t = dyv[0:TB]
        for k in range(CONF_K):
            du = du + dw_ref[k:k + 1, :] * dy_s[pl.ds(CONF_K - 1 - k, TB), :]
            ddw_ref[k:k + 1, :] += jnp.sum(dyv_t * u_s[pl.ds(HALO - (CONF_K - 1) + k, TB), :], axis=0, keepdims=True)
        dz_ref[:, 0:256] = (du * sig_b).astype(BF16)
        dz_ref[:, 256:512] = (du * a * sig_b * (1.0 - sig_b)).astype(BF16)

        dz_ref[:, 512:768] = dq_ref[...].astype(BF16)
        dz_ref[:, 768:1024] = dk_ref[...].astype(BF16)
        dz_ref[:, 1024:1280] = dv_ref[...].astype(BF16)

        sc_h, sc_b, sc_c = z_ref[:, 512:768], z_ref[:, 768:1024], z_ref[:, 1024:1280]
        ch_s[0:HALO, :] = zp_ref[:, 1024:1280] * zp_ref[:, 512:768] * hm
        ch_s[HALO:HALO + TB, :] = sc_c * sc_h
        ch_s[HALO + TB:e2, :] = zn_ref[:, 1024:1280] * zn_ref[:, 512:768]
        cv = jnp.zeros((TB, DG), F32)
        for k in range(SC_K):
            cv = cv + sc_ref[k:k + 1, :] * ch_s[pl.ds(HALO - (SC_K - 1) + k, TB), :]
        dy_sc = dsp_ref[:, 0:256]
        dcv_t = dy_sc * sc_b
        dcv_s[0:TB, :] = dcv_t
        dcv_s[TB:e1, :] = dspn_ref[:, 0:256] * nm * zn_ref[:, 768:1024]
        dch = jnp.zeros((TB, DG), F32)
        for k in range(SC_K):
            dch = dch + sc_ref[k:k + 1, :] * dcv_s[pl.ds(SC_K - 1 - k, TB), :]
            dsc_ref[k:k + 1, :] += jnp.sum(dcv_t * ch_s[pl.ds(HALO - (SC_K - 1) + k, TB), :], axis=0, keepdims=True)
        dz_ref[:, 1280:1536] = (dch * sc_c).astype(BF16)
        dz_ref[:, 1536:1792] = (dy_sc * cv).astype(BF16)
        dz_ref[:, 1792:2048] = (dch * sc_h).astype(BF16)

        v_t = z_ref[:, 1280:1536]
        p0[0:HALO, :] = zp_ref[:, 1280:1536] * hm
        p0[HALO:HALO + TB, :] = v_t
        s2, s4, s8, s16 = _pool_window_sums(p0, p1, p2, p3)
        cnt, lane = _pool_counts(i * TB, e1)
        dlt = (_lane_group_select(lane[0:TB], s2, s4, s8, s16) / cnt[0:TB] - v_t).astype(BF16)
        dyp_t = dsp_ref[:, 256:512]
        dps_ref[...] += jnp.sum(dyp_t * _dot(dlt, pool_ref[...]), axis=0, keepdims=True)
        dpre = (jnp.concatenate([dyp_t, dspn_ref[:, 256:512] * nm], axis=0) * ps_ref[...]).astype(BF16)
        dpool_ref[...] += _dot_tn(dlt, dpre[0:TB])
        dd = _dot_nt(dpre, pool_ref[...])
        g0[...] = dd / cnt
        g1[0:TB + 24, :] = g0[pl.ds(0, TB + 24), :] + g0[pl.ds(1, TB + 24), :]
        g2[0:TB + 16, :] = g1[pl.ds(0, TB + 16), :] + g1[pl.ds(2, TB + 16), :]
        g3[0:TB + 8, :] = g2[pl.ds(0, TB + 8), :] + g2[pl.ds(4, TB + 8), :]
        f16 = g3[pl.ds(0, TB), :] + g3[pl.ds(8, TB), :]
        fwd_sum = _lane_group_select(lane[0:TB], g1[pl.ds(0, TB), :], g2[pl.ds(0, TB), :], g3[pl.ds(0, TB), :], f16)
        dz_ref[:, 2048:2304] = (fwd_sum - dd[0:TB]).astype(BF16)

    vec = _full((1, DG))
    mat = _full((DG, DG))
    scr = ([pltpu.VMEM((e2, DG), F32), pltpu.VMEM((e1, DG), F32), pltpu.VMEM((e2, DG), F32), pltpu.VMEM((e1, DG), F32)]
           + [pltpu.VMEM((HALO + TB, DG), F32)] * 4 + [pltpu.VMEM((e1, DG), F32)] * 4)
    return _pcall(
        body, name="branch_bwd", grid=(n_t,),
        in_specs=[_tok(1536), _halo_prev(1536), _halo_next(1536, t_len),
                  _tokcol(DG, 0), _halo_next(DG, t_len, 0), _tokcol(512, 1), _halo_next(512, t_len, 1),
                  _tok(DG), _tok(DG), _tok(DG),
                  _full((32, DG)), vec, vec, mat, _full((8, DG)), mat, vec],
        out_specs=[_tok(W_MAIN), _full((32, DG)), vec, vec, mat, _full((8, DG)), mat, vec],
        out_shape=[_sds((t_len, W_MAIN), BF16), _sds((32, DG), F32), _sds((1, DG), F32), _sds((1, DG), F32),
                   _sds((DG, DG), F32), _sds((8, DG), F32), _sds((DG, DG), F32), _sds((1, DG), F32)],
        scratch_shapes=scr, compiler_params=_cparams(1),
    )(zc, zc, zc, dcat, dcat, dcat, dcat, dq, dk, dv, w_dw, ln_g, ln_b, w_pw, w_sc, w_pool, pool_scale)


def _mixin_bwd(dh, h, g, dz, dfl, win, wf):
    t_len = dh.shape[0]

    def body(dh_ref, h_ref, g_ref, dz_ref, dfl_ref, win_ref, wf_ref, dh0_ref, xn_ref, dg_ref):
        _acc_init([dg_ref])
        dxn = _dot_nt(dz_ref[...], win_ref[...].reshape(D, W_MAIN)) + _dot_nt(dfl_ref[...].astype(BF16), wf_ref[...])
        dx, dg, xn = _rms_bwd(h_ref[...], g_ref[...], dxn)
        dg_ref[...] += dg
        xn_ref[...] = xn.astype(BF16)
        dh0_ref[...] = dh_ref[...] + dx

    return _pcall(
        body, name="mixin_bwd", grid=(t_len // TB,),
        in_specs=[_tok(D), _tok(D), _full((1, D)), _tok(W_MAIN), _tok(LANES), _full((N_DEV, D // N_DEV, W_MAIN)),
                  _full((D, LANES))],
        out_specs=[_tok(D), _tok(D), _full((1, D))],
        out_shape=[_sds((t_len, D), F32), _sds((t_len, D), BF16), _sds((1, D), F32)],
        compiler_params=_cparams(1),
    )(dh, h, g, dz, dfl, win, wf)


def _matmul_tn(name, a, b, tm, tn, out_dtype, block_major=False):
    t_len, m = a.shape
    n = b.shape[1]
    tk = min(t_len, 1024)
    n_k = t_len // tk

    def body(a_ref, b_ref, o_ref, acc):
        k = pl.program_id(2)

        @pl.when(k == 0)
        def _():
            acc[...] = jnp.zeros_like(acc)

        acc[...] += _dot_tn(a_ref[...].astype(BF16), b_ref[...].astype(BF16))

        @pl.when(k == n_k - 1)
        def _():
            o_ref[...] = acc[...].astype(out_dtype)

    if block_major:
        out_spec = pl.BlockSpec((None, tm, tn), lambda i, j, k: (j, i, 0))
        out_shape = _sds((n // tn, m, tn), out_dtype)
    else:
        out_spec = pl.BlockSpec((tm, tn), lambda i, j, k: (i, j))
        out_shape = _sds((m, n), out_dtype)
    return _pcall(
        body, name=name, grid=(m // tm, n // tn, n_k),
        in_specs=[pl.BlockSpec((tk, tm), lambda i, j, k: (k, i)), pl.BlockSpec((tk, tn), lambda i, j, k: (k, j))],
        out_specs=out_spec, out_shape=out_shape, scratch_shapes=[pltpu.VMEM((tm, tn), F32)], compiler_params=_cparams(3),
    )(a, b)


_HBM = pl.BlockSpec(memory_space=pltpu.HBM)


def _mesh_place():
    return lax.axis_index("x"), lax.axis_index("y"), lax.axis_index("c")


def _allgather(name, srcs):
    n = len(srcs)

    def body(*refs):
        src, dst = refs[:n], refs[n:2 * n]
        send_sems, recv_sems, local_sems = refs[2 * n:]
        x, y, c = _mesh_place()
        me, sibling = (x, y, c), (x, y, 1 - c)
        chips = [(1 - x, y), (x, 1 - y), (1 - x, 1 - y)]

        def slot(px, py, pc):
            return 4 * px + 2 * py + pc

        def copy(t, k, block, to, from_src=False):
            return pltpu.make_async_remote_copy(
                src_ref=src[t] if from_src else dst[t].at[slot(*block)], dst_ref=dst[t].at[slot(*block)],
                send_sem=send_sems.at[t, k], recv_sem=recv_sems.at[t, k], device_id=to, device_id_type=MESH_ID)

        mine = [pltpu.make_async_copy(src[t], dst[t].at[slot(*me)], local_sems.at[t]) for t in range(n)]
        for cp in mine:
            cp.start()
        started = []
        for t in range(n):
            started.append(copy(t, 0, me, sibling, from_src=True))
            started += [copy(t, 1 + j, me, (*chip, c), from_src=True) for j, chip in enumerate(chips)]
        for cp in started:
            cp.start()
        for j, chip in enumerate(chips):
            for t in range(n):
                copy(t, 1 + j, (*chip, c), me).wait_recv()
                fwd = copy(t, 4 + j, (*chip, c), sibling)
                fwd.start()
                started.append(fwd)
        for t in range(n):
            copy(t, 0, sibling, me).wait_recv()
            for j, chip in enumerate(chips):
                copy(t, 4 + j, (*chip, 1 - c), me).wait_recv()
        for cp in started:
            cp.wait_send()
        for cp in mine:
            cp.wait()

    return _pcall(
        body, name=name, in_specs=[_HBM] * n, out_specs=[_HBM] * n,
        out_shape=[_sds((N_DEV,) + s.shape, s.dtype) for s in srcs],
        scratch_shapes=[pltpu.SemaphoreType.DMA((n, 7)), pltpu.SemaphoreType.DMA((n, 7)), pltpu.SemaphoreType.DMA((n,))],
    )(*srcs)


def _exchange(name, srcs, per_dest, groups):
    n = len(srcs)
    n_out = max(k for k, _ in groups) + 1
    out_shapes = []
    for k in range(n_out):
        members = [t for t in range(n) if groups[t][0] == k]
        blk = srcs[members[0]].shape[1:] if per_dest[members[0]] else srcs[members[0]].shape
        out_shapes.append(_sds((len(members), N_DEV) + tuple(blk), srcs[members[0]].dtype))

    def body(*refs):
        src, dst = refs[:n], refs[n:n + n_out]
        send_sems, recv_sems, local_sems = refs[n + n_out:]
        x, y, c = _mesh_place()
        me_slot = 4 * x + 2 * y + c
        peers = []
        for r in range(1, N_DEV):
            px, py, pc = x ^ ((r >> 2) & 1), y ^ ((r >> 1) & 1), c ^ (r & 1)
            peers.append(((px, py, pc), 4 * px + 2 * py + pc))

        def piece(t, dest_slot):
            return src[t].at[dest_slot] if per_dest[t] else src[t]

        def landing(t, sender_slot):
            k, g = groups[t]
            return dst[k].at[g, sender_slot]

        mine = [pltpu.make_async_copy(piece(t, me_slot), landing(t, me_slot), local_sems.at[t]) for t in range(n)]
        for cp in mine:
            cp.start()
        sends = []
        for r, (peer, peer_slot) in enumerate(peers):
            for t in range(n):
                sends.append(pltpu.make_async_remote_copy(
                    src_ref=piece(t, peer_slot), dst_ref=landing(t, me_slot), send_sem=send_sems.at[t, r],
                    recv_sem=recv_sems.at[t, r], device_id=peer, device_id_type=MESH_ID))
        for cp in sends:
            cp.start()
        for r, (peer, peer_slot) in enumerate(peers):
            for t in range(n):
                pltpu.make_async_remote_copy(
                    src_ref=piece(t, peer_slot), dst_ref=landing(t, peer_slot), send_sem=send_sems.at[t, r],
                    recv_sem=recv_sems.at[t, r], device_id=peer, device_id_type=MESH_ID).wait_recv()
        for cp in sends:
            cp.wait_send()
        for cp in mine:
            cp.wait()

    return _pcall(
        body, name=name, in_specs=[_HBM] * n, out_specs=[_HBM] * n_out, out_shape=out_shapes,
        scratch_shapes=[pltpu.SemaphoreType.DMA((n, 7)), pltpu.SemaphoreType.DMA((n, 7)), pltpu.SemaphoreType.DMA((n,))],
    )(*srcs)


def _adam_math(w, g, m, v):
    m = ADAM_B1 * m + (1.0 - ADAM_B1) * g
    v = ADAM_B2 * v + (1.0 - ADAM_B2) * (g * g)
    m_hat = m / (1.0 - ADAM_B1 ** ADAM_STEP)
    v_hat = v / (1.0 - ADAM_B2 ** ADAM_STEP)
    delta = -ADAM_LR * (m_hat / (jnp.sqrt(v_hat) + ADAM_EPS) + ADAM_WD * w)
    return delta, m, v


def _adam_rows(name, parts, w, m, v, row_tile, row_block_offset):
    n_l, rows, cols = w.shape

    def body(p_ref, w_ref, m_ref, v_ref, g_out, d_out, m_out, v_out):
        g = p_ref[0].astype(F32)
        for s in range(1, N_DEV):
            g = g + p_ref[s].astype(F32)
        delta, m_new, v_new = _adam_math(w_ref[...], g, m_ref[...], v_ref[...])
        g_out[...] = g
        d_out[...] = delta
        m_out[...] = m_new
        v_out[...] = v_new

    blk = pl.BlockSpec((None, row_tile, cols), lambda l, i: (l, i, 0))
    return _pcall(
        body, name=name, grid=(n_l, rows // row_tile),
        in_specs=[pl.BlockSpec((None, N_DEV, row_tile, cols), lambda l, i: (l, 0, row_block_offset + i, 0)), blk, blk, blk],
        out_specs=[blk] * 4, out_shape=[_sds(w.shape, F32)] * 4, compiler_params=_cparams(2),
    )(parts, w, m, v)


def _adam_packed(name, parts, w, m, v):
    def body(p_ref, w_ref, m_ref, v_ref, g_out, d_out, m_out, v_out):
        g = p_ref[0]
        for s in range(1, N_DEV):
            g = g + p_ref[s]
        delta, m_new, v_new = _adam_math(w_ref[...], g, m_ref[...], v_ref[...])
        g_out[...] = g
        d_out[...] = delta
        m_out[...] = m_new
        v_out[...] = v_new

    return _pcall(
        body, name=name, grid=(1,), in_specs=[_full(parts.shape), _full(w.shape), _full(w.shape), _full(w.shape)],
        out_specs=[_full(w.shape)] * 4, out_shape=[_sds(w.shape, F32)] * 4, compiler_params=_cparams(1),
    )(parts, w, m, v)


def _pack_rows(flat_parts, lead=()):
    flat = jnp.concatenate(flat_parts, axis=-1)
    n = flat.shape[-1]
    rows = -(-n // LANES)
    rows = -(-rows // 8) * 8
    flat = jnp.pad(flat, [(0, 0)] * len(lead) + [(0, rows * LANES - n)])
    return flat.reshape(lead + (rows, LANES))


def _unpack_rows(packed, shapes, lead=()):
    flat = packed.reshape(lead + (-1,))
    out, off = [], 0
    for shp in shapes:
        size = 1
        for s in shp:
            size *= s
        out.append(flat[..., off:off + size].reshape(lead + tuple(shp)))
        off += size
    return out


_SMALL_SHARD_SHAPES = [(N_LAYERS, 128, 4), (N_LAYERS, 32, DG), (N_LAYERS, D_PLE, 128), (N_LAYERS, CONF_K, 32), (N_LAYERS, SC_K, 32)]
_REP_SHAPES = [(N_LAYERS, D)] * 6 + [(N_LAYERS, DG)] * 3 + [(N_LAYERS, N_HEADS), (N_LAYERS, 4, 64, 64)]


def _small_full_to_shards(fcol, pw, proj, dw, sc):
    return [
        fcol.reshape(N_LAYERS, N_DEV, 128, 4).transpose(1, 0, 2, 3),
        pw.reshape(N_LAYERS, N_DEV, 32, DG).transpose(1, 0, 2, 3),
        proj.reshape(N_LAYERS, D_PLE, N_DEV, 128).transpose(2, 0, 1, 3),
        dw.reshape(N_LAYERS, CONF_K, N_DEV, 32).transpose(2, 0, 1, 3),
        sc.reshape(N_LAYERS, SC_K, N_DEV, 32).transpose(2, 0, 1, 3),
    ]


def _small_shards_to_full(fcol, pw, proj, dw, sc):
    return [
        fcol.transpose(1, 0, 2, 3).reshape(N_LAYERS, D, 4),
        pw.transpose(1, 0, 2, 3).reshape(N_LAYERS, DG, DG),
        proj.transpose(1, 2, 0, 3).reshape(N_LAYERS, D_PLE, D),
        dw.transpose(1, 2, 0, 3).reshape(N_LAYERS, CONF_K, DG),
        sc.transpose(1, 2, 0, 3).reshape(N_LAYERS, SC_K, DG),
    ]


def _pad_rows(a, rows):
    return jnp.pad(a, ((0, rows - a.shape[0]), (0, 0)))


def _block_diag4(w):
    z = jnp.zeros((64, 64), w.dtype)
    return jnp.concatenate([jnp.concatenate([w[g] if k == g else z for k in range(4)], axis=1) for g in range(4)], axis=0)


def kernel(x, p, g_mix_pre, w_in, b_forget, w_conf_dw, conf_ln_g, conf_ln_b, w_conf_pw, w_sc, w_pool, pool_scale, w_out, g_mix_post, g_mlp_pre, w_up, w_down, g_mlp_post, g_ple_pre, w_ple_gate, w_ple_proj, g_ple_post, loss_target, m_g_mix_pre, m_w_in, m_b_forget, m_w_conf_dw, m_conf_ln_g, m_conf_ln_b, m_w_conf_pw, m_w_sc, m_w_pool, m_pool_scale, m_w_out, m_g_mix_post, m_g_mlp_pre, m_w_up, m_w_down, m_g_mlp_post, m_g_ple_pre, m_w_ple_gate, m_w_ple_proj, m_g_ple_post, v_g_mix_pre, v_w_in, v_b_forget, v_w_conf_dw, v_conf_ln_g, v_conf_ln_b, v_w_conf_pw, v_w_sc, v_w_pool, v_pool_scale, v_w_out, v_g_mix_post, v_g_mlp_pre, v_w_up, v_w_down, v_g_mlp_post, v_g_ple_pre, v_w_ple_gate, v_w_ple_proj, v_g_ple_post):
    n_l = N_LAYERS
    t_len = x.shape[1]
    assert t_len % TB == 0 and x.shape[0] == 1 and x.shape[2] == D

    def main_cols(a):
        return jnp.concatenate([a[..., :F_LO], a[..., F_HI:]], axis=-1)

    def fcols(a):
        return a[..., F_LO:F_HI]

    def rows_pack(down, out, gate):
        return jnp.concatenate([down, out, gate], axis=1)

    rows_b = rows_pack(w_down, w_out, w_ple_gate).astype(BF16)
    win_b = main_cols(w_in).astype(BF16)
    wup_b = w_up.astype(BF16)
    small_local = _pack_rows([a.reshape(-1) for a in (fcols(w_in), w_conf_pw, w_ple_proj, w_conf_dw, w_sc)])
    srcs = [rows_b[l] for l in range(n_l)] + [win_b[l] for l in range(n_l)] + [wup_b[l] for l in range(n_l)] + [small_local]
    gathered = _allgather("weight_allgather", srcs)
    rows_g, win_g, wup_g = gathered[0:n_l], gathered[n_l:2 * n_l], gathered[2 * n_l:3 * n_l]
    small_g = _unpack_rows(gathered[3 * n_l], _SMALL_SHARD_SHAPES, lead=(N_DEV,))
    fcol_f, pw_f, proj_f, dw_f, sc_f = _small_shards_to_full(*small_g)
    wf_b = jnp.pad(fcol_f, ((0, 0), (0, 0), (0, LANES - 4))).astype(BF16)
    pw_b, proj_b = pw_f.astype(BF16), proj_f.astype(BF16)
    dw_pad = jnp.pad(dw_f, ((0, 0), (0, 32 - CONF_K), (0, 0)))
    sc_pad = jnp.pad(sc_f, ((0, 0), (0, 8 - SC_K), (0, 0)))
    pool_bd = jnp.stack([_block_diag4(w_pool[l]) for l in range(n_l)]).astype(BF16)
    b_row = jnp.pad(b_forget, ((0, 0), (0, LANES - N_HEADS)))[:, None, :]

    def vec(a, l):
        return a[l][None, :]

    h = x[0]
    saved = []
    for l in range(n_l):
        zc, qkv, fl = _mixin_fwd(h, vec(g_mix_pre, l), win_g[l], wf_b[l])
        c, ct = _cumsum_fwd(fl, b_row[l])
        cat3 = _branch_fwd(zc, dw_pad[l], vec(conf_ln_g, l), vec(conf_ln_b, l), pw_b[l], sc_pad[l], pool_bd[l], vec(pool_scale, l))
        o, lse, lset = _attn_fwd(qkv, c, ct)
        mix, h1 = _mixout_fwd(h, cat3, o, rows_g[l], vec(g_mix_post, l))
        u, ff, h2 = _mlp_fwd(h1, vec(g_mlp_pre, l), wup_g[l], rows_g[l], vec(g_mlp_post, l))
        pp, gate, h3 = _ple_fwd(h2, p[l, 0], vec(g_ple_pre, l), rows_g[l], proj_b[l], vec(g_ple_post, l))
        saved.append(dict(h0=h, zc=zc, qkv=qkv, fl=fl, c=c, ct=ct, cat3=cat3, o=o, lse=lse, lset=lset, mix=mix, h1=h1, u=u, ff=ff,
                          h2=h2, pp=pp, gate=gate))
        h = h3

    dh, loss_part = _loss_bwd(h, loss_target[0])
    loss = lax.psum(loss_part[0, 0], ("x", "y", "c"))

    d_rows, d_win, d_wup = [None] * n_l, [None] * n_l, [None] * n_l
    small_grads = {k: [None] * n_l for k in ("fcol", "pw", "proj", "dw", "sc")}
    rep_grads = {k: [None] * n_l for k in ("g_mix_pre", "g_mix_post", "g_mlp_pre", "g_mlp_post", "g_ple_pre", "g_ple_post",
                                           "ln_g", "ln_b", "pool_scale", "b_forget", "w_pool")}
    for l in reversed(range(n_l)):
        s = saved[l]
        dh, dpp_b, dpre_b, hn3_b, dg_ple_post, dg_ple_pre = _ple_bwd(
            dh, s["h2"], s["pp"], s["gate"], vec(g_ple_post, l), vec(g_ple_pre, l), rows_g[l])
        small_grads["proj"][l] = _matmul_tn("wgrad_proj", p[l, 0], dpp_b, D_PLE, D, F32)
        d_gate = _matmul_tn("wgrad_gate", hn3_b, dpre_b, 512, D, BF16)
        dh, a2_b, du_b, dff_b, hn2_b, dg_mlp_post, dg_mlp_pre = _mlp_bwd(
            dh, s["h1"], s["u"], s["ff"], vec(g_mlp_post, l), vec(g_mlp_pre, l), wup_g[l], rows_g[l])
        d_down = _matmul_tn("wgrad_down", a2_b, dff_b, 512, D, BF16)
        d_wup[l] = _matmul_tn("wgrad_up", hn2_b, du_b, 512, FF_BLK, BF16, block_major=True)
        dcat, dmix_b, cat_b, dg_mix_post = _mixout_bwd(dh, s["mix"], s["cat3"], s["o"], vec(g_mix_post, l), rows_g[l])
        d_out = _matmul_tn("wgrad_out", cat_b, dmix_b, 512, D, BF16)
        dq, dob = _attn_bwd_dq(s["qkv"], s["c"], s["ct"], s["o"], dcat, s["lse"])
        dt = _attn_bwd_dsum(s["qkv"], dcat, s["c"], s["ct"], s["lset"])
        dk, dv, dc = _attn_bwd_dkv(s["qkv"], dob, s["c"], s["ct"], s["lset"], dt)
        dfl, db_f = _forget_bwd(dc, s["fl"], b_row[l])
        dz_b, ddw, dln_g, dln_b, dpw, dsc, dpool, dps = _branch_bwd(
            s["zc"], dcat, dq, dk, dv, dw_pad[l], vec(conf_ln_g, l), vec(conf_ln_b, l), pw_b[l], sc_pad[l], pool_bd[l],
            vec(pool_scale, l))
        dh, xn_b, dg_mix_pre = _mixin_bwd(dh, s["h0"], vec(g_mix_pre, l), dz_b, dfl, win_g[l], wf_b[l])
        d_win[l] = _matmul_tn("wgrad_in", xn_b, dz_b, 512, W_MAIN // 2, BF16).reshape(N_DEV, 128, W_MAIN)
        small_grads["fcol"][l] = _matmul_tn("wgrad_fcol", xn_b, dfl, 512, LANES, F32)[:, 0:4]
        d_rows[l] = jnp.concatenate([d_down.reshape(N_DEV, FF_BLK, D), d_out.reshape(N_DEV, 128, D),
                                     d_gate.reshape(N_DEV, 128, D)], axis=1)
        small_grads["pw"][l], small_grads["dw"][l], small_grads["sc"][l] = dpw, ddw[0:CONF_K], dsc[0:SC_K]
        rep_grads["g_mix_pre"][l], rep_grads["g_mix_post"][l] = dg_mix_pre[0], dg_mix_post[0]
        rep_grads["g_mlp_pre"][l], rep_grads["g_mlp_post"][l] = dg_mlp_pre[0], dg_mlp_post[0]
        rep_grads["g_ple_pre"][l], rep_grads["g_ple_post"][l] = dg_ple_pre[0], dg_ple_post[0]
        rep_grads["ln_g"][l], rep_grads["ln_b"][l], rep_grads["pool_scale"][l] = dln_g[0], dln_b[0], dps[0]
        rep_grads["b_forget"][l] = db_f[0, 0:N_HEADS]
        rep_grads["w_pool"][l] = jnp.stack([dpool[64 * g:64 * g + 64, 64 * g:64 * g + 64] for g in range(4)])
    grad_x = dh[None]

    small_part = _pack_rows(
        [a.reshape(N_DEV, -1) for a in _small_full_to_shards(*[jnp.stack(small_grads[k]) for k in ("fcol", "pw", "proj", "dw", "sc")])],
        lead=(N_DEV,))
    rep_order = ("g_mix_pre", "g_mix_post", "g_mlp_pre", "g_mlp_post", "g_ple_pre", "g_ple_post", "ln_g", "ln_b",
                 "pool_scale", "b_forget", "w_pool")
    rep_part = _pack_rows([jnp.stack(rep_grads[k]).reshape(-1) for k in rep_order])
    ex_srcs = d_rows + d_win + d_wup + [small_part, rep_part]
    ex_per_dest = [True] * (3 * n_l + 1) + [False]
    ex_groups = [(0, l) for l in range(n_l)] + [(1, l) for l in range(n_l)] + [(2, l) for l in range(n_l)] + [(3, 0), (4, 0)]
    r_rows, r_win, r_wup, r_small, r_rep = _exchange("grad_exchange", ex_srcs, ex_per_dest, ex_groups)

    res = {}
    res["w_down"] = _adam_rows("adam_down", r_rows, w_down, m_w_down, v_w_down, 128, 0)
    res["w_out"] = _adam_rows("adam_out", r_rows, w_out, m_w_out, v_w_out, 128, 4)
    res["w_ple_gate"] = _adam_rows("adam_gate", r_rows, w_ple_gate, m_w_ple_gate, v_w_ple_gate, 128, 5)
    res["w_up"] = _adam_rows("adam_up", r_wup, w_up, m_w_up, v_w_up, 256, 0)
    win_main = _adam_rows("adam_in", r_win, main_cols(w_in), main_cols(m_w_in), main_cols(v_w_in), 128, 0)

    small_w = [(fcols(w_in), w_conf_pw, w_ple_proj, w_conf_dw, w_sc), (fcols(m_w_in), m_w_conf_pw, m_w_ple_proj, m_w_conf_dw, m_w_sc),
               (fcols(v_w_in), v_w_conf_pw, v_w_ple_proj, v_w_conf_dw, v_w_sc)]
    small_packed = [_pack_rows([a.reshape(-1) for a in grp]) for grp in small_w]
    small_res = [_unpack_rows(a, _SMALL_SHARD_SHAPES) for a in _adam_packed("adam_small", r_small[0], *small_packed)]
    rep_w = [(g_mix_pre, g_mix_post, g_mlp_pre, g_mlp_post, g_ple_pre, g_ple_post, conf_ln_g, conf_ln_b, pool_scale, b_forget, w_pool),
             (m_g_mix_pre, m_g_mix_post, m_g_mlp_pre, m_g_mlp_post, m_g_ple_pre, m_g_ple_post, m_conf_ln_g, m_conf_ln_b, m_pool_scale,
              m_b_forget, m_w_pool),
             (v_g_mix_pre, v_g_mix_post, v_g_mlp_pre, v_g_mlp_post, v_g_ple_pre, v_g_ple_post, v_conf_ln_g, v_conf_ln_b, v_pool_scale,
              v_b_forget, v_w_pool)]
    rep_packed = [_pack_rows([a.reshape(-1) for a in grp]) for grp in rep_w]
    rep_res = [_unpack_rows(a, _REP_SHAPES) for a in _adam_packed("adam_replicated", r_rep[0], *rep_packed)]

    for kind in range(4):
        fc, pw, proj, dwc, scc = small_res[kind]
        main = win_main[kind]
        (rg_mix_pre, rg_mix_post, rg_mlp_pre, rg_mlp_post, rg_ple_pre, rg_ple_post, r_ln_g, r_ln_b, r_ps, r_bf, r_wpool) = rep_res[kind]
        res.setdefault("by_kind", []).append(dict(
            g_mix_pre=rg_mix_pre, w_in=jnp.concatenate([main[..., :F_LO], fc, main[..., F_LO:]], axis=-1), b_forget=r_bf,
            w_conf_dw=dwc, conf_ln_g=r_ln_g, conf_ln_b=r_ln_b, w_conf_pw=pw, w_sc=scc, w_pool=r_wpool, pool_scale=r_ps,
            w_out=res["w_out"][kind], g_mix_post=rg_mix_post, g_mlp_pre=rg_mlp_pre, w_up=res["w_up"][kind],
            w_down=res["w_down"][kind], g_mlp_post=rg_mlp_post, g_ple_pre=rg_ple_pre, w_ple_gate=res["w_ple_gate"][kind],
            w_ple_proj=proj, g_ple_post=rg_ple_post))
    names = ("g_mix_pre", "w_in", "b_forget", "w_conf_dw", "conf_ln_g", "conf_ln_b", "w_conf_pw", "w_sc", "w_pool", "pool_scale",
             "w_out", "g_mix_post", "g_mlp_pre", "w_up", "w_down", "g_mlp_post", "g_ple_pre", "w_ple_gate", "w_ple_proj", "g_ple_post")
    outs = [loss, grad_x]
    for kind in range(4):
        outs += [res["by_kind"][kind][nm] for nm in names]
    return tuple(outs)
```

```python
import jax
import jax.numpy as jnp
from jax import lax
from jax.experimental import pallas as pl
from jax.experimental.pallas import tpu as pltpu

F32, BF16 = jnp.float32, jnp.bfloat16

D = 1024
DG = 256
N_HEADS = 4
HEAD_DIM = 64
CONF_K = 31
SC_K = 3
POOL_WINDOWS = (2, 4, 8, 16)
D_FF = 4096
D_PLE = 256
N_LAYERS = 4
N_DEV = 8
EPS = 1e-6
SCALE = HEAD_DIM ** -0.5
W_MAIN = 2304
F_LO, F_HI = 1280, 1284

ADAM_LR, ADAM_B1, ADAM_B2, ADAM_EPS, ADAM_WD, ADAM_STEP = 0.001, 0.9, 0.999, 1e-08, 0.01, 10

TB = 512
HALO = 32
LANES = 128
FF_BLK = D_FF // N_DEV
VMEM_LIMIT = 56 * 1024 * 1024

NT_DIMS = (((1,), (1,)), ((), ()))
TN_DIMS = (((0,), (0,)), ((), ()))
MESH_ID = pl.DeviceIdType.MESH


def _pcall(body, **kw):
    return pl.pallas_call(body, **kw)


def _cparams(n_axes):
    return pltpu.CompilerParams(dimension_semantics=("arbitrary",) * n_axes, vmem_limit_bytes=VMEM_LIMIT)


def _sds(shape, dtype):
    return jax.ShapeDtypeStruct(shape, dtype)


def _tok(width, tb=TB):
    return pl.BlockSpec((tb, width), lambda i: (i, 0))


def _tokcol(width, col):
    return pl.BlockSpec((TB, width), lambda i: (i, col))


def _full(shape):
    zeros = (0,) * len(shape)
    return pl.BlockSpec(shape, lambda *_: zeros)


def _halo_prev(width, col=0):
    return pl.BlockSpec((HALO, width), lambda i: (jnp.maximum(i * (TB // HALO) - 1, 0), col))


def _halo_next(width, n_rows, col=0):
    last = n_rows // HALO - 1
    return pl.BlockSpec((HALO, width), lambda i: (jnp.minimum((i + 1) * (TB // HALO), last), col))


def _dot(a, b):
    return jnp.dot(a, b, preferred_element_type=F32)


def _dot_nt(a, b):
    return lax.dot_general(a, b, NT_DIMS, preferred_element_type=F32)


def _dot_tn(a, b):
    return lax.dot_general(a, b, TN_DIMS, preferred_element_type=F32)


def _dot_exact(a, b):
    return jnp.dot(a, b, precision=lax.Precision.HIGHEST, preferred_element_type=F32)


def _rms(x, g):
    r = lax.rsqrt(jnp.mean(x * x, axis=-1, keepdims=True) + EPS)
    return x * r * g


def _rms_bwd(x, g, dy):
    r = lax.rsqrt(jnp.mean(x * x, axis=-1, keepdims=True) + EPS)
    n = x * r
    dg = jnp.sum(dy * n, axis=0, keepdims=True)
    dn = dy * g
    dx = r * (dn - n * jnp.mean(dn * n, axis=-1, keepdims=True))
    return dx, dg, n * g


def _sigmoid(x):
    return jax.nn.sigmoid(x)


def _log_sigmoid(x):
    return jnp.minimum(x, 0.0) - jnp.log(1.0 + jnp.exp(-jnp.abs(x)))


def _lane_group_select(lane, v2, v4, v8, v16):
    return jnp.where(lane < 64, v2, jnp.where(lane < 128, v4, jnp.where(lane < 192, v8, v16)))


def _pool_counts(t0, rows):
    lane = lax.broadcasted_iota(jnp.int32, (rows, DG), 1)
    t = lax.broadcasted_iota(jnp.int32, (rows, DG), 0) + t0
    win = _lane_group_select(lane, 2, 4, 8, 16)
    return jnp.minimum(t + 1, win).astype(F32), lane


def _mixin_fwd(h, g, win, wf):
    t_len = h.shape[0]

    def body(h_ref, g_ref, win_ref, wf_ref, zc_ref, qkv_ref, fl_ref):
        xn = _rms(h_ref[...], g_ref[...]).astype(BF16)
        z = _dot(xn, win_ref[...].reshape(D, W_MAIN))
        zc_ref[:, 0:512] = z[:, 0:512]
        zc_ref[:, 512:1536] = z[:, 1280:2304]
        qkv_ref[:, 0:256] = (z[:, 512:768] * SCALE).astype(BF16)
        qkv_ref[:, 256:768] = z[:, 768:1280].astype(BF16)
        fl_ref[...] = _dot(xn, wf_ref[...])

    return _pcall(
        body, name="mixin_fwd", grid=(t_len // TB,),
        in_specs=[_tok(D), _full((1, D)), _full((N_DEV, D // N_DEV, W_MAIN)), _full((D, LANES))],
        out_specs=[_tok(1536), _tok(768), _tok(LANES)],
        out_shape=[_sds((t_len, 1536), F32), _sds((t_len, 768), BF16), _sds((t_len, LANES), F32)],
        compiler_params=_cparams(1),
    )(h, g, win, wf)


def _transpose_lanes8(x):
    eye = (lax.broadcasted_iota(jnp.int32, (8, LANES), 0) == lax.broadcasted_iota(jnp.int32, (8, LANES), 1)).astype(F32)
    return lax.dot_general(eye, x, NT_DIMS, precision=lax.Precision.HIGHEST, preferred_element_type=F32)


def _cumsum_fwd(fl, b_row):
    t_len = fl.shape[0]

    def body(fl_ref, b_ref, c_ref, ct_ref, carry):
        @pl.when(pl.program_id(0) == 0)
        def _():
            carry[...] = jnp.zeros_like(carry)

        r = lax.broadcasted_iota(jnp.int32, (TB, TB), 0)
        s = lax.broadcasted_iota(jnp.int32, (TB, TB), 1)
        lf = _log_sigmoid(fl_ref[...] + b_ref[...])
        c = _dot_exact((r >= s).astype(F32), lf) + carry[0:1, :]
        c_ref[...] = c
        ct_ref[...] = _transpose_lanes8(c)
        carry[...] += jnp.sum(lf, axis=0, keepdims=True)

    return _pcall(
        body, name="cumsum_fwd", grid=(t_len // TB,),
        in_specs=[_tok(LANES), _full((1, LANES))],
        out_specs=[_tok(LANES), pl.BlockSpec((8, TB), lambda i: (0, i))],
        out_shape=[_sds((t_len, LANES), F32), _sds((8, t_len), F32)],
        scratch_shapes=[pltpu.VMEM((8, LANES), F32)],
        compiler_params=_cparams(1),
    )(fl, b_row)


def _layer_norm_parts(y, g, b):
    mu = jnp.mean(y, axis=-1, keepdims=True)
    yc = y - mu
    r = lax.rsqrt(jnp.mean(yc * yc, axis=-1, keepdims=True) + EPS)
    n = yc * r
    return n, r, n * g + b


def _pool_window_sums(p0, p1, p2, p3):
    e = HALO + TB
    p1[8:e, :] = p0[pl.ds(8, e - 8), :] + p0[pl.ds(7, e - 8), :]
    p2[16:e, :] = p1[pl.ds(16, e - 16), :] + p1[pl.ds(14, e - 16), :]
    p3[24:e, :] = p2[pl.ds(24, e - 24), :] + p2[pl.ds(20, e - 24), :]
    s16 = p3[pl.ds(HALO, TB), :] + p3[pl.ds(HALO - 8, TB), :]
    return p1[pl.ds(HALO, TB), :], p2[pl.ds(HALO, TB), :], p3[pl.ds(HALO, TB), :], s16


def _branch_fwd(zc, w_dw, ln_g, ln_b, w_pw, w_sc, w_pool, pool_scale):
    t_len = zc.shape[0]
    e = HALO + TB

    def body(z_ref, zh_ref, dw_ref, g_ref, b_ref, pw_ref, sc_ref, pool_ref, ps_ref, cat_ref, u_s, ch_s, p0, p1, p2, p3):
        i = pl.program_id(0)
        hm = (i > 0).astype(F32)
        u_s[0:HALO, :] = zh_ref[:, 0:256] * _sigmoid(zh_ref[:, 256:512]) * hm
        u_s[HALO:e, :] = z_ref[:, 0:256] * _sigmoid(z_ref[:, 256:512])
        y = jnp.zeros((TB, DG), F32)
        for k in range(CONF_K):
            y = y + dw_ref[k:k + 1, :] * u_s[pl.ds(HALO - (CONF_K - 1) + k, TB), :]
        _, _, yn = _layer_norm_parts(y, g_ref[...], b_ref[...])
        s = yn * _sigmoid(yn)
        cat_ref[:, 0:256] = _dot(s.astype(BF16), pw_ref[...])
        ch_s[0:HALO, :] = zh_ref[:, 1024:1280] * zh_ref[:, 512:768] * hm
        ch_s[HALO:e, :] = z_ref[:, 1024:1280] * z_ref[:, 512:768]
        cv = jnp.zeros((TB, DG), F32)
        for k in range(SC_K):
            cv = cv + sc_ref[k:k + 1, :] * ch_s[pl.ds(HALO - (SC_K - 1) + k, TB), :]
        cat_ref[:, 256:512] = z_ref[:, 768:1024] * cv
        p0[0:HALO, :] = zh_ref[:, 1280:1536] * hm
        p0[HALO:e, :] = z_ref[:, 1280:1536]
        s2, s4, s8, s16 = _pool_window_sums(p0, p1, p2, p3)
        cnt, lane = _pool_counts(i * TB, TB)
        dlt = _lane_group_select(lane, s2, s4, s8, s16) / cnt - z_ref[:, 1280:1536]
        cat_ref[:, 512:768] = _dot(dlt.astype(BF16), pool_ref[...]) * ps_ref[...]

    scr = [pltpu.VMEM((e, DG), F32) for _ in range(6)]
    return _pcall(
        body, name="branch_fwd", grid=(t_len // TB,),
        in_specs=[_tok(1536), _halo_prev(1536), _full((32, DG)), _full((1, DG)), _full((1, DG)), _full((DG, DG)),
                  _full((8, DG)), _full((DG, DG)), _full((1, DG))],
        out_specs=_tok(768), out_shape=_sds((t_len, 768), F32), scratch_shapes=scr, compiler_params=_cparams(1),
    )(zc, zc, w_dw, ln_g, ln_b, w_pw, w_sc, w_pool, pool_scale)


def _head_masks(rows):
    lane = lax.broadcasted_iota(jnp.int32, (rows, LANES), 1)
    return lane, (lane < HEAD_DIM, lane >= HEAD_DIM)


def _keep_lanes(mask, x):
    return jnp.where(mask, x.astype(F32), 0.0).astype(BF16)


def _attn_fwd(qkv, c, ct):
    t_len = qkv.shape[0]

    def body(q_ref, k_ref, v_ref, c_ref, ct_ref, o_ref, lset_ref):
        i = pl.program_id(0)
        lane, halves = _head_masks(TB)
        crow = c_ref[...]
        causal = lax.broadcasted_iota(jnp.int32, (TB, TB), 0) >= lax.broadcasted_iota(jnp.int32, (TB, TB), 1)
        lse_out = jnp.zeros((TB, LANES), F32)
        for g in range(2):
            cols = slice(g * LANES, (g + 1) * LANES)
            qg = q_ref[:, cols]
            qms = [_keep_lanes(halves[hh], qg) for hh in range(2)]
            cqs = [jnp.sum(jnp.where(lane == 2 * g + hh, crow, 0.0), axis=1, keepdims=True) for hh in range(2)]

            def block(j, carry, masked):
                off = pl.multiple_of(j * TB, TB)
                kj = k_ref[pl.ds(off, TB), cols]
                vj = v_ref[pl.ds(off, TB), cols]
                new = []
                for hh in range(2):
                    m, l, acc = carry[hh]
                    s = _dot_nt(qms[hh], kj) + (cqs[hh] - ct_ref[2 * g + hh:2 * g + hh + 1, pl.ds(off, TB)])
                    if masked:
                        s = jnp.where(causal, s, -jnp.inf)
                    m_new = jnp.maximum(m, jnp.max(s, axis=1, keepdims=True))
                    alpha = jnp.exp(m - m_new)
                    p = jnp.exp(s - m_new)
                    l = alpha * l + jnp.sum(p, axis=1, keepdims=True)
                    acc = alpha * acc + _dot(p.astype(BF16), vj)
                    new.append((m_new, l, acc))
                return tuple(new)

            init = tuple((jnp.full((TB, 1), -jnp.inf, F32), jnp.zeros((TB, 1), F32), jnp.zeros((TB, LANES), F32))
                         for _ in range(2))
            carry = lax.fori_loop(0, i, lambda j, cr: block(j, cr, False), init)
            (m0, l0, acc0), (m1, l1, acc1) = block(i, carry, True)
            o_ref[:, cols] = jnp.where(halves[0], acc0 / l0, acc1 / l1)
            lse_out = jnp.where(lane == 2 * g, m0 + jnp.log(l0), lse_out)
            lse_out = jnp.where(lane == 2 * g + 1, m1 + jnp.log(l1), lse_out)
        lset_ref[...] = _transpose_lanes8(lse_out)

    return _pcall(
        body, name="attn_fwd", grid=(t_len // TB,),
        in_specs=[_tokcol(DG, 0), pl.BlockSpec((t_len, DG), lambda i: (0, 1)), pl.BlockSpec((t_len, DG), lambda i: (0, 2)),
                  _tok(LANES), _full((8, t_len))],
        out_specs=[_tok(DG), pl.BlockSpec((8, TB), lambda i: (0, i))],
        out_shape=[_sds((t_len, DG), F32), _sds((8, t_len), F32)],
        compiler_params=_cparams(1),
    )(qkv, qkv, qkv, c, ct)


def _mixout_fwd(h, cat3, o, w_rows, g):
    t_len = h.shape[0]

    def body(h_ref, cat_ref, o_ref, w_ref, g_ref, mix_ref, h1_ref):
        w = w_ref[...].reshape(D, D)
        mix = (_dot(cat_ref[:, 0:256].astype(BF16), w[0:256]) + _dot(o_ref[...].astype(BF16), w[256:512])
               + _dot(cat_ref[:, 256:768].astype(BF16), w[512:1024]))
        mix_ref[...] = mix
        h1_ref[...] = h_ref[...] + _rms(mix, g_ref[...])

    return _pcall(
        body, name="mixout_fwd", grid=(t_len // TB,),
        in_specs=[_tok(D), _tok(768), _tok(DG), pl.BlockSpec((N_DEV, 128, D), lambda i: (0, 4, 0)), _full((1, D))],
        out_specs=[_tok(D), _tok(D)], out_shape=[_sds((t_len, D), F32)] * 2, compiler_params=_cparams(1),
    )(h, cat3, o, w_rows, g)


def _mlp_fwd(h, g_pre, w_up, w_rows, g_post):
    t_len = h.shape[0]

    def body(h_ref, g1_ref, up_ref, dn_ref, g2_ref, u_ref, ff_ref, h2_ref, hn_s, acc_s):
        j = pl.program_id(1)

        @pl.when(j == 0)
        def _():
            hn_s[...] = _rms(h_ref[...], g1_ref[...]).astype(BF16)
            acc_s[...] = jnp.zeros_like(acc_s)

        u = _dot(hn_s[...], up_ref[...])
        u_ref[...] = u
        r = jnp.maximum(u, 0.0)
        acc_s[...] += _dot((r * r).astype(BF16), dn_ref[...])

        @pl.when(j == N_DEV - 1)
        def _():
            ff = acc_s[...]
            ff_ref[...] = ff
            h2_ref[...] = h_ref[...] + _rms(ff, g2_ref[...])

    tok2 = pl.BlockSpec((TB, D), lambda i, j: (i, 0))
    vec2 = pl.BlockSpec((1, D), lambda i, j: (0, 0))
    return _pcall(
        body, name="mlp_fwd", grid=(t_len // TB, N_DEV),
        in_specs=[tok2, vec2, pl.BlockSpec((None, D, FF_BLK), lambda i, j: (j, 0, 0)),
                  pl.BlockSpec((None, FF_BLK, D), lambda i, j: (j, 0, 0)), vec2],
        out_specs=[pl.BlockSpec((TB, FF_BLK), lambda i, j: (i, j)), tok2, tok2],
        out_shape=[_sds((t_len, D_FF), F32), _sds((t_len, D), F32), _sds((t_len, D), F32)],
        scratch_shapes=[pltpu.VMEM((TB, D), BF16), pltpu.VMEM((TB, D), F32)], compiler_params=_cparams(2),
    )(h, g_pre, w_up, w_rows, g_post)


def _ple_fwd(h, p, g_pre, w_rows, w_proj, g_post):
    t_len = h.shape[0]

    def body(h_ref, p_ref, g1_ref, wg_ref, wp_ref, g2_ref, pp_ref, gate_ref, h3_ref):
        hn = _rms(h_ref[...], g1_ref[...]).astype(BF16)
        gate = _sigmoid(_dot(hn, wg_ref[...].reshape(D, D)))
        pp = _dot(p_ref[...].astype(BF16), wp_ref[...])
        pp_ref[...] = pp
        gate_ref[...] = gate
        h3_ref[...] = h_ref[...] + _rms(pp * gate, g2_ref[...])

    return _pcall(
        body, name="ple_fwd", grid=(t_len // TB,),
        in_specs=[_tok(D), _tok(D_PLE), _full((1, D)), pl.BlockSpec((N_DEV, 128, D), lambda i: (0, 5, 0)),
                  _full((D_PLE, D)), _full((1, D))],
        out_specs=[_tok(D)] * 3, out_shape=[_sds((t_len, D), F32)] * 3, compiler_params=_cparams(1),
    )(h, p, g_pre, w_rows, w_proj, g_post)


def _loss_bwd(h, target):
    t_len = h.shape[0]

    def body(h_ref, t_ref, dh_ref, loss_ref):
        @pl.when(pl.program_id(0) == 0)
        def _():
            loss_ref[...] = jnp.zeros_like(loss_ref)

        d = h_ref[...] - t_ref[...]
        dh_ref[...] = d * (1.0 / D)
        loss_ref[...] += 0.5 * jnp.sum(jnp.mean(d * d, axis=-1, keepdims=True), axis=0, keepdims=True)

    return _pcall(
        body, name="loss_bwd", grid=(t_len // TB,), in_specs=[_tok(D), _tok(D)],
        out_specs=[_tok(D), _full((8, LANES))], out_shape=[_sds((t_len, D), F32), _sds((8, LANES), F32)],
        compiler_params=_cparams(1),
    )(h, target)


def _acc_init(refs):
    @pl.when(pl.program_id(0) == 0)
    def _():
        for r in refs:
            r[...] = jnp.zeros_like(r)


def _ple_bwd(dh3, h2, pp, gate, g_post, g_pre, w_rows):
    t_len = dh3.shape[0]

    def body(dh_ref, h_ref, pp_ref, gate_ref, g2_ref, g1_ref, wg_ref, dh2_ref, dpp_ref, dpre_ref, hn_ref, dg2_ref, dg1_ref):
        _acc_init([dg2_ref, dg1_ref])
        dh = dh_ref[...]
        pp, gate = pp_ref[...], gate_ref[...]
        de, dg2, _ = _rms_bwd(pp * gate, g2_ref[...], dh)
        dg2_ref[...] += dg2
        dpp_ref[...] = (de * gate).astype(BF16)
        dpre = (de * pp * gate * (1.0 - gate)).astype(BF16)
        dpre_ref[...] = dpre
        dhn = _dot_nt(dpre, wg_ref[...].reshape(D, D))
        dx, dg1, hn = _rms_bwd(h_ref[...], g1_ref[...], dhn)
        dg1_ref[...] += dg1
        hn_ref[...] = hn.astype(BF16)
        dh2_ref[...] = dh + dx

    return _pcall(
        body, name="ple_bwd", grid=(t_len // TB,),
        in_specs=[_tok(D)] * 4 + [_full((1, D)), _full((1, D)), pl.BlockSpec((N_DEV, 128, D), lambda i: (0, 5, 0))],
        out_specs=[_tok(D)] * 4 + [_full((1, D))] * 2,
        out_shape=[_sds((t_len, D), F32)] + [_sds((t_len, D), BF16)] * 3 + [_sds((1, D), F32)] * 2,
        compiler_params=_cparams(1),
    )(dh3, h2, pp, gate, g_post, g_pre, w_rows)


def _mlp_bwd(dh2, h1, u, ff, g_post, g_pre, w_up, w_rows):
    t_len = dh2.shape[0]

    def body(dh_ref, h_ref, u_ref, ff_ref, g2_ref, g1_ref, up_ref, dn_ref,
             dh1_ref, a2_ref, du_ref, dff_ref, hn_ref, dg2_ref, dg1_ref, dff_s, acc_s):
        i, j = pl.program_id(0), pl.program_id(1)

        @pl.when((i == 0) & (j == 0))
        def _():
            dg2_ref[...] = jnp.zeros_like(dg2_ref)
            dg1_ref[...] = jnp.zeros_like(dg1_ref)

        @pl.when(j == 0)
        def _():
            dff, dg2, _ = _rms_bwd(ff_ref[...], g2_ref[...], dh_ref[...])
            dg2_ref[...] += dg2
            dff_s[...] = dff.astype(BF16)
            dff_ref[...] = dff.astype(BF16)
            acc_s[...] = jnp.zeros_like(acc_s)

        r = jnp.maximum(u_ref[...], 0.0)
        a2_ref[...] = (r * r).astype(BF16)
        du = (_dot_nt(dff_s[...], dn_ref[...]) * (2.0 * r)).astype(BF16)
        du_ref[...] = du
        acc_s[...] += _dot_nt(du, up_ref[...])

        @pl.when(j == N_DEV - 1)
        def _():
            dx, dg1, hn = _rms_bwd(h_ref[...], g1_ref[...], acc_s[...])
            dg1_ref[...] += dg1
            hn_ref[...] = hn.astype(BF16)
            dh1_ref[...] = dh_ref[...] + dx

    tok2 = pl.BlockSpec((TB, D), lambda i, j: (i, 0))
    vec2 = pl.BlockSpec((1, D), lambda i, j: (0, 0))
    blk2 = pl.BlockSpec((TB, FF_BLK), lambda i, j: (i, j))
    return _pcall(
        body, name="mlp_bwd", grid=(t_len // TB, N_DEV),
        in_specs=[tok2, tok2, blk2, tok2, vec2, vec2, pl.BlockSpec((None, D, FF_BLK), lambda i, j: (j, 0, 0)),
                  pl.BlockSpec((None, FF_BLK, D), lambda i, j: (j, 0, 0))],
        out_specs=[tok2, blk2, blk2, tok2, tok2, vec2, vec2],
        out_shape=[_sds((t_len, D), F32), _sds((t_len, D_FF), BF16), _sds((t_len, D_FF), BF16), _sds((t_len, D), BF16),
                   _sds((t_len, D), BF16), _sds((1, D), F32), _sds((1, D), F32)],
        scratch_shapes=[pltpu.VMEM((TB, D), BF16), pltpu.VMEM((TB, D), F32)], compiler_params=_cparams(2),
    )(dh2, h1, u, ff, g_post, g_pre, w_up, w_rows)


def _mixout_bwd(dh1, mix, cat3, o, g, w_rows):
    t_len = dh1.shape[0]

    def body(dh_ref, mix_ref, cat_ref, o_ref, g_ref, w_ref, dcat_ref, dmix_ref, catb_ref, dg_ref):
        _acc_init([dg_ref])
        dmix, dg, _ = _rms_bwd(mix_ref[...], g_ref[...], dh_ref[...])
        dg_ref[...] += dg
        dmix = dmix.astype(BF16)
        dmix_ref[...] = dmix
        dcat_ref[...] = _dot_nt(dmix, w_ref[...].reshape(D, D))
        catb_ref[:, 0:256] = cat_ref[:, 0:256].astype(BF16)
        catb_ref[:, 256:512] = o_ref[...].astype(BF16)
        catb_ref[:, 512:1024] = cat_ref[:, 256:768].astype(BF16)

    return _pcall(
        body, name="mixout_bwd", grid=(t_len // TB,),
        in_specs=[_tok(D), _tok(D), _tok(768), _tok(DG), _full((1, D)), pl.BlockSpec((N_DEV, 128, D), lambda i: (0, 4, 0))],
        out_specs=[_tok(D), _tok(D), _tok(D), _full((1, D))],
        out_shape=[_sds((t_len, D), F32), _sds((t_len, D), BF16), _sds((t_len, D), BF16), _sds((1, D), F32)],
        compiler_params=_cparams(1),
    )(dh1, mix, cat3, o, g, w_rows)


def _attn_bwd_dsum(qkv, dcat, c, ct, lset):
    t_len = qkv.shape[0]

    def body(q_ref, do_ref, k_ref, v_ref, c_ref, ct_ref, lset_ref, dt_ref, dob_ref):
        i = pl.program_id(0)
        lane, halves = _head_masks(TB)
        causal_t = lax.broadcasted_iota(jnp.int32, (TB, TB), 1) >= lax.broadcasted_iota(jnp.int32, (TB, TB), 0)
        sub = lax.broadcasted_iota(jnp.int32, (8, TB), 0)
        dob_ref[...] = do_ref[...].astype(BF16)
        out = jnp.zeros((8, TB), F32)
        for g in range(2):
            cols = slice(g * LANES, (g + 1) * LANES)
            qi = q_ref[:, cols]
            doi = do_ref[:, cols].astype(BF16)
            doms = [_keep_lanes(halves[hh], doi) for hh in range(2)]
            cqs = [ct_ref[2 * g + hh:2 * g + hh + 1, :] for hh in range(2)]
            lses = [lset_ref[2 * g + hh:2 * g + hh + 1, :] for hh in range(2)]

            def block(j, accs, masked):
                off = pl.multiple_of(j * TB, TB)
                kj = k_ref[pl.ds(off, TB), cols]
                vj = v_ref[pl.ds(off, TB), cols]
                cj = c_ref[pl.ds(off, TB), :]
                new = []
                for hh in range(2):
                    ck = jnp.sum(jnp.where(lane == 2 * g + hh, cj, 0.0), axis=1, keepdims=True)
                    st = _dot_nt(_keep_lanes(halves[hh], kj), qi) + (cqs[hh] - ck)
                    if masked:
                        st = jnp.where(causal_t, st, -jnp.inf)
                    pt = jnp.exp(st - lses[hh])
                    new.append(accs[hh] + jnp.sum(pt * _dot_nt(vj, doms[hh]), axis=0, keepdims=True))
                return tuple(new)

            init = (jnp.zeros((1, TB), F32), jnp.zeros((1, TB), F32))
            accs = block(i, lax.fori_loop(0, i, lambda j, cr: block(j, cr, False), init), True)
            out = jnp.where(sub == 2 * g, accs[0], out)
            out = jnp.where(sub == 2 * g + 1, accs[1], out)
        dt_ref[...] = out

    row8 = pl.BlockSpec((8, TB), lambda i: (0, i))
    return _pcall(
        body, name="attn_bwd_dsum", grid=(t_len // TB,),
        in_specs=[_tokcol(DG, 0), _tokcol(DG, 1), pl.BlockSpec((t_len, DG), lambda i: (0, 1)),
                  pl.BlockSpec((t_len, DG), lambda i: (0, 2)), _full((t_len, LANES)), row8, row8],
        out_specs=[row8, _tok(DG)], out_shape=[_sds((8, t_len), F32), _sds((t_len, DG), BF16)],
        compiler_params=_cparams(1),
    )(qkv, dcat, qkv, qkv, c, ct, lset)


def _attn_bwd(qkv, dob, c, ct, lset, dt):
    t_len = qkv.shape[0]
    n_q = t_len // TB

    def body(q_ref, dob_ref, k_ref, v_ref, c_ref, ct_ref, lset_ref, dt_ref, dq_ref, dk_ref, dv_ref, dc_ref):
        j = pl.program_id(0)

        @pl.when(j == 0)
        def _():
            dq_ref[...] = jnp.zeros_like(dq_ref)

        lane, halves = _head_masks(TB)
        crow = c_ref[...]
        causal_t = lax.broadcasted_iota(jnp.int32, (TB, TB), 1) >= lax.broadcasted_iota(jnp.int32, (TB, TB), 0)
        dc_out = jnp.zeros((TB, LANES), F32)
        for g in range(2):
            cols = slice(g * LANES, (g + 1) * LANES)
            kg, vg = k_ref[:, cols], v_ref[:, cols]
            kms = [_keep_lanes(halves[hh], kg) for hh in range(2)]
            cks = [jnp.sum(jnp.where(lane == 2 * g + hh, crow, 0.0), axis=1, keepdims=True) for hh in range(2)]

            def block(i, carry, masked):
                dk, dv, dcs = carry
                off = pl.multiple_of(i * TB, TB)
                qi = q_ref[pl.ds(off, TB), cols]
                doi = dob_ref[pl.ds(off, TB), cols]
                dq_add = jnp.zeros((TB, LANES), F32)
                dcs_new = []
                for hh in range(2):
                    h = 2 * g + hh
                    dom = _keep_lanes(halves[hh], doi)
                    st = _dot_nt(kms[hh], qi) + (ct_ref[h:h + 1, pl.ds(off, TB)] - cks[hh])
                    if masked:
                        st = jnp.where(causal_t, st, -jnp.inf)
                    pt = jnp.exp(st - lset_ref[h:h + 1, pl.ds(off, TB)])
                    dv = dv + _dot(pt.astype(BF16), dom)
                    dst = pt * (_dot_nt(vg, dom) - dt_ref[h:h + 1, pl.ds(off, TB)])
                    dsb = dst.astype(BF16)
                    dk = dk + _dot(dsb, _keep_lanes(halves[hh], qi))
                    dcs_new.append(dcs[hh] + jnp.sum(dst, axis=1, keepdims=True))
                    dq_add = dq_add + _dot_tn(dsb, kms[hh])
                dq_ref[pl.ds(off, TB), cols] += dq_add
                return dk, dv, tuple(dcs_new)

            init = (jnp.zeros((TB, LANES), F32), jnp.zeros((TB, LANES), F32),
                    (jnp.zeros((TB, 1), F32), jnp.zeros((TB, 1), F32)))
            carry = block(j, init, True)
            dk, dv, dcs = lax.fori_loop(j + 1, n_q, lambda i, cr: block(i, cr, False), carry)
            dk_ref[:, cols] = dk
            dv_ref[:, cols] = dv
            dc_out = jnp.where(lane == 2 * g, -dcs[0], dc_out)
            dc_out = jnp.where(lane == 2 * g + 1, -dcs[1], dc_out)
        dc_ref[...] = dc_out

    return _pcall(
        body, name="attn_bwd", grid=(n_q,),
        in_specs=[pl.BlockSpec((t_len, DG), lambda i: (0, 0)), _full((t_len, DG)), _tokcol(DG, 1), _tokcol(DG, 2),
                  _tok(LANES), _full((8, t_len)), _full((8, t_len)), _full((8, t_len))],
        out_specs=[_full((t_len, DG)), _tok(DG), _tok(DG), _tok(LANES)],
        out_shape=[_sds((t_len, DG), F32), _sds((t_len, DG), F32), _sds((t_len, DG), F32), _sds((t_len, LANES), F32)],
        compiler_params=_cparams(1),
    )(qkv, dob, qkv, qkv, c, ct, lset, dt)


def _forget_bwd(dc, fl, b_row):
    t_len = dc.shape[0]
    n_t = t_len // TB
    rev = pl.BlockSpec((TB, LANES), lambda i: (n_t - 1 - i, 0))

    def body(dc_ref, fl_ref, b_ref, dfl_ref, db_ref, carry):
        @pl.when(pl.program_id(0) == 0)
        def _():
            carry[...] = jnp.zeros_like(carry)
            db_ref[...] = jnp.zeros_like(db_ref)

        r = lax.broadcasted_iota(jnp.int32, (TB, TB), 0)
        s = lax.broadcasted_iota(jnp.int32, (TB, TB), 1)
        dc = dc_ref[...]
        dl = _dot_exact((r <= s).astype(F32), dc) + carry[0:1, :]
        carry[...] += jnp.sum(dc, axis=0, keepdims=True)
        dfl = dl * _sigmoid(-(fl_ref[...] + b_ref[...]))
        dfl_ref[...] = dfl
        db_ref[...] += jnp.sum(dfl, axis=0, keepdims=True)

    return _pcall(
        body, name="forget_bwd", grid=(n_t,), in_specs=[rev, rev, _full((1, LANES))],
        out_specs=[rev, _full((1, LANES))], out_shape=[_sds((t_len, LANES), F32), _sds((1, LANES), F32)],
        scratch_shapes=[pltpu.VMEM((8, LANES), F32)], compiler_params=_cparams(1),
    )(dc, fl, b_row)


def _branch_bwd(zc, dcat, dq, dk, dv, w_dw, ln_g, ln_b, w_pw, w_sc, w_pool, pool_scale):
    t_len = zc.shape[0]
    n_t = t_len // TB
    e2 = HALO + TB + HALO
    e1 = TB + HALO

    def body(z_ref, zp_ref, zn_ref, dcf_ref, dcfn_ref, dsp_ref, dspn_ref, dq_ref, dk_ref, dv_ref,
             dw_ref, g_ref, b_ref, pw_ref, sc_ref, pool_ref, ps_ref,
             dz_ref, ddw_ref, dg_ref, db_ref, dpw_ref, dsc_ref, dpool_ref, dps_ref,
             u_s, dy_s, ch_s, dcv_s, p0, p1, p2, p3, g0, g1, g2, g3):
        i = pl.program_id(0)
        _acc_init([ddw_ref, dg_ref, db_ref, dpw_ref, dsc_ref, dpool_ref, dps_ref])
        hm = (i > 0).astype(F32)
        nm = (i < n_t - 1).astype(F32)

        sig_b = _sigmoid(z_ref[:, 256:512])
        a = z_ref[:, 0:256]
        u_s[0:HALO, :] = zp_ref[:, 0:256] * _sigmoid(zp_ref[:, 256:512]) * hm
        u_s[HALO:HALO + TB, :] = a * sig_b
        u_s[HALO + TB:e2, :] = zn_ref[:, 0:256] * _sigmoid(zn_ref[:, 256:512])
        y = jnp.zeros((e1, DG), F32)
        for k in range(CONF_K):
            y = y + dw_ref[k:k + 1, :] * u_s[pl.ds(HALO - (CONF_K - 1) + k, e1), :]
        n, r, yn = _layer_norm_parts(y, g_ref[...], b_ref[...])
        sg = _sigmoid(yn)
        dyc = jnp.concatenate([dcf_ref[...], dcfn_ref[...] * nm], axis=0)
        ds = _dot_nt(dyc.astype(BF16), pw_ref[...])
        dyn = ds * sg * (1.0 + yn * (1.0 - sg))
        dg_ref[...] += jnp.sum((dyn * n)[0:TB], axis=0, keepdims=True)
        db_ref[...] += jnp.sum(dyn[0:TB], axis=0, keepdims=True)
        dn = dyn * g_ref[...]
        dyv = r * (dn - jnp.mean(dn, axis=-1, keepdims=True) - n * jnp.mean(dn * n, axis=-1, keepdims=True))
        dpw_ref[...] += _dot_tn((yn * sg)[0:TB].astype(BF16), dcf_ref[...].astype(BF16))
        dy_s[...] = dyv
        du = jnp.zeros((TB, DG), F32)
        dyv_t = dyv[0:TB]
        for k in range(CONF_K):
            du = du + dw_ref[k:k + 1, :] * dy_s[pl.ds(CONF_K - 1 - k, TB), :]
            ddw_ref[k:k + 1, :] += jnp.sum(dyv_t * u_s[pl.ds(HALO - (CONF_K - 1) + k, TB), :], axis=0, keepdims=True)
        dz_ref[:, 0:256] = (du * sig_b).astype(BF16)
        dz_ref[:, 256:512] = (du * a * sig_b * (1.0 - sig_b)).astype(BF16)

        dz_ref[:, 512:768] = (dq_ref[...] * SCALE).astype(BF16)
        dz_ref[:, 768:1024] = dk_ref[...].astype(BF16)
        dz_ref[:, 1024:1280] = dv_ref[...].astype(BF16)

        sc_h, sc_b, sc_c = z_ref[:, 512:768], z_ref[:, 768:1024], z_ref[:, 1024:1280]
        ch_s[0:HALO, :] = zp_ref[:, 1024:1280] * zp_ref[:, 512:768] * hm
        ch_s[HALO:HALO + TB, :] = sc_c * sc_h
        ch_s[HALO + TB:e2, :] = zn_ref[:, 1024:1280] * zn_ref[:, 512:768]
        cv = jnp.zeros((TB, DG), F32)
        for k in range(SC_K):
            cv = cv + sc_ref[k:k + 1, :] * ch_s[pl.ds(HALO - (SC_K - 1) + k, TB), :]
        dy_sc = dsp_ref[:, 0:256]
        dcv_t = dy_sc * sc_b
        dcv_s[0:TB, :] = dcv_t
        dcv_s[TB:e1, :] = dspn_ref[:, 0:256] * nm * zn_ref[:, 768:1024]
        dch = jnp.zeros((TB, DG), F32)
        for k in range(SC_K):
            dch = dch + sc_ref[k:k + 1, :] * dcv_s[pl.ds(SC_K - 1 - k, TB), :]
            dsc_ref[k:k + 1, :] += jnp.sum(dcv_t * ch_s[pl.ds(HALO - (SC_K - 1) + k, TB), :], axis=0, keepdims=True)
        dz_ref[:, 1280:1536] = (dch * sc_c).astype(BF16)
        dz_ref[:, 1536:1792] = (dy_sc * cv).astype(BF16)
        dz_ref[:, 1792:2048] = (dch * sc_h).astype(BF16)

        v_t = z_ref[:, 1280:1536]
        p0[0:HALO, :] = zp_ref[:, 1280:1536] * hm
        p0[HALO:HALO + TB, :] = v_t
        s2, s4, s8, s16 = _pool_window_sums(p0, p1, p2, p3)
        cnt, lane = _pool_counts(i * TB, e1)
        dlt = (_lane_group_select(lane[0:TB], s2, s4, s8, s16) / cnt[0:TB] - v_t).astype(BF16)
        dyp_t = dsp_ref[:, 256:512]
        dps_ref[...] += jnp.sum(dyp_t * _dot(dlt, pool_ref[...]), axis=0, keepdims=True)
        dpre = (jnp.concatenate([dyp_t, dspn_ref[:, 256:512] * nm], axis=0) * ps_ref[...]).astype(BF16)
        dpool_ref[...] += _dot_tn(dlt, dpre[0:TB])
        dd = _dot_nt(dpre, pool_ref[...])
        g0[...] = dd / cnt
        g1[0:TB + 24, :] = g0[pl.ds(0, TB + 24), :] + g0[pl.ds(1, TB + 24), :]
        g2[0:TB + 16, :] = g1[pl.ds(0, TB + 16), :] + g1[pl.ds(2, TB + 16), :]
        g3[0:TB + 8, :] = g2[pl.ds(0, TB + 8), :] + g2[pl.ds(4, TB + 8), :]
        f16 = g3[pl.ds(0, TB), :] + g3[pl.ds(8, TB), :]
        fwd_sum = _lane_group_select(lane[0:TB], g1[pl.ds(0, TB), :], g2[pl.ds(0, TB), :], g3[pl.ds(0, TB), :], f16)
        dz_ref[:, 2048:2304] = (fwd_sum - dd[0:TB]).astype(BF16)

    vec = _full((1, DG))
    mat = _full((DG, DG))
    scr = ([pltpu.VMEM((e2, DG), F32), pltpu.VMEM((e1, DG), F32), pltpu.VMEM((e2, DG), F32), pltpu.VMEM((e1, DG), F32)]
           + [pltpu.VMEM((HALO + TB, DG), F32)] * 4 + [pltpu.VMEM((e1, DG), F32)] * 4)
    return _pcall(
        body, name="branch_bwd", grid=(n_t,),
        in_specs=[_tok(1536), _halo_prev(1536), _halo_next(1536, t_len),
                  _tokcol(DG, 0), _halo_next(DG, t_len, 0), _tokcol(512, 1), _halo_next(512, t_len, 1),
                  _tok(DG), _tok(DG), _tok(DG),
                  _full((32, DG)), vec, vec, mat, _full((8, DG)), mat, vec],
        out_specs=[_tok(W_MAIN), _full((32, DG)), vec, vec, mat, _full((8, DG)), mat, vec],
        out_shape=[_sds((t_len, W_MAIN), BF16), _sds((32, DG), F32), _sds((1, DG), F32), _sds((1, DG), F32),
                   _sds((DG, DG), F32), _sds((8, DG), F32), _sds((DG, DG), F32), _sds((1, DG), F32)],
        scratch_shapes=scr, compiler_params=_cparams(1),
    )(zc, zc, zc, dcat, dcat, dcat, dcat, dq, dk, dv, w_dw, ln_g, ln_b, w_pw, w_sc, w_pool, pool_scale)


def _mixin_bwd(dh, h, g, dz, dfl, win, wf):
    t_len = dh.shape[0]

    def body(dh_ref, h_ref, g_ref, dz_ref, dfl_ref, win_ref, wf_ref, dh0_ref, xn_ref, dg_ref):
        _acc_init([dg_ref])
        dxn = _dot_nt(dz_ref[...], win_ref[...].reshape(D, W_MAIN)) + _dot_nt(dfl_ref[...].astype(BF16), wf_ref[...])
        dx, dg, xn = _rms_bwd(h_ref[...], g_ref[...], dxn)
        dg_ref[...] += dg
        xn_ref[...] = xn.astype(BF16)
        dh0_ref[...] = dh_ref[...] + dx

    return _pcall(
        body, name="mixin_bwd", grid=(t_len // TB,),
        in_specs=[_tok(D), _tok(D), _full((1, D)), _tok(W_MAIN), _tok(LANES), _full((N_DEV, D // N_DEV, W_MAIN)),
                  _full((D, LANES))],
        out_specs=[_tok(D), _tok(D), _full((1, D))],
        out_shape=[_sds((t_len, D), F32), _sds((t_len, D), BF16), _sds((1, D), F32)],
        compiler_params=_cparams(1),
    )(dh, h, g, dz, dfl, win, wf)


def _matmul_tn(name, a, b, tm, tn, out_dtype, block_major=False):
    t_len, m = a.shape
    n = b.shape[1]
    tk = min(t_len, 1024)
    n_k = t_len // tk

    def body(a_ref, b_ref, o_ref, acc):
        k = pl.program_id(2)

        @pl.when(k == 0)
        def _():
            acc[...] = jnp.zeros_like(acc)

        acc[...] += _dot_tn(a_ref[...].astype(BF16), b_ref[...].astype(BF16))

        @pl.when(k == n_k - 1)
        def _():
            if block_major:
                for blk in range(tn // FF_BLK):
                    o_ref[blk] = acc[:, blk * FF_BLK:(blk + 1) * FF_BLK].astype(out_dtype)
            else:
                o_ref[...] = acc[...].astype(out_dtype)

    if block_major:
        out_spec = pl.BlockSpec((tn // FF_BLK, tm, FF_BLK), lambda i, j, k: (j, i, 0))
        out_shape = _sds((n // FF_BLK, m, FF_BLK), out_dtype)
    else:
        out_spec = pl.BlockSpec((tm, tn), lambda i, j, k: (i, j))
        out_shape = _sds((m, n), out_dtype)
    return _pcall(
        body, name=name, grid=(m // tm, n // tn, n_k),
        in_specs=[pl.BlockSpec((tk, tm), lambda i, j, k: (k, i)), pl.BlockSpec((tk, tn), lambda i, j, k: (k, j))],
        out_specs=out_spec, out_shape=out_shape, scratch_shapes=[pltpu.VMEM((tm, tn), F32)], compiler_params=_cparams(3),
    )(a, b)


_HBM = pl.BlockSpec(memory_space=pltpu.HBM)


def _mesh_place():
    return lax.axis_index("x"), lax.axis_index("y"), lax.axis_index("c")


def _allgather(name, srcs):
    n = len(srcs)

    def body(*refs):
        src, dst = refs[:n], refs[n:2 * n]
        send_sems, recv_sems, local_sems = refs[2 * n:]
        x, y, c = _mesh_place()
        me, sibling = (x, y, c), (x, y, 1 - c)
        chips = [(1 - x, y), (x, 1 - y), (1 - x, 1 - y)]

        def slot(px, py, pc):
            return 4 * px + 2 * py + pc

        def copy(t, k, block, to, from_src=False):
            return pltpu.make_async_remote_copy(
                src_ref=src[t] if from_src else dst[t].at[slot(*block)], dst_ref=dst[t].at[slot(*block)],
                send_sem=send_sems.at[t, k], recv_sem=recv_sems.at[t, k], device_id=to, device_id_type=MESH_ID)

        mine = [pltpu.make_async_copy(src[t], dst[t].at[slot(*me)], local_sems.at[t]) for t in range(n)]
        for cp in mine:
            cp.start()
        started = []
        for t in range(n):
            started.append(copy(t, 0, me, sibling, from_src=True))
            started += [copy(t, 1 + j, me, (*chip, c), from_src=True) for j, chip in enumerate(chips)]
        for cp in started:
            cp.start()
        for j, chip in enumerate(chips):
            for t in range(n):
                copy(t, 1 + j, (*chip, c), me).wait_recv()
                fwd = copy(t, 4 + j, (*chip, c), sibling)
                fwd.start()
                started.append(fwd)
        for t in range(n):
            copy(t, 0, sibling, me).wait_recv()
            for j, chip in enumerate(chips):
                copy(t, 4 + j, (*chip, 1 - c), me).wait_recv()
        for cp in started:
            cp.wait_send()
        for cp in mine:
            cp.wait()

    return _pcall(
        body, name=name, in_specs=[_HBM] * n, out_specs=[_HBM] * n,
        out_shape=[_sds((N_DEV,) + s.shape, s.dtype) for s in srcs],
        scratch_shapes=[pltpu.SemaphoreType.DMA((n, 7)), pltpu.SemaphoreType.DMA((n, 7)), pltpu.SemaphoreType.DMA((n,))],
    )(*srcs)


def _exchange(name, srcs, per_dest, groups):
    n = len(srcs)
    n_out = max(k for k, _ in groups) + 1
    out_shapes = []
    for k in range(n_out):
        members = [t for t in range(n) if groups[t][0] == k]
        blk = srcs[members[0]].shape[1:] if per_dest[members[0]] else srcs[members[0]].shape
        out_shapes.append(_sds((len(members), N_DEV) + tuple(blk), srcs[members[0]].dtype))

    def body(*refs):
        src, dst = refs[:n], refs[n:n + n_out]
        send_sems, recv_sems, local_sems = refs[n + n_out:]
        x, y, c = _mesh_place()
        me_slot = 4 * x + 2 * y + c
        peers = []
        for r in range(1, N_DEV):
            px, py, pc = x ^ ((r >> 2) & 1), y ^ ((r >> 1) & 1), c ^ (r & 1)
            peers.append(((px, py, pc), 4 * px + 2 * py + pc))

        def piece(t, dest_slot):
            return src[t].at[dest_slot] if per_dest[t] else src[t]

        def landing(t, sender_slot):
            k, g = groups[t]
            return dst[k].at[g, sender_slot]

        mine = [pltpu.make_async_copy(piece(t, me_slot), landing(t, me_slot), local_sems.at[t]) for t in range(n)]
        for cp in mine:
            cp.start()
        sends = []
        for r, (peer, peer_slot) in enumerate(peers):
            for t in range(n):
                sends.append(pltpu.make_async_remote_copy(
                    src_ref=piece(t, peer_slot), dst_ref=landing(t, me_slot), send_sem=send_sems.at[t, r],
                    recv_sem=recv_sems.at[t, r], device_id=peer, device_id_type=MESH_ID))
        for cp in sends:
            cp.start()
        for r, (peer, peer_slot) in enumerate(peers):
            for t in range(n):
                pltpu.make_async_remote_copy(
                    src_ref=piece(t, peer_slot), dst_ref=landing(t, peer_slot), send_sem=send_sems.at[t, r],
                    recv_sem=recv_sems.at[t, r], device_id=peer, device_id_type=MESH_ID).wait_recv()
        for cp in sends:
            cp.wait_send()
        for cp in mine:
            cp.wait()

    return _pcall(
        body, name=name, in_specs=[_HBM] * n, out_specs=[_HBM] * n_out, out_shape=out_shapes,
        scratch_shapes=[pltpu.SemaphoreType.DMA((n, 7)), pltpu.SemaphoreType.DMA((n, 7)), pltpu.SemaphoreType.DMA((n,))],
    )(*srcs)


def _adam_math(w, g, m, v):
    m = ADAM_B1 * m + (1.0 - ADAM_B1) * g
    v = ADAM_B2 * v + (1.0 - ADAM_B2) * (g * g)
    m_hat = m / (1.0 - ADAM_B1 ** ADAM_STEP)
    v_hat = v / (1.0 - ADAM_B2 ** ADAM_STEP)
    delta = -ADAM_LR * (m_hat / (jnp.sqrt(v_hat) + ADAM_EPS) + ADAM_WD * w)
    return delta, m, v


def _adam_rows(name, parts, w, m, v, row_tile, row_block_offset):
    n_l, rows, cols = w.shape

    def body(p_ref, w_ref, m_ref, v_ref, g_out, d_out, m_out, v_out):
        g = p_ref[0].astype(F32)
        for s in range(1, N_DEV):
            g = g + p_ref[s].astype(F32)
        delta, m_new, v_new = _adam_math(w_ref[...], g, m_ref[...], v_ref[...])
        g_out[...] = g
        d_out[...] = delta
        m_out[...] = m_new
        v_out[...] = v_new

    blk = pl.BlockSpec((None, row_tile, cols), lambda l, i: (l, i, 0))
    return _pcall(
        body, name=name, grid=(n_l, rows // row_tile),
        in_specs=[pl.BlockSpec((None, N_DEV, row_tile, cols), lambda l, i: (l, 0, row_block_offset + i, 0)), blk, blk, blk],
        out_specs=[blk] * 4, out_shape=[_sds(w.shape, F32)] * 4, compiler_params=_cparams(2),
    )(parts, w, m, v)


def _adam_packed(name, parts, w, m, v):
    def body(p_ref, w_ref, m_ref, v_ref, g_out, d_out, m_out, v_out):
        g = p_ref[0]
        for s in range(1, N_DEV):
            g = g + p_ref[s]
        delta, m_new, v_new = _adam_math(w_ref[...], g, m_ref[...], v_ref[...])
        g_out[...] = g
        d_out[...] = delta
        m_out[...] = m_new
        v_out[...] = v_new

    return _pcall(
        body, name=name, grid=(1,), in_specs=[_full(parts.shape), _full(w.shape), _full(w.shape), _full(w.shape)],
        out_specs=[_full(w.shape)] * 4, out_shape=[_sds(w.shape, F32)] * 4, compiler_params=_cparams(1),
    )(parts, w, m, v)


def _pack_rows(flat_parts, lead=()):
    flat = jnp.concatenate(flat_parts, axis=-1)
    n = flat.shape[-1]
    rows = -(-n // LANES)
    rows = -(-rows // 8) * 8
    flat = jnp.pad(flat, [(0, 0)] * len(lead) + [(0, rows * LANES - n)])
    return flat.reshape(lead + (rows, LANES))


def _unpack_rows(packed, shapes, lead=()):
    flat = packed.reshape(lead + (-1,))
    out, off = [], 0
    for shp in shapes:
        size = 1
        for s in shp:
            size *= s
        out.append(flat[..., off:off + size].reshape(lead + tuple(shp)))
        off += size
    return out


_SMALL_SHARD_SHAPES = [(N_LAYERS, 128, 4), (N_LAYERS, 32, DG), (N_LAYERS, D_PLE, 128), (N_LAYERS, CONF_K, 32), (N_LAYERS, SC_K, 32)]
_REP_SHAPES = [(N_LAYERS, D)] * 6 + [(N_LAYERS, DG)] * 3 + [(N_LAYERS, N_HEADS), (N_LAYERS, 4, 64, 64)]


def _small_full_to_shards(fcol, pw, proj, dw, sc):
    return [
        fcol.reshape(N_LAYERS, N_DEV, 128, 4).transpose(1, 0, 2, 3),
        pw.reshape(N_LAYERS, N_DEV, 32, DG).transpose(1, 0, 2, 3),
        proj.reshape(N_LAYERS, D_PLE, N_DEV, 128).transpose(2, 0, 1, 3),
        dw.reshape(N_LAYERS, CONF_K, N_DEV, 32).transpose(2, 0, 1, 3),
        sc.reshape(N_LAYERS, SC_K, N_DEV, 32).transpose(2, 0, 1, 3),
    ]


def _small_shards_to_full(fcol, pw, proj, dw, sc):
    return [
        fcol.transpose(1, 0, 2, 3).reshape(N_LAYERS, D, 4),
        pw.transpose(1, 0, 2, 3).reshape(N_LAYERS, DG, DG),
        proj.transpose(1, 2, 0, 3).reshape(N_LAYERS, D_PLE, D),
        dw.transpose(1, 2, 0, 3).reshape(N_LAYERS, CONF_K, DG),
        sc.transpose(1, 2, 0, 3).reshape(N_LAYERS, SC_K, DG),
    ]


def _pad_rows(a, rows):
    return jnp.pad(a, ((0, rows - a.shape[0]), (0, 0)))


def _block_diag4(w):
    z = jnp.zeros((64, 64), w.dtype)
    return jnp.concatenate([jnp.concatenate([w[g] if k == g else z for k in range(4)], axis=1) for g in range(4)], axis=0)


def kernel(x, p, g_mix_pre, w_in, b_forget, w_conf_dw, conf_ln_g, conf_ln_b, w_conf_pw, w_sc, w_pool, pool_scale, w_out, g_mix_post, g_mlp_pre, w_up, w_down, g_mlp_post, g_ple_pre, w_ple_gate, w_ple_proj, g_ple_post, loss_target, m_g_mix_pre, m_w_in, m_b_forget, m_w_conf_dw, m_conf_ln_g, m_conf_ln_b, m_w_conf_pw, m_w_sc, m_w_pool, m_pool_scale, m_w_out, m_g_mix_post, m_g_mlp_pre, m_w_up, m_w_down, m_g_mlp_post, m_g_ple_pre, m_w_ple_gate, m_w_ple_proj, m_g_ple_post, v_g_mix_pre, v_w_in, v_b_forget, v_w_conf_dw, v_conf_ln_g, v_conf_ln_b, v_w_conf_pw, v_w_sc, v_w_pool, v_pool_scale, v_w_out, v_g_mix_post, v_g_mlp_pre, v_w_up, v_w_down, v_g_mlp_post, v_g_ple_pre, v_w_ple_gate, v_w_ple_proj, v_g_ple_post):
    n_l = N_LAYERS
    t_len = x.shape[1]
    assert t_len % TB == 0 and x.shape[0] == 1 and x.shape[2] == D

    def main_cols(a):
        return jnp.concatenate([a[..., :F_LO], a[..., F_HI:]], axis=-1)

    def fcols(a):
        return a[..., F_LO:F_HI]

    def rows_pack(down, out, gate):
        return jnp.concatenate([down, out, gate], axis=1)

    rows_b = rows_pack(w_down, w_out, w_ple_gate).astype(BF16)
    win_b = main_cols(w_in).astype(BF16)
    wup_b = w_up.astype(BF16)
    small_local = _pack_rows([a.reshape(-1) for a in (fcols(w_in), w_conf_pw, w_ple_proj, w_conf_dw, w_sc)])
    srcs = [rows_b[l] for l in range(n_l)] + [win_b[l] for l in range(n_l)] + [wup_b[l] for l in range(n_l)] + [small_local]
    gathered = _allgather("weight_allgather", srcs)
    rows_g, win_g, wup_g = gathered[0:n_l], gathered[n_l:2 * n_l], gathered[2 * n_l:3 * n_l]
    small_g = _unpack_rows(gathered[3 * n_l], _SMALL_SHARD_SHAPES, lead=(N_DEV,))
    fcol_f, pw_f, proj_f, dw_f, sc_f = _small_shards_to_full(*small_g)
    wf_b = jnp.pad(fcol_f, ((0, 0), (0, 0), (0, LANES - 4))).astype(BF16)
    pw_b, proj_b = pw_f.astype(BF16), proj_f.astype(BF16)
    dw_pad = jnp.pad(dw_f, ((0, 0), (0, 32 - CONF_K), (0, 0)))
    sc_pad = jnp.pad(sc_f, ((0, 0), (0, 8 - SC_K), (0, 0)))
    pool_bd = jnp.stack([_block_diag4(w_pool[l]) for l in range(n_l)]).astype(BF16)
    b_row = jnp.pad(b_forget, ((0, 0), (0, LANES - N_HEADS)))[:, None, :]

    def vec(a, l):
        return a[l][None, :]

    h = x[0]
    saved = []
    for l in range(n_l):
        zc, qkv, fl = _mixin_fwd(h, vec(g_mix_pre, l), win_g[l], wf_b[l])
        c, ct = _cumsum_fwd(fl, b_row[l])
        cat3 = _branch_fwd(zc, dw_pad[l], vec(conf_ln_g, l), vec(conf_ln_b, l), pw_b[l], sc_pad[l], pool_bd[l], vec(pool_scale, l))
        o, lset = _attn_fwd(qkv, c, ct)
        mix, h1 = _mixout_fwd(h, cat3, o, rows_g[l], vec(g_mix_post, l))
        u, ff, h2 = _mlp_fwd(h1, vec(g_mlp_pre, l), wup_g[l], rows_g[l], vec(g_mlp_post, l))
        pp, gate, h3 = _ple_fwd(h2, p[l, 0], vec(g_ple_pre, l), rows_g[l], proj_b[l], vec(g_ple_post, l))
        saved.append(dict(h0=h, zc=zc, qkv=qkv, fl=fl, c=c, ct=ct, cat3=cat3, o=o, lset=lset, mix=mix, h1=h1, u=u, ff=ff,
                          h2=h2, pp=pp, gate=gate))
        h = h3

    dh, loss_part = _loss_bwd(h, loss_target[0])
    loss = lax.psum(loss_part[0, 0], ("x", "y", "c"))

    d_rows, d_win, d_wup = [None] * n_l, [None] * n_l, [None] * n_l
    small_grads = {k: [None] * n_l for k in ("fcol", "pw", "proj", "dw", "sc")}
    rep_grads = {k: [None] * n_l for k in ("g_mix_pre", "g_mix_post", "g_mlp_pre", "g_mlp_post", "g_ple_pre", "g_ple_post",
                                           "ln_g", "ln_b", "pool_scale", "b_forget", "w_pool")}
    for l in reversed(range(n_l)):
        s = saved[l]
        dh, dpp_b, dpre_b, hn3_b, dg_ple_post, dg_ple_pre = _ple_bwd(
            dh, s["h2"], s["pp"], s["gate"], vec(g_ple_post, l), vec(g_ple_pre, l), rows_g[l])
        small_grads["proj"][l] = _matmul_tn("wgrad_proj", p[l, 0], dpp_b, D_PLE, D, F32)
        d_gate = _matmul_tn("wgrad_gate", hn3_b, dpre_b, 512, D, BF16)
        dh, a2_b, du_b, dff_b, hn2_b, dg_mlp_post, dg_mlp_pre = _mlp_bwd(
            dh, s["h1"], s["u"], s["ff"], vec(g_mlp_post, l), vec(g_mlp_pre, l), wup_g[l], rows_g[l])
        d_down = _matmul_tn("wgrad_down", a2_b, dff_b, 512, D, BF16)
        d_wup[l] = _matmul_tn("wgrad_up", hn2_b, du_b, 512, 2 * FF_BLK, BF16, block_major=True)
        dcat, dmix_b, cat_b, dg_mix_post = _mixout_bwd(dh, s["mix"], s["cat3"], s["o"], vec(g_mix_post, l), rows_g[l])
        d_out = _matmul_tn("wgrad_out", cat_b, dmix_b, 512, D, BF16)
        dt, dob = _attn_bwd_dsum(s["qkv"], dcat, s["c"], s["ct"], s["lset"])
        dq, dk, dv, dc = _attn_bwd(s["qkv"], dob, s["c"], s["ct"], s["lset"], dt)
        dfl, db_f = _forget_bwd(dc, s["fl"], b_row[l])
        dz_b, ddw, dln_g, dln_b, dpw, dsc, dpool, dps = _branch_bwd(
            s["zc"], dcat, dq, dk, dv, dw_pad[l], vec(conf_ln_g, l), vec(conf_ln_b, l), pw_b[l], sc_pad[l], pool_bd[l],
            vec(pool_scale, l))
        dh, xn_b, dg_mix_pre = _mixin_bwd(dh, s["h0"], vec(g_mix_pre, l), dz_b, dfl, win_g[l], wf_b[l])
        d_win[l] = _matmul_tn("wgrad_in", xn_b, dz_b, 512, W_MAIN // 2, BF16).reshape(N_DEV, 128, W_MAIN)
        small_grads["fcol"][l] = _matmul_tn("wgrad_fcol", xn_b, dfl, 512, LANES, F32)[:, 0:4]
        d_rows[l] = jnp.concatenate([d_down.reshape(N_DEV, FF_BLK, D), d_out.reshape(N_DEV, 128, D),
                                     d_gate.reshape(N_DEV, 128, D)], axis=1)
        small_grads["pw"][l], small_grads["dw"][l], small_grads["sc"][l] = dpw, ddw[0:CONF_K], dsc[0:SC_K]
        rep_grads["g_mix_pre"][l], rep_grads["g_mix_post"][l] = dg_mix_pre[0], dg_mix_post[0]
        rep_grads["g_mlp_pre"][l], rep_grads["g_mlp_post"][l] = dg_mlp_pre[0], dg_mlp_post[0]
        rep_grads["g_ple_pre"][l], rep_grads["g_ple_post"][l] = dg_ple_pre[0], dg_ple_post[0]
        rep_grads["ln_g"][l], rep_grads["ln_b"][l], rep_grads["pool_scale"][l] = dln_g[0], dln_b[0], dps[0]
        rep_grads["b_forget"][l] = db_f[0, 0:N_HEADS]
        rep_grads["w_pool"][l] = jnp.stack([dpool[64 * g:64 * g + 64, 64 * g:64 * g + 64] for g in range(4)])
    grad_x = dh[None]

    small_part = _pack_rows(
        [a.reshape(N_DEV, -1) for a in _small_full_to_shards(*[jnp.stack(small_grads[k]) for k in ("fcol", "pw", "proj", "dw", "sc")])],
        lead=(N_DEV,))
    rep_order = ("g_mix_pre", "g_mix_post", "g_mlp_pre", "g_mlp_post", "g_ple_pre", "g_ple_post", "ln_g", "ln_b",
                 "pool_scale", "b_forget", "w_pool")
    rep_part = _pack_rows([jnp.stack(rep_grads[k]).reshape(-1) for k in rep_order])
    ex_srcs = d_rows + d_win + d_wup + [small_part, rep_part]
    ex_per_dest = [True] * (3 * n_l + 1) + [False]
    ex_groups = [(0, l) for l in range(n_l)] + [(1, l) for l in range(n_l)] + [(2, l) for l in range(n_l)] + [(3, 0), (4, 0)]
    r_rows, r_win, r_wup, r_small, r_rep = _exchange("grad_exchange", ex_srcs, ex_per_dest, ex_groups)

    res = {}
    res["w_down"] = _adam_rows("adam_down", r_rows, w_down, m_w_down, v_w_down, 128, 0)
    res["w_out"] = _adam_rows("adam_out", r_rows, w_out, m_w_out, v_w_out, 128, 4)
    res["w_ple_gate"] = _adam_rows("adam_gate", r_rows, w_ple_gate, m_w_ple_gate, v_w_ple_gate, 128, 5)
    res["w_up"] = _adam_rows("adam_up", r_wup, w_up, m_w_up, v_w_up, 256, 0)
    win_main = _adam_rows("adam_in", r_win, main_cols(w_in), main_cols(m_w_in), main_cols(v_w_in), 128, 0)

    small_w = [(fcols(w_in), w_conf_pw, w_ple_proj, w_conf_dw, w_sc), (fcols(m_w_in), m_w_conf_pw, m_w_ple_proj, m_w_conf_dw, m_w_sc),
               (fcols(v_w_in), v_w_conf_pw, v_w_ple_proj, v_w_conf_dw, v_w_sc)]
    small_packed = [_pack_rows([a.reshape(-1) for a in grp]) for grp in small_w]
    small_res = [_unpack_rows(a, _SMALL_SHARD_SHAPES) for a in _adam_packed("adam_small", r_small[0], *small_packed)]
    rep_w = [(g_mix_pre, g_mix_post, g_mlp_pre, g_mlp_post, g_ple_pre, g_ple_post, conf_ln_g, conf_ln_b, pool_scale, b_forget, w_pool),
             (m_g_mix_pre, m_g_mix_post, m_g_mlp_pre, m_g_mlp_post, m_g_ple_pre, m_g_ple_post, m_conf_ln_g, m_conf_ln_b, m_pool_scale,
              m_b_forget, m_w_pool),
             (v_g_mix_pre, v_g_mix_post, v_g_mlp_pre, v_g_mlp_post, v_g_ple_pre, v_g_ple_post, v_conf_ln_g, v_conf_ln_b, v_pool_scale,
              v_b_forget, v_w_pool)]
    rep_packed = [_pack_rows([a.reshape(-1) for a in grp]) for grp in rep_w]
    rep_res = [_unpack_rows(a, _REP_SHAPES) for a in _adam_packed("adam_replicated", r_rep[0], *rep_packed)]

    for kind in range(4):
        fc, pw, proj, dwc, scc = small_res[kind]
        main = win_main[kind]
        (rg_mix_pre, rg_mix_post, rg_mlp_pre, rg_mlp_post, rg_ple_pre, rg_ple_post, r_ln_g, r_ln_b, r_ps, r_bf, r_wpool) = rep_res[kind]
        res.setdefault("by_kind", []).append(dict(
            g_mix_pre=rg_mix_pre, w_in=jnp.concatenate([main[..., :F_LO], fc, main[..., F_LO:]], axis=-1), b_forget=r_bf,
            w_conf_dw=dwc, conf_ln_g=r_ln_g, conf_ln_b=r_ln_b, w_conf_pw=pw, w_sc=scc, w_pool=r_wpool, pool_scale=r_ps,
            w_out=res["w_out"][kind], g_mix_post=rg_mix_post, g_mlp_pre=rg_mlp_pre, w_up=res["w_up"][kind],
            w_down=res["w_down"][kind], g_mlp_post=rg_mlp_post, g_ple_pre=rg_ple_pre, w_ple_gate=res["w_ple_gate"][kind],
            w_ple_proj=proj, g_ple_post=rg_ple_post))
    names = ("g_mix_pre", "w_in", "b_forget", "w_conf_dw", "conf_ln_g", "conf_ln_b", "w_conf_pw", "w_sc", "w_pool", "pool_scale",
             "w_out", "g_mix_post", "g_mlp_pre", "w_up", "w_down", "g_mlp_post", "g_ple_pre", "w_ple_gate", "w_ple_proj", "g_ple_post")
    outs = [loss, grad_x]
    for kind in range(4):
        outs += [res["by_kind"][kind][nm] for nm in names]
    return tuple(outs)
```

```python
import jax
import jax.numpy as jnp
from jax import lax
from jax.experimental import pallas as pl
from jax.experimental.pallas import tpu as pltpu

F32, BF16 = jnp.float32, jnp.bfloat16

D = 1024
DG = 256
N_HEADS = 4
HEAD_DIM = 64
CONF_K = 31
SC_K = 3
POOL_WINDOWS = (2, 4, 8, 16)
D_FF = 4096
D_PLE = 256
N_LAYERS = 4
N_DEV = 8
EPS = 1e-6
SCALE = HEAD_DIM ** -0.5
W_MAIN = 2304
F_LO, F_HI = 1280, 1284

ADAM_LR, ADAM_B1, ADAM_B2, ADAM_EPS, ADAM_WD, ADAM_STEP = 0.001, 0.9, 0.999, 1e-08, 0.01, 10

TB = 512
HALO = 32
LANES = 128
FF_BLK = D_FF // N_DEV
VMEM_LIMIT = 56 * 1024 * 1024

NT_DIMS = (((1,), (1,)), ((), ()))
TN_DIMS = (((0,), (0,)), ((), ()))
MESH_ID = pl.DeviceIdType.MESH


def _pcall(body, **kw):
    return pl.pallas_call(body, **kw)


def _cparams(n_axes):
    return pltpu.CompilerParams(dimension_semantics=("arbitrary",) * n_axes, vmem_limit_bytes=VMEM_LIMIT)


def _sds(shape, dtype):
    return jax.ShapeDtypeStruct(shape, dtype)


def _tok(width, tb=TB):
    return pl.BlockSpec((tb, width), lambda i: (i, 0))


def _tokcol(width, col):
    return pl.BlockSpec((TB, width), lambda i: (i, col))


def _full(shape):
    zeros = (0,) * len(shape)
    return pl.BlockSpec(shape, lambda *_: zeros)


def _halo_prev(width, col=0):
    return pl.BlockSpec((HALO, width), lambda i: (jnp.maximum(i * (TB // HALO) - 1, 0), col))


def _halo_next(width, n_rows, col=0):
    last = n_rows // HALO - 1
    return pl.BlockSpec((HALO, width), lambda i: (jnp.minimum((i + 1) * (TB // HALO), last), col))


def _dot(a, b):
    return jnp.dot(a, b, preferred_element_type=F32)


def _dot_nt(a, b):
    return lax.dot_general(a, b, NT_DIMS, preferred_element_type=F32)


def _dot_tn(a, b):
    return lax.dot_general(a, b, TN_DIMS, preferred_element_type=F32)


def _dot_exact(a, b):
    return jnp.dot(a, b, precision=lax.Precision.HIGHEST, preferred_element_type=F32)


def _rms(x, g):
    r = lax.rsqrt(jnp.mean(x * x, axis=-1, keepdims=True) + EPS)
    return x * r * g


def _rms_bwd(x, g, dy):
    r = lax.rsqrt(jnp.mean(x * x, axis=-1, keepdims=True) + EPS)
    n = x * r
    dg = jnp.sum(dy * n, axis=0, keepdims=True)
    dn = dy * g
    dx = r * (dn - n * jnp.mean(dn * n, axis=-1, keepdims=True))
    return dx, dg, n * g


def _sigmoid(x):
    return jax.nn.sigmoid(x)


def _log_sigmoid(x):
    return jnp.minimum(x, 0.0) - jnp.log(1.0 + jnp.exp(-jnp.abs(x)))


def _lane_group_select(lane, v2, v4, v8, v16):
    return jnp.where(lane < 64, v2, jnp.where(lane < 128, v4, jnp.where(lane < 192, v8, v16)))


def _pool_counts(t0, rows):
    lane = lax.broadcasted_iota(jnp.int32, (rows, DG), 1)
    t = lax.broadcasted_iota(jnp.int32, (rows, DG), 0) + t0
    win = _lane_group_select(lane, 2, 4, 8, 16)
    return jnp.minimum(t + 1, win).astype(F32), lane


def _mixin_fwd(h, g, win, wf):
    t_len = h.shape[0]

    def body(h_ref, g_ref, win_ref, wf_ref, zc_ref, qkv_ref, fl_ref):
        xn = _rms(h_ref[...], g_ref[...]).astype(BF16)
        z = _dot(xn, win_ref[...].reshape(D, W_MAIN))
        zc_ref[:, 0:512] = z[:, 0:512]
        zc_ref[:, 512:1536] = z[:, 1280:2304]
        qkv_ref[:, 0:256] = (z[:, 512:768] * SCALE).astype(BF16)
        qkv_ref[:, 256:768] = z[:, 768:1280].astype(BF16)
        fl_ref[...] = _dot(xn, wf_ref[...])

    return _pcall(
        body, name="mixin_fwd", grid=(t_len // TB,),
        in_specs=[_tok(D), _full((1, D)), _full((N_DEV, D // N_DEV, W_MAIN)), _full((D, LANES))],
        out_specs=[_tok(1536), _tok(768), _tok(LANES)],
        out_shape=[_sds((t_len, 1536), F32), _sds((t_len, 768), BF16), _sds((t_len, LANES), F32)],
        compiler_params=_cparams(1),
    )(h, g, win, wf)


def _transpose_lanes8(x):
    eye = (lax.broadcasted_iota(jnp.int32, (8, LANES), 0) == lax.broadcasted_iota(jnp.int32, (8, LANES), 1)).astype(F32)
    return lax.dot_general(eye, x, NT_DIMS, precision=lax.Precision.HIGHEST, preferred_element_type=F32)


def _cumsum_fwd(fl, b_row):
    t_len = fl.shape[0]

    def body(fl_ref, b_ref, c_ref, ct_ref, carry):
        @pl.when(pl.program_id(0) == 0)
        def _():
            carry[...] = jnp.zeros_like(carry)

        r = lax.broadcasted_iota(jnp.int32, (TB, TB), 0)
        s = lax.broadcasted_iota(jnp.int32, (TB, TB), 1)
        lf = _log_sigmoid(fl_ref[...] + b_ref[...])
        c = _dot_exact((r >= s).astype(F32), lf) + carry[0:1, :]
        c_ref[...] = c
        ct_ref[...] = _transpose_lanes8(c)
        carry[...] += jnp.sum(lf, axis=0, keepdims=True)

    return _pcall(
        body, name="cumsum_fwd", grid=(t_len // TB,),
        in_specs=[_tok(LANES), _full((1, LANES))],
        out_specs=[_tok(LANES), pl.BlockSpec((8, TB), lambda i: (0, i))],
        out_shape=[_sds((t_len, LANES), F32), _sds((8, t_len), F32)],
        scratch_shapes=[pltpu.VMEM((8, LANES), F32)],
        compiler_params=_cparams(1),
    )(fl, b_row)


def _layer_norm_parts(y, g, b):
    mu = jnp.mean(y, axis=-1, keepdims=True)
    yc = y - mu
    r = lax.rsqrt(jnp.mean(yc * yc, axis=-1, keepdims=True) + EPS)
    n = yc * r
    return n, r, n * g + b


def _pool_window_sums(p0, p1, p2, p3):
    e = HALO + TB
    p1[8:e, :] = p0[pl.ds(8, e - 8), :] + p0[pl.ds(7, e - 8), :]
    p2[16:e, :] = p1[pl.ds(16, e - 16), :] + p1[pl.ds(14, e - 16), :]
    p3[24:e, :] = p2[pl.ds(24, e - 24), :] + p2[pl.ds(20, e - 24), :]
    s16 = p3[pl.ds(HALO, TB), :] + p3[pl.ds(HALO - 8, TB), :]
    return p1[pl.ds(HALO, TB), :], p2[pl.ds(HALO, TB), :], p3[pl.ds(HALO, TB), :], s16


def _branch_fwd(zc, w_dw, ln_g, ln_b, w_pw, w_sc, w_pool, pool_scale):
    t_len = zc.shape[0]
    e = HALO + TB

    def body(z_ref, zh_ref, dw_ref, g_ref, b_ref, pw_ref, sc_ref, pool_ref, ps_ref, cat_ref, u_s, ch_s, p0, p1, p2, p3):
        i = pl.program_id(0)
        hm = (i > 0).astype(F32)
        u_s[0:HALO, :] = zh_ref[:, 0:256] * _sigmoid(zh_ref[:, 256:512]) * hm
        u_s[HALO:e, :] = z_ref[:, 0:256] * _sigmoid(z_ref[:, 256:512])
        y = jnp.zeros((TB, DG), F32)
        for k in range(CONF_K):
            y = y + dw_ref[k:k + 1, :] * u_s[pl.ds(HALO - (CONF_K - 1) + k, TB), :]
        _, _, yn = _layer_norm_parts(y, g_ref[...], b_ref[...])
        s = yn * _sigmoid(yn)
        cat_ref[:, 0:256] = _dot(s.astype(BF16), pw_ref[...])
        ch_s[0:HALO, :] = zh_ref[:, 1024:1280] * zh_ref[:, 512:768] * hm
        ch_s[HALO:e, :] = z_ref[:, 1024:1280] * z_ref[:, 512:768]
        cv = jnp.zeros((TB, DG), F32)
        for k in range(SC_K):
            cv = cv + sc_ref[k:k + 1, :] * ch_s[pl.ds(HALO - (SC_K - 1) + k, TB), :]
        cat_ref[:, 256:512] = z_ref[:, 768:1024] * cv
        p0[0:HALO, :] = zh_ref[:, 1280:1536] * hm
        p0[HALO:e, :] = z_ref[:, 1280:1536]
        s2, s4, s8, s16 = _pool_window_sums(p0, p1, p2, p3)
        cnt, lane = _pool_counts(i * TB, TB)
        dlt = _lane_group_select(lane, s2, s4, s8, s16) / cnt - z_ref[:, 1280:1536]
        cat_ref[:, 512:768] = _dot(dlt.astype(BF16), pool_ref[...]) * ps_ref[...]

    scr = [pltpu.VMEM((e, DG), F32) for _ in range(6)]
    return _pcall(
        body, name="branch_fwd", grid=(t_len // TB,),
        in_specs=[_tok(1536), _halo_prev(1536), _full((32, DG)), _full((1, DG)), _full((1, DG)), _full((DG, DG)),
                  _full((8, DG)), _full((DG, DG)), _full((1, DG))],
        out_specs=_tok(768), out_shape=_sds((t_len, 768), F32), scratch_shapes=scr, compiler_params=_cparams(1),
    )(zc, zc, w_dw, ln_g, ln_b, w_pw, w_sc, w_pool, pool_scale)


def _head_masks(rows):
    lane = lax.broadcasted_iota(jnp.int32, (rows, LANES), 1)
    return lane, (lane < HEAD_DIM, lane >= HEAD_DIM)


def _keep_lanes(mask, x):
    return jnp.where(mask, x.astype(F32), 0.0).astype(BF16)


def _with_exchange(refs, n_in, n_out, n_x, per_dest, first, last):
    ins, x_src = refs[:n_in], refs[n_in:n_in + n_x]
    outs, x_dst = refs[n_in + n_x:n_in + n_x + n_out], refs[n_in + n_x + n_out:n_in + 2 * n_x + n_out]
    begin = finish = None
    if n_x:
        start, wait = _exchange_ops(x_src, x_dst, [per_dest] * n_x, *refs[n_in + 2 * n_x + n_out:])

        def begin():
            pl.when(first)(start)

        def finish():
            pl.when(last)(wait)

    return ins, outs, begin, finish


def _attn_fwd(qkv, c, ct, bcast=()):
    t_len = qkv.shape[0]
    n_t = t_len // TB
    n_x = len(bcast)

    def body(*refs):
        i = pl.program_id(0)
        (q_ref, k_ref, v_ref, c_ref, ct_ref), (o_ref, lset_ref), begin, finish = _with_exchange(
            refs, 5, 2, n_x, False, i == 0, i == n_t - 1)
        if begin:
            begin()
        lane, halves = _head_masks(TB)
        crow = c_ref[...]
        causal = lax.broadcasted_iota(jnp.int32, (TB, TB), 0) >= lax.broadcasted_iota(jnp.int32, (TB, TB), 1)
        lse_out = jnp.zeros((TB, LANES), F32)
        for g in range(2):
            cols = slice(g * LANES, (g + 1) * LANES)
            qg = q_ref[:, cols]
            qms = [_keep_lanes(halves[hh], qg) for hh in range(2)]
            cqs = [jnp.sum(jnp.where(lane == 2 * g + hh, crow, 0.0), axis=1, keepdims=True) for hh in range(2)]

            def block(j, carry, masked):
                off = pl.multiple_of(j * TB, TB)
                kj = k_ref[pl.ds(off, TB), cols]
                vj = v_ref[pl.ds(off, TB), cols]
                new = []
                for hh in range(2):
                    m, l, acc = carry[hh]
                    s = _dot_nt(qms[hh], kj) + (cqs[hh] - ct_ref[2 * g + hh:2 * g + hh + 1, pl.ds(off, TB)])
                    if masked:
                        s = jnp.where(causal, s, -jnp.inf)
                    m_new = jnp.maximum(m, jnp.max(s, axis=1, keepdims=True))
                    alpha = jnp.exp(m - m_new)
                    p = jnp.exp(s - m_new)
                    l = alpha * l + jnp.sum(p, axis=1, keepdims=True)
                    acc = alpha * acc + _dot(p.astype(BF16), vj)
                    new.append((m_new, l, acc))
                return tuple(new)

            init = tuple((jnp.full((TB, 1), -jnp.inf, F32), jnp.zeros((TB, 1), F32), jnp.zeros((TB, LANES), F32))
                         for _ in range(2))
            carry = lax.fori_loop(0, i, lambda j, cr: block(j, cr, False), init)
            (m0, l0, acc0), (m1, l1, acc1) = block(i, carry, True)
            o_ref[:, cols] = jnp.where(halves[0], acc0 / l0, acc1 / l1)
            lse_out = jnp.where(lane == 2 * g, m0 + jnp.log(l0), lse_out)
            lse_out = jnp.where(lane == 2 * g + 1, m1 + jnp.log(l1), lse_out)
        lset_ref[...] = _transpose_lanes8(lse_out)
        if finish:
            finish()

    return _pcall(
        body, name="attn_fwd_gather" if n_x else "attn_fwd", grid=(n_t,),
        in_specs=[_tokcol(DG, 0), pl.BlockSpec((t_len, DG), lambda i: (0, 1)), pl.BlockSpec((t_len, DG), lambda i: (0, 2)),
                  _tok(LANES), _full((8, t_len))] + [_HBM] * n_x,
        out_specs=[_tok(DG), pl.BlockSpec((8, TB), lambda i: (0, i))] + [_HBM] * n_x,
        out_shape=[_sds((t_len, DG), F32), _sds((8, t_len), F32)] + _exchange_shapes(bcast, [False] * n_x),
        scratch_shapes=_exchange_scratch(n_x) if n_x else [],
        compiler_params=_cparams(1),
    )(qkv, qkv, qkv, c, ct, *bcast)


def _mixout_fwd(h, cat3, o, w_rows, g):
    t_len = h.shape[0]

    def body(h_ref, cat_ref, o_ref, w_ref, g_ref, mix_ref, h1_ref):
        w = w_ref[...].reshape(D, D)
        mix = (_dot(cat_ref[:, 0:256].astype(BF16), w[0:256]) + _dot(o_ref[...].astype(BF16), w[256:512])
               + _dot(cat_ref[:, 256:768].astype(BF16), w[512:1024]))
        mix_ref[...] = mix
        h1_ref[...] = h_ref[...] + _rms(mix, g_ref[...])

    return _pcall(
        body, name="mixout_fwd", grid=(t_len // TB,),
        in_specs=[_tok(D), _tok(768), _tok(DG), pl.BlockSpec((N_DEV, 128, D), lambda i: (0, 4, 0)), _full((1, D))],
        out_specs=[_tok(D), _tok(D)], out_shape=[_sds((t_len, D), F32)] * 2, compiler_params=_cparams(1),
    )(h, cat3, o, w_rows, g)


def _mlp_fwd(h, g_pre, w_up, w_rows, g_post):
    t_len = h.shape[0]

    def body(h_ref, g1_ref, up_ref, dn_ref, g2_ref, u_ref, ff_ref, h2_ref, hn_s, acc_s):
        j = pl.program_id(1)

        @pl.when(j == 0)
        def _():
            hn_s[...] = _rms(h_ref[...], g1_ref[...]).astype(BF16)
            acc_s[...] = jnp.zeros_like(acc_s)

        u = _dot(hn_s[...], up_ref[...])
        u_ref[...] = u
        r = jnp.maximum(u, 0.0)
        acc_s[...] += _dot((r * r).astype(BF16), dn_ref[...])

        @pl.when(j == N_DEV - 1)
        def _():
            ff = acc_s[...]
            ff_ref[...] = ff
            h2_ref[...] = h_ref[...] + _rms(ff, g2_ref[...])

    tok2 = pl.BlockSpec((TB, D), lambda i, j: (i, 0))
    vec2 = pl.BlockSpec((1, D), lambda i, j: (0, 0))
    return _pcall(
        body, name="mlp_fwd", grid=(t_len // TB, N_DEV),
        in_specs=[tok2, vec2, pl.BlockSpec((None, D, FF_BLK), lambda i, j: (j, 0, 0)),
                  pl.BlockSpec((None, FF_BLK, D), lambda i, j: (j, 0, 0)), vec2],
        out_specs=[pl.BlockSpec((TB, FF_BLK), lambda i, j: (i, j)), tok2, tok2],
        out_shape=[_sds((t_len, D_FF), F32), _sds((t_len, D), F32), _sds((t_len, D), F32)],
        scratch_shapes=[pltpu.VMEM((TB, D), BF16), pltpu.VMEM((TB, D), F32)], compiler_params=_cparams(2),
    )(h, g_pre, w_up, w_rows, g_post)


def _ple_fwd(h, p, g_pre, w_rows, w_proj, g_post):
    t_len = h.shape[0]

    def body(h_ref, p_ref, g1_ref, wg_ref, wp_ref, g2_ref, pp_ref, gate_ref, h3_ref):
        hn = _rms(h_ref[...], g1_ref[...]).astype(BF16)
        gate = _sigmoid(_dot(hn, wg_ref[...].reshape(D, D)))
        pp = _dot(p_ref[...].astype(BF16), wp_ref[...])
        pp_ref[...] = pp
        gate_ref[...] = gate
        h3_ref[...] = h_ref[...] + _rms(pp * gate, g2_ref[...])

    return _pcall(
        body, name="ple_fwd", grid=(t_len // TB,),
        in_specs=[_tok(D), _tok(D_PLE), _full((1, D)), pl.BlockSpec((N_DEV, 128, D), lambda i: (0, 5, 0)),
                  _full((D_PLE, D)), _full((1, D))],
        out_specs=[_tok(D)] * 3, out_shape=[_sds((t_len, D), F32)] * 3, compiler_params=_cparams(1),
    )(h, p, g_pre, w_rows, w_proj, g_post)


def _loss_bwd(h, target):
    t_len = h.shape[0]

    def body(h_ref, t_ref, dh_ref, loss_ref):
        @pl.when(pl.program_id(0) == 0)
        def _():
            loss_ref[...] = jnp.zeros_like(loss_ref)

        d = h_ref[...] - t_ref[...]
        dh_ref[...] = d * (1.0 / D)
        loss_ref[...] += 0.5 * jnp.sum(jnp.mean(d * d, axis=-1, keepdims=True), axis=0, keepdims=True)

    return _pcall(
        body, name="loss_bwd", grid=(t_len // TB,), in_specs=[_tok(D), _tok(D)],
        out_specs=[_tok(D), _full((8, LANES))], out_shape=[_sds((t_len, D), F32), _sds((8, LANES), F32)],
        compiler_params=_cparams(1),
    )(h, target)


def _acc_init(refs):
    @pl.when(pl.program_id(0) == 0)
    def _():
        for r in refs:
            r[...] = jnp.zeros_like(r)


def _ple_bwd(dh3, h2, pp, gate, g_post, g_pre, w_rows):
    t_len = dh3.shape[0]

    def body(dh_ref, h_ref, pp_ref, gate_ref, g2_ref, g1_ref, wg_ref, dh2_ref, dpp_ref, dpre_ref, hn_ref, dg2_ref, dg1_ref):
        _acc_init([dg2_ref, dg1_ref])
        dh = dh_ref[...]
        pp, gate = pp_ref[...], gate_ref[...]
        de, dg2, _ = _rms_bwd(pp * gate, g2_ref[...], dh)
        dg2_ref[...] += dg2
        dpp_ref[...] = (de * gate).astype(BF16)
        dpre = (de * pp * gate * (1.0 - gate)).astype(BF16)
        dpre_ref[...] = dpre
        dhn = _dot_nt(dpre, wg_ref[...].reshape(D, D))
        dx, dg1, hn = _rms_bwd(h_ref[...], g1_ref[...], dhn)
        dg1_ref[...] += dg1
        hn_ref[...] = hn.astype(BF16)
        dh2_ref[...] = dh + dx

    return _pcall(
        body, name="ple_bwd", grid=(t_len // TB,),
        in_specs=[_tok(D)] * 4 + [_full((1, D)), _full((1, D)), pl.BlockSpec((N_DEV, 128, D), lambda i: (0, 5, 0))],
        out_specs=[_tok(D)] * 4 + [_full((1, D))] * 2,
        out_shape=[_sds((t_len, D), F32)] + [_sds((t_len, D), BF16)] * 3 + [_sds((1, D), F32)] * 2,
        compiler_params=_cparams(1),
    )(dh3, h2, pp, gate, g_post, g_pre, w_rows)


def _mlp_bwd(dh2, h1, u, ff, g_post, g_pre, w_up, w_rows):
    t_len = dh2.shape[0]

    def body(dh_ref, h_ref, u_ref, ff_ref, g2_ref, g1_ref, up_ref, dn_ref,
             dh1_ref, a2_ref, du_ref, dff_ref, hn_ref, dg2_ref, dg1_ref, dff_s, acc_s):
        i, j = pl.program_id(0), pl.program_id(1)

        @pl.when((i == 0) & (j == 0))
        def _():
            dg2_ref[...] = jnp.zeros_like(dg2_ref)
            dg1_ref[...] = jnp.zeros_like(dg1_ref)

        @pl.when(j == 0)
        def _():
            dff, dg2, _ = _rms_bwd(ff_ref[...], g2_ref[...], dh_ref[...])
            dg2_ref[...] += dg2
            dff_s[...] = dff.astype(BF16)
            dff_ref[...] = dff.astype(BF16)
            acc_s[...] = jnp.zeros_like(acc_s)

        r = jnp.maximum(u_ref[...], 0.0)
        a2_ref[...] = (r * r).astype(BF16)
        du = (_dot_nt(dff_s[...], dn_ref[...]) * (2.0 * r)).astype(BF16)
        du_ref[...] = du
        acc_s[...] += _dot_nt(du, up_ref[...])

        @pl.when(j == N_DEV - 1)
        def _():
            dx, dg1, hn = _rms_bwd(h_ref[...], g1_ref[...], acc_s[...])
            dg1_ref[...] += dg1
            hn_ref[...] = hn.astype(BF16)
            dh1_ref[...] = dh_ref[...] + dx

    tok2 = pl.BlockSpec((TB, D), lambda i, j: (i, 0))
    vec2 = pl.BlockSpec((1, D), lambda i, j: (0, 0))
    blk2 = pl.BlockSpec((TB, FF_BLK), lambda i, j: (i, j))
    return _pcall(
        body, name="mlp_bwd", grid=(t_len // TB, N_DEV),
        in_specs=[tok2, tok2, blk2, tok2, vec2, vec2, pl.BlockSpec((None, D, FF_BLK), lambda i, j: (j, 0, 0)),
                  pl.BlockSpec((None, FF_BLK, D), lambda i, j: (j, 0, 0))],
        out_specs=[tok2, blk2, blk2, tok2, tok2, vec2, vec2],
        out_shape=[_sds((t_len, D), F32), _sds((t_len, D_FF), BF16), _sds((t_len, D_FF), BF16), _sds((t_len, D), BF16),
                   _sds((t_len, D), BF16), _sds((1, D), F32), _sds((1, D), F32)],
        scratch_shapes=[pltpu.VMEM((TB, D), BF16), pltpu.VMEM((TB, D), F32)], compiler_params=_cparams(2),
    )(dh2, h1, u, ff, g_post, g_pre, w_up, w_rows)


def _mixout_bwd(dh1, mix, cat3, o, g, w_rows):
    t_len = dh1.shape[0]

    def body(dh_ref, mix_ref, cat_ref, o_ref, g_ref, w_ref, dcat_ref, dmix_ref, catb_ref, dg_ref):
        _acc_init([dg_ref])
        dmix, dg, _ = _rms_bwd(mix_ref[...], g_ref[...], dh_ref[...])
        dg_ref[...] += dg
        dmix = dmix.astype(BF16)
        dmix_ref[...] = dmix
        dcat_ref[...] = _dot_nt(dmix, w_ref[...].reshape(D, D))
        catb_ref[:, 0:256] = cat_ref[:, 0:256].astype(BF16)
        catb_ref[:, 256:512] = o_ref[...].astype(BF16)
        catb_ref[:, 512:1024] = cat_ref[:, 256:768].astype(BF16)

    return _pcall(
        body, name="mixout_bwd", grid=(t_len // TB,),
        in_specs=[_tok(D), _tok(D), _tok(768), _tok(DG), _full((1, D)), pl.BlockSpec((N_DEV, 128, D), lambda i: (0, 4, 0))],
        out_specs=[_tok(D), _tok(D), _tok(D), _full((1, D))],
        out_shape=[_sds((t_len, D), F32), _sds((t_len, D), BF16), _sds((t_len, D), BF16), _sds((1, D), F32)],
        compiler_params=_cparams(1),
    )(dh1, mix, cat3, o, g, w_rows)


def _attn_bwd_dsum(qkv, dcat, c, ct, lset):
    t_len = qkv.shape[0]

    def body(q_ref, do_ref, k_ref, v_ref, c_ref, ct_ref, lset_ref, dt_ref, dob_ref):
        i = pl.program_id(0)
        lane, halves = _head_masks(TB)
        causal_t = lax.broadcasted_iota(jnp.int32, (TB, TB), 1) >= lax.broadcasted_iota(jnp.int32, (TB, TB), 0)
        sub = lax.broadcasted_iota(jnp.int32, (8, TB), 0)
        dob_ref[...] = do_ref[...].astype(BF16)
        out = jnp.zeros((8, TB), F32)
        for g in range(2):
            cols = slice(g * LANES, (g + 1) * LANES)
            qi = q_ref[:, cols]
            doi = do_ref[:, cols].astype(BF16)
            doms = [_keep_lanes(halves[hh], doi) for hh in range(2)]
            cqs = [ct_ref[2 * g + hh:2 * g + hh + 1, :] for hh in range(2)]
            lses = [lset_ref[2 * g + hh:2 * g + hh + 1, :] for hh in range(2)]

            def block(j, accs, masked):
                off = pl.multiple_of(j * TB, TB)
                kj = k_ref[pl.ds(off, TB), cols]
                vj = v_ref[pl.ds(off, TB), cols]
                cj = c_ref[pl.ds(off, TB), :]
                new = []
                for hh in range(2):
                    ck = jnp.sum(jnp.where(lane == 2 * g + hh, cj, 0.0), axis=1, keepdims=True)
                    st = _dot_nt(_keep_lanes(halves[hh], kj), qi) + (cqs[hh] - ck)
                    if masked:
                        st = jnp.where(causal_t, st, -jnp.inf)
                    pt = jnp.exp(st - lses[hh])
                    new.append(accs[hh] + jnp.sum(pt * _dot_nt(vj, doms[hh]), axis=0, keepdims=True))
                return tuple(new)

            init = (jnp.zeros((1, TB), F32), jnp.zeros((1, TB), F32))
            accs = block(i, lax.fori_loop(0, i, lambda j, cr: block(j, cr, False), init), True)
            out = jnp.where(sub == 2 * g, accs[0], out)
            out = jnp.where(sub == 2 * g + 1, accs[1], out)
        dt_ref[...] = out

    row8 = pl.BlockSpec((8, TB), lambda i: (0, i))
    return _pcall(
        body, name="attn_bwd_dsum", grid=(t_len // TB,),
        in_specs=[_tokcol(DG, 0), _tokcol(DG, 1), pl.BlockSpec((t_len, DG), lambda i: (0, 1)),
                  pl.BlockSpec((t_len, DG), lambda i: (0, 2)), _full((t_len, LANES)), row8, row8],
        out_specs=[row8, _tok(DG)], out_shape=[_sds((8, t_len), F32), _sds((t_len, DG), BF16)],
        compiler_params=_cparams(1),
    )(qkv, dcat, qkv, qkv, c, ct, lset)


def _attn_bwd(qkv, dob, c, ct, lset, dt, xchg=()):
    t_len = qkv.shape[0]
    n_q = t_len // TB
    n_x = len(xchg)

    def body(*refs):
        j = pl.program_id(0)
        ins, outs, begin, finish = _with_exchange(refs, 8, 4, n_x, True, j == 0, j == n_q - 1)
        q_ref, dob_ref, k_ref, v_ref, c_ref, ct_ref, lset_ref, dt_ref = ins
        dq_ref, dk_ref, dv_ref, dc_ref = outs
        if begin:
            begin()

        @pl.when(j == 0)
        def _():
            dq_ref[...] = jnp.zeros_like(dq_ref)

        lane, halves = _head_masks(TB)
        crow = c_ref[...]
        causal_t = lax.broadcasted_iota(jnp.int32, (TB, TB), 1) >= lax.broadcasted_iota(jnp.int32, (TB, TB), 0)
        dc_out = jnp.zeros((TB, LANES), F32)
        for g in range(2):
            cols = slice(g * LANES, (g + 1) * LANES)
            kg, vg = k_ref[:, cols], v_ref[:, cols]
            kms = [_keep_lanes(halves[hh], kg) for hh in range(2)]
            cks = [jnp.sum(jnp.where(lane == 2 * g + hh, crow, 0.0), axis=1, keepdims=True) for hh in range(2)]

            def block(i, carry, masked):
                dk, dv, dcs = carry
                off = pl.multiple_of(i * TB, TB)
                qi = q_ref[pl.ds(off, TB), cols]
                doi = dob_ref[pl.ds(off, TB), cols]
                dq_add = jnp.zeros((TB, LANES), F32)
                dcs_new = []
                for hh in range(2):
                    h = 2 * g + hh
                    dom = _keep_lanes(halves[hh], doi)
                    st = _dot_nt(kms[hh], qi) + (ct_ref[h:h + 1, pl.ds(off, TB)] - cks[hh])
                    if masked:
                        st = jnp.where(causal_t, st, -jnp.inf)
                    pt = jnp.exp(st - lset_ref[h:h + 1, pl.ds(off, TB)])
                    dv = dv + _dot(pt.astype(BF16), dom)
                    dst = pt * (_dot_nt(vg, dom) - dt_ref[h:h + 1, pl.ds(off, TB)])
                    dsb = dst.astype(BF16)
                    dk = dk + _dot(dsb, _keep_lanes(halves[hh], qi))
                    dcs_new.append(dcs[hh] + jnp.sum(dst, axis=1, keepdims=True))
                    dq_add = dq_add + _dot_tn(dsb, kms[hh])
                dq_ref[pl.ds(off, TB), cols] += dq_add
                return dk, dv, tuple(dcs_new)

            init = (jnp.zeros((TB, LANES), F32), jnp.zeros((TB, LANES), F32),
                    (jnp.zeros((TB, 1), F32), jnp.zeros((TB, 1), F32)))
            carry = block(j, init, True)
            dk, dv, dcs = lax.fori_loop(j + 1, n_q, lambda i, cr: block(i, cr, False), carry)
            dk_ref[:, cols] = dk
            dv_ref[:, cols] = dv
            dc_out = jnp.where(lane == 2 * g, -dcs[0], dc_out)
            dc_out = jnp.where(lane == 2 * g + 1, -dcs[1], dc_out)
        dc_ref[...] = dc_out
        if finish:
            finish()

    return _pcall(
        body, name="attn_bwd_exchange" if n_x else "attn_bwd", grid=(n_q,),
        in_specs=[pl.BlockSpec((t_len, DG), lambda i: (0, 0)), _full((t_len, DG)), _tokcol(DG, 1), _tokcol(DG, 2),
                  _tok(LANES), _full((8, t_len)), _full((8, t_len)), _full((8, t_len))] + [_HBM] * n_x,
        out_specs=[_full((t_len, DG)), _tok(DG), _tok(DG), _tok(LANES)] + [_HBM] * n_x,
        out_shape=[_sds((t_len, DG), F32), _sds((t_len, DG), F32), _sds((t_len, DG), F32), _sds((t_len, LANES), F32)]
        + _exchange_shapes(xchg, [True] * n_x),
        scratch_shapes=_exchange_scratch(n_x) if n_x else [],
        compiler_params=_cparams(1),
    )(qkv, dob, qkv, qkv, c, ct, lset, dt, *xchg)


def _forget_bwd(dc, fl, b_row):
    t_len = dc.shape[0]
    n_t = t_len // TB
    rev = pl.BlockSpec((TB, LANES), lambda i: (n_t - 1 - i, 0))

    def body(dc_ref, fl_ref, b_ref, dfl_ref, db_ref, carry):
        @pl.when(pl.program_id(0) == 0)
        def _():
            carry[...] = jnp.zeros_like(carry)
            db_ref[...] = jnp.zeros_like(db_ref)

        r = lax.broadcasted_iota(jnp.int32, (TB, TB), 0)
        s = lax.broadcasted_iota(jnp.int32, (TB, TB), 1)
        dc = dc_ref[...]
        dl = _dot_exact((r <= s).astype(F32), dc) + carry[0:1, :]
        carry[...] += jnp.sum(dc, axis=0, keepdims=True)
        dfl = dl * _sigmoid(-(fl_ref[...] + b_ref[...]))
        dfl_ref[...] = dfl
        db_ref[...] += jnp.sum(dfl, axis=0, keepdims=True)

    return _pcall(
        body, name="forget_bwd", grid=(n_t,), in_specs=[rev, rev, _full((1, LANES))],
        out_specs=[rev, _full((1, LANES))], out_shape=[_sds((t_len, LANES), F32), _sds((1, LANES), F32)],
        scratch_shapes=[pltpu.VMEM((8, LANES), F32)], compiler_params=_cparams(1),
    )(dc, fl, b_row)


def _branch_bwd(zc, dcat, dq, dk, dv, w_dw, ln_g, ln_b, w_pw, w_sc, w_pool, pool_scale):
    t_len = zc.shape[0]
    n_t = t_len // TB
    e2 = HALO + TB + HALO
    e1 = TB + HALO

    def body(z_ref, zp_ref, zn_ref, dcf_ref, dcfn_ref, dsp_ref, dspn_ref, dq_ref, dk_ref, dv_ref,
             dw_ref, g_ref, b_ref, pw_ref, sc_ref, pool_ref, ps_ref,
             dz_ref, ddw_ref, dg_ref, db_ref, dpw_ref, dsc_ref, dpool_ref, dps_ref,
             u_s, dy_s, ch_s, dcv_s, p0, p1, p2, p3, g0, g1, g2, g3):
        i = pl.program_id(0)
        _acc_init([ddw_ref, dg_ref, db_ref, dpw_ref, dsc_ref, dpool_ref, dps_ref])
        hm = (i > 0).astype(F32)
        nm = (i < n_t - 1).astype(F32)

        sig_b = _sigmoid(z_ref[:, 256:512])
        a = z_ref[:, 0:256]
        u_s[0:HALO, :] = zp_ref[:, 0:256] * _sigmoid(zp_ref[:, 256:512]) * hm
        u_s[HALO:HALO + TB, :] = a * sig_b
        u_s[HALO + TB:e2, :] = zn_ref[:, 0:256] * _sigmoid(zn_ref[:, 256:512])
        y = jnp.zeros((e1, DG), F32)
        for k in range(CONF_K):
            y = y + dw_ref[k:k + 1, :] * u_s[pl.ds(HALO - (CONF_K - 1) + k, e1), :]
        n, r, yn = _layer_norm_parts(y, g_ref[...], b_ref[...])
        sg = _sigmoid(yn)
        dyc = jnp.concatenate([dcf_ref[...], dcfn_ref[...] * nm], axis=0)
        ds = _dot_nt(dyc.astype(BF16), pw_ref[...])
        dyn = ds * sg * (1.0 + yn * (1.0 - sg))
        dg_ref[...] += jnp.sum((dyn * n)[0:TB], axis=0, keepdims=True)
        db_ref[...] += jnp.sum(dyn[0:TB], axis=0, keepdims=True)
        dn = dyn * g_ref[...]
        dyv = r * (dn - jnp.mean(dn, axis=-1, keepdims=True) - n * jnp.mean(dn * n, axis=-1, keepdims=True))
        dpw_ref[...] += _dot_tn((yn * sg)[0:TB].astype(BF16), dcf_ref[...].astype(BF16))
        dy_s[...] = dyv
        du = jnp.zeros((TB, DG), F32)
        dyv_t = dyv[0:TB]
        for k in range(CONF_K):
            du = du + dw_ref[k:k + 1, :] * dy_s[pl.ds(CONF_K - 1 - k, TB), :]
            ddw_ref[k:k + 1, :] += jnp.sum(dyv_t * u_s[pl.ds(HALO - (CONF_K - 1) + k, TB), :], axis=0, keepdims=True)
        dz_ref[:, 0:256] = (du * sig_b).astype(BF16)
        dz_ref[:, 256:512] = (du * a * sig_b * (1.0 - sig_b)).astype(BF16)

        dz_ref[:, 512:768] = (dq_ref[...] * SCALE).astype(BF16)
        dz_ref[:, 768:1024] = dk_ref[...].astype(BF16)
        dz_ref[:, 1024:1280] = dv_ref[...].astype(BF16)

        sc_h, sc_b, sc_c = z_ref[:, 512:768], z_ref[:, 768:1024], z_ref[:, 1024:1280]
        ch_s[0:HALO, :] = zp_ref[:, 1024:1280] * zp_ref[:, 512:768] * hm
        ch_s[HALO:HALO + TB, :] = sc_c * sc_h
        ch_s[HALO + TB:e2, :] = zn_ref[:, 1024:1280] * zn_ref[:, 512:768]
        cv = jnp.zeros((TB, DG), F32)
        for k in range(SC_K):
            cv = cv + sc_ref[k:k + 1, :] * ch_s[pl.ds(HALO - (SC_K - 1) + k, TB), :]
        dy_sc = dsp_ref[:, 0:256]
        dcv_t = dy_sc * sc_b
        dcv_s[0:TB, :] = dcv_t
        dcv_s[TB:e1, :] = dspn_ref[:, 0:256] * nm * zn_ref[:, 768:1024]
        dch = jnp.zeros((TB, DG), F32)
        for k in range(SC_K):
            dch = dch + sc_ref[k:k + 1, :] * dcv_s[pl.ds(SC_K - 1 - k, TB), :]
            dsc_ref[k:k + 1, :] += jnp.sum(dcv_t * ch_s[pl.ds(HALO - (SC_K - 1) + k, TB), :], axis=0, keepdims=True)
        dz_ref[:, 1280:1536] = (dch * sc_c).astype(BF16)
        dz_ref[:, 1536:1792] = (dy_sc * cv).astype(BF16)
        dz_ref[:, 1792:2048] = (dch * sc_h).astype(BF16)

        v_t = z_ref[:, 1280:1536]
        p0[0:HALO, :] = zp_ref[:, 1280:1536] * hm
        p0[HALO:HALO + TB, :] = v_t
        s2, s4, s8, s16 = _pool_window_sums(p0, p1, p2, p3)
        cnt, lane = _pool_counts(i * TB, e1)
        dlt = (_lane_group_select(lane[0:TB], s2, s4, s8, s16) / cnt[0:TB] - v_t).astype(BF16)
        dyp_t = dsp_ref[:, 256:512]
        dps_ref[...] += jnp.sum(dyp_t * _dot(dlt, pool_ref[...]), axis=0, keepdims=True)
        dpre = (jnp.concatenate([dyp_t, dspn_ref[:, 256:512] * nm], axis=0) * ps_ref[...]).astype(BF16)
        dpool_ref[...] += _dot_tn(dlt, dpre[0:TB])
        dd = _dot_nt(dpre, pool_ref[...])
        g0[...] = dd / cnt
        g1[0:TB + 24, :] = g0[pl.ds(0, TB + 24), :] + g0[pl.ds(1, TB + 24), :]
        g2[0:TB + 16, :] = g1[pl.ds(0, TB + 16), :] + g1[pl.ds(2, TB + 16), :]
        g3[0:TB + 8, :] = g2[pl.ds(0, TB + 8), :] + g2[pl.ds(4, TB + 8), :]
        f16 = g3[pl.ds(0, TB), :] + g3[pl.ds(8, TB), :]
        fwd_sum = _lane_group_select(lane[0:TB], g1[pl.ds(0, TB), :], g2[pl.ds(0, TB), :], g3[pl.ds(0, TB), :], f16)
        dz_ref[:, 2048:2304] = (fwd_sum - dd[0:TB]).astype(BF16)

    vec = _full((1, DG))
    mat = _full((DG, DG))
    scr = ([pltpu.VMEM((e2, DG), F32), pltpu.VMEM((e1, DG), F32), pltpu.VMEM((e2, DG), F32), pltpu.VMEM((e1, DG), F32)]
           + [pltpu.VMEM((HALO + TB, DG), F32)] * 4 + [pltpu.VMEM((e1, DG), F32)] * 4)
    return _pcall(
        body, name="branch_bwd", grid=(n_t,),
        in_specs=[_tok(1536), _halo_prev(1536), _halo_next(1536, t_len),
                  _tokcol(DG, 0), _halo_next(DG, t_len, 0), _tokcol(512, 1), _halo_next(512, t_len, 1),
                  _tok(DG), _tok(DG), _tok(DG),
                  _full((32, DG)), vec, vec, mat, _full((8, DG)), mat, vec],
        out_specs=[_tok(W_MAIN), _full((32, DG)), vec, vec, mat, _full((8, DG)), mat, vec],
        out_shape=[_sds((t_len, W_MAIN), BF16), _sds((32, DG), F32), _sds((1, DG), F32), _sds((1, DG), F32),
                   _sds((DG, DG), F32), _sds((8, DG), F32), _sds((DG, DG), F32), _sds((1, DG), F32)],
        scratch_shapes=scr, compiler_params=_cparams(1),
    )(zc, zc, zc, dcat, dcat, dcat, dcat, dq, dk, dv, w_dw, ln_g, ln_b, w_pw, w_sc, w_pool, pool_scale)


def _mixin_bwd(dh, h, g, dz, dfl, win, wf):
    t_len = dh.shape[0]

    def body(dh_ref, h_ref, g_ref, dz_ref, dfl_ref, win_ref, wf_ref, dh0_ref, xn_ref, dg_ref):
        _acc_init([dg_ref])
        dxn = _dot_nt(dz_ref[...], win_ref[...].reshape(D, W_MAIN)) + _dot_nt(dfl_ref[...].astype(BF16), wf_ref[...])
        dx, dg, xn = _rms_bwd(h_ref[...], g_ref[...], dxn)
        dg_ref[...] += dg
        xn_ref[...] = xn.astype(BF16)
        dh0_ref[...] = dh_ref[...] + dx

    return _pcall(
        body, name="mixin_bwd", grid=(t_len // TB,),
        in_specs=[_tok(D), _tok(D), _full((1, D)), _tok(W_MAIN), _tok(LANES), _full((N_DEV, D // N_DEV, W_MAIN)),
                  _full((D, LANES))],
        out_specs=[_tok(D), _tok(D), _full((1, D))],
        out_shape=[_sds((t_len, D), F32), _sds((t_len, D), BF16), _sds((1, D), F32)],
        compiler_params=_cparams(1),
    )(dh, h, g, dz, dfl, win, wf)


def _matmul_tn(name, a, b, tm, tn, out_dtype, block_major=False):
    t_len, m = a.shape
    n = b.shape[1]
    tk = min(t_len, 1024)
    n_k = t_len // tk

    def body(a_ref, b_ref, o_ref, acc):
        k = pl.program_id(2)

        @pl.when(k == 0)
        def _():
            acc[...] = jnp.zeros_like(acc)

        acc[...] += _dot_tn(a_ref[...].astype(BF16), b_ref[...].astype(BF16))

        @pl.when(k == n_k - 1)
        def _():
            if block_major:
                for blk in range(tn // FF_BLK):
                    o_ref[blk] = acc[:, blk * FF_BLK:(blk + 1) * FF_BLK].astype(out_dtype)
            else:
                o_ref[...] = acc[...].astype(out_dtype)

    if block_major:
        out_spec = pl.BlockSpec((tn // FF_BLK, tm, FF_BLK), lambda i, j, k: (j, i, 0))
        out_shape = _sds((n // FF_BLK, m, FF_BLK), out_dtype)
    else:
        out_spec = pl.BlockSpec((tm, tn), lambda i, j, k: (i, j))
        out_shape = _sds((m, n), out_dtype)
    return _pcall(
        body, name=name, grid=(m // tm, n // tn, n_k),
        in_specs=[pl.BlockSpec((tk, tm), lambda i, j, k: (k, i)), pl.BlockSpec((tk, tn), lambda i, j, k: (k, j))],
        out_specs=out_spec, out_shape=out_shape, scratch_shapes=[pltpu.VMEM((tm, tn), F32)], compiler_params=_cparams(3),
    )(a, b)


_HBM = pl.BlockSpec(memory_space=pltpu.HBM)


def _mesh_place():
    return lax.axis_index("x"), lax.axis_index("y"), lax.axis_index("c")


def _allgather(name, srcs):
    n = len(srcs)

    def body(*refs):
        src, dst = refs[:n], refs[n:2 * n]
        send_sems, recv_sems, local_sems = refs[2 * n:]
        x, y, c = _mesh_place()
        me, sibling = (x, y, c), (x, y, 1 - c)
        chips = [(1 - x, y), (x, 1 - y), (1 - x, 1 - y)]

        def slot(px, py, pc):
            return 4 * px + 2 * py + pc

        def copy(t, k, block, to, from_src=False):
            return pltpu.make_async_remote_copy(
                src_ref=src[t] if from_src else dst[t].at[slot(*block)], dst_ref=dst[t].at[slot(*block)],
                send_sem=send_sems.at[t, k], recv_sem=recv_sems.at[t, k], device_id=to, device_id_type=MESH_ID)

        mine = [pltpu.make_async_copy(src[t], dst[t].at[slot(*me)], local_sems.at[t]) for t in range(n)]
        for cp in mine:
            cp.start()
        started = []
        for t in range(n):
            started.append(copy(t, 0, me, sibling, from_src=True))
            started += [copy(t, 1 + j, me, (*chip, c), from_src=True) for j, chip in enumerate(chips)]
        for cp in started:
            cp.start()
        for j, chip in enumerate(chips):
            for t in range(n):
                copy(t, 1 + j, (*chip, c), me).wait_recv()
                fwd = copy(t, 4 + j, (*chip, c), sibling)
                fwd.start()
                started.append(fwd)
        for t in range(n):
            copy(t, 0, sibling, me).wait_recv()
            for j, chip in enumerate(chips):
                copy(t, 4 + j, (*chip, 1 - c), me).wait_recv()
        for cp in started:
            cp.wait_send()
        for cp in mine:
            cp.wait()

    return _pcall(
        body, name=name, in_specs=[_HBM] * n, out_specs=[_HBM] * n,
        out_shape=[_sds((N_DEV,) + s.shape, s.dtype) for s in srcs],
        scratch_shapes=[pltpu.SemaphoreType.DMA((n, 7)), pltpu.SemaphoreType.DMA((n, 7)), pltpu.SemaphoreType.DMA((n,))],
    )(*srcs)


def _exchange_ops(src, dst, per_dest, send_sems, recv_sems, local_sems):
    n = len(src)
    x, y, c = _mesh_place()
    me_slot = 4 * x + 2 * y + c
    peers = []
    for r in range(1, N_DEV):
        px, py, pc = x ^ ((r >> 2) & 1), y ^ ((r >> 1) & 1), c ^ (r & 1)
        peers.append(((px, py, pc), 4 * px + 2 * py + pc))

    def piece(t, dest_slot):
        return src[t].at[dest_slot] if per_dest[t] else src[t]

    def local(t):
        return pltpu.make_async_copy(piece(t, me_slot), dst[t].at[me_slot], local_sems.at[t])

    def remote(t, r, landing_slot):
        peer, peer_slot = peers[r]
        return pltpu.make_async_remote_copy(
            src_ref=piece(t, peer_slot), dst_ref=dst[t].at[landing_slot], send_sem=send_sems.at[t, r],
            recv_sem=recv_sems.at[t, r], device_id=peer, device_id_type=MESH_ID)

    def start():
        for t in range(n):
            local(t).start()
        for r in range(N_DEV - 1):
            for t in range(n):
                remote(t, r, me_slot).start()

    def wait():
        for r in range(N_DEV - 1):
            for t in range(n):
                remote(t, r, peers[r][1]).wait_recv()
        for r in range(N_DEV - 1):
            for t in range(n):
                remote(t, r, me_slot).wait_send()
        for t in range(n):
            local(t).wait()

    return start, wait


def _exchange_shapes(srcs, per_dest):
    return [_sds((N_DEV,) + tuple(s.shape[1:] if pd else s.shape), s.dtype) for s, pd in zip(srcs, per_dest)]


def _exchange_scratch(n):
    return [pltpu.SemaphoreType.DMA((n, N_DEV - 1)), pltpu.SemaphoreType.DMA((n, N_DEV - 1)), pltpu.SemaphoreType.DMA((n,))]


def _exchange(name, srcs, per_dest):
    n = len(srcs)

    def body(*refs):
        start, wait = _exchange_ops(refs[:n], refs[n:2 * n], per_dest, *refs[2 * n:])
        start()
        wait()

    return _pcall(
        body, name=name, in_specs=[_HBM] * n, out_specs=[_HBM] * n, out_shape=_exchange_shapes(srcs, per_dest),
        scratch_shapes=_exchange_scratch(n),
    )(*srcs)


def _adam_math(w, g, m, v):
    m = ADAM_B1 * m + (1.0 - ADAM_B1) * g
    v = ADAM_B2 * v + (1.0 - ADAM_B2) * (g * g)
    m_hat = m / (1.0 - ADAM_B1 ** ADAM_STEP)
    v_hat = v / (1.0 - ADAM_B2 ** ADAM_STEP)
    delta = -ADAM_LR * (m_hat / (jnp.sqrt(v_hat) + ADAM_EPS) + ADAM_WD * w)
    return delta, m, v


def _adam_rows(name, parts, w, m, v, row_tile, row_block_offset):
    n_l, rows, cols = w.shape

    def body(*refs):
        p_refs = refs[:n_l]
        w_ref, m_ref, v_ref, g_out, d_out, m_out, v_out = refs[n_l:]
        layer = pl.program_id(0)
        for k in range(n_l):
            @pl.when(layer == k)
            def _(p_ref=p_refs[k]):
                g = p_ref[0].astype(F32)
                for s in range(1, N_DEV):
                    g = g + p_ref[s].astype(F32)
                delta, m_new, v_new = _adam_math(w_ref[...], g, m_ref[...], v_ref[...])
                g_out[...] = g
                d_out[...] = delta
                m_out[...] = m_new
                v_out[...] = v_new

    def part_spec(k):
        return pl.BlockSpec((N_DEV, row_tile, cols),
                            lambda l, i: (0, row_block_offset + jnp.where(l == k, i, 0), 0))

    blk = pl.BlockSpec((None, row_tile, cols), lambda l, i: (l, i, 0))
    return _pcall(
        body, name=name, grid=(n_l, rows // row_tile),
        in_specs=[part_spec(k) for k in range(n_l)] + [blk, blk, blk],
        out_specs=[blk] * 4, out_shape=[_sds(w.shape, F32)] * 4, compiler_params=_cparams(2),
    )(*parts, w, m, v)


def _adam_packed(name, parts, w, m, v):
    def body(p_ref, w_ref, m_ref, v_ref, g_out, d_out, m_out, v_out):
        g = p_ref[0]
        for s in range(1, N_DEV):
            g = g + p_ref[s]
        delta, m_new, v_new = _adam_math(w_ref[...], g, m_ref[...], v_ref[...])
        g_out[...] = g
        d_out[...] = delta
        m_out[...] = m_new
        v_out[...] = v_new

    return _pcall(
        body, name=name, grid=(1,), in_specs=[_full(parts.shape), _full(w.shape), _full(w.shape), _full(w.shape)],
        out_specs=[_full(w.shape)] * 4, out_shape=[_sds(w.shape, F32)] * 4, compiler_params=_cparams(1),
    )(parts, w, m, v)


def _pack_rows(flat_parts, lead=()):
    flat = jnp.concatenate(flat_parts, axis=-1)
    n = flat.shape[-1]
    rows = -(-n // LANES)
    rows = -(-rows // 8) * 8
    flat = jnp.pad(flat, [(0, 0)] * len(lead) + [(0, rows * LANES - n)])
    return flat.reshape(lead + (rows, LANES))


def _unpack_rows(packed, shapes, lead=()):
    flat = packed.reshape(lead + (-1,))
    out, off = [], 0
    for shp in shapes:
        size = 1
        for s in shp:
            size *= s
        out.append(flat[..., off:off + size].reshape(lead + tuple(shp)))
        off += size
    return out


_SMALL_SHARD_SHAPES = [(N_LAYERS, 128, 4), (N_LAYERS, 32, DG), (N_LAYERS, D_PLE, 128), (N_LAYERS, CONF_K, 32), (N_LAYERS, SC_K, 32)]
_REP_SHAPES = [(N_LAYERS, D)] * 6 + [(N_LAYERS, DG)] * 3 + [(N_LAYERS, N_HEADS), (N_LAYERS, 4, 64, 64)]


def _small_full_to_shards(fcol, pw, proj, dw, sc):
    return [
        fcol.reshape(N_LAYERS, N_DEV, 128, 4).transpose(1, 0, 2, 3),
        pw.reshape(N_LAYERS, N_DEV, 32, DG).transpose(1, 0, 2, 3),
        proj.reshape(N_LAYERS, D_PLE, N_DEV, 128).transpose(2, 0, 1, 3),
        dw.reshape(N_LAYERS, CONF_K, N_DEV, 32).transpose(2, 0, 1, 3),
        sc.reshape(N_LAYERS, SC_K, N_DEV, 32).transpose(2, 0, 1, 3),
    ]


def _small_shards_to_full(fcol, pw, proj, dw, sc):
    return [
        fcol.transpose(1, 0, 2, 3).reshape(N_LAYERS, D, 4),
        pw.transpose(1, 0, 2, 3).reshape(N_LAYERS, DG, DG),
        proj.transpose(1, 2, 0, 3).reshape(N_LAYERS, D_PLE, D),
        dw.transpose(1, 2, 0, 3).reshape(N_LAYERS, CONF_K, DG),
        sc.transpose(1, 2, 0, 3).reshape(N_LAYERS, SC_K, DG),
    ]


def _pad_rows(a, rows):
    return jnp.pad(a, ((0, rows - a.shape[0]), (0, 0)))


def _block_diag4(w):
    z = jnp.zeros((64, 64), w.dtype)
    return jnp.concatenate([jnp.concatenate([w[g] if k == g else z for k in range(4)], axis=1) for g in range(4)], axis=0)


def kernel(x, p, g_mix_pre, w_in, b_forget, w_conf_dw, conf_ln_g, conf_ln_b, w_conf_pw, w_sc, w_pool, pool_scale, w_out, g_mix_post, g_mlp_pre, w_up, w_down, g_mlp_post, g_ple_pre, w_ple_gate, w_ple_proj, g_ple_post, loss_target, m_g_mix_pre, m_w_in, m_b_forget, m_w_conf_dw, m_conf_ln_g, m_conf_ln_b, m_w_conf_pw, m_w_sc, m_w_pool, m_pool_scale, m_w_out, m_g_mix_post, m_g_mlp_pre, m_w_up, m_w_down, m_g_mlp_post, m_g_ple_pre, m_w_ple_gate, m_w_ple_proj, m_g_ple_post, v_g_mix_pre, v_w_in, v_b_forget, v_w_conf_dw, v_conf_ln_g, v_conf_ln_b, v_w_conf_pw, v_w_sc, v_w_pool, v_pool_scale, v_w_out, v_g_mix_post, v_g_mlp_pre, v_w_up, v_w_down, v_g_mlp_post, v_g_ple_pre, v_w_ple_gate, v_w_ple_proj, v_g_ple_post):
    n_l = N_LAYERS
    t_len = x.shape[1]
    assert t_len % TB == 0 and x.shape[0] == 1 and x.shape[2] == D

    def main_cols(a):
        return jnp.concatenate([a[..., :F_LO], a[..., F_HI:]], axis=-1)

    def fcols(a):
        return a[..., F_LO:F_HI]

    def rows_pack(down, out, gate):
        return jnp.concatenate([down, out, gate], axis=1)

    rows_b = rows_pack(w_down, w_out, w_ple_gate).astype(BF16)
    win_b = main_cols(w_in).astype(BF16)
    wup_b = w_up.astype(BF16)
    small_local = _pack_rows([a.reshape(-1) for a in (fcols(w_in), w_conf_pw, w_ple_proj, w_conf_dw, w_sc)])
    rows_g, win_g, wup_g = [None] * n_l, [None] * n_l, [None] * n_l
    rows_g[0], win_g[0], wup_g[0], small_all = _allgather("weight_allgather", [rows_b[0], win_b[0], wup_b[0], small_local])
    small_g = _unpack_rows(small_all, _SMALL_SHARD_SHAPES, lead=(N_DEV,))
    fcol_f, pw_f, proj_f, dw_f, sc_f = _small_shards_to_full(*small_g)
    wf_b = jnp.pad(fcol_f, ((0, 0), (0, 0), (0, LANES - 4))).astype(BF16)
    pw_b, proj_b = pw_f.astype(BF16), proj_f.astype(BF16)
    dw_pad = jnp.pad(dw_f, ((0, 0), (0, 32 - CONF_K), (0, 0)))
    sc_pad = jnp.pad(sc_f, ((0, 0), (0, 8 - SC_K), (0, 0)))
    pool_bd = jnp.stack([_block_diag4(w_pool[l]) for l in range(n_l)]).astype(BF16)
    b_row = jnp.pad(b_forget, ((0, 0), (0, LANES - N_HEADS)))[:, None, :]

    def vec(a, l):
        return a[l][None, :]

    h = x[0]
    saved = []
    for l in range(n_l):
        zc, qkv, fl = _mixin_fwd(h, vec(g_mix_pre, l), win_g[l], wf_b[l])
        c, ct = _cumsum_fwd(fl, b_row[l])
        cat3 = _branch_fwd(zc, dw_pad[l], vec(conf_ln_g, l), vec(conf_ln_b, l), pw_b[l], sc_pad[l], pool_bd[l], vec(pool_scale, l))
        if l + 1 < n_l:
            o, lset, rows_g[l + 1], win_g[l + 1], wup_g[l + 1] = _attn_fwd(qkv, c, ct, [rows_b[l + 1], win_b[l + 1], wup_b[l + 1]])
        else:
            o, lset = _attn_fwd(qkv, c, ct)
        mix, h1 = _mixout_fwd(h, cat3, o, rows_g[l], vec(g_mix_post, l))
        u, ff, h2 = _mlp_fwd(h1, vec(g_mlp_pre, l), wup_g[l], rows_g[l], vec(g_mlp_post, l))
        pp, gate, h3 = _ple_fwd(h2, p[l, 0], vec(g_ple_pre, l), rows_g[l], proj_b[l], vec(g_ple_post, l))
        saved.append(dict(h0=h, zc=zc, qkv=qkv, fl=fl, c=c, ct=ct, cat3=cat3, o=o, lset=lset, mix=mix, h1=h1, u=u, ff=ff,
                          h2=h2, pp=pp, gate=gate))
        h = h3

    dh, loss_part = _loss_bwd(h, loss_target[0])
    loss = lax.psum(loss_part[0, 0], ("x", "y", "c"))

    d_rows, d_win, d_wup = [None] * n_l, [None] * n_l, [None] * n_l
    r_rows, r_win, r_wup = [None] * n_l, [None] * n_l, [None] * n_l
    small_grads = {k: [None] * n_l for k in ("fcol", "pw", "proj", "dw", "sc")}
    rep_grads = {k: [None] * n_l for k in ("g_mix_pre", "g_mix_post", "g_mlp_pre", "g_mlp_post", "g_ple_pre", "g_ple_post",
                                           "ln_g", "ln_b", "pool_scale", "b_forget", "w_pool")}
    for l in reversed(range(n_l)):
        s = saved[l]
        dh, dpp_b, dpre_b, hn3_b, dg_ple_post, dg_ple_pre = _ple_bwd(
            dh, s["h2"], s["pp"], s["gate"], vec(g_ple_post, l), vec(g_ple_pre, l), rows_g[l])
        small_grads["proj"][l] = _matmul_tn("wgrad_proj", p[l, 0], dpp_b, D_PLE, D, F32)
        d_gate = _matmul_tn("wgrad_gate", hn3_b, dpre_b, 512, D, BF16)
        dh, a2_b, du_b, dff_b, hn2_b, dg_mlp_post, dg_mlp_pre = _mlp_bwd(
            dh, s["h1"], s["u"], s["ff"], vec(g_mlp_post, l), vec(g_mlp_pre, l), wup_g[l], rows_g[l])
        d_down = _matmul_tn("wgrad_down", a2_b, dff_b, 512, D, BF16)
        d_wup[l] = _matmul_tn("wgrad_up", hn2_b, du_b, 512, 2 * FF_BLK, BF16, block_major=True)
        dcat, dmix_b, cat_b, dg_mix_post = _mixout_bwd(dh, s["mix"], s["cat3"], s["o"], vec(g_mix_post, l), rows_g[l])
        d_out = _matmul_tn("wgrad_out", cat_b, dmix_b, 512, D, BF16)
        dt, dob = _attn_bwd_dsum(s["qkv"], dcat, s["c"], s["ct"], s["lset"])
        if l + 1 < n_l:
            dq, dk, dv, dc, r_rows[l + 1], r_win[l + 1], r_wup[l + 1] = _attn_bwd(
                s["qkv"], dob, s["c"], s["ct"], s["lset"], dt, [d_rows[l + 1], d_win[l + 1], d_wup[l + 1]])
        else:
            dq, dk, dv, dc = _attn_bwd(s["qkv"], dob, s["c"], s["ct"], s["lset"], dt)
        dfl, db_f = _forget_bwd(dc, s["fl"], b_row[l])
        dz_b, ddw, dln_g, dln_b, dpw, dsc, dpool, dps = _branch_bwd(
            s["zc"], dcat, dq, dk, dv, dw_pad[l], vec(conf_ln_g, l), vec(conf_ln_b, l), pw_b[l], sc_pad[l], pool_bd[l],
            vec(pool_scale, l))
        dh, xn_b, dg_mix_pre = _mixin_bwd(dh, s["h0"], vec(g_mix_pre, l), dz_b, dfl, win_g[l], wf_b[l])
        d_win[l] = _matmul_tn("wgrad_in", xn_b, dz_b, 512, W_MAIN // 2, BF16).reshape(N_DEV, 128, W_MAIN)
        small_grads["fcol"][l] = _matmul_tn("wgrad_fcol", xn_b, dfl, 512, LANES, F32)[:, 0:4]
        d_rows[l] = jnp.concatenate([d_down.reshape(N_DEV, FF_BLK, D), d_out.reshape(N_DEV, 128, D),
                                     d_gate.reshape(N_DEV, 128, D)], axis=1)
        small_grads["pw"][l], small_grads["dw"][l], small_grads["sc"][l] = dpw, ddw[0:CONF_K], dsc[0:SC_K]
        rep_grads["g_mix_pre"][l], rep_grads["g_mix_post"][l] = dg_mix_pre[0], dg_mix_post[0]
        rep_grads["g_mlp_pre"][l], rep_grads["g_mlp_post"][l] = dg_mlp_pre[0], dg_mlp_post[0]
        rep_grads["g_ple_pre"][l], rep_grads["g_ple_post"][l] = dg_ple_pre[0], dg_ple_post[0]
        rep_grads["ln_g"][l], rep_grads["ln_b"][l], rep_grads["pool_scale"][l] = dln_g[0], dln_b[0], dps[0]
        rep_grads["b_forget"][l] = db_f[0, 0:N_HEADS]
        rep_grads["w_pool"][l] = jnp.stack([dpool[64 * g:64 * g + 64, 64 * g:64 * g + 64] for g in range(4)])
    grad_x = dh[None]

    small_part = _pack_rows(
        [a.reshape(N_DEV, -1) for a in _small_full_to_shards(*[jnp.stack(small_grads[k]) for k in ("fcol", "pw", "proj", "dw", "sc")])],
        lead=(N_DEV,))
    rep_order = ("g_mix_pre", "g_mix_post", "g_mlp_pre", "g_mlp_post", "g_ple_pre", "g_ple_post", "ln_g", "ln_b",
                 "pool_scale", "b_forget", "w_pool")
    rep_part = _pack_rows([jnp.stack(rep_grads[k]).reshape(-1) for k in rep_order])
    r_rows[0], r_win[0], r_wup[0], r_small, r_rep = _exchange(
        "grad_exchange", [d_rows[0], d_win[0], d_wup[0], small_part, rep_part], [True, True, True, True, False])

    res = {}
    res["w_down"] = _adam_rows("adam_down", r_rows, w_down, m_w_down, v_w_down, 128, 0)
    res["w_out"] = _adam_rows("adam_out", r_rows, w_out, m_w_out, v_w_out, 128, 4)
    res["w_ple_gate"] = _adam_rows("adam_gate", r_rows, w_ple_gate, m_w_ple_gate, v_w_ple_gate, 128, 5)
    res["w_up"] = _adam_rows("adam_up", r_wup, w_up, m_w_up, v_w_up, 256, 0)
    win_main = _adam_rows("adam_in", r_win, main_cols(w_in), main_cols(m_w_in), main_cols(v_w_in), 128, 0)

    small_w = [(fcols(w_in), w_conf_pw, w_ple_proj, w_conf_dw, w_sc), (fcols(m_w_in), m_w_conf_pw, m_w_ple_proj, m_w_conf_dw, m_w_sc),
               (fcols(v_w_in), v_w_conf_pw, v_w_ple_proj, v_w_conf_dw, v_w_sc)]
    small_packed = [_pack_rows([a.reshape(-1) for a in grp]) for grp in small_w]
    small_res = [_unpack_rows(a, _SMALL_SHARD_SHAPES) for a in _adam_packed("adam_small", r_small, *small_packed)]
    rep_w = [(g_mix_pre, g_mix_post, g_mlp_pre, g_mlp_post, g_ple_pre, g_ple_post, conf_ln_g, conf_ln_b, pool_scale, b_forget, w_pool),
             (m_g_mix_pre, m_g_mix_post, m_g_mlp_pre, m_g_mlp_post, m_g_ple_pre, m_g_ple_post, m_conf_ln_g, m_conf_ln_b, m_pool_scale,
              m_b_forget, m_w_pool),
             (v_g_mix_pre, v_g_mix_post, v_g_mlp_pre, v_g_mlp_post, v_g_ple_pre, v_g_ple_post, v_conf_ln_g, v_conf_ln_b, v_pool_scale,
              v_b_forget, v_w_pool)]
    rep_packed = [_pack_rows([a.reshape(-1) for a in grp]) for grp in rep_w]
    rep_res = [_unpack_rows(a, _REP_SHAPES) for a in _adam_packed("adam_replicated", r_rep, *rep_packed)]

    for kind in range(4):
        fc, pw, proj, dwc, scc = small_res[kind]
        main = win_main[kind]
        (rg_mix_pre, rg_mix_post, rg_mlp_pre, rg_mlp_post, rg_ple_pre, rg_ple_post, r_ln_g, r_ln_b, r_ps, r_bf, r_wpool) = rep_res[kind]
        res.setdefault("by_kind", []).append(dict(
            g_mix_pre=rg_mix_pre, w_in=jnp.concatenate([main[..., :F_LO], fc, main[..., F_LO:]], axis=-1), b_forget=r_bf,
            w_conf_dw=dwc, conf_ln_g=r_ln_g, conf_ln_b=r_ln_b, w_conf_pw=pw, w_sc=scc, w_pool=r_wpool, pool_scale=r_ps,
            w_out=res["w_out"][kind], g_mix_post=rg_mix_post, g_mlp_pre=rg_mlp_pre, w_up=res["w_up"][kind],
            w_down=res["w_down"][kind], g_mlp_post=rg_mlp_post, g_ple_pre=rg_ple_pre, w_ple_gate=res["w_ple_gate"][kind],
            w_ple_proj=proj, g_ple_post=rg_ple_post))
    names = ("g_mix_pre", "w_in", "b_forget", "w_conf_dw", "conf_ln_g", "conf_ln_b", "w_conf_pw", "w_sc", "w_pool", "pool_scale",
             "w_out", "g_mix_post", "g_mlp_pre", "w_up", "w_down", "g_mlp_post", "g_ple_pre", "w_ple_gate", "w_ple_proj", "g_ple_post")
    outs = [loss, grad_x]
    for kind in range(4):
        outs += [res["by_kind"][kind][nm] for nm in names]
    return tuple(outs)
```

```python
import jax
import jax.numpy as jnp
from jax import lax
from jax.experimental import pallas as pl
from jax.experimental.pallas import tpu as pltpu

F32, BF16 = jnp.float32, jnp.bfloat16

D = 1024
DG = 256
N_HEADS = 4
HEAD_DIM = 64
CONF_K = 31
SC_K = 3
POOL_WINDOWS = (2, 4, 8, 16)
D_FF = 4096
D_PLE = 256
N_LAYERS = 4
N_DEV = 8
EPS = 1e-6
SCALE = HEAD_DIM ** -0.5
W_MAIN = 2304
F_LO, F_HI = 1280, 1284

ADAM_LR, ADAM_B1, ADAM_B2, ADAM_EPS, ADAM_WD, ADAM_STEP = 0.001, 0.9, 0.999, 1e-08, 0.01, 10

TB = 512
HALO = 32
LANES = 128
FF_BLK = D_FF // N_DEV
VMEM_LIMIT = 56 * 1024 * 1024

NT_DIMS = (((1,), (1,)), ((), ()))
TN_DIMS = (((0,), (0,)), ((), ()))
MESH_ID = pl.DeviceIdType.MESH


def _pcall(body, **kw):
    return pl.pallas_call(body, **kw)


def _cparams(n_axes):
    return pltpu.CompilerParams(dimension_semantics=("arbitrary",) * n_axes, vmem_limit_bytes=VMEM_LIMIT)


def _sds(shape, dtype):
    return jax.ShapeDtypeStruct(shape, dtype)


def _tok(width, tb=TB):
    return pl.BlockSpec((tb, width), lambda i: (i, 0))


def _tokcol(width, col):
    return pl.BlockSpec((TB, width), lambda i: (i, col))


def _full(shape):
    zeros = (0,) * len(shape)
    return pl.BlockSpec(shape, lambda *_: zeros)


def _resident2(shape):
    zeros = (0,) * len(shape)
    return pl.BlockSpec(shape, lambda i, j: zeros, pipeline_mode=pl.Buffered(1))


def _halo_prev(width, col=0):
    return pl.BlockSpec((HALO, width), lambda i: (jnp.maximum(i * (TB // HALO) - 1, 0), col))


def _halo_next(width, n_rows, col=0):
    last = n_rows // HALO - 1
    return pl.BlockSpec((HALO, width), lambda i: (jnp.minimum((i + 1) * (TB // HALO), last), col))


def _dot(a, b):
    return jnp.dot(a, b, preferred_element_type=F32)


def _dot_nt(a, b):
    return lax.dot_general(a, b, NT_DIMS, preferred_element_type=F32)


def _dot_tn(a, b):
    return lax.dot_general(a, b, TN_DIMS, preferred_element_type=F32)


def _dot_exact(a, b):
    return jnp.dot(a, b, precision=lax.Precision.HIGHEST, preferred_element_type=F32)


def _rms(x, g):
    r = lax.rsqrt(jnp.mean(x * x, axis=-1, keepdims=True) + EPS)
    return x * r * g


def _rms_bwd(x, g, dy):
    r = lax.rsqrt(jnp.mean(x * x, axis=-1, keepdims=True) + EPS)
    n = x * r
    dg = jnp.sum(dy * n, axis=0, keepdims=True)
    dn = dy * g
    dx = r * (dn - n * jnp.mean(dn * n, axis=-1, keepdims=True))
    return dx, dg, n * g


def _sigmoid(x):
    return jax.nn.sigmoid(x)


def _log_sigmoid(x):
    return jnp.minimum(x, 0.0) - jnp.log(1.0 + jnp.exp(-jnp.abs(x)))


def _lane_group_select(lane, v2, v4, v8, v16):
    return jnp.where(lane < 64, v2, jnp.where(lane < 128, v4, jnp.where(lane < 192, v8, v16)))


def _pool_counts(t0, rows):
    lane = lax.broadcasted_iota(jnp.int32, (rows, DG), 1)
    t = lax.broadcasted_iota(jnp.int32, (rows, DG), 0) + t0
    win = _lane_group_select(lane, 2, 4, 8, 16)
    return jnp.minimum(t + 1, win).astype(F32), lane


def _mixin_fwd(h, g, win, wf):
    t_len = h.shape[0]

    def body(h_ref, g_ref, win_ref, wf_ref, zc_ref, qkv_ref, fl_ref):
        xn = _rms(h_ref[...], g_ref[...]).astype(BF16)
        z = _dot(xn, win_ref[...].reshape(D, W_MAIN))
        zc_ref[:, 0:512] = z[:, 0:512]
        zc_ref[:, 512:1536] = z[:, 1280:2304]
        qkv_ref[:, 0:256] = (z[:, 512:768] * SCALE).astype(BF16)
        qkv_ref[:, 256:768] = z[:, 768:1280].astype(BF16)
        fl_ref[...] = _dot(xn, wf_ref[...])

    return _pcall(
        body, name="mixin_fwd", grid=(t_len // TB,),
        in_specs=[_tok(D), _full((1, D)), _full((N_DEV, D // N_DEV, W_MAIN)), _full((D, LANES))],
        out_specs=[_tok(1536), _tok(768), _tok(LANES)],
        out_shape=[_sds((t_len, 1536), F32), _sds((t_len, 768), BF16), _sds((t_len, LANES), F32)],
        compiler_params=_cparams(1),
    )(h, g, win, wf)


def _transpose_lanes8(x):
    eye = (lax.broadcasted_iota(jnp.int32, (8, LANES), 0) == lax.broadcasted_iota(jnp.int32, (8, LANES), 1)).astype(F32)
    return lax.dot_general(eye, x, NT_DIMS, precision=lax.Precision.HIGHEST, preferred_element_type=F32)


def _cumsum_fwd(fl, b_row):
    t_len = fl.shape[0]

    def body(fl_ref, b_ref, c_ref, ct_ref, carry):
        @pl.when(pl.program_id(0) == 0)
        def _():
            carry[...] = jnp.zeros_like(carry)

        r = lax.broadcasted_iota(jnp.int32, (TB, TB), 0)
        s = lax.broadcasted_iota(jnp.int32, (TB, TB), 1)
        lf = _log_sigmoid(fl_ref[...] + b_ref[...])
        c = _dot_exact((r >= s).astype(F32), lf) + carry[0:1, :]
        c_ref[...] = c
        ct_ref[...] = _transpose_lanes8(c)
        carry[...] += jnp.sum(lf, axis=0, keepdims=True)

    return _pcall(
        body, name="cumsum_fwd", grid=(t_len // TB,),
        in_specs=[_tok(LANES), _full((1, LANES))],
        out_specs=[_tok(LANES), pl.BlockSpec((8, TB), lambda i: (0, i))],
        out_shape=[_sds((t_len, LANES), F32), _sds((8, t_len), F32)],
        scratch_shapes=[pltpu.VMEM((8, LANES), F32)],
        compiler_params=_cparams(1),
    )(fl, b_row)


def _layer_norm_parts(y, g, b):
    mu = jnp.mean(y, axis=-1, keepdims=True)
    yc = y - mu
    r = lax.rsqrt(jnp.mean(yc * yc, axis=-1, keepdims=True) + EPS)
    n = yc * r
    return n, r, n * g + b


def _pool_window_sums(p0, p1, p2, p3):
    e = HALO + TB
    p1[8:e, :] = p0[pl.ds(8, e - 8), :] + p0[pl.ds(7, e - 8), :]
    p2[16:e, :] = p1[pl.ds(16, e - 16), :] + p1[pl.ds(14, e - 16), :]
    p3[24:e, :] = p2[pl.ds(24, e - 24), :] + p2[pl.ds(20, e - 24), :]
    s16 = p3[pl.ds(HALO, TB), :] + p3[pl.ds(HALO - 8, TB), :]
    return p1[pl.ds(HALO, TB), :], p2[pl.ds(HALO, TB), :], p3[pl.ds(HALO, TB), :], s16


def _branch_fwd(zc, w_dw, ln_g, ln_b, w_pw, w_sc, w_pool, pool_scale):
    t_len = zc.shape[0]
    e = HALO + TB

    def body(z_ref, zh_ref, dw_ref, g_ref, b_ref, pw_ref, sc_ref, pool_ref, ps_ref, cat_ref, u_s, ch_s, p0, p1, p2, p3):
        i = pl.program_id(0)
        hm = (i > 0).astype(F32)
        u_s[0:HALO, :] = zh_ref[:, 0:256] * _sigmoid(zh_ref[:, 256:512]) * hm
        u_s[HALO:e, :] = z_ref[:, 0:256] * _sigmoid(z_ref[:, 256:512])
        y = jnp.zeros((TB, DG), F32)
        for k in range(CONF_K):
            y = y + dw_ref[k:k + 1, :] * u_s[pl.ds(HALO - (CONF_K - 1) + k, TB), :]
        _, _, yn = _layer_norm_parts(y, g_ref[...], b_ref[...])
        s = yn * _sigmoid(yn)
        cat_ref[:, 0:256] = _dot(s.astype(BF16), pw_ref[...])
        ch_s[0:HALO, :] = zh_ref[:, 1024:1280] * zh_ref[:, 512:768] * hm
        ch_s[HALO:e, :] = z_ref[:, 1024:1280] * z_ref[:, 512:768]
        cv = jnp.zeros((TB, DG), F32)
        for k in range(SC_K):
            cv = cv + sc_ref[k:k + 1, :] * ch_s[pl.ds(HALO - (SC_K - 1) + k, TB), :]
        cat_ref[:, 256:512] = z_ref[:, 768:1024] * cv
        p0[0:HALO, :] = zh_ref[:, 1280:1536] * hm
        p0[HALO:e, :] = z_ref[:, 1280:1536]
        s2, s4, s8, s16 = _pool_window_sums(p0, p1, p2, p3)
        cnt, lane = _pool_counts(i * TB, TB)
        dlt = _lane_group_select(lane, s2, s4, s8, s16) / cnt - z_ref[:, 1280:1536]
        cat_ref[:, 512:768] = _dot(dlt.astype(BF16), pool_ref[...]) * ps_ref[...]

    scr = [pltpu.VMEM((e, DG), F32) for _ in range(6)]
    return _pcall(
        body, name="branch_fwd", grid=(t_len // TB,),
        in_specs=[_tok(1536), _halo_prev(1536), _full((32, DG)), _full((1, DG)), _full((1, DG)), _full((DG, DG)),
                  _full((8, DG)), _full((DG, DG)), _full((1, DG))],
        out_specs=_tok(768), out_shape=_sds((t_len, 768), F32), scratch_shapes=scr, compiler_params=_cparams(1),
    )(zc, zc, w_dw, ln_g, ln_b, w_pw, w_sc, w_pool, pool_scale)


def _head_masks(rows):
    lane = lax.broadcasted_iota(jnp.int32, (rows, LANES), 1)
    return lane, (lane < HEAD_DIM, lane >= HEAD_DIM)


def _keep_lanes(mask, x):
    return jnp.where(mask, x.astype(F32), 0.0).astype(BF16)


def _with_exchange(refs, n_in, n_out, n_x, per_dest, first, last):
    ins, x_src = refs[:n_in], refs[n_in:n_in + n_x]
    outs, x_dst = refs[n_in + n_x:n_in + n_x + n_out], refs[n_in + n_x + n_out:n_in + 2 * n_x + n_out]
    begin = finish = None
    if n_x:
        start, wait = _exchange_ops(x_src, x_dst, [per_dest] * n_x, *refs[n_in + 2 * n_x + n_out:])

        def begin():
            pl.when(first)(start)

        def finish():
            pl.when(last)(wait)

    return ins, outs, begin, finish


def _attn_fwd(qkv, c, ct, bcast=()):
    t_len = qkv.shape[0]
    n_t = t_len // TB
    n_x = len(bcast)

    def body(*refs):
        i = pl.program_id(0)
        (q_ref, k_ref, v_ref, c_ref, ct_ref), (o_ref, lset_ref), begin, finish = _with_exchange(
            refs, 5, 2, n_x, False, i == 0, i == n_t - 1)
        if begin:
            begin()
        lane, halves = _head_masks(TB)
        crow = c_ref[...]
        causal = lax.broadcasted_iota(jnp.int32, (TB, TB), 0) >= lax.broadcasted_iota(jnp.int32, (TB, TB), 1)
        lse_out = jnp.zeros((TB, LANES), F32)
        for g in range(2):
            cols = slice(g * LANES, (g + 1) * LANES)
            qg = q_ref[:, cols]
            qms = [_keep_lanes(halves[hh], qg) for hh in range(2)]
            cqs = [jnp.sum(jnp.where(lane == 2 * g + hh, crow, 0.0), axis=1, keepdims=True) for hh in range(2)]

            def block(j, carry, masked):
                off = pl.multiple_of(j * TB, TB)
                kj = k_ref[pl.ds(off, TB), cols]
                vj = v_ref[pl.ds(off, TB), cols]
                new = []
                for hh in range(2):
                    m, l, acc = carry[hh]
                    s = _dot_nt(qms[hh], kj) + (cqs[hh] - ct_ref[2 * g + hh:2 * g + hh + 1, pl.ds(off, TB)])
                    if masked:
                        s = jnp.where(causal, s, -jnp.inf)
                    m_new = jnp.maximum(m, jnp.max(s, axis=1, keepdims=True))
                    alpha = jnp.exp(m - m_new)
                    p = jnp.exp(s - m_new)
                    l = alpha * l + jnp.sum(p, axis=1, keepdims=True)
                    acc = alpha * acc + _dot(p.astype(BF16), vj)
                    new.append((m_new, l, acc))
                return tuple(new)

            init = tuple((jnp.full((TB, 1), -jnp.inf, F32), jnp.zeros((TB, 1), F32), jnp.zeros((TB, LANES), F32))
                         for _ in range(2))
            carry = lax.fori_loop(0, i, lambda j, cr: block(j, cr, False), init)
            (m0, l0, acc0), (m1, l1, acc1) = block(i, carry, True)
            o_ref[:, cols] = jnp.where(halves[0], acc0 / l0, acc1 / l1)
            lse_out = jnp.where(lane == 2 * g, m0 + jnp.log(l0), lse_out)
            lse_out = jnp.where(lane == 2 * g + 1, m1 + jnp.log(l1), lse_out)
        lset_ref[...] = _transpose_lanes8(lse_out)
        if finish:
            finish()

    return _pcall(
        body, name="attn_fwd_gather" if n_x else "attn_fwd", grid=(n_t,),
        in_specs=[_tokcol(DG, 0), pl.BlockSpec((t_len, DG), lambda i: (0, 1)), pl.BlockSpec((t_len, DG), lambda i: (0, 2)),
                  _tok(LANES), _full((8, t_len))] + [_HBM] * n_x,
        out_specs=[_tok(DG), pl.BlockSpec((8, TB), lambda i: (0, i))] + [_HBM] * n_x,
        out_shape=[_sds((t_len, DG), F32), _sds((8, t_len), F32)] + _exchange_shapes(bcast, [False] * n_x),
        scratch_shapes=_exchange_scratch(n_x) if n_x else [],
        compiler_params=_cparams(1),
    )(qkv, qkv, qkv, c, ct, *bcast)


def _mixout_fwd(h, cat3, o, w_rows, g):
    t_len = h.shape[0]

    def body(h_ref, cat_ref, o_ref, w_ref, g_ref, mix_ref, h1_ref):
        w = w_ref[...].reshape(D, D)
        mix = (_dot(cat_ref[:, 0:256].astype(BF16), w[0:256]) + _dot(o_ref[...].astype(BF16), w[256:512])
               + _dot(cat_ref[:, 256:768].astype(BF16), w[512:1024]))
        mix_ref[...] = mix
        h1_ref[...] = h_ref[...] + _rms(mix, g_ref[...])

    return _pcall(
        body, name="mixout_fwd", grid=(t_len // TB,),
        in_specs=[_tok(D), _tok(768), _tok(DG), pl.BlockSpec((N_DEV, 128, D), lambda i: (0, 4, 0)), _full((1, D))],
        out_specs=[_tok(D), _tok(D)], out_shape=[_sds((t_len, D), F32)] * 2, compiler_params=_cparams(1),
    )(h, cat3, o, w_rows, g)


def _mlp_fwd(h, g_pre, w_up, w_rows, g_post):
    t_len = h.shape[0]

    def body(h_ref, g1_ref, up_ref, dn_ref, g2_ref, u_ref, ff_ref, h2_ref, hn_s, acc_s):
        j = pl.program_id(1)

        @pl.when(j == 0)
        def _():
            hn_s[...] = _rms(h_ref[...], g1_ref[...]).astype(BF16)
            acc_s[...] = jnp.zeros_like(acc_s)

        u = _dot(hn_s[...], up_ref[j])
        u_ref[...] = u
        r = jnp.maximum(u, 0.0)
        acc_s[...] += _dot((r * r).astype(BF16), dn_ref[j])

        @pl.when(j == N_DEV - 1)
        def _():
            ff = acc_s[...]
            ff_ref[...] = ff
            h2_ref[...] = h_ref[...] + _rms(ff, g2_ref[...])

    tok2 = pl.BlockSpec((TB, D), lambda i, j: (i, 0))
    vec2 = pl.BlockSpec((1, D), lambda i, j: (0, 0))
    return _pcall(
        body, name="mlp_fwd", grid=(t_len // TB, N_DEV),
        in_specs=[tok2, vec2, _resident2((N_DEV, D, FF_BLK)), _resident2((N_DEV, FF_BLK, D)), vec2],
        out_specs=[pl.BlockSpec((TB, FF_BLK), lambda i, j: (i, j)), tok2, tok2],
        out_shape=[_sds((t_len, D_FF), F32), _sds((t_len, D), F32), _sds((t_len, D), F32)],
        scratch_shapes=[pltpu.VMEM((TB, D), BF16), pltpu.VMEM((TB, D), F32)], compiler_params=_cparams(2),
    )(h, g_pre, w_up, w_rows, g_post)


def _ple_fwd(h, p_all, layer, g_pre, w_rows, w_proj, g_post):
    t_len = h.shape[0]
    n_t = t_len // TB

    def body(h_ref, p_ref, g1_ref, wg_ref, wp_ref, g2_ref, pp_ref, gate_ref, h3_ref):
        hn = _rms(h_ref[...], g1_ref[...]).astype(BF16)
        gate = _sigmoid(_dot(hn, wg_ref[...].reshape(D, D)))
        pp = _dot(p_ref[...].astype(BF16), wp_ref[...])
        pp_ref[...] = pp
        gate_ref[...] = gate
        h3_ref[...] = h_ref[...] + _rms(pp * gate, g2_ref[...])

    return _pcall(
        body, name="ple_fwd", grid=(n_t,),
        in_specs=[_tok(D), pl.BlockSpec((TB, D_PLE), lambda i: (layer * n_t + i, 0)), _full((1, D)),
                  pl.BlockSpec((N_DEV, 128, D), lambda i: (0, 5, 0)), _full((D_PLE, D)), _full((1, D))],
        out_specs=[_tok(D)] * 3, out_shape=[_sds((t_len, D), F32)] * 3, compiler_params=_cparams(1),
    )(h, p_all, g_pre, w_rows, w_proj, g_post)


def _loss_bwd(h, target):
    t_len = h.shape[0]

    def body(h_ref, t_ref, dh_ref, loss_ref):
        @pl.when(pl.program_id(0) == 0)
        def _():
            loss_ref[...] = jnp.zeros_like(loss_ref)

        d = h_ref[...] - t_ref[...]
        dh_ref[...] = d * (1.0 / D)
        loss_ref[...] += 0.5 * jnp.sum(jnp.mean(d * d, axis=-1, keepdims=True), axis=0, keepdims=True)

    return _pcall(
        body, name="loss_bwd", grid=(t_len // TB,), in_specs=[_tok(D), _tok(D)],
        out_specs=[_tok(D), _full((8, LANES))], out_shape=[_sds((t_len, D), F32), _sds((8, LANES), F32)],
        compiler_params=_cparams(1),
    )(h, target)


def _acc_init(refs):
    @pl.when(pl.program_id(0) == 0)
    def _():
        for r in refs:
            r[...] = jnp.zeros_like(r)


def _ple_bwd(dh3, h2, pp, gate, g_post, g_pre, w_rows):
    t_len = dh3.shape[0]

    def body(dh_ref, h_ref, pp_ref, gate_ref, g2_ref, g1_ref, wg_ref, dh2_ref, dpp_ref, dpre_ref, hn_ref, dg2_ref, dg1_ref):
        _acc_init([dg2_ref, dg1_ref])
        dh = dh_ref[...]
        pp, gate = pp_ref[...], gate_ref[...]
        de, dg2, _ = _rms_bwd(pp * gate, g2_ref[...], dh)
        dg2_ref[...] += dg2
        dpp_ref[...] = (de * gate).astype(BF16)
        dpre = (de * pp * gate * (1.0 - gate)).astype(BF16)
        dpre_ref[...] = dpre
        dhn = _dot_nt(dpre, wg_ref[...].reshape(D, D))
        dx, dg1, hn = _rms_bwd(h_ref[...], g1_ref[...], dhn)
        dg1_ref[...] += dg1
        hn_ref[...] = hn.astype(BF16)
        dh2_ref[...] = dh + dx

    return _pcall(
        body, name="ple_bwd", grid=(t_len // TB,),
        in_specs=[_tok(D)] * 4 + [_full((1, D)), _full((1, D)), pl.BlockSpec((N_DEV, 128, D), lambda i: (0, 5, 0))],
        out_specs=[_tok(D)] * 4 + [_full((1, D))] * 2,
        out_shape=[_sds((t_len, D), F32)] + [_sds((t_len, D), BF16)] * 3 + [_sds((1, D), F32)] * 2,
        compiler_params=_cparams(1),
    )(dh3, h2, pp, gate, g_post, g_pre, w_rows)


def _mlp_bwd(dh2, h1, u, ff, g_post, g_pre, w_up, w_rows):
    t_len = dh2.shape[0]

    def body(dh_ref, h_ref, u_ref, ff_ref, g2_ref, g1_ref, up_ref, dn_ref,
             dh1_ref, a2_ref, du_ref, dff_ref, hn_ref, dg2_ref, dg1_ref, dff_s, acc_s):
        i, j = pl.program_id(0), pl.program_id(1)

        @pl.when((i == 0) & (j == 0))
        def _():
            dg2_ref[...] = jnp.zeros_like(dg2_ref)
            dg1_ref[...] = jnp.zeros_like(dg1_ref)

        @pl.when(j == 0)
        def _():
            dff, dg2, _ = _rms_bwd(ff_ref[...], g2_ref[...], dh_ref[...])
            dg2_ref[...] += dg2
            dff_s[...] = dff.astype(BF16)
            dff_ref[...] = dff.astype(BF16)
            acc_s[...] = jnp.zeros_like(acc_s)

        r = jnp.maximum(u_ref[...], 0.0)
        a2_ref[...] = (r * r).astype(BF16)
        du = (_dot_nt(dff_s[...], dn_ref[j]) * (2.0 * r)).astype(BF16)
        du_ref[...] = du
        acc_s[...] += _dot_nt(du, up_ref[j])

        @pl.when(j == N_DEV - 1)
        def _():
            dx, dg1, hn = _rms_bwd(h_ref[...], g1_ref[...], acc_s[...])
            dg1_ref[...] += dg1
            hn_ref[...] = hn.astype(BF16)
            dh1_ref[...] = dh_ref[...] + dx

    tok2 = pl.BlockSpec((TB, D), lambda i, j: (i, 0))
    vec2 = pl.BlockSpec((1, D), lambda i, j: (0, 0))
    blk2 = pl.BlockSpec((TB, FF_BLK), lambda i, j: (i, j))
    return _pcall(
        body, name="mlp_bwd", grid=(t_len // TB, N_DEV),
        in_specs=[tok2, tok2, blk2, tok2, vec2, vec2, _resident2((N_DEV, D, FF_BLK)), _resident2((N_DEV, FF_BLK, D))],
        out_specs=[tok2, blk2, blk2, tok2, tok2, vec2, vec2],
        out_shape=[_sds((t_len, D), F32), _sds((t_len, D_FF), BF16), _sds((t_len, D_FF), BF16), _sds((t_len, D), BF16),
                   _sds((t_len, D), BF16), _sds((1, D), F32), _sds((1, D), F32)],
        scratch_shapes=[pltpu.VMEM((TB, D), BF16), pltpu.VMEM((TB, D), F32)], compiler_params=_cparams(2),
    )(dh2, h1, u, ff, g_post, g_pre, w_up, w_rows)


def _mixout_bwd(dh1, mix, cat3, o, g, w_rows):
    t_len = dh1.shape[0]

    def body(dh_ref, mix_ref, cat_ref, o_ref, g_ref, w_ref, dcat_ref, dmix_ref, catb_ref, dg_ref):
        _acc_init([dg_ref])
        dmix, dg, _ = _rms_bwd(mix_ref[...], g_ref[...], dh_ref[...])
        dg_ref[...] += dg
        dmix = dmix.astype(BF16)
        dmix_ref[...] = dmix
        dcat_ref[...] = _dot_nt(dmix, w_ref[...].reshape(D, D))
        catb_ref[:, 0:256] = cat_ref[:, 0:256].astype(BF16)
        catb_ref[:, 256:512] = o_ref[...].astype(BF16)
        catb_ref[:, 512:1024] = cat_ref[:, 256:768].astype(BF16)

    return _pcall(
        body, name="mixout_bwd", grid=(t_len // TB,),
        in_specs=[_tok(D), _tok(D), _tok(768), _tok(DG), _full((1, D)), pl.BlockSpec((N_DEV, 128, D), lambda i: (0, 4, 0))],
        out_specs=[_tok(D), _tok(D), _tok(D), _full((1, D))],
        out_shape=[_sds((t_len, D), F32), _sds((t_len, D), BF16), _sds((t_len, D), BF16), _sds((1, D), F32)],
        compiler_params=_cparams(1),
    )(dh1, mix, cat3, o, g, w_rows)


def _attn_bwd_dsum(qkv, dcat, c, ct, lset):
    t_len = qkv.shape[0]

    def body(q_ref, do_ref, k_ref, v_ref, c_ref, ct_ref, lset_ref, dt_ref, dob_ref):
        i = pl.program_id(0)
        lane, halves = _head_masks(TB)
        causal_t = lax.broadcasted_iota(jnp.int32, (TB, TB), 1) >= lax.broadcasted_iota(jnp.int32, (TB, TB), 0)
        sub = lax.broadcasted_iota(jnp.int32, (8, TB), 0)
        dob_ref[...] = do_ref[...].astype(BF16)
        out = jnp.zeros((8, TB), F32)
        for g in range(2):
            cols = slice(g * LANES, (g + 1) * LANES)
            qi = q_ref[:, cols]
            doi = do_ref[:, cols].astype(BF16)
            doms = [_keep_lanes(halves[hh], doi) for hh in range(2)]
            cqs = [ct_ref[2 * g + hh:2 * g + hh + 1, :] for hh in range(2)]
            lses = [lset_ref[2 * g + hh:2 * g + hh + 1, :] for hh in range(2)]

            def block(j, accs, masked):
                off = pl.multiple_of(j * TB, TB)
                kj = k_ref[pl.ds(off, TB), cols]
                vj = v_ref[pl.ds(off, TB), cols]
                cj = c_ref[pl.ds(off, TB), :]
                new = []
                for hh in range(2):
                    ck = jnp.sum(jnp.where(lane == 2 * g + hh, cj, 0.0), axis=1, keepdims=True)
                    st = _dot_nt(_keep_lanes(halves[hh], kj), qi) + (cqs[hh] - ck)
                    if masked:
                        st = jnp.where(causal_t, st, -jnp.inf)
                    pt = jnp.exp(st - lses[hh])
                    new.append(accs[hh] + jnp.sum(pt * _dot_nt(vj, doms[hh]), axis=0, keepdims=True))
                return tuple(new)

            init = (jnp.zeros((1, TB), F32), jnp.zeros((1, TB), F32))
            accs = block(i, lax.fori_loop(0, i, lambda j, cr: block(j, cr, False), init), True)
            out = jnp.where(sub == 2 * g, accs[0], out)
            out = jnp.where(sub == 2 * g + 1, accs[1], out)
        dt_ref[...] = out

    row8 = pl.BlockSpec((8, TB), lambda i: (0, i))
    return _pcall(
        body, name="attn_bwd_dsum", grid=(t_len // TB,),
        in_specs=[_tokcol(DG, 0), _tokcol(DG, 1), pl.BlockSpec((t_len, DG), lambda i: (0, 1)),
                  pl.BlockSpec((t_len, DG), lambda i: (0, 2)), _full((t_len, LANES)), row8, row8],
        out_specs=[row8, _tok(DG)], out_shape=[_sds((8, t_len), F32), _sds((t_len, DG), BF16)],
        compiler_params=_cparams(1),
    )(qkv, dcat, qkv, qkv, c, ct, lset)


def _attn_bwd(qkv, dob, c, ct, lset, dt, xchg=()):
    t_len = qkv.shape[0]
    n_q = t_len // TB
    n_x = len(xchg)

    def body(*refs):
        j = pl.program_id(0)
        ins, outs, begin, finish = _with_exchange(refs, 8, 4, n_x, True, j == 0, j == n_q - 1)
        q_ref, dob_ref, k_ref, v_ref, c_ref, ct_ref, lset_ref, dt_ref = ins
        dq_ref, dk_ref, dv_ref, dc_ref = outs
        if begin:
            begin()

        @pl.when(j == 0)
        def _():
            dq_ref[...] = jnp.zeros_like(dq_ref)

        lane, halves = _head_masks(TB)
        crow = c_ref[...]
        causal_t = lax.broadcasted_iota(jnp.int32, (TB, TB), 1) >= lax.broadcasted_iota(jnp.int32, (TB, TB), 0)
        dc_out = jnp.zeros((TB, LANES), F32)
        for g in range(2):
            cols = slice(g * LANES, (g + 1) * LANES)
            kg, vg = k_ref[:, cols], v_ref[:, cols]
            kms = [_keep_lanes(halves[hh], kg) for hh in range(2)]
            cks = [jnp.sum(jnp.where(lane == 2 * g + hh, crow, 0.0), axis=1, keepdims=True) for hh in range(2)]

            def block(i, carry, masked):
                dk, dv, dcs = carry
                off = pl.multiple_of(i * TB, TB)
                qi = q_ref[pl.ds(off, TB), cols]
                doi = dob_ref[pl.ds(off, TB), cols]
                dq_add = jnp.zeros((TB, LANES), F32)
                dcs_new = []
                for hh in range(2):
                    h = 2 * g + hh
                    dom = _keep_lanes(halves[hh], doi)
                    st = _dot_nt(kms[hh], qi) + (ct_ref[h:h + 1, pl.ds(off, TB)] - cks[hh])
                    if masked:
                        st = jnp.where(causal_t, st, -jnp.inf)
                    pt = jnp.exp(st - lset_ref[h:h + 1, pl.ds(off, TB)])
                    dv = dv + _dot(pt.astype(BF16), dom)
                    dst = pt * (_dot_nt(vg, dom) - dt_ref[h:h + 1, pl.ds(off, TB)])
                    dsb = dst.astype(BF16)
                    dk = dk + _dot(dsb, _keep_lanes(halves[hh], qi))
                    dcs_new.append(dcs[hh] + jnp.sum(dst, axis=1, keepdims=True))
                    dq_add = dq_add + _dot_tn(dsb, kms[hh])
                dq_ref[pl.ds(off, TB), cols] += dq_add
                return dk, dv, tuple(dcs_new)

            init = (jnp.zeros((TB, LANES), F32), jnp.zeros((TB, LANES), F32),
                    (jnp.zeros((TB, 1), F32), jnp.zeros((TB, 1), F32)))
            carry = block(j, init, True)
            dk, dv, dcs = lax.fori_loop(j + 1, n_q, lambda i, cr: block(i, cr, False), carry)
            dk_ref[:, cols] = dk
            dv_ref[:, cols] = dv
            dc_out = jnp.where(lane == 2 * g, -dcs[0], dc_out)
            dc_out = jnp.where(lane == 2 * g + 1, -dcs[1], dc_out)
        dc_ref[...] = dc_out
        if finish:
            finish()

    return _pcall(
        body, name="attn_bwd_exchange" if n_x else "attn_bwd", grid=(n_q,),
        in_specs=[pl.BlockSpec((t_len, DG), lambda i: (0, 0)), _full((t_len, DG)), _tokcol(DG, 1), _tokcol(DG, 2),
                  _tok(LANES), _full((8, t_len)), _full((8, t_len)), _full((8, t_len))] + [_HBM] * n_x,
        out_specs=[_full((t_len, DG)), _tok(DG), _tok(DG), _tok(LANES)] + [_HBM] * n_x,
        out_shape=[_sds((t_len, DG), F32), _sds((t_len, DG), F32), _sds((t_len, DG), F32), _sds((t_len, LANES), F32)]
        + _exchange_shapes(xchg, [True] * n_x),
        scratch_shapes=_exchange_scratch(n_x) if n_x else [],
        compiler_params=_cparams(1),
    )(qkv, dob, qkv, qkv, c, ct, lset, dt, *xchg)


def _forget_bwd(dc, fl, b_row):
    t_len = dc.shape[0]
    n_t = t_len // TB
    rev = pl.BlockSpec((TB, LANES), lambda i: (n_t - 1 - i, 0))

    def body(dc_ref, fl_ref, b_ref, dfl_ref, db_ref, carry):
        @pl.when(pl.program_id(0) == 0)
        def _():
            carry[...] = jnp.zeros_like(carry)
            db_ref[...] = jnp.zeros_like(db_ref)

        r = lax.broadcasted_iota(jnp.int32, (TB, TB), 0)
        s = lax.broadcasted_iota(jnp.int32, (TB, TB), 1)
        dc = dc_ref[...]
        dl = _dot_exact((r <= s).astype(F32), dc) + carry[0:1, :]
        carry[...] += jnp.sum(dc, axis=0, keepdims=True)
        dfl = dl * _sigmoid(-(fl_ref[...] + b_ref[...]))
        dfl_ref[...] = dfl
        db_ref[...] += jnp.sum(dfl, axis=0, keepdims=True)

    return _pcall(
        body, name="forget_bwd", grid=(n_t,), in_specs=[rev, rev, _full((1, LANES))],
        out_specs=[rev, _full((1, LANES))], out_shape=[_sds((t_len, LANES), F32), _sds((1, LANES), F32)],
        scratch_shapes=[pltpu.VMEM((8, LANES), F32)], compiler_params=_cparams(1),
    )(dc, fl, b_row)


def _branch_bwd(zc, dcat, dq, dk, dv, w_dw, ln_g, ln_b, w_pw, w_sc, w_pool, pool_scale):
    t_len = zc.shape[0]
    n_t = t_len // TB
    e2 = HALO + TB + HALO
    e1 = TB + HALO

    def body(z_ref, zp_ref, zn_ref, dcf_ref, dcfn_ref, dsp_ref, dspn_ref, dq_ref, dk_ref, dv_ref,
             dw_ref, g_ref, b_ref, pw_ref, sc_ref, pool_ref, ps_ref,
             dz_ref, ddw_ref, dg_ref, db_ref, dpw_ref, dsc_ref, dpool_ref, dps_ref,
             u_s, dy_s, ch_s, dcv_s, p0, p1, p2, p3, g0, g1, g2, g3):
        i = pl.program_id(0)
        _acc_init([ddw_ref, dg_ref, db_ref, dpw_ref, dsc_ref, dpool_ref, dps_ref])
        hm = (i > 0).astype(F32)
        nm = (i < n_t - 1).astype(F32)

        sig_b = _sigmoid(z_ref[:, 256:512])
        a = z_ref[:, 0:256]
        u_s[0:HALO, :] = zp_ref[:, 0:256] * _sigmoid(zp_ref[:, 256:512]) * hm
        u_s[HALO:HALO + TB, :] = a * sig_b
        u_s[HALO + TB:e2, :] = zn_ref[:, 0:256] * _sigmoid(zn_ref[:, 256:512])
        y = jnp.zeros((e1, DG), F32)
        for k in range(CONF_K):
            y = y + dw_ref[k:k + 1, :] * u_s[pl.ds(HALO - (CONF_K - 1) + k, e1), :]
        n, r, yn = _layer_norm_parts(y, g_ref[...], b_ref[...])
        sg = _sigmoid(yn)
        dyc = jnp.concatenate([dcf_ref[...], dcfn_ref[...] * nm], axis=0)
        ds = _dot_nt(dyc.astype(BF16), pw_ref[...])
        dyn = ds * sg * (1.0 + yn * (1.0 - sg))
        dg_ref[...] += jnp.sum((dyn * n)[0:TB], axis=0, keepdims=True)
        db_ref[...] += jnp.sum(dyn[0:TB], axis=0, keepdims=True)
        dn = dyn * g_ref[...]
        dyv = r * (dn - jnp.mean(dn, axis=-1, keepdims=True) - n * jnp.mean(dn * n, axis=-1, keepdims=True))
        dpw_ref[...] += _dot_tn((yn * sg)[0:TB].astype(BF16), dcf_ref[...].astype(BF16))
        dy_s[...] = dyv
        du = jnp.zeros((TB, DG), F32)
        dyv_t = dyv[0:TB]
        for k in range(CONF_K):
            du = du + dw_ref[k:k + 1, :] * dy_s[pl.ds(CONF_K - 1 - k, TB), :]
            ddw_ref[k:k + 1, :] += jnp.sum(dyv_t * u_s[pl.ds(HALO - (CONF_K - 1) + k, TB), :], axis=0, keepdims=True)
        dz_ref[:, 0:256] = (du * sig_b).astype(BF16)
        dz_ref[:, 256:512] = (du * a * sig_b * (1.0 - sig_b)).astype(BF16)

        dz_ref[:, 512:768] = (dq_ref[...] * SCALE).astype(BF16)
        dz_ref[:, 768:1024] = dk_ref[...].astype(BF16)
        dz_ref[:, 1024:1280] = dv_ref[...].astype(BF16)

        sc_h, sc_b, sc_c = z_ref[:, 512:768], z_ref[:, 768:1024], z_ref[:, 1024:1280]
        ch_s[0:HALO, :] = zp_ref[:, 1024:1280] * zp_ref[:, 512:768] * hm
        ch_s[HALO:HALO + TB, :] = sc_c * sc_h
        ch_s[HALO + TB:e2, :] = zn_ref[:, 1024:1280] * zn_ref[:, 512:768]
        cv = jnp.zeros((TB, DG), F32)
        for k in range(SC_K):
            cv = cv + sc_ref[k:k + 1, :] * ch_s[pl.ds(HALO - (SC_K - 1) + k, TB), :]
        dy_sc = dsp_ref[:, 0:256]
        dcv_t = dy_sc * sc_b
        dcv_s[0:TB, :] = dcv_t
        dcv_s[TB:e1, :] = dspn_ref[:, 0:256] * nm * zn_ref[:, 768:1024]
        dch = jnp.zeros((TB, DG), F32)
        for k in range(SC_K):
            dch = dch + sc_ref[k:k + 1, :] * dcv_s[pl.ds(SC_K - 1 - k, TB), :]
            dsc_ref[k:k + 1, :] += jnp.sum(dcv_t * ch_s[pl.ds(HALO - (SC_K - 1) + k, TB), :], axis=0, keepdims=True)
        dz_ref[:, 1280:1536] = (dch * sc_c).astype(BF16)
        dz_ref[:, 1536:1792] = (dy_sc * cv).astype(BF16)
        dz_ref[:, 1792:2048] = (dch * sc_h).astype(BF16)

        v_t = z_ref[:, 1280:1536]
        p0[0:HALO, :] = zp_ref[:, 1280:1536] * hm
        p0[HALO:HALO + TB, :] = v_t
        s2, s4, s8, s16 = _pool_window_sums(p0, p1, p2, p3)
        cnt, lane = _pool_counts(i * TB, e1)
        dlt = (_lane_group_select(lane[0:TB], s2, s4, s8, s16) / cnt[0:TB] - v_t).astype(BF16)
        dyp_t = dsp_ref[:, 256:512]
        dps_ref[...] += jnp.sum(dyp_t * _dot(dlt, pool_ref[...]), axis=0, keepdims=True)
        dpre = (jnp.concatenate([dyp_t, dspn_ref[:, 256:512] * nm], axis=0) * ps_ref[...]).astype(BF16)
        dpool_ref[...] += _dot_tn(dlt, dpre[0:TB])
        dd = _dot_nt(dpre, pool_ref[...])
        g0[...] = dd / cnt
        g1[0:TB + 24, :] = g0[pl.ds(0, TB + 24), :] + g0[pl.ds(1, TB + 24), :]
        g2[0:TB + 16, :] = g1[pl.ds(0, TB + 16), :] + g1[pl.ds(2, TB + 16), :]
        g3[0:TB + 8, :] = g2[pl.ds(0, TB + 8), :] + g2[pl.ds(4, TB + 8), :]
        f16 = g3[pl.ds(0, TB), :] + g3[pl.ds(8, TB), :]
        fwd_sum = _lane_group_select(lane[0:TB], g1[pl.ds(0, TB), :], g2[pl.ds(0, TB), :], g3[pl.ds(0, TB), :], f16)
        dz_ref[:, 2048:2304] = (fwd_sum - dd[0:TB]).astype(BF16)

    vec = _full((1, DG))
    mat = _full((DG, DG))
    scr = ([pltpu.VMEM((e2, DG), F32), pltpu.VMEM((e1, DG), F32), pltpu.VMEM((e2, DG), F32), pltpu.VMEM((e1, DG), F32)]
           + [pltpu.VMEM((HALO + TB, DG), F32)] * 4 + [pltpu.VMEM((e1, DG), F32)] * 4)
    return _pcall(
        body, name="branch_bwd", grid=(n_t,),
        in_specs=[_tok(1536), _halo_prev(1536), _halo_next(1536, t_len),
                  _tokcol(DG, 0), _halo_next(DG, t_len, 0), _tokcol(512, 1), _halo_next(512, t_len, 1),
                  _tok(DG), _tok(DG), _tok(DG),
                  _full((32, DG)), vec, vec, mat, _full((8, DG)), mat, vec],
        out_specs=[_tok(W_MAIN), _full((32, DG)), vec, vec, mat, _full((8, DG)), mat, vec],
        out_shape=[_sds((t_len, W_MAIN), BF16), _sds((32, DG), F32), _sds((1, DG), F32), _sds((1, DG), F32),
                   _sds((DG, DG), F32), _sds((8, DG), F32), _sds((DG, DG), F32), _sds((1, DG), F32)],
        scratch_shapes=scr, compiler_params=_cparams(1),
    )(zc, zc, zc, dcat, dcat, dcat, dcat, dq, dk, dv, w_dw, ln_g, ln_b, w_pw, w_sc, w_pool, pool_scale)


def _mixin_bwd(dh, h, g, dz, dfl, win, wf):
    t_len = dh.shape[0]

    def body(dh_ref, h_ref, g_ref, dz_ref, dfl_ref, win_ref, wf_ref, dh0_ref, xn_ref, dg_ref):
        _acc_init([dg_ref])
        dxn = _dot_nt(dz_ref[...], win_ref[...].reshape(D, W_MAIN)) + _dot_nt(dfl_ref[...].astype(BF16), wf_ref[...])
        dx, dg, xn = _rms_bwd(h_ref[...], g_ref[...], dxn)
        dg_ref[...] += dg
        xn_ref[...] = xn.astype(BF16)
        dh0_ref[...] = dh_ref[...] + dx

    return _pcall(
        body, name="mixin_bwd", grid=(t_len // TB,),
        in_specs=[_tok(D), _tok(D), _full((1, D)), _tok(W_MAIN), _tok(LANES), _full((N_DEV, D // N_DEV, W_MAIN)),
                  _full((D, LANES))],
        out_specs=[_tok(D), _tok(D), _full((1, D))],
        out_shape=[_sds((t_len, D), F32), _sds((t_len, D), BF16), _sds((1, D), F32)],
        compiler_params=_cparams(1),
    )(dh, h, g, dz, dfl, win, wf)


def _matmul_tn(name, a, b, tm, tn, out_dtype, block_major=False, a_section=0):
    t_len, n = b.shape
    m = a.shape[1]
    tk = min(t_len, 1024)
    n_k = t_len // tk

    def body(a_ref, b_ref, o_ref, acc):
        k = pl.program_id(2)

        @pl.when(k == 0)
        def _():
            acc[...] = jnp.zeros_like(acc)

        acc[...] += _dot_tn(a_ref[...].astype(BF16), b_ref[...].astype(BF16))

        @pl.when(k == n_k - 1)
        def _():
            if block_major:
                for blk in range(tn // FF_BLK):
                    o_ref[blk] = acc[:, blk * FF_BLK:(blk + 1) * FF_BLK].astype(out_dtype)
            else:
                o_ref[...] = acc[...].astype(out_dtype)

    if block_major:
        out_spec = pl.BlockSpec((tn // FF_BLK, tm, FF_BLK), lambda i, j, k: (j, i, 0))
        out_shape = _sds((n // FF_BLK, m, FF_BLK), out_dtype)
    else:
        out_spec = pl.BlockSpec((tm, tn), lambda i, j, k: (i, j))
        out_shape = _sds((m, n), out_dtype)
    return _pcall(
        body, name=name, grid=(m // tm, n // tn, n_k),
        in_specs=[pl.BlockSpec((tk, tm), lambda i, j, k: (a_section * n_k + k, i)), pl.BlockSpec((tk, tn), lambda i, j, k: (k, j))],
        out_specs=out_spec, out_shape=out_shape, scratch_shapes=[pltpu.VMEM((tm, tn), F32)], compiler_params=_cparams(3),
    )(a, b)


_HBM = pl.BlockSpec(memory_space=pltpu.HBM)


def _mesh_place():
    return lax.axis_index("x"), lax.axis_index("y"), lax.axis_index("c")


def _allgather(name, srcs):
    n = len(srcs)

    def body(*refs):
        src, dst = refs[:n], refs[n:2 * n]
        send_sems, recv_sems, local_sems = refs[2 * n:]
        x, y, c = _mesh_place()
        me, sibling = (x, y, c), (x, y, 1 - c)
        chips = [(1 - x, y), (x, 1 - y), (1 - x, 1 - y)]

        def slot(px, py, pc):
            return 4 * px + 2 * py + pc

        def copy(t, k, block, to, from_src=False):
            return pltpu.make_async_remote_copy(
                src_ref=src[t] if from_src else dst[t].at[slot(*block)], dst_ref=dst[t].at[slot(*block)],
                send_sem=send_sems.at[t, k], recv_sem=recv_sems.at[t, k], device_id=to, device_id_type=MESH_ID)

        mine = [pltpu.make_async_copy(src[t], dst[t].at[slot(*me)], local_sems.at[t]) for t in range(n)]
        for cp in mine:
            cp.start()
        started = []
        for t in range(n):
            started.append(copy(t, 0, me, sibling, from_src=True))
            started += [copy(t, 1 + j, me, (*chip, c), from_src=True) for j, chip in enumerate(chips)]
        for cp in started:
            cp.start()
        for j, chip in enumerate(chips):
            for t in range(n):
                copy(t, 1 + j, (*chip, c), me).wait_recv()
                fwd = copy(t, 4 + j, (*chip, c), sibling)
                fwd.start()
                started.append(fwd)
        for t in range(n):
            copy(t, 0, sibling, me).wait_recv()
            for j, chip in enumerate(chips):
                copy(t, 4 + j, (*chip, 1 - c), me).wait_recv()
        for cp in started:
            cp.wait_send()
        for cp in mine:
            cp.wait()

    return _pcall(
        body, name=name, in_specs=[_HBM] * n, out_specs=[_HBM] * n,
        out_shape=[_sds((N_DEV,) + s.shape, s.dtype) for s in srcs],
        scratch_shapes=[pltpu.SemaphoreType.DMA((n, 7)), pltpu.SemaphoreType.DMA((n, 7)), pltpu.SemaphoreType.DMA((n,))],
    )(*srcs)


def _exchange_ops(src, dst, per_dest, send_sems, recv_sems, local_sems):
    n = len(src)
    x, y, c = _mesh_place()
    me_slot = 4 * x + 2 * y + c
    peers = []
    for r in range(1, N_DEV):
        px, py, pc = x ^ ((r >> 2) & 1), y ^ ((r >> 1) & 1), c ^ (r & 1)
        peers.append(((px, py, pc), 4 * px + 2 * py + pc))

    def piece(t, dest_slot):
        return src[t].at[dest_slot] if per_dest[t] else src[t]

    def local(t):
        return pltpu.make_async_copy(piece(t, me_slot), dst[t].at[me_slot], local_sems.at[t])

    def remote(t, r, landing_slot):
        peer, peer_slot = peers[r]
        return pltpu.make_async_remote_copy(
            src_ref=piece(t, peer_slot), dst_ref=dst[t].at[landing_slot], send_sem=send_sems.at[t, r],
            recv_sem=recv_sems.at[t, r], device_id=peer, device_id_type=MESH_ID)

    def start():
        for t in range(n):
            local(t).start()
        for r in range(N_DEV - 1):
            for t in range(n):
                remote(t, r, me_slot).start()

    def wait():
        for r in range(N_DEV - 1):
            for t in range(n):
                remote(t, r, peers[r][1]).wait_recv()
        for r in range(N_DEV - 1):
            for t in range(n):
                remote(t, r, me_slot).wait_send()
        for t in range(n):
            local(t).wait()

    return start, wait


def _exchange_shapes(srcs, per_dest):
    return [_sds((N_DEV,) + tuple(s.shape[1:] if pd else s.shape), s.dtype) for s, pd in zip(srcs, per_dest)]


def _exchange_scratch(n):
    return [pltpu.SemaphoreType.DMA((n, N_DEV - 1)), pltpu.SemaphoreType.DMA((n, N_DEV - 1)), pltpu.SemaphoreType.DMA((n,))]


def _exchange(name, srcs, per_dest):
    n = len(srcs)

    def body(*refs):
        start, wait = _exchange_ops(refs[:n], refs[n:2 * n], per_dest, *refs[2 * n:])
        start()
        wait()

    return _pcall(
        body, name=name, in_specs=[_HBM] * n, out_specs=[_HBM] * n, out_shape=_exchange_shapes(srcs, per_dest),
        scratch_shapes=_exchange_scratch(n),
    )(*srcs)


def _adam_math(w, g, m, v):
    m = ADAM_B1 * m + (1.0 - ADAM_B1) * g
    v = ADAM_B2 * v + (1.0 - ADAM_B2) * (g * g)
    m_hat = m / (1.0 - ADAM_B1 ** ADAM_STEP)
    v_hat = v / (1.0 - ADAM_B2 ** ADAM_STEP)
    delta = -ADAM_LR * (m_hat / (jnp.sqrt(v_hat) + ADAM_EPS) + ADAM_WD * w)
    return delta, m, v


def _adam_rows(name, parts, w, m, v, row_tile):
    n_l, rows, cols = w.shape

    def body(*refs):
        p_refs = refs[:n_l]
        w_ref, m_ref, v_ref, g_out, d_out, m_out, v_out = refs[n_l:]
        layer = pl.program_id(0)
        for k in range(n_l):
            @pl.when(layer == k)
            def _(p_ref=p_refs[k]):
                g = p_ref[0].astype(F32)
                for s in range(1, N_DEV):
                    g = g + p_ref[s].astype(F32)
                delta, m_new, v_new = _adam_math(w_ref[...], g, m_ref[...], v_ref[...])
                g_out[...] = g
                d_out[...] = delta
                m_out[...] = m_new
                v_out[...] = v_new

    def part_spec(k):
        return pl.BlockSpec((N_DEV, row_tile, cols),
                            lambda l, i: (0, jnp.where(l == k, i, 0), 0))

    blk = pl.BlockSpec((None, row_tile, cols), lambda l, i: (l, i, 0))
    return _pcall(
        body, name=name, grid=(n_l, rows // row_tile),
        in_specs=[part_spec(k) for k in range(n_l)] + [blk, blk, blk],
        out_specs=[blk] * 4, out_shape=[_sds(w.shape, F32)] * 4, compiler_params=_cparams(2),
    )(*parts, w, m, v)


def _adam_packed(name, parts, w, m, v):
    def body(p_ref, w_ref, m_ref, v_ref, g_out, d_out, m_out, v_out):
        g = p_ref[0]
        for s in range(1, N_DEV):
            g = g + p_ref[s]
        delta, m_new, v_new = _adam_math(w_ref[...], g, m_ref[...], v_ref[...])
        g_out[...] = g
        d_out[...] = delta
        m_out[...] = m_new
        v_out[...] = v_new

    return _pcall(
        body, name=name, grid=(1,), in_specs=[_full(parts.shape), _full(w.shape), _full(w.shape), _full(w.shape)],
        out_specs=[_full(w.shape)] * 4, out_shape=[_sds(w.shape, F32)] * 4, compiler_params=_cparams(1),
    )(parts, w, m, v)


def _pack_rows(flat_parts, lead=()):
    flat = jnp.concatenate(flat_parts, axis=-1)
    n = flat.shape[-1]
    rows = -(-n // LANES)
    rows = -(-rows // 8) * 8
    flat = jnp.pad(flat, [(0, 0)] * len(lead) + [(0, rows * LANES - n)])
    return flat.reshape(lead + (rows, LANES))


def _unpack_rows(packed, shapes, lead=()):
    flat = packed.reshape(lead + (-1,))
    out, off = [], 0
    for shp in shapes:
        size = 1
        for s in shp:
            size *= s
        out.append(flat[..., off:off + size].reshape(lead + tuple(shp)))
        off += size
    return out


_SMALL_SHARD_SHAPES = [(N_LAYERS, 128, 4), (N_LAYERS, 32, DG), (N_LAYERS, D_PLE, 128), (N_LAYERS, CONF_K, 32), (N_LAYERS, SC_K, 32)]
_REP_SHAPES = [(N_LAYERS, D)] * 6 + [(N_LAYERS, DG)] * 3 + [(N_LAYERS, N_HEADS), (N_LAYERS, 4, 64, 64)]


def _small_full_to_shards(fcol, pw, proj, dw, sc):
    return [
        fcol.reshape(N_LAYERS, N_DEV, 128, 4).transpose(1, 0, 2, 3),
        pw.reshape(N_LAYERS, N_DEV, 32, DG).transpose(1, 0, 2, 3),
        proj.reshape(N_LAYERS, D_PLE, N_DEV, 128).transpose(2, 0, 1, 3),
        dw.reshape(N_LAYERS, CONF_K, N_DEV, 32).transpose(2, 0, 1, 3),
        sc.reshape(N_LAYERS, SC_K, N_DEV, 32).transpose(2, 0, 1, 3),
    ]


def _small_shards_to_full(fcol, pw, proj, dw, sc):
    return [
        fcol.transpose(1, 0, 2, 3).reshape(N_LAYERS, D, 4),
        pw.transpose(1, 0, 2, 3).reshape(N_LAYERS, DG, DG),
        proj.transpose(1, 2, 0, 3).reshape(N_LAYERS, D_PLE, D),
        dw.transpose(1, 2, 0, 3).reshape(N_LAYERS, CONF_K, DG),
        sc.transpose(1, 2, 0, 3).reshape(N_LAYERS, SC_K, DG),
    ]


def _pad_rows(a, rows):
    return jnp.pad(a, ((0, rows - a.shape[0]), (0, 0)))


def _block_diag4(w):
    z = jnp.zeros((64, 64), w.dtype)
    return jnp.concatenate([jnp.concatenate([w[g] if k == g else z for k in range(4)], axis=1) for g in range(4)], axis=0)


def kernel(x, p, g_mix_pre, w_in, b_forget, w_conf_dw, conf_ln_g, conf_ln_b, w_conf_pw, w_sc, w_pool, pool_scale, w_out, g_mix_post, g_mlp_pre, w_up, w_down, g_mlp_post, g_ple_pre, w_ple_gate, w_ple_proj, g_ple_post, loss_target, m_g_mix_pre, m_w_in, m_b_forget, m_w_conf_dw, m_conf_ln_g, m_conf_ln_b, m_w_conf_pw, m_w_sc, m_w_pool, m_pool_scale, m_w_out, m_g_mix_post, m_g_mlp_pre, m_w_up, m_w_down, m_g_mlp_post, m_g_ple_pre, m_w_ple_gate, m_w_ple_proj, m_g_ple_post, v_g_mix_pre, v_w_in, v_b_forget, v_w_conf_dw, v_conf_ln_g, v_conf_ln_b, v_w_conf_pw, v_w_sc, v_w_pool, v_pool_scale, v_w_out, v_g_mix_post, v_g_mlp_pre, v_w_up, v_w_down, v_g_mlp_post, v_g_ple_pre, v_w_ple_gate, v_w_ple_proj, v_g_ple_post):
    n_l = N_LAYERS
    t_len = x.shape[1]
    assert t_len % TB == 0 and x.shape[0] == 1 and x.shape[2] == D

    def main_cols(a):
        return jnp.concatenate([a[..., :F_LO], a[..., F_HI:]], axis=-1)

    def fcols(a):
        return a[..., F_LO:F_HI]

    def rows_pack(down, out, gate):
        return jnp.concatenate([down, out, gate], axis=1)

    rows_b = rows_pack(w_down, w_out, w_ple_gate).astype(BF16)
    win_b = main_cols(w_in).astype(BF16)
    wup_b = w_up.astype(BF16)
    small_local = _pack_rows([a.reshape(-1) for a in (fcols(w_in), w_conf_pw, w_ple_proj, w_conf_dw, w_sc)])
    rows_g, win_g, wup_g = [None] * n_l, [None] * n_l, [None] * n_l
    rows_g[0], win_g[0], wup_g[0], small_all = _allgather("weight_allgather", [rows_b[0], win_b[0], wup_b[0], small_local])
    small_g = _unpack_rows(small_all, _SMALL_SHARD_SHAPES, lead=(N_DEV,))
    fcol_f, pw_f, proj_f, dw_f, sc_f = _small_shards_to_full(*small_g)
    wf_b = jnp.pad(fcol_f, ((0, 0), (0, 0), (0, LANES - 4))).astype(BF16)
    pw_b, proj_b = pw_f.astype(BF16), proj_f.astype(BF16)
    dw_pad = jnp.pad(dw_f, ((0, 0), (0, 32 - CONF_K), (0, 0)))
    sc_pad = jnp.pad(sc_f, ((0, 0), (0, 8 - SC_K), (0, 0)))
    pool_bd = jnp.stack([_block_diag4(w_pool[l]) for l in range(n_l)]).astype(BF16)
    b_row = jnp.pad(b_forget, ((0, 0), (0, LANES - N_HEADS)))[:, None, :]

    def vec(a, l):
        return a[l][None, :]

    p_all = p.reshape(n_l * t_len, D_PLE)
    h = x[0]
    saved = []
    for l in range(n_l):
        zc, qkv, fl = _mixin_fwd(h, vec(g_mix_pre, l), win_g[l], wf_b[l])
        c, ct = _cumsum_fwd(fl, b_row[l])
        cat3 = _branch_fwd(zc, dw_pad[l], vec(conf_ln_g, l), vec(conf_ln_b, l), pw_b[l], sc_pad[l], pool_bd[l], vec(pool_scale, l))
        if l + 1 < n_l:
            o, lset, rows_g[l + 1], win_g[l + 1], wup_g[l + 1] = _attn_fwd(qkv, c, ct, [rows_b[l + 1], win_b[l + 1], wup_b[l + 1]])
        else:
            o, lset = _attn_fwd(qkv, c, ct)
        mix, h1 = _mixout_fwd(h, cat3, o, rows_g[l], vec(g_mix_post, l))
        u, ff, h2 = _mlp_fwd(h1, vec(g_mlp_pre, l), wup_g[l], rows_g[l], vec(g_mlp_post, l))
        pp, gate, h3 = _ple_fwd(h2, p_all, l, vec(g_ple_pre, l), rows_g[l], proj_b[l], vec(g_ple_post, l))
        saved.append(dict(h0=h, zc=zc, qkv=qkv, fl=fl, c=c, ct=ct, cat3=cat3, o=o, lset=lset, mix=mix, h1=h1, u=u, ff=ff,
                          h2=h2, pp=pp, gate=gate))
        h = h3

    dh, loss_part = _loss_bwd(h, loss_target[0])
    loss = lax.psum(loss_part[0, 0], ("x", "y", "c"))

    d_win = [None] * n_l
    r_down, r_out, r_gate, r_wup, r_win = ([None] * n_l for _ in range(5))
    small_grads = {k: [None] * n_l for k in ("fcol", "pw", "proj", "dw", "sc")}
    rep_grads = {k: [None] * n_l for k in ("g_mix_pre", "g_mix_post", "g_mlp_pre", "g_mlp_post", "g_ple_pre", "g_ple_post",
                                           "ln_g", "ln_b", "pool_scale", "b_forget", "w_pool")}
    for l in reversed(range(n_l)):
        s = saved[l]
        dh, dpp_b, dpre_b, hn3_b, dg_ple_post, dg_ple_pre = _ple_bwd(
            dh, s["h2"], s["pp"], s["gate"], vec(g_ple_post, l), vec(g_ple_pre, l), rows_g[l])
        small_grads["proj"][l] = _matmul_tn("wgrad_proj", p_all, dpp_b, D_PLE, D, F32, a_section=l)
        d_gate = _matmul_tn("wgrad_gate", hn3_b, dpre_b, D, D, BF16).reshape(N_DEV, 128, D)
        dh, a2_b, du_b, dff_b, hn2_b, dg_mlp_post, dg_mlp_pre = _mlp_bwd(
            dh, s["h1"], s["u"], s["ff"], vec(g_mlp_post, l), vec(g_mlp_pre, l), wup_g[l], rows_g[l])
        d_down = _matmul_tn("wgrad_down", a2_b, dff_b, 2 * D, D, BF16).reshape(N_DEV, FF_BLK, D)
        d_wup = _matmul_tn("wgrad_up", hn2_b, du_b, D, 2 * FF_BLK, BF16, block_major=True)
        dcat, dmix_b, cat_b, dg_mix_post = _mixout_bwd(dh, s["mix"], s["cat3"], s["o"], vec(g_mix_post, l), rows_g[l])
        d_out = _matmul_tn("wgrad_out", cat_b, dmix_b, D, D, BF16).reshape(N_DEV, 128, D)
        dt, dob = _attn_bwd_dsum(s["qkv"], dcat, s["c"], s["ct"], s["lset"])
        riders = [d_down, d_out, d_gate, d_wup] + ([d_win[l + 1]] if l + 1 < n_l else [])
        dq, dk, dv, dc, r_down[l], r_out[l], r_gate[l], r_wup[l], *landed = _attn_bwd(
            s["qkv"], dob, s["c"], s["ct"], s["lset"], dt, riders)
        if landed:
            r_win[l + 1] = landed[0]
        dfl, db_f = _forget_bwd(dc, s["fl"], b_row[l])
        dz_b, ddw, dln_g, dln_b, dpw, dsc, dpool, dps = _branch_bwd(
            s["zc"], dcat, dq, dk, dv, dw_pad[l], vec(conf_ln_g, l), vec(conf_ln_b, l), pw_b[l], sc_pad[l], pool_bd[l],
            vec(pool_scale, l))
        dh, xn_b, dg_mix_pre = _mixin_bwd(dh, s["h0"], vec(g_mix_pre, l), dz_b, dfl, win_g[l], wf_b[l])
        d_win[l] = _matmul_tn("wgrad_in", xn_b, dz_b, D, W_MAIN // 2, BF16).reshape(N_DEV, 128, W_MAIN)
        small_grads["fcol"][l] = _matmul_tn("wgrad_fcol", xn_b, dfl, D, LANES, F32)[:, 0:4]
        small_grads["pw"][l], small_grads["dw"][l], small_grads["sc"][l] = dpw, ddw[0:CONF_K], dsc[0:SC_K]
        rep_grads["g_mix_pre"][l], rep_grads["g_mix_post"][l] = dg_mix_pre[0], dg_mix_post[0]
        rep_grads["g_mlp_pre"][l], rep_grads["g_mlp_post"][l] = dg_mlp_pre[0], dg_mlp_post[0]
        rep_grads["g_ple_pre"][l], rep_grads["g_ple_post"][l] = dg_ple_pre[0], dg_ple_post[0]
        rep_grads["ln_g"][l], rep_grads["ln_b"][l], rep_grads["pool_scale"][l] = dln_g[0], dln_b[0], dps[0]
        rep_grads["b_forget"][l] = db_f[0, 0:N_HEADS]
        rep_grads["w_pool"][l] = jnp.stack([dpool[64 * g:64 * g + 64, 64 * g:64 * g + 64] for g in range(4)])
    grad_x = dh[None]

    small_part = _pack_rows(
        [a.reshape(N_DEV, -1) for a in _small_full_to_shards(*[jnp.stack(small_grads[k]) for k in ("fcol", "pw", "proj", "dw", "sc")])],
        lead=(N_DEV,))
    rep_order = ("g_mix_pre", "g_mix_post", "g_mlp_pre", "g_mlp_post", "g_ple_pre", "g_ple_post", "ln_g", "ln_b",
                 "pool_scale", "b_forget", "w_pool")
    rep_part = _pack_rows([jnp.stack(rep_grads[k]).reshape(-1) for k in rep_order])
    r_win[0], r_small, r_rep = _exchange("grad_exchange", [d_win[0], small_part, rep_part], [True, True, False])

    res = {}
    res["w_down"] = _adam_rows("adam_down", r_down, w_down, m_w_down, v_w_down, 128)
    res["w_out"] = _adam_rows("adam_out", r_out, w_out, m_w_out, v_w_out, 128)
    res["w_ple_gate"] = _adam_rows("adam_gate", r_gate, w_ple_gate, m_w_ple_gate, v_w_ple_gate, 128)
    res["w_up"] = _adam_rows("adam_up", r_wup, w_up, m_w_up, v_w_up, 256)
    win_main = _adam_rows("adam_in", r_win, main_cols(w_in), main_cols(m_w_in), main_cols(v_w_in), 128)

    small_w = [(fcols(w_in), w_conf_pw, w_ple_proj, w_conf_dw, w_sc), (fcols(m_w_in), m_w_conf_pw, m_w_ple_proj, m_w_conf_dw, m_w_sc),
               (fcols(v_w_in), v_w_conf_pw, v_w_ple_proj, v_w_conf_dw, v_w_sc)]
    small_packed = [_pack_rows([a.reshape(-1) for a in grp]) for grp in small_w]
    small_res = [_unpack_rows(a, _SMALL_SHARD_SHAPES) for a in _adam_packed("adam_small", r_small, *small_packed)]
    rep_w = [(g_mix_pre, g_mix_post, g_mlp_pre, g_mlp_post, g_ple_pre, g_ple_post, conf_ln_g, conf_ln_b, pool_scale, b_forget, w_pool),
             (m_g_mix_pre, m_g_mix_post, m_g_mlp_pre, m_g_mlp_post, m_g_ple_pre, m_g_ple_post, m_conf_ln_g, m_conf_ln_b, m_pool_scale,
              m_b_forget, m_w_pool),
             (v_g_mix_pre, v_g_mix_post, v_g_mlp_pre, v_g_mlp_post, v_g_ple_pre, v_g_ple_post, v_conf_ln_g, v_conf_ln_b, v_pool_scale,
              v_b_forget, v_w_pool)]
    rep_packed = [_pack_rows([a.reshape(-1) for a in grp]) for grp in rep_w]
    rep_res = [_unpack_rows(a, _REP_SHAPES) for a in _adam_packed("adam_replicated", r_rep, *rep_packed)]

    for kind in range(4):
        fc, pw, proj, dwc, scc = small_res[kind]
        main = win_main[kind]
        (rg_mix_pre, rg_mix_post, rg_mlp_pre, rg_mlp_post, rg_ple_pre, rg_ple_post, r_ln_g, r_ln_b, r_ps, r_bf, r_wpool) = rep_res[kind]
        res.setdefault("by_kind", []).append(dict(
            g_mix_pre=rg_mix_pre, w_in=jnp.concatenate([main[..., :F_LO], fc, main[..., F_LO:]], axis=-1), b_forget=r_bf,
            w_conf_dw=dwc, conf_ln_g=r_ln_g, conf_ln_b=r_ln_b, w_conf_pw=pw, w_sc=scc, w_pool=r_wpool, pool_scale=r_ps,
            w_out=res["w_out"][kind], g_mix_post=rg_mix_post, g_mlp_pre=rg_mlp_pre, w_up=res["w_up"][kind],
            w_down=res["w_down"][kind], g_mlp_post=rg_mlp_post, g_ple_pre=rg_ple_pre, w_ple_gate=res["w_ple_gate"][kind],
            w_ple_proj=proj, g_ple_post=rg_ple_post))
    names = ("g_mix_pre", "w_in", "b_forget", "w_conf_dw", "conf_ln_g", "conf_ln_b", "w_conf_pw", "w_sc", "w_pool", "pool_scale",
             "w_out", "g_mix_post", "g_mlp_pre", "w_up", "w_down", "g_mlp_post", "g_ple_pre", "w_ple_gate", "w_ple_proj", "g_ple_post")
    outs = [loss, grad_x]
    for kind in range(4):
        outs += [res["by_kind"][kind][nm] for nm in names]
    return tuple(outs)
```

```python
import jax
import jax.numpy as jnp
from jax import lax
from jax.experimental import pallas as pl
from jax.experimental.pallas import tpu as pltpu

F32, BF16 = jnp.float32, jnp.bfloat16

D = 1024
DG = 256
N_HEADS = 4
HEAD_DIM = 64
CONF_K = 31
SC_K = 3
POOL_WINDOWS = (2, 4, 8, 16)
D_FF = 4096
D_PLE = 256
N_LAYERS = 4
N_DEV = 8
EPS = 1e-6
SCALE = HEAD_DIM ** -0.5
W_MAIN = 2304
F_LO, F_HI = 1280, 1284

ADAM_LR, ADAM_B1, ADAM_B2, ADAM_EPS, ADAM_WD, ADAM_STEP = 0.001, 0.9, 0.999, 1e-08, 0.01, 10

TB = 512
HALO = 32
LANES = 128
FF_BLK = D_FF // N_DEV
VMEM_LIMIT = 56 * 1024 * 1024

NT_DIMS = (((1,), (1,)), ((), ()))
TN_DIMS = (((0,), (0,)), ((), ()))
MESH_ID = pl.DeviceIdType.MESH


def _pcall(body, **kw):
    return pl.pallas_call(body, **kw)


def _cparams(n_axes):
    return pltpu.CompilerParams(dimension_semantics=("arbitrary",) * n_axes, vmem_limit_bytes=VMEM_LIMIT)


def _sds(shape, dtype):
    return jax.ShapeDtypeStruct(shape, dtype)


def _tok(width, tb=TB):
    return pl.BlockSpec((tb, width), lambda i: (i, 0))


def _tokcol(width, col):
    return pl.BlockSpec((TB, width), lambda i: (i, col))


def _full(shape):
    zeros = (0,) * len(shape)
    return pl.BlockSpec(shape, lambda *_: zeros)


def _resident2(shape):
    zeros = (0,) * len(shape)
    return pl.BlockSpec(shape, lambda i, j: zeros, pipeline_mode=pl.Buffered(1))


def _halo_prev(width, col=0):
    return pl.BlockSpec((HALO, width), lambda i: (jnp.maximum(i * (TB // HALO) - 1, 0), col))


def _halo_next(width, n_rows, col=0):
    last = n_rows // HALO - 1
    return pl.BlockSpec((HALO, width), lambda i: (jnp.minimum((i + 1) * (TB // HALO), last), col))


def _dot(a, b):
    return jnp.dot(a, b, preferred_element_type=F32)


def _dot_nt(a, b):
    return lax.dot_general(a, b, NT_DIMS, preferred_element_type=F32)


def _dot_tn(a, b):
    return lax.dot_general(a, b, TN_DIMS, preferred_element_type=F32)


def _dot_exact(a, b):
    return jnp.dot(a, b, precision=lax.Precision.HIGHEST, preferred_element_type=F32)


def _rms(x, g):
    r = lax.rsqrt(jnp.mean(x * x, axis=-1, keepdims=True) + EPS)
    return x * r * g


def _rms_bwd(x, g, dy):
    r = lax.rsqrt(jnp.mean(x * x, axis=-1, keepdims=True) + EPS)
    n = x * r
    dg = jnp.sum(dy * n, axis=0, keepdims=True)
    dn = dy * g
    dx = r * (dn - n * jnp.mean(dn * n, axis=-1, keepdims=True))
    return dx, dg, n * g


def _sigmoid(x):
    return jax.nn.sigmoid(x)


def _log_sigmoid(x):
    return jnp.minimum(x, 0.0) - jnp.log(1.0 + jnp.exp(-jnp.abs(x)))


def _lane_group_select(lane, v2, v4, v8, v16):
    return jnp.where(lane < 64, v2, jnp.where(lane < 128, v4, jnp.where(lane < 192, v8, v16)))


def _pool_counts(t0, rows):
    lane = lax.broadcasted_iota(jnp.int32, (rows, DG), 1)
    t = lax.broadcasted_iota(jnp.int32, (rows, DG), 0) + t0
    win = _lane_group_select(lane, 2, 4, 8, 16)
    return jnp.minimum(t + 1, win).astype(F32), lane


def _mixin_fwd(h, g, win, wf):
    t_len = h.shape[0]

    def body(h_ref, g_ref, win_ref, wf_ref, zc_ref, qkv_ref, fl_ref):
        xn = _rms(h_ref[...], g_ref[...]).astype(BF16)
        z = _dot(xn, win_ref[...].reshape(D, W_MAIN))
        zc_ref[:, 0:512] = z[:, 0:512]
        zc_ref[:, 512:1536] = z[:, 1280:2304]
        qkv_ref[:, 0:256] = (z[:, 512:768] * SCALE).astype(BF16)
        qkv_ref[:, 256:768] = z[:, 768:1280].astype(BF16)
        fl_ref[...] = _dot(xn, wf_ref[...])

    return _pcall(
        body, name="mixin_fwd", grid=(t_len // TB,),
        in_specs=[_tok(D), _full((1, D)), _full((N_DEV, D // N_DEV, W_MAIN)), _full((D, LANES))],
        out_specs=[_tok(1536), _tok(768), _tok(LANES)],
        out_shape=[_sds((t_len, 1536), F32), _sds((t_len, 768), BF16), _sds((t_len, LANES), F32)],
        compiler_params=_cparams(1),
    )(h, g, win, wf)


def _transpose_lanes8(x):
    eye = (lax.broadcasted_iota(jnp.int32, (8, LANES), 0) == lax.broadcasted_iota(jnp.int32, (8, LANES), 1)).astype(F32)
    return lax.dot_general(eye, x, NT_DIMS, precision=lax.Precision.HIGHEST, preferred_element_type=F32)


def _cumsum_fwd(fl, b_row):
    t_len = fl.shape[0]

    def body(fl_ref, b_ref, c_ref, ct_ref, carry):
        @pl.when(pl.program_id(0) == 0)
        def _():
            carry[...] = jnp.zeros_like(carry)

        r = lax.broadcasted_iota(jnp.int32, (TB, TB), 0)
        s = lax.broadcasted_iota(jnp.int32, (TB, TB), 1)
        lf = _log_sigmoid(fl_ref[...] + b_ref[...])
        c = _dot_exact((r >= s).astype(F32), lf) + carry[0:1, :]
        c_ref[...] = c
        ct_ref[...] = _transpose_lanes8(c)
        carry[...] += jnp.sum(lf, axis=0, keepdims=True)

    return _pcall(
        body, name="cumsum_fwd", grid=(t_len // TB,),
        in_specs=[_tok(LANES), _full((1, LANES))],
        out_specs=[_tok(LANES), pl.BlockSpec((8, TB), lambda i: (0, i))],
        out_shape=[_sds((t_len, LANES), F32), _sds((8, t_len), F32)],
        scratch_shapes=[pltpu.VMEM((8, LANES), F32)],
        compiler_params=_cparams(1),
    )(fl, b_row)


def _layer_norm_parts(y, g, b):
    mu = jnp.mean(y, axis=-1, keepdims=True)
    yc = y - mu
    r = lax.rsqrt(jnp.mean(yc * yc, axis=-1, keepdims=True) + EPS)
    n = yc * r
    return n, r, n * g + b


def _phase_copies(src, dst, rows):
    for p in range(1, 8):
        dst[p - 1, 0:rows - 8, :] = src[pl.ds(p, rows - 8), :]


def _tap(src, copies, off, n):
    p = off % 8
    return src[pl.ds(off, n), :] if p == 0 else copies[p - 1, pl.ds(off - p, n), :]


def _pool_window_sums(p0, p1, p2, p3):
    e = HALO + TB
    p1[8:e, :] = p0[pl.ds(8, e - 8), :] + p0[pl.ds(7, e - 8), :]
    p2[16:e, :] = p1[pl.ds(16, e - 16), :] + p1[pl.ds(14, e - 16), :]
    p3[24:e, :] = p2[pl.ds(24, e - 24), :] + p2[pl.ds(20, e - 24), :]
    s16 = p3[pl.ds(HALO, TB), :] + p3[pl.ds(HALO - 8, TB), :]
    return p1[pl.ds(HALO, TB), :], p2[pl.ds(HALO, TB), :], p3[pl.ds(HALO, TB), :], s16


def _branch_fwd(zc, w_dw, ln_g, ln_b, w_pw, w_sc, w_pool, pool_scale):
    t_len = zc.shape[0]
    e = HALO + TB

    def body(z_ref, zh_ref, dw_ref, g_ref, b_ref, pw_ref, sc_ref, pool_ref, ps_ref, cat_ref, u_s, ch_s, p0, p1, p2, p3, u_ph):
        i = pl.program_id(0)
        hm = (i > 0).astype(F32)
        u_s[0:HALO, :] = zh_ref[:, 0:256] * _sigmoid(zh_ref[:, 256:512]) * hm
        u_s[HALO:e, :] = z_ref[:, 0:256] * _sigmoid(z_ref[:, 256:512])
        _phase_copies(u_s, u_ph, e)
        y = jnp.zeros((TB, DG), F32)
        for k in range(CONF_K):
            y = y + dw_ref[k:k + 1, :] * _tap(u_s, u_ph, HALO - (CONF_K - 1) + k, TB)
        _, _, yn = _layer_norm_parts(y, g_ref[...], b_ref[...])
        s = yn * _sigmoid(yn)
        cat_ref[:, 0:256] = _dot(s.astype(BF16), pw_ref[...])
        ch_s[0:HALO, :] = zh_ref[:, 1024:1280] * zh_ref[:, 512:768] * hm
        ch_s[HALO:e, :] = z_ref[:, 1024:1280] * z_ref[:, 512:768]
        cv = jnp.zeros((TB, DG), F32)
        for k in range(SC_K):
            cv = cv + sc_ref[k:k + 1, :] * ch_s[pl.ds(HALO - (SC_K - 1) + k, TB), :]
        cat_ref[:, 256:512] = z_ref[:, 768:1024] * cv
        p0[0:HALO, :] = zh_ref[:, 1280:1536] * hm
        p0[HALO:e, :] = z_ref[:, 1280:1536]
        s2, s4, s8, s16 = _pool_window_sums(p0, p1, p2, p3)
        cnt, lane = _pool_counts(i * TB, TB)
        dlt = _lane_group_select(lane, s2, s4, s8, s16) / cnt - z_ref[:, 1280:1536]
        cat_ref[:, 512:768] = _dot(dlt.astype(BF16), pool_ref[...]) * ps_ref[...]

    scr = [pltpu.VMEM((e, DG), F32) for _ in range(6)] + [pltpu.VMEM((7, e, DG), F32)]
    return _pcall(
        body, name="branch_fwd", grid=(t_len // TB,),
        in_specs=[_tok(1536), _halo_prev(1536), _full((32, DG)), _full((1, DG)), _full((1, DG)), _full((DG, DG)),
                  _full((8, DG)), _full((DG, DG)), _full((1, DG))],
        out_specs=_tok(768), out_shape=_sds((t_len, 768), F32), scratch_shapes=scr, compiler_params=_cparams(1),
    )(zc, zc, w_dw, ln_g, ln_b, w_pw, w_sc, w_pool, pool_scale)


def _head_masks(rows):
    lane = lax.broadcasted_iota(jnp.int32, (rows, LANES), 1)
    return lane, (lane < HEAD_DIM, lane >= HEAD_DIM)


def _keep_lanes(mask, x):
    return jnp.where(mask, x.astype(F32), 0.0).astype(BF16)


def _with_exchange(refs, n_in, n_out, n_x, per_dest, first, last):
    ins, x_src = refs[:n_in], refs[n_in:n_in + n_x]
    outs, x_dst = refs[n_in + n_x:n_in + n_x + n_out], refs[n_in + n_x + n_out:n_in + 2 * n_x + n_out]
    begin = finish = None
    if n_x:
        start, wait = _exchange_ops(x_src, x_dst, [per_dest] * n_x, *refs[n_in + 2 * n_x + n_out:])

        def begin():
            pl.when(first)(start)

        def finish():
            pl.when(last)(wait)

    return ins, outs, begin, finish


def _attn_fwd(qkv, c, ct, bcast=()):
    t_len = qkv.shape[0]
    n_t = t_len // TB
    n_x = len(bcast)

    def body(*refs):
        i = pl.program_id(0)
        (q_ref, k_ref, v_ref, c_ref, ct_ref), (o_ref, lset_ref), begin, finish = _with_exchange(
            refs, 5, 2, n_x, False, i == 0, i == n_t - 1)
        if begin:
            begin()
        lane, halves = _head_masks(TB)
        crow = c_ref[...]
        causal = lax.broadcasted_iota(jnp.int32, (TB, TB), 0) >= lax.broadcasted_iota(jnp.int32, (TB, TB), 1)
        lse_out = jnp.zeros((TB, LANES), F32)
        for g in range(2):
            cols = slice(g * LANES, (g + 1) * LANES)
            qg = q_ref[:, cols]
            qms = [_keep_lanes(halves[hh], qg) for hh in range(2)]
            cqs = [jnp.sum(jnp.where(lane == 2 * g + hh, crow, 0.0), axis=1, keepdims=True) for hh in range(2)]

            def block(j, carry, masked):
                off = pl.multiple_of(j * TB, TB)
                kj = k_ref[pl.ds(off, TB), cols]
                vj = v_ref[pl.ds(off, TB), cols]
                new = []
                for hh in range(2):
                    m, l, acc = carry[hh]
                    s = _dot_nt(qms[hh], kj) + (cqs[hh] - ct_ref[2 * g + hh:2 * g + hh + 1, pl.ds(off, TB)])
                    if masked:
                        s = jnp.where(causal, s, -jnp.inf)
                    m_new = jnp.maximum(m, jnp.max(s, axis=1, keepdims=True))
                    alpha = jnp.exp(m - m_new)
                    p = jnp.exp(s - m_new)
                    l = alpha * l + jnp.sum(p, axis=1, keepdims=True)
                    acc = alpha * acc + _dot(p.astype(BF16), vj)
                    new.append((m_new, l, acc))
                return tuple(new)

            init = tuple((jnp.full((TB, 1), -jnp.inf, F32), jnp.zeros((TB, 1), F32), jnp.zeros((TB, LANES), F32))
                         for _ in range(2))
            carry = lax.fori_loop(0, i, lambda j, cr: block(j, cr, False), init)
            (m0, l0, acc0), (m1, l1, acc1) = block(i, carry, True)
            o_ref[:, cols] = jnp.where(halves[0], acc0 / l0, acc1 / l1)
            lse_out = jnp.where(lane == 2 * g, m0 + jnp.log(l0), lse_out)
            lse_out = jnp.where(lane == 2 * g + 1, m1 + jnp.log(l1), lse_out)
        lset_ref[...] = _transpose_lanes8(lse_out)
        if finish:
            finish()

    return _pcall(
        body, name="attn_fwd_gather" if n_x else "attn_fwd", grid=(n_t,),
        in_specs=[_tokcol(DG, 0), pl.BlockSpec((t_len, DG), lambda i: (0, 1)), pl.BlockSpec((t_len, DG), lambda i: (0, 2)),
                  _tok(LANES), _full((8, t_len))] + [_HBM] * n_x,
        out_specs=[_tok(DG), pl.BlockSpec((8, TB), lambda i: (0, i))] + [_HBM] * n_x,
        out_shape=[_sds((t_len, DG), F32), _sds((8, t_len), F32)] + _exchange_shapes(bcast, [False] * n_x),
        scratch_shapes=_exchange_scratch(n_x) if n_x else [],
        compiler_params=_cparams(1),
    )(qkv, qkv, qkv, c, ct, *bcast)


def _mix_projection(cat_ref, o_ref, w):
    return (_dot(cat_ref[:, 0:256].astype(BF16), w[0:256]) + _dot(o_ref[...].astype(BF16), w[256:512])
            + _dot(cat_ref[:, 256:768].astype(BF16), w[512:1024]))


def _mixout_fwd(h, cat3, o, w_rows, g):
    t_len = h.shape[0]

    def body(h_ref, cat_ref, o_ref, w_ref, g_ref, h1_ref):
        mix = _mix_projection(cat_ref, o_ref, w_ref[...].reshape(D, D))
        h1_ref[...] = h_ref[...] + _rms(mix, g_ref[...])

    return _pcall(
        body, name="mixout_fwd", grid=(t_len // TB,),
        in_specs=[_tok(D), _tok(768), _tok(DG), pl.BlockSpec((N_DEV, 128, D), lambda i: (0, 4, 0)), _full((1, D))],
        out_specs=_tok(D), out_shape=_sds((t_len, D), F32), compiler_params=_cparams(1),
    )(h, cat3, o, w_rows, g)


def _mlp_fwd(h, g_pre, w_up, w_rows, g_post):
    t_len = h.shape[0]

    def body(h_ref, g1_ref, up_ref, dn_ref, g2_ref, u_ref, ff_ref, h2_ref, hn_s, acc_s):
        j = pl.program_id(1)

        @pl.when(j == 0)
        def _():
            hn_s[...] = _rms(h_ref[...], g1_ref[...]).astype(BF16)
            acc_s[...] = jnp.zeros_like(acc_s)

        u = _dot(hn_s[...], up_ref[j])
        u_ref[...] = u.astype(BF16)
        r = jnp.maximum(u, 0.0)
        acc_s[...] += _dot((r * r).astype(BF16), dn_ref[j])

        @pl.when(j == N_DEV - 1)
        def _():
            ff = acc_s[...]
            ff_ref[...] = ff
            h2_ref[...] = h_ref[...] + _rms(ff, g2_ref[...])

    tok2 = pl.BlockSpec((TB, D), lambda i, j: (i, 0))
    vec2 = pl.BlockSpec((1, D), lambda i, j: (0, 0))
    return _pcall(
        body, name="mlp_fwd", grid=(t_len // TB, N_DEV),
        in_specs=[tok2, vec2, _resident2((N_DEV, D, FF_BLK)), _resident2((N_DEV, FF_BLK, D)), vec2],
        out_specs=[pl.BlockSpec((TB, FF_BLK), lambda i, j: (i, j)), tok2, tok2],
        out_shape=[_sds((t_len, D_FF), BF16), _sds((t_len, D), F32), _sds((t_len, D), F32)],
        scratch_shapes=[pltpu.VMEM((TB, D), BF16), pltpu.VMEM((TB, D), F32)], compiler_params=_cparams(2),
    )(h, g_pre, w_up, w_rows, g_post)


def _ple_fwd(h, p_all, layer, g_pre, w_rows, w_proj, g_post):
    t_len = h.shape[0]
    n_t = t_len // TB

    def body(h_ref, p_ref, g1_ref, wg_ref, wp_ref, g2_ref, h3_ref):
        pp, gate = _ple_parts(h_ref, p_ref, g1_ref, wg_ref, wp_ref)
        h3_ref[...] = h_ref[...] + _rms(pp * gate, g2_ref[...])

    return _pcall(
        body, name="ple_fwd", grid=(n_t,),
        in_specs=[_tok(D), pl.BlockSpec((TB, D_PLE), lambda i: (layer * n_t + i, 0)), _full((1, D)),
                  pl.BlockSpec((N_DEV, 128, D), lambda i: (0, 5, 0)), _full((D_PLE, D)), _full((1, D))],
        out_specs=_tok(D), out_shape=_sds((t_len, D), F32), compiler_params=_cparams(1),
    )(h, p_all, g_pre, w_rows, w_proj, g_post)


def _ple_parts(h_ref, p_ref, g1_ref, wg_ref, wp_ref):
    hn = _rms(h_ref[...], g1_ref[...]).astype(BF16)
    gate = _sigmoid(_dot(hn, wg_ref[...].reshape(D, D)))
    return _dot(p_ref[...].astype(BF16), wp_ref[...]), gate


def _loss_bwd(h, target):
    t_len = h.shape[0]

    def body(h_ref, t_ref, dh_ref, loss_ref):
        @pl.when(pl.program_id(0) == 0)
        def _():
            loss_ref[...] = jnp.zeros_like(loss_ref)

        d = h_ref[...] - t_ref[...]
        dh_ref[...] = d * (1.0 / D)
        loss_ref[...] += 0.5 * jnp.sum(jnp.mean(d * d, axis=-1, keepdims=True), axis=0, keepdims=True)

    return _pcall(
        body, name="loss_bwd", grid=(t_len // TB,), in_specs=[_tok(D), _tok(D)],
        out_specs=[_tok(D), _full((8, LANES))], out_shape=[_sds((t_len, D), F32), _sds((8, LANES), F32)],
        compiler_params=_cparams(1),
    )(h, target)


def _acc_init(refs):
    @pl.when(pl.program_id(0) == 0)
    def _():
        for r in refs:
            r[...] = jnp.zeros_like(r)


def _ple_bwd(dh3, h2, p_all, layer, g_post, g_pre, w_rows, w_proj):
    t_len = dh3.shape[0]
    n_t = t_len // TB

    def body(dh_ref, h_ref, p_ref, g2_ref, g1_ref, wg_ref, wp_ref, dh2_ref, dpp_ref, dpre_ref, hn_ref, dg2_ref, dg1_ref):
        _acc_init([dg2_ref, dg1_ref])
        dh = dh_ref[...]
        pp, gate = _ple_parts(h_ref, p_ref, g1_ref, wg_ref, wp_ref)
        de, dg2, _ = _rms_bwd(pp * gate, g2_ref[...], dh)
        dg2_ref[...] += dg2
        dpp_ref[...] = (de * gate).astype(BF16)
        dpre = (de * pp * gate * (1.0 - gate)).astype(BF16)
        dpre_ref[...] = dpre
        dhn = _dot_nt(dpre, wg_ref[...].reshape(D, D))
        dx, dg1, hn = _rms_bwd(h_ref[...], g1_ref[...], dhn)
        dg1_ref[...] += dg1
        hn_ref[...] = hn.astype(BF16)
        dh2_ref[...] = dh + dx

    return _pcall(
        body, name="ple_bwd", grid=(n_t,),
        in_specs=[_tok(D), _tok(D), pl.BlockSpec((TB, D_PLE), lambda i: (layer * n_t + i, 0)), _full((1, D)), _full((1, D)),
                  pl.BlockSpec((N_DEV, 128, D), lambda i: (0, 5, 0)), _full((D_PLE, D))],
        out_specs=[_tok(D)] * 4 + [_full((1, D))] * 2,
        out_shape=[_sds((t_len, D), F32)] + [_sds((t_len, D), BF16)] * 3 + [_sds((1, D), F32)] * 2,
        compiler_params=_cparams(1),
    )(dh3, h2, p_all, g_post, g_pre, w_rows, w_proj)


def _mlp_bwd(dh2, h1, u, ff, g_post, g_pre, w_up, w_rows):
    t_len = dh2.shape[0]

    def body(dh_ref, h_ref, u_ref, ff_ref, g2_ref, g1_ref, up_ref, dn_ref,
             dh1_ref, a2_ref, du_ref, dff_ref, hn_ref, dg2_ref, dg1_ref, dff_s, acc_s):
        i, j = pl.program_id(0), pl.program_id(1)

        @pl.when((i == 0) & (j == 0))
        def _():
            dg2_ref[...] = jnp.zeros_like(dg2_ref)
            dg1_ref[...] = jnp.zeros_like(dg1_ref)

        @pl.when(j == 0)
        def _():
            dff, dg2, _ = _rms_bwd(ff_ref[...], g2_ref[...], dh_ref[...])
            dg2_ref[...] += dg2
            dff_s[...] = dff.astype(BF16)
            dff_ref[...] = dff.astype(BF16)
            acc_s[...] = jnp.zeros_like(acc_s)

        r = jnp.maximum(u_ref[...].astype(F32), 0.0)
        a2_ref[...] = (r * r).astype(BF16)
        du = (_dot_nt(dff_s[...], dn_ref[j]) * (2.0 * r)).astype(BF16)
        du_ref[...] = du
        acc_s[...] += _dot_nt(du, up_ref[j])

        @pl.when(j == N_DEV - 1)
        def _():
            dx, dg1, hn = _rms_bwd(h_ref[...], g1_ref[...], acc_s[...])
            dg1_ref[...] += dg1
            hn_ref[...] = hn.astype(BF16)
            dh1_ref[...] = dh_ref[...] + dx

    tok2 = pl.BlockSpec((TB, D), lambda i, j: (i, 0))
    vec2 = pl.BlockSpec((1, D), lambda i, j: (0, 0))
    blk2 = pl.BlockSpec((TB, FF_BLK), lambda i, j: (i, j))
    return _pcall(
        body, name="mlp_bwd", grid=(t_len // TB, N_DEV),
        in_specs=[tok2, tok2, blk2, tok2, vec2, vec2, _resident2((N_DEV, D, FF_BLK)), _resident2((N_DEV, FF_BLK, D))],
        out_specs=[tok2, blk2, blk2, tok2, tok2, vec2, vec2],
        out_shape=[_sds((t_len, D), F32), _sds((t_len, D_FF), BF16), _sds((t_len, D_FF), BF16), _sds((t_len, D), BF16),
                   _sds((t_len, D), BF16), _sds((1, D), F32), _sds((1, D), F32)],
        scratch_shapes=[pltpu.VMEM((TB, D), BF16), pltpu.VMEM((TB, D), F32)], compiler_params=_cparams(2),
    )(dh2, h1, u, ff, g_post, g_pre, w_up, w_rows)


def _mixout_bwd(dh1, cat3, o, g, w_rows):
    t_len = dh1.shape[0]

    def body(dh_ref, cat_ref, o_ref, g_ref, w_ref, dcat_ref, dmix_ref, catb_ref, dg_ref):
        _acc_init([dg_ref])
        w = w_ref[...].reshape(D, D)
        dmix, dg, _ = _rms_bwd(_mix_projection(cat_ref, o_ref, w), g_ref[...], dh_ref[...])
        dg_ref[...] += dg
        dmix = dmix.astype(BF16)
        dmix_ref[...] = dmix
        dcat_ref[...] = _dot_nt(dmix, w)
        catb_ref[:, 0:256] = cat_ref[:, 0:256].astype(BF16)
        catb_ref[:, 256:512] = o_ref[...].astype(BF16)
        catb_ref[:, 512:1024] = cat_ref[:, 256:768].astype(BF16)

    return _pcall(
        body, name="mixout_bwd", grid=(t_len // TB,),
        in_specs=[_tok(D), _tok(768), _tok(DG), _full((1, D)), pl.BlockSpec((N_DEV, 128, D), lambda i: (0, 4, 0))],
        out_specs=[_tok(D), _tok(D), _tok(D), _full((1, D))],
        out_shape=[_sds((t_len, D), F32), _sds((t_len, D), BF16), _sds((t_len, D), BF16), _sds((1, D), F32)],
        compiler_params=_cparams(1),
    )(dh1, cat3, o, g, w_rows)


def _attn_bwd_dsum(qkv, dcat, c, ct, lset):
    t_len = qkv.shape[0]

    def body(q_ref, do_ref, k_ref, v_ref, c_ref, ct_ref, lset_ref, dt_ref, dob_ref):
        i = pl.program_id(0)
        lane, halves = _head_masks(TB)
        causal_t = lax.broadcasted_iota(jnp.int32, (TB, TB), 1) >= lax.broadcasted_iota(jnp.int32, (TB, TB), 0)
        sub = lax.broadcasted_iota(jnp.int32, (8, TB), 0)
        dob_ref[...] = do_ref[...].astype(BF16)
        out = jnp.zeros((8, TB), F32)
        for g in range(2):
            cols = slice(g * LANES, (g + 1) * LANES)
            qi = q_ref[:, cols]
            doi = do_ref[:, cols].astype(BF16)
            doms = [_keep_lanes(halves[hh], doi) for hh in range(2)]
            cqs = [ct_ref[2 * g + hh:2 * g + hh + 1, :] for hh in range(2)]
            lses = [lset_ref[2 * g + hh:2 * g + hh + 1, :] for hh in range(2)]

            def block(j, accs, masked):
                off = pl.multiple_of(j * TB, TB)
                kj = k_ref[pl.ds(off, TB), cols]
                vj = v_ref[pl.ds(off, TB), cols]
                cj = c_ref[pl.ds(off, TB), :]
                new = []
                for hh in range(2):
                    ck = jnp.sum(jnp.where(lane == 2 * g + hh, cj, 0.0), axis=1, keepdims=True)
                    st = _dot_nt(_keep_lanes(halves[hh], kj), qi) + (cqs[hh] - ck)
                    if masked:
                        st = jnp.where(causal_t, st, -jnp.inf)
                    pt = jnp.exp(st - lses[hh])
                    new.append(accs[hh] + jnp.sum(pt * _dot_nt(vj, doms[hh]), axis=0, keepdims=True))
                return tuple(new)

            init = (jnp.zeros((1, TB), F32), jnp.zeros((1, TB), F32))
            accs = block(i, lax.fori_loop(0, i, lambda j, cr: block(j, cr, False), init), True)
            out = jnp.where(sub == 2 * g, accs[0], out)
            out = jnp.where(sub == 2 * g + 1, accs[1], out)
        dt_ref[...] = out

    row8 = pl.BlockSpec((8, TB), lambda i: (0, i))
    return _pcall(
        body, name="attn_bwd_dsum", grid=(t_len // TB,),
        in_specs=[_tokcol(DG, 0), _tokcol(DG, 1), pl.BlockSpec((t_len, DG), lambda i: (0, 1)),
                  pl.BlockSpec((t_len, DG), lambda i: (0, 2)), _full((t_len, LANES)), row8, row8],
        out_specs=[row8, _tok(DG)], out_shape=[_sds((8, t_len), F32), _sds((t_len, DG), BF16)],
        compiler_params=_cparams(1),
    )(qkv, dcat, qkv, qkv, c, ct, lset)


def _attn_bwd(qkv, dob, c, ct, lset, dt, xchg=()):
    t_len = qkv.shape[0]
    n_q = t_len // TB
    n_x = len(xchg)

    def body(*refs):
        j = pl.program_id(0)
        ins, outs, begin, finish = _with_exchange(refs, 8, 4, n_x, True, j == 0, j == n_q - 1)
        q_ref, dob_ref, k_ref, v_ref, c_ref, ct_ref, lset_ref, dt_ref = ins
        dq_ref, dk_ref, dv_ref, dc_ref = outs
        if begin:
            begin()

        @pl.when(j == 0)
        def _():
            dq_ref[...] = jnp.zeros_like(dq_ref)

        lane, halves = _head_masks(TB)
        crow = c_ref[...]
        causal_t = lax.broadcasted_iota(jnp.int32, (TB, TB), 1) >= lax.broadcasted_iota(jnp.int32, (TB, TB), 0)
        dc_out = jnp.zeros((TB, LANES), F32)
        for g in range(2):
            cols = slice(g * LANES, (g + 1) * LANES)
            kg, vg = k_ref[:, cols], v_ref[:, cols]
            kms = [_keep_lanes(halves[hh], kg) for hh in range(2)]
            cks = [jnp.sum(jnp.where(lane == 2 * g + hh, crow, 0.0), axis=1, keepdims=True) for hh in range(2)]

            def block(i, carry, masked):
                dk, dv, dcs = carry
                off = pl.multiple_of(i * TB, TB)
                qi = q_ref[pl.ds(off, TB), cols]
                doi = dob_ref[pl.ds(off, TB), cols]
                dq_add = jnp.zeros((TB, LANES), F32)
                dcs_new = []
                for hh in range(2):
                    h = 2 * g + hh
                    dom = _keep_lanes(halves[hh], doi)
                    st = _dot_nt(kms[hh], qi) + (ct_ref[h:h + 1, pl.ds(off, TB)] - cks[hh])
                    if masked:
                        st = jnp.where(causal_t, st, -jnp.inf)
                    pt = jnp.exp(st - lset_ref[h:h + 1, pl.ds(off, TB)])
                    dv = dv + _dot(pt.astype(BF16), dom)
                    dst = pt * (_dot_nt(vg, dom) - dt_ref[h:h + 1, pl.ds(off, TB)])
                    dsb = dst.astype(BF16)
                    dk = dk + _dot(dsb, _keep_lanes(halves[hh], qi))
                    dcs_new.append(dcs[hh] + jnp.sum(dst, axis=1, keepdims=True))
                    dq_add = dq_add + _dot_tn(dsb, kms[hh])
                dq_ref[pl.ds(off, TB), cols] += dq_add
                return dk, dv, tuple(dcs_new)

            init = (jnp.zeros((TB, LANES), F32), jnp.zeros((TB, LANES), F32),
                    (jnp.zeros((TB, 1), F32), jnp.zeros((TB, 1), F32)))
            carry = block(j, init, True)
            dk, dv, dcs = lax.fori_loop(j + 1, n_q, lambda i, cr: block(i, cr, False), carry)
            dk_ref[:, cols] = dk
            dv_ref[:, cols] = dv
            dc_out = jnp.where(lane == 2 * g, -dcs[0], dc_out)
            dc_out = jnp.where(lane == 2 * g + 1, -dcs[1], dc_out)
        dc_ref[...] = dc_out
        if finish:
            finish()

    return _pcall(
        body, name="attn_bwd_exchange" if n_x else "attn_bwd", grid=(n_q,),
        in_specs=[pl.BlockSpec((t_len, DG), lambda i: (0, 0)), _full((t_len, DG)), _tokcol(DG, 1), _tokcol(DG, 2),
                  _tok(LANES), _full((8, t_len)), _full((8, t_len)), _full((8, t_len))] + [_HBM] * n_x,
        out_specs=[_full((t_len, DG)), _tok(DG), _tok(DG), _tok(LANES)] + [_HBM] * n_x,
        out_shape=[_sds((t_len, DG), F32), _sds((t_len, DG), F32), _sds((t_len, DG), F32), _sds((t_len, LANES), F32)]
        + _exchange_shapes(xchg, [True] * n_x),
        scratch_shapes=_exchange_scratch(n_x) if n_x else [],
        compiler_params=_cparams(1),
    )(qkv, dob, qkv, qkv, c, ct, lset, dt, *xchg)


def _forget_bwd(dc, fl, b_row):
    t_len = dc.shape[0]
    n_t = t_len // TB
    rev = pl.BlockSpec((TB, LANES), lambda i: (n_t - 1 - i, 0))

    def body(dc_ref, fl_ref, b_ref, dfl_ref, db_ref, carry):
        @pl.when(pl.program_id(0) == 0)
        def _():
            carry[...] = jnp.zeros_like(carry)
            db_ref[...] = jnp.zeros_like(db_ref)

        r = lax.broadcasted_iota(jnp.int32, (TB, TB), 0)
        s = lax.broadcasted_iota(jnp.int32, (TB, TB), 1)
        dc = dc_ref[...]
        dl = _dot_exact((r <= s).astype(F32), dc) + carry[0:1, :]
        carry[...] += jnp.sum(dc, axis=0, keepdims=True)
        dfl = dl * _sigmoid(-(fl_ref[...] + b_ref[...]))
        dfl_ref[...] = dfl
        db_ref[...] += jnp.sum(dfl, axis=0, keepdims=True)

    return _pcall(
        body, name="forget_bwd", grid=(n_t,), in_specs=[rev, rev, _full((1, LANES))],
        out_specs=[rev, _full((1, LANES))], out_shape=[_sds((t_len, LANES), F32), _sds((1, LANES), F32)],
        scratch_shapes=[pltpu.VMEM((8, LANES), F32)], compiler_params=_cparams(1),
    )(dc, fl, b_row)


def _branch_bwd(zc, dcat, dq, dk, dv, w_dw, ln_g, ln_b, w_pw, w_sc, w_pool, pool_scale):
    t_len = zc.shape[0]
    n_t = t_len // TB
    e2 = HALO + TB + HALO
    e1 = TB + HALO

    def body(z_ref, zp_ref, zn_ref, dcf_ref, dcfn_ref, dsp_ref, dspn_ref, dq_ref, dk_ref, dv_ref,
             dw_ref, g_ref, b_ref, pw_ref, sc_ref, pool_ref, ps_ref,
             dz_ref, ddw_ref, dg_ref, db_ref, dpw_ref, dsc_ref, dpool_ref, dps_ref,
             u_s, dy_s, ch_s, dcv_s, p0, p1, p2, p3, g0, g1, g2, g3, u_ph, dy_ph):
        i = pl.program_id(0)
        _acc_init([ddw_ref, dg_ref, db_ref, dpw_ref, dsc_ref, dpool_ref, dps_ref])
        hm = (i > 0).astype(F32)
        nm = (i < n_t - 1).astype(F32)

        sig_b = _sigmoid(z_ref[:, 256:512])
        a = z_ref[:, 0:256]
        u_s[0:HALO, :] = zp_ref[:, 0:256] * _sigmoid(zp_ref[:, 256:512]) * hm
        u_s[HALO:HALO + TB, :] = a * sig_b
        u_s[HALO + TB:e2, :] = zn_ref[:, 0:256] * _sigmoid(zn_ref[:, 256:512])
        _phase_copies(u_s, u_ph, e2)
        y = jnp.zeros((e1, DG), F32)
        for k in range(CONF_K):
            y = y + dw_ref[k:k + 1, :] * _tap(u_s, u_ph, HALO - (CONF_K - 1) + k, e1)
        n, r, yn = _layer_norm_parts(y, g_ref[...], b_ref[...])
        sg = _sigmoid(yn)
        dyc = jnp.concatenate([dcf_ref[...], dcfn_ref[...] * nm], axis=0)
        ds = _dot_nt(dyc.astype(BF16), pw_ref[...])
        dyn = ds * sg * (1.0 + yn * (1.0 - sg))
        dg_ref[...] += jnp.sum((dyn * n)[0:TB], axis=0, keepdims=True)
        db_ref[...] += jnp.sum(dyn[0:TB], axis=0, keepdims=True)
        dn = dyn * g_ref[...]
        dyv = r * (dn - jnp.mean(dn, axis=-1, keepdims=True) - n * jnp.mean(dn * n, axis=-1, keepdims=True))
        dpw_ref[...] += _dot_tn((yn * sg)[0:TB].astype(BF16), dcf_ref[...].astype(BF16))
        dy_s[...] = dyv
        _phase_copies(dy_s, dy_ph, e1)
        du = jnp.zeros((TB, DG), F32)
        dyv_t = dyv[0:TB]
        for k in range(CONF_K):
            du = du + dw_ref[k:k + 1, :] * _tap(dy_s, dy_ph, CONF_K - 1 - k, TB)
            ddw_ref[k:k + 1, :] += jnp.sum(dyv_t * _tap(u_s, u_ph, HALO - (CONF_K - 1) + k, TB), axis=0, keepdims=True)
        dz_ref[:, 0:256] = (du * sig_b).astype(BF16)
        dz_ref[:, 256:512] = (du * a * sig_b * (1.0 - sig_b)).astype(BF16)

        dz_ref[:, 512:768] = (dq_ref[...] * SCALE).astype(BF16)
        dz_ref[:, 768:1024] = dk_ref[...].astype(BF16)
        dz_ref[:, 1024:1280] = dv_ref[...].astype(BF16)

        sc_h, sc_b, sc_c = z_ref[:, 512:768], z_ref[:, 768:1024], z_ref[:, 1024:1280]
        ch_s[0:HALO, :] = zp_ref[:, 1024:1280] * zp_ref[:, 512:768] * hm
        ch_s[HALO:HALO + TB, :] = sc_c * sc_h
        ch_s[HALO + TB:e2, :] = zn_ref[:, 1024:1280] * zn_ref[:, 512:768]
        cv = jnp.zeros((TB, DG), F32)
        for k in range(SC_K):
            cv = cv + sc_ref[k:k + 1, :] * ch_s[pl.ds(HALO - (SC_K - 1) + k, TB), :]
        dy_sc = dsp_ref[:, 0:256]
        dcv_t = dy_sc * sc_b
        dcv_s[0:TB, :] = dcv_t
        dcv_s[TB:e1, :] = dspn_ref[:, 0:256] * nm * zn_ref[:, 768:1024]
        dch = jnp.zeros((TB, DG), F32)
        for k in range(SC_K):
            dch = dch + sc_ref[k:k + 1, :] * dcv_s[pl.ds(SC_K - 1 - k, TB), :]
            dsc_ref[k:k + 1, :] += jnp.sum(dcv_t * ch_s[pl.ds(HALO - (SC_K - 1) + k, TB), :], axis=0, keepdims=True)
        dz_ref[:, 1280:1536] = (dch * sc_c).astype(BF16)
        dz_ref[:, 1536:1792] = (dy_sc * cv).astype(BF16)
        dz_ref[:, 1792:2048] = (dch * sc_h).astype(BF16)

        v_t = z_ref[:, 1280:1536]
        p0[0:HALO, :] = zp_ref[:, 1280:1536] * hm
        p0[HALO:HALO + TB, :] = v_t
        s2, s4, s8, s16 = _pool_window_sums(p0, p1, p2, p3)
        cnt, lane = _pool_counts(i * TB, e1)
        dlt = (_lane_group_select(lane[0:TB], s2, s4, s8, s16) / cnt[0:TB] - v_t).astype(BF16)
        dyp_t = dsp_ref[:, 256:512]
        dps_ref[...] += jnp.sum(dyp_t * _dot(dlt, pool_ref[...]), axis=0, keepdims=True)
        dpre = (jnp.concatenate([dyp_t, dspn_ref[:, 256:512] * nm], axis=0) * ps_ref[...]).astype(BF16)
        dpool_ref[...] += _dot_tn(dlt, dpre[0:TB])
        dd = _dot_nt(dpre, pool_ref[...])
        g0[...] = dd / cnt
        g1[0:TB + 24, :] = g0[pl.ds(0, TB + 24), :] + g0[pl.ds(1, TB + 24), :]
        g2[0:TB + 16, :] = g1[pl.ds(0, TB + 16), :] + g1[pl.ds(2, TB + 16), :]
        g3[0:TB + 8, :] = g2[pl.ds(0, TB + 8), :] + g2[pl.ds(4, TB + 8), :]
        f16 = g3[pl.ds(0, TB), :] + g3[pl.ds(8, TB), :]
        fwd_sum = _lane_group_select(lane[0:TB], g1[pl.ds(0, TB), :], g2[pl.ds(0, TB), :], g3[pl.ds(0, TB), :], f16)
        dz_ref[:, 2048:2304] = (fwd_sum - dd[0:TB]).astype(BF16)

    vec = _full((1, DG))
    mat = _full((DG, DG))
    scr = ([pltpu.VMEM((e2, DG), F32), pltpu.VMEM((e1, DG), F32), pltpu.VMEM((e2, DG), F32), pltpu.VMEM((e1, DG), F32)]
           + [pltpu.VMEM((HALO + TB, DG), F32)] * 4 + [pltpu.VMEM((e1, DG), F32)] * 4
           + [pltpu.VMEM((7, e2, DG), F32), pltpu.VMEM((7, e1, DG), F32)])
    return _pcall(
        body, name="branch_bwd", grid=(n_t,),
        in_specs=[_tok(1536), _halo_prev(1536), _halo_next(1536, t_len),
                  _tokcol(DG, 0), _halo_next(DG, t_len, 0), _tokcol(512, 1), _halo_next(512, t_len, 1),
                  _tok(DG), _tok(DG), _tok(DG),
                  _full((32, DG)), vec, vec, mat, _full((8, DG)), mat, vec],
        out_specs=[_tok(W_MAIN), _full((32, DG)), vec, vec, mat, _full((8, DG)), mat, vec],
        out_shape=[_sds((t_len, W_MAIN), BF16), _sds((32, DG), F32), _sds((1, DG), F32), _sds((1, DG), F32),
                   _sds((DG, DG), F32), _sds((8, DG), F32), _sds((DG, DG), F32), _sds((1, DG), F32)],
        scratch_shapes=scr, compiler_params=_cparams(1),
    )(zc, zc, zc, dcat, dcat, dcat, dcat, dq, dk, dv, w_dw, ln_g, ln_b, w_pw, w_sc, w_pool, pool_scale)


def _mixin_bwd(dh, h, g, dz, dfl, win, wf):
    t_len = dh.shape[0]

    def body(dh_ref, h_ref, g_ref, dz_ref, dfl_ref, win_ref, wf_ref, dh0_ref, xn_ref, dg_ref):
        _acc_init([dg_ref])
        dxn = _dot_nt(dz_ref[...], win_ref[...].reshape(D, W_MAIN)) + _dot_nt(dfl_ref[...].astype(BF16), wf_ref[...])
        dx, dg, xn = _rms_bwd(h_ref[...], g_ref[...], dxn)
        dg_ref[...] += dg
        xn_ref[...] = xn.astype(BF16)
        dh0_ref[...] = dh_ref[...] + dx

    return _pcall(
        body, name="mixin_bwd", grid=(t_len // TB,),
        in_specs=[_tok(D), _tok(D), _full((1, D)), _tok(W_MAIN), _tok(LANES), _full((N_DEV, D // N_DEV, W_MAIN)),
                  _full((D, LANES))],
        out_specs=[_tok(D), _tok(D), _full((1, D))],
        out_shape=[_sds((t_len, D), F32), _sds((t_len, D), BF16), _sds((1, D), F32)],
        compiler_params=_cparams(1),
    )(dh, h, g, dz, dfl, win, wf)


def _matmul_tn(name, a, b, tm, tn, out_dtype, block_major=False, a_section=0):
    t_len, n = b.shape
    m = a.shape[1]
    tk = min(t_len, 1024)
    n_k = t_len // tk

    def body(a_ref, b_ref, o_ref, acc):
        k = pl.program_id(2)

        @pl.when(k == 0)
        def _():
            acc[...] = jnp.zeros_like(acc)

        acc[...] += _dot_tn(a_ref[...].astype(BF16), b_ref[...].astype(BF16))

        @pl.when(k == n_k - 1)
        def _():
            if block_major:
                for blk in range(tn // FF_BLK):
                    o_ref[blk] = acc[:, blk * FF_BLK:(blk + 1) * FF_BLK].astype(out_dtype)
            else:
                o_ref[...] = acc[...].astype(out_dtype)

    if block_major:
        out_spec = pl.BlockSpec((tn // FF_BLK, tm, FF_BLK), lambda i, j, k: (j, i, 0))
        out_shape = _sds((n // FF_BLK, m, FF_BLK), out_dtype)
    else:
        out_spec = pl.BlockSpec((tm, tn), lambda i, j, k: (i, j))
        out_shape = _sds((m, n), out_dtype)
    return _pcall(
        body, name=name, grid=(m // tm, n // tn, n_k),
        in_specs=[pl.BlockSpec((tk, tm), lambda i, j, k: (a_section * n_k + k, i)), pl.BlockSpec((tk, tn), lambda i, j, k: (k, j))],
        out_specs=out_spec, out_shape=out_shape, scratch_shapes=[pltpu.VMEM((tm, tn), F32)], compiler_params=_cparams(3),
    )(a, b)


_HBM = pl.BlockSpec(memory_space=pltpu.HBM)


def _mesh_place():
    return lax.axis_index("x"), lax.axis_index("y"), lax.axis_index("c")


def _allgather(name, srcs):
    n = len(srcs)

    def body(*refs):
        src, dst = refs[:n], refs[n:2 * n]
        send_sems, recv_sems, local_sems = refs[2 * n:]
        x, y, c = _mesh_place()
        me, sibling = (x, y, c), (x, y, 1 - c)
        chips = [(1 - x, y), (x, 1 - y), (1 - x, 1 - y)]

        def slot(px, py, pc):
            return 4 * px + 2 * py + pc

        def copy(t, k, block, to, from_src=False):
            return pltpu.make_async_remote_copy(
                src_ref=src[t] if from_src else dst[t].at[slot(*block)], dst_ref=dst[t].at[slot(*block)],
                send_sem=send_sems.at[t, k], recv_sem=recv_sems.at[t, k], device_id=to, device_id_type=MESH_ID)

        mine = [pltpu.make_async_copy(src[t], dst[t].at[slot(*me)], local_sems.at[t]) for t in range(n)]
        for cp in mine:
            cp.start()
        started = []
        for t in range(n):
            started.append(copy(t, 0, me, sibling, from_src=True))
            started += [copy(t, 1 + j, me, (*chip, c), from_src=True) for j, chip in enumerate(chips)]
        for cp in started:
            cp.start()
        for j, chip in enumerate(chips):
            for t in range(n):
                copy(t, 1 + j, (*chip, c), me).wait_recv()
                fwd = copy(t, 4 + j, (*chip, c), sibling)
                fwd.start()
                started.append(fwd)
        for t in range(n):
            copy(t, 0, sibling, me).wait_recv()
            for j, chip in enumerate(chips):
                copy(t, 4 + j, (*chip, 1 - c), me).wait_recv()
        for cp in started:
            cp.wait_send()
        for cp in mine:
            cp.wait()

    return _pcall(
        body, name=name, in_specs=[_HBM] * n, out_specs=[_HBM] * n,
        out_shape=[_sds((N_DEV,) + s.shape, s.dtype) for s in srcs],
        scratch_shapes=[pltpu.SemaphoreType.DMA((n, 7)), pltpu.SemaphoreType.DMA((n, 7)), pltpu.SemaphoreType.DMA((n,))],
    )(*srcs)


def _exchange_ops(src, dst, per_dest, send_sems, recv_sems, local_sems):
    n = len(src)
    x, y, c = _mesh_place()
    me_slot = 4 * x + 2 * y + c
    peers = []
    for r in range(1, N_DEV):
        px, py, pc = x ^ ((r >> 2) & 1), y ^ ((r >> 1) & 1), c ^ (r & 1)
        peers.append(((px, py, pc), 4 * px + 2 * py + pc))

    def piece(t, dest_slot):
        return src[t].at[dest_slot] if per_dest[t] else src[t]

    def local(t):
        return pltpu.make_async_copy(piece(t, me_slot), dst[t].at[me_slot], local_sems.at[t])

    def remote(t, r, landing_slot):
        peer, peer_slot = peers[r]
        return pltpu.make_async_remote_copy(
            src_ref=piece(t, peer_slot), dst_ref=dst[t].at[landing_slot], send_sem=send_sems.at[t, r],
            recv_sem=recv_sems.at[t, r], device_id=peer, device_id_type=MESH_ID)

    def start():
        for t in range(n):
            local(t).start()
        for r in range(N_DEV - 1):
            for t in range(n):
                remote(t, r, me_slot).start()

    def wait():
        for r in range(N_DEV - 1):
            for t in range(n):
                remote(t, r, peers[r][1]).wait_recv()
        for r in range(N_DEV - 1):
            for t in range(n):
                remote(t, r, me_slot).wait_send()
        for t in range(n):
            local(t).wait()

    return start, wait


def _exchange_shapes(srcs, per_dest):
    return [_sds((N_DEV,) + tuple(s.shape[1:] if pd else s.shape), s.dtype) for s, pd in zip(srcs, per_dest)]


def _exchange_scratch(n):
    return [pltpu.SemaphoreType.DMA((n, N_DEV - 1)), pltpu.SemaphoreType.DMA((n, N_DEV - 1)), pltpu.SemaphoreType.DMA((n,))]


def _exchange(name, srcs, per_dest):
    n = len(srcs)

    def body(*refs):
        start, wait = _exchange_ops(refs[:n], refs[n:2 * n], per_dest, *refs[2 * n:])
        start()
        wait()

    return _pcall(
        body, name=name, in_specs=[_HBM] * n, out_specs=[_HBM] * n, out_shape=_exchange_shapes(srcs, per_dest),
        scratch_shapes=_exchange_scratch(n),
    )(*srcs)


def _adam_math(w, g, m, v):
    m = ADAM_B1 * m + (1.0 - ADAM_B1) * g
    v = ADAM_B2 * v + (1.0 - ADAM_B2) * (g * g)
    m_hat = m / (1.0 - ADAM_B1 ** ADAM_STEP)
    v_hat = v / (1.0 - ADAM_B2 ** ADAM_STEP)
    delta = -ADAM_LR * (m_hat / (jnp.sqrt(v_hat) + ADAM_EPS) + ADAM_WD * w)
    return delta, m, v


def _adam_rows(name, parts, w, m, v, row_tile):
    n_l, rows, cols = w.shape

    def body(*refs):
        p_refs = refs[:n_l]
        w_ref, m_ref, v_ref, g_out, d_out, m_out, v_out = refs[n_l:]
        layer = pl.program_id(0)
        for k in range(n_l):
            @pl.when(layer == k)
            def _(p_ref=p_refs[k]):
                g = p_ref[0].astype(F32)
                for s in range(1, N_DEV):
                    g = g + p_ref[s].astype(F32)
                delta, m_new, v_new = _adam_math(w_ref[...], g, m_ref[...], v_ref[...])
                g_out[...] = g
                d_out[...] = delta
                m_out[...] = m_new
                v_out[...] = v_new

    def part_spec(k):
        return pl.BlockSpec((N_DEV, row_tile, cols),
                            lambda l, i: (0, jnp.where(l == k, i, 0), 0))

    blk = pl.BlockSpec((None, row_tile, cols), lambda l, i: (l, i, 0))
    return _pcall(
        body, name=name, grid=(n_l, rows // row_tile),
        in_specs=[part_spec(k) for k in range(n_l)] + [blk, blk, blk],
        out_specs=[blk] * 4, out_shape=[_sds(w.shape, F32)] * 4, compiler_params=_cparams(2),
    )(*parts, w, m, v)


def _adam_packed(name, parts, w, m, v):
    def body(p_ref, w_ref, m_ref, v_ref, g_out, d_out, m_out, v_out):
        g = p_ref[0]
        for s in range(1, N_DEV):
            g = g + p_ref[s]
        delta, m_new, v_new = _adam_math(w_ref[...], g, m_ref[...], v_ref[...])
        g_out[...] = g
        d_out[...] = delta
        m_out[...] = m_new
        v_out[...] = v_new

    return _pcall(
        body, name=name, grid=(1,), in_specs=[_full(parts.shape), _full(w.shape), _full(w.shape), _full(w.shape)],
        out_specs=[_full(w.shape)] * 4, out_shape=[_sds(w.shape, F32)] * 4, compiler_params=_cparams(1),
    )(parts, w, m, v)


def _pack_rows(flat_parts, lead=()):
    flat = jnp.concatenate(flat_parts, axis=-1)
    n = flat.shape[-1]
    rows = -(-n // LANES)
    rows = -(-rows // 8) * 8
    flat = jnp.pad(flat, [(0, 0)] * len(lead) + [(0, rows * LANES - n)])
    return flat.reshape(lead + (rows, LANES))


def _unpack_rows(packed, shapes, lead=()):
    flat = packed.reshape(lead + (-1,))
    out, off = [], 0
    for shp in shapes:
        size = 1
        for s in shp:
            size *= s
        out.append(flat[..., off:off + size].reshape(lead + tuple(shp)))
        off += size
    return out


_SMALL_SHARD_SHAPES = [(N_LAYERS, 128, 4), (N_LAYERS, 32, DG), (N_LAYERS, D_PLE, 128), (N_LAYERS, CONF_K, 32), (N_LAYERS, SC_K, 32)]
_REP_SHAPES = [(N_LAYERS, D)] * 6 + [(N_LAYERS, DG)] * 3 + [(N_LAYERS, N_HEADS), (N_LAYERS, 4, 64, 64)]


def _small_full_to_shards(fcol, pw, proj, dw, sc):
    return [
        fcol.reshape(N_LAYERS, N_DEV, 128, 4).transpose(1, 0, 2, 3),
        pw.reshape(N_LAYERS, N_DEV, 32, DG).transpose(1, 0, 2, 3),
        proj.reshape(N_LAYERS, D_PLE, N_DEV, 128).transpose(2, 0, 1, 3),
        dw.reshape(N_LAYERS, CONF_K, N_DEV, 32).transpose(2, 0, 1, 3),
        sc.reshape(N_LAYERS, SC_K, N_DEV, 32).transpose(2, 0, 1, 3),
    ]


def _small_shards_to_full(fcol, pw, proj, dw, sc):
    return [
        fcol.transpose(1, 0, 2, 3).reshape(N_LAYERS, D, 4),
        pw.transpose(1, 0, 2, 3).reshape(N_LAYERS, DG, DG),
        proj.transpose(1, 2, 0, 3).reshape(N_LAYERS, D_PLE, D),
        dw.transpose(1, 2, 0, 3).reshape(N_LAYERS, CONF_K, DG),
        sc.transpose(1, 2, 0, 3).reshape(N_LAYERS, SC_K, DG),
    ]


def _pad_rows(a, rows):
    return jnp.pad(a, ((0, rows - a.shape[0]), (0, 0)))


def _block_diag4(w):
    z = jnp.zeros((64, 64), w.dtype)
    return jnp.concatenate([jnp.concatenate([w[g] if k == g else z for k in range(4)], axis=1) for g in range(4)], axis=0)


def kernel(x, p, g_mix_pre, w_in, b_forget, w_conf_dw, conf_ln_g, conf_ln_b, w_conf_pw, w_sc, w_pool, pool_scale, w_out, g_mix_post, g_mlp_pre, w_up, w_down, g_mlp_post, g_ple_pre, w_ple_gate, w_ple_proj, g_ple_post, loss_target, m_g_mix_pre, m_w_in, m_b_forget, m_w_conf_dw, m_conf_ln_g, m_conf_ln_b, m_w_conf_pw, m_w_sc, m_w_pool, m_pool_scale, m_w_out, m_g_mix_post, m_g_mlp_pre, m_w_up, m_w_down, m_g_mlp_post, m_g_ple_pre, m_w_ple_gate, m_w_ple_proj, m_g_ple_post, v_g_mix_pre, v_w_in, v_b_forget, v_w_conf_dw, v_conf_ln_g, v_conf_ln_b, v_w_conf_pw, v_w_sc, v_w_pool, v_pool_scale, v_w_out, v_g_mix_post, v_g_mlp_pre, v_w_up, v_w_down, v_g_mlp_post, v_g_ple_pre, v_w_ple_gate, v_w_ple_proj, v_g_ple_post):
    n_l = N_LAYERS
    t_len = x.shape[1]
    assert t_len % TB == 0 and x.shape[0] == 1 and x.shape[2] == D

    def main_cols(a):
        return jnp.concatenate([a[..., :F_LO], a[..., F_HI:]], axis=-1)

    def fcols(a):
        return a[..., F_LO:F_HI]

    def rows_pack(down, out, gate):
        return jnp.concatenate([down, out, gate], axis=1)

    rows_b = rows_pack(w_down, w_out, w_ple_gate).astype(BF16)
    win_b = main_cols(w_in).astype(BF16)
    wup_b = w_up.astype(BF16)
    small_local = _pack_rows([a.reshape(-1) for a in (fcols(w_in), w_conf_pw, w_ple_proj, w_conf_dw, w_sc)])
    rows_g, win_g, wup_g = [None] * n_l, [None] * n_l, [None] * n_l
    rows_g[0], win_g[0], wup_g[0], small_all = _allgather("weight_allgather", [rows_b[0], win_b[0], wup_b[0], small_local])
    small_g = _unpack_rows(small_all, _SMALL_SHARD_SHAPES, lead=(N_DEV,))
    fcol_f, pw_f, proj_f, dw_f, sc_f = _small_shards_to_full(*small_g)
    wf_b = jnp.pad(fcol_f, ((0, 0), (0, 0), (0, LANES - 4))).astype(BF16)
    pw_b, proj_b = pw_f.astype(BF16), proj_f.astype(BF16)
    dw_pad = jnp.pad(dw_f, ((0, 0), (0, 32 - CONF_K), (0, 0)))
    sc_pad = jnp.pad(sc_f, ((0, 0), (0, 8 - SC_K), (0, 0)))
    pool_bd = jnp.stack([_block_diag4(w_pool[l]) for l in range(n_l)]).astype(BF16)
    b_row = jnp.pad(b_forget, ((0, 0), (0, LANES - N_HEADS)))[:, None, :]

    def vec(a, l):
        return a[l][None, :]

    p_all = p.reshape(n_l * t_len, D_PLE)
    h = x[0]
    saved = []
    for l in range(n_l):
        zc, qkv, fl = _mixin_fwd(h, vec(g_mix_pre, l), win_g[l], wf_b[l])
        c, ct = _cumsum_fwd(fl, b_row[l])
        cat3 = _branch_fwd(zc, dw_pad[l], vec(conf_ln_g, l), vec(conf_ln_b, l), pw_b[l], sc_pad[l], pool_bd[l], vec(pool_scale, l))
        if l + 1 < n_l:
            o, lset, rows_g[l + 1], win_g[l + 1], wup_g[l + 1] = _attn_fwd(qkv, c, ct, [rows_b[l + 1], win_b[l + 1], wup_b[l + 1]])
        else:
            o, lset = _attn_fwd(qkv, c, ct)
        h1 = _mixout_fwd(h, cat3, o, rows_g[l], vec(g_mix_post, l))
        u, ff, h2 = _mlp_fwd(h1, vec(g_mlp_pre, l), wup_g[l], rows_g[l], vec(g_mlp_post, l))
        h3 = _ple_fwd(h2, p_all, l, vec(g_ple_pre, l), rows_g[l], proj_b[l], vec(g_ple_post, l))
        saved.append(dict(h0=h, zc=zc, qkv=qkv, fl=fl, c=c, ct=ct, cat3=cat3, o=o, lset=lset, h1=h1, u=u, ff=ff, h2=h2))
        h = h3

    dh, loss_part = _loss_bwd(h, loss_target[0])
    loss = lax.psum(loss_part[0, 0], ("x", "y", "c"))

    d_win = [None] * n_l
    r_down, r_out, r_gate, r_wup, r_win = ([None] * n_l for _ in range(5))
    small_grads = {k: [None] * n_l for k in ("fcol", "pw", "proj", "dw", "sc")}
    rep_grads = {k: [None] * n_l for k in ("g_mix_pre", "g_mix_post", "g_mlp_pre", "g_mlp_post", "g_ple_pre", "g_ple_post",
                                           "ln_g", "ln_b", "pool_scale", "b_forget", "w_pool")}
    for l in reversed(range(n_l)):
        s = saved[l]
        dh, dpp_b, dpre_b, hn3_b, dg_ple_post, dg_ple_pre = _ple_bwd(
            dh, s["h2"], p_all, l, vec(g_ple_post, l), vec(g_ple_pre, l), rows_g[l], proj_b[l])
        small_grads["proj"][l] = _matmul_tn("wgrad_proj", p_all, dpp_b, D_PLE, D, F32, a_section=l)
        d_gate = _matmul_tn("wgrad_gate", hn3_b, dpre_b, D, D, BF16).reshape(N_DEV, 128, D)
        dh, a2_b, du_b, dff_b, hn2_b, dg_mlp_post, dg_mlp_pre = _mlp_bwd(
            dh, s["h1"], s["u"], s["ff"], vec(g_mlp_post, l), vec(g_mlp_pre, l), wup_g[l], rows_g[l])
        d_down = _matmul_tn("wgrad_down", a2_b, dff_b, 2 * D, D, BF16).reshape(N_DEV, FF_BLK, D)
        d_wup = _matmul_tn("wgrad_up", hn2_b, du_b, D, 2 * FF_BLK, BF16, block_major=True)
        dcat, dmix_b, cat_b, dg_mix_post = _mixout_bwd(dh, s["cat3"], s["o"], vec(g_mix_post, l), rows_g[l])
        d_out = _matmul_tn("wgrad_out", cat_b, dmix_b, D, D, BF16).reshape(N_DEV, 128, D)
        dt, dob = _attn_bwd_dsum(s["qkv"], dcat, s["c"], s["ct"], s["lset"])
        riders = [d_down, d_out, d_gate, d_wup] + ([d_win[l + 1]] if l + 1 < n_l else [])
        dq, dk, dv, dc, r_down[l], r_out[l], r_gate[l], r_wup[l], *landed = _attn_bwd(
            s["qkv"], dob, s["c"], s["ct"], s["lset"], dt, riders)
        if landed:
            r_win[l + 1] = landed[0]
        dfl, db_f = _forget_bwd(dc, s["fl"], b_row[l])
        dz_b, ddw, dln_g, dln_b, dpw, dsc, dpool, dps = _branch_bwd(
            s["zc"], dcat, dq, dk, dv, dw_pad[l], vec(conf_ln_g, l), vec(conf_ln_b, l), pw_b[l], sc_pad[l], pool_bd[l],
            vec(pool_scale, l))
        dh, xn_b, dg_mix_pre = _mixin_bwd(dh, s["h0"], vec(g_mix_pre, l), dz_b, dfl, win_g[l], wf_b[l])
        d_win[l] = _matmul_tn("wgrad_in", xn_b, dz_b, D, W_MAIN // 2, BF16).reshape(N_DEV, 128, W_MAIN)
        small_grads["fcol"][l] = _matmul_tn("wgrad_fcol", xn_b, dfl, D, LANES, F32)[:, 0:4]
        small_grads["pw"][l], small_grads["dw"][l], small_grads["sc"][l] = dpw, ddw[0:CONF_K], dsc[0:SC_K]
        rep_grads["g_mix_pre"][l], rep_grads["g_mix_post"][l] = dg_mix_pre[0], dg_mix_post[0]
        rep_grads["g_mlp_pre"][l], rep_grads["g_mlp_post"][l] = dg_mlp_pre[0], dg_mlp_post[0]
        rep_grads["g_ple_pre"][l], rep_grads["g_ple_post"][l] = dg_ple_pre[0], dg_ple_post[0]
        rep_grads["ln_g"][l], rep_grads["ln_b"][l], rep_grads["pool_scale"][l] = dln_g[0], dln_b[0], dps[0]
        rep_grads["b_forget"][l] = db_f[0, 0:N_HEADS]
        rep_grads["w_pool"][l] = jnp.stack([dpool[64 * g:64 * g + 64, 64 * g:64 * g + 64] for g in range(4)])
    grad_x = dh[None]

    small_part = _pack_rows(
        [a.reshape(N_DEV, -1) for a in _small_full_to_shards(*[jnp.stack(small_grads[k]) for k in ("fcol", "pw", "proj", "dw", "sc")])],
        lead=(N_DEV,))
    rep_order = ("g_mix_pre", "g_mix_post", "g_mlp_pre", "g_mlp_post", "g_ple_pre", "g_ple_post", "ln_g", "ln_b",
                 "pool_scale", "b_forget", "w_pool")
    rep_part = _pack_rows([jnp.stack(rep_grads[k]).reshape(-1) for k in rep_order])
    r_win[0], r_small, r_rep = _exchange("grad_exchange", [d_win[0], small_part, rep_part], [True, True, False])

    res = {}
    res["w_down"] = _adam_rows("adam_down", r_down, w_down, m_w_down, v_w_down, 128)
    res["w_out"] = _adam_rows("adam_out", r_out, w_out, m_w_out, v_w_out, 128)
    res["w_ple_gate"] = _adam_rows("adam_gate", r_gate, w_ple_gate, m_w_ple_gate, v_w_ple_gate, 128)
    res["w_up"] = _adam_rows("adam_up", r_wup, w_up, m_w_up, v_w_up, 256)
    win_main = _adam_rows("adam_in", r_win, main_cols(w_in), main_cols(m_w_in), main_cols(v_w_in), 128)

    small_w = [(fcols(w_in), w_conf_pw, w_ple_proj, w_conf_dw, w_sc), (fcols(m_w_in), m_w_conf_pw, m_w_ple_proj, m_w_conf_dw, m_w_sc),
               (fcols(v_w_in), v_w_conf_pw, v_w_ple_proj, v_w_conf_dw, v_w_sc)]
    small_packed = [_pack_rows([a.reshape(-1) for a in grp]) for grp in small_w]
    small_res = [_unpack_rows(a, _SMALL_SHARD_SHAPES) for a in _adam_packed("adam_small", r_small, *small_packed)]
    rep_w = [(g_mix_pre, g_mix_post, g_mlp_pre, g_mlp_post, g_ple_pre, g_ple_post, conf_ln_g, conf_ln_b, pool_scale, b_forget, w_pool),
             (m_g_mix_pre, m_g_mix_post, m_g_mlp_pre, m_g_mlp_post, m_g_ple_pre, m_g_ple_post, m_conf_ln_g, m_conf_ln_b, m_pool_scale,
              m_b_forget, m_w_pool),
             (v_g_mix_pre, v_g_mix_post, v_g_mlp_pre, v_g_mlp_post, v_g_ple_pre, v_g_ple_post, v_conf_ln_g, v_conf_ln_b, v_pool_scale,
              v_b_forget, v_w_pool)]
    rep_packed = [_pack_rows([a.reshape(-1) for a in grp]) for grp in rep_w]
    rep_res = [_unpack_rows(a, _REP_SHAPES) for a in _adam_packed("adam_replicated", r_rep, *rep_packed)]

    for kind in range(4):
        fc, pw, proj, dwc, scc = small_res[kind]
        main = win_main[kind]
        (rg_mix_pre, rg_mix_post, rg_mlp_pre, rg_mlp_post, rg_ple_pre, rg_ple_post, r_ln_g, r_ln_b, r_ps, r_bf, r_wpool) = rep_res[kind]
        res.setdefault("by_kind", []).append(dict(
            g_mix_pre=rg_mix_pre, w_in=jnp.concatenate([main[..., :F_LO], fc, main[..., F_LO:]], axis=-1), b_forget=r_bf,
            w_conf_dw=dwc, conf_ln_g=r_ln_g, conf_ln_b=r_ln_b, w_conf_pw=pw, w_sc=scc, w_pool=r_wpool, pool_scale=r_ps,
            w_out=res["w_out"][kind], g_mix_post=rg_mix_post, g_mlp_pre=rg_mlp_pre, w_up=res["w_up"][kind],
            w_down=res["w_down"][kind], g_mlp_post=rg_mlp_post, g_ple_pre=rg_ple_pre, w_ple_gate=res["w_ple_gate"][kind],
            w_ple_proj=proj, g_ple_post=rg_ple_post))
    names = ("g_mix_pre", "w_in", "b_forget", "w_conf_dw", "conf_ln_g", "conf_ln_b", "w_conf_pw", "w_sc", "w_pool", "pool_scale",
             "w_out", "g_mix_post", "g_mlp_pre", "w_up", "w_down", "g_mlp_post", "g_ple_pre", "w_ple_gate", "w_ple_proj", "g_ple_post")
    outs = [loss, grad_x]
    for kind in range(4):
        outs += [res["by_kind"][kind][nm] for nm in names]
    return tuple(outs)
```

```python
import jax
import jax.numpy as jnp
from jax import lax
from jax.experimental import pallas as pl
from jax.experimental.pallas import tpu as pltpu

F32, BF16 = jnp.float32, jnp.bfloat16

D = 1024
DG = 256
N_HEADS = 4
HEAD_DIM = 64
CONF_K = 31
SC_K = 3
POOL_WINDOWS = (2, 4, 8, 16)
D_FF = 4096
D_PLE = 256
N_LAYERS = 4
N_DEV = 8
EPS = 1e-6
SCALE = HEAD_DIM ** -0.5
W_MAIN = 2304
F_LO, F_HI = 1280, 1284

ADAM_LR, ADAM_B1, ADAM_B2, ADAM_EPS, ADAM_WD, ADAM_STEP = 0.001, 0.9, 0.999, 1e-08, 0.01, 10

TB = 512
HALO = 32
LANES = 128
FF_BLK = D_FF // N_DEV
MLP_BWD_BLOCKS = 2
VMEM_LIMIT = 56 * 1024 * 1024

NT_DIMS = (((1,), (1,)), ((), ()))
TN_DIMS = (((0,), (0,)), ((), ()))
MESH_ID = pl.DeviceIdType.MESH


def _pcall(body, **kw):
    return pl.pallas_call(body, **kw)


def _cparams(n_axes):
    return pltpu.CompilerParams(dimension_semantics=("arbitrary",) * n_axes, vmem_limit_bytes=VMEM_LIMIT)


def _sds(shape, dtype):
    return jax.ShapeDtypeStruct(shape, dtype)


def _tok(width, tb=TB):
    return pl.BlockSpec((tb, width), lambda i: (i, 0))


def _tokcol(width, col):
    return pl.BlockSpec((TB, width), lambda i: (i, col))


def _full(shape):
    zeros = (0,) * len(shape)
    return pl.BlockSpec(shape, lambda *_: zeros)


def _resident(shape):
    zeros = (0,) * len(shape)
    return pl.BlockSpec(shape, lambda *_: zeros, pipeline_mode=pl.Buffered(1))


def _halo_prev(width, col=0):
    return pl.BlockSpec((HALO, width), lambda i: (jnp.maximum(i * (TB // HALO) - 1, 0), col))


def _halo_next(width, n_rows, col=0):
    last = n_rows // HALO - 1
    return pl.BlockSpec((HALO, width), lambda i: (jnp.minimum((i + 1) * (TB // HALO), last), col))


def _dot(a, b):
    return jnp.dot(a, b, preferred_element_type=F32)


def _dot_nt(a, b):
    return lax.dot_general(a, b, NT_DIMS, preferred_element_type=F32)


def _dot_tn(a, b):
    return lax.dot_general(a, b, TN_DIMS, preferred_element_type=F32)


def _dot_exact(a, b):
    return jnp.dot(a, b, precision=lax.Precision.HIGHEST, preferred_element_type=F32)


def _rms(x, g):
    r = lax.rsqrt(jnp.mean(x * x, axis=-1, keepdims=True) + EPS)
    return x * r * g


def _rms_bwd(x, g, dy):
    r = lax.rsqrt(jnp.mean(x * x, axis=-1, keepdims=True) + EPS)
    n = x * r
    dg = jnp.sum(dy * n, axis=0, keepdims=True)
    dn = dy * g
    dx = r * (dn - n * jnp.mean(dn * n, axis=-1, keepdims=True))
    return dx, dg, n * g


def _sigmoid(x):
    return jax.nn.sigmoid(x)


def _log_sigmoid(x):
    return jnp.minimum(x, 0.0) - jnp.log(1.0 + jnp.exp(-jnp.abs(x)))


def _lane_group_select(lane, v2, v4, v8, v16):
    return jnp.where(lane < 64, v2, jnp.where(lane < 128, v4, jnp.where(lane < 192, v8, v16)))


def _pool_counts(t0, rows):
    lane = lax.broadcasted_iota(jnp.int32, (rows, DG), 1)
    t = lax.broadcasted_iota(jnp.int32, (rows, DG), 0) + t0
    win = _lane_group_select(lane, 2, 4, 8, 16)
    return jnp.minimum(t + 1, win).astype(F32), lane


def _mixin_fwd(h, g, win, wf):
    t_len = h.shape[0]

    def body(h_ref, g_ref, win_ref, wf_ref, zc_ref, qkv_ref, fl_ref):
        xn = _rms(h_ref[...], g_ref[...]).astype(BF16)
        z = _dot(xn, win_ref[...].reshape(D, W_MAIN))
        zc_ref[:, 0:512] = z[:, 0:512]
        zc_ref[:, 512:1536] = z[:, 1280:2304]
        qkv_ref[:, 0:256] = (z[:, 512:768] * SCALE).astype(BF16)
        qkv_ref[:, 256:768] = z[:, 768:1280].astype(BF16)
        fl_ref[...] = _dot(xn, wf_ref[...])

    return _pcall(
        body, name="mixin_fwd", grid=(t_len // TB,),
        in_specs=[_tok(D), _full((1, D)), _full((N_DEV, D // N_DEV, W_MAIN)), _full((D, LANES))],
        out_specs=[_tok(1536), _tok(768), _tok(LANES)],
        out_shape=[_sds((t_len, 1536), F32), _sds((t_len, 768), BF16), _sds((t_len, LANES), F32)],
        compiler_params=_cparams(1),
    )(h, g, win, wf)


def _transpose_lanes8(x):
    eye = (lax.broadcasted_iota(jnp.int32, (8, LANES), 0) == lax.broadcasted_iota(jnp.int32, (8, LANES), 1)).astype(F32)
    return lax.dot_general(eye, x, NT_DIMS, precision=lax.Precision.HIGHEST, preferred_element_type=F32)


def _cumsum_fwd(fl, b_row):
    t_len = fl.shape[0]

    def body(fl_ref, b_ref, c_ref, ct_ref, carry):
        @pl.when(pl.program_id(0) == 0)
        def _():
            carry[...] = jnp.zeros_like(carry)

        r = lax.broadcasted_iota(jnp.int32, (TB, TB), 0)
        s = lax.broadcasted_iota(jnp.int32, (TB, TB), 1)
        lf = _log_sigmoid(fl_ref[...] + b_ref[...])
        c = _dot_exact((r >= s).astype(F32), lf) + carry[0:1, :]
        c_ref[...] = c
        ct_ref[...] = _transpose_lanes8(c)
        carry[...] += jnp.sum(lf, axis=0, keepdims=True)

    return _pcall(
        body, name="cumsum_fwd", grid=(t_len // TB,),
        in_specs=[_tok(LANES), _full((1, LANES))],
        out_specs=[_tok(LANES), pl.BlockSpec((8, TB), lambda i: (0, i))],
        out_shape=[_sds((t_len, LANES), F32), _sds((8, t_len), F32)],
        scratch_shapes=[pltpu.VMEM((8, LANES), F32)],
        compiler_params=_cparams(1),
    )(fl, b_row)


def _layer_norm_parts(y, g, b):
    mu = jnp.mean(y, axis=-1, keepdims=True)
    yc = y - mu
    r = lax.rsqrt(jnp.mean(yc * yc, axis=-1, keepdims=True) + EPS)
    n = yc * r
    return n, r, n * g + b


def _phase_copies(src, dst, rows):
    for p in range(1, 8):
        dst[p - 1, 0:rows - 8, :] = src[pl.ds(p, rows - 8), :]


def _tap(src, copies, off, n):
    p = off % 8
    return src[pl.ds(off, n), :] if p == 0 else copies[p - 1, pl.ds(off - p, n), :]


def _pool_window_sums(p0, p1, p2, p3):
    e = HALO + TB
    p1[8:e, :] = p0[pl.ds(8, e - 8), :] + p0[pl.ds(7, e - 8), :]
    p2[16:e, :] = p1[pl.ds(16, e - 16), :] + p1[pl.ds(14, e - 16), :]
    p3[24:e, :] = p2[pl.ds(24, e - 24), :] + p2[pl.ds(20, e - 24), :]
    s16 = p3[pl.ds(HALO, TB), :] + p3[pl.ds(HALO - 8, TB), :]
    return p1[pl.ds(HALO, TB), :], p2[pl.ds(HALO, TB), :], p3[pl.ds(HALO, TB), :], s16


def _branch_fwd(zc, w_dw, ln_g, ln_b, w_pw, w_sc, w_pool, pool_scale):
    t_len = zc.shape[0]
    e = HALO + TB

    def body(z_ref, zh_ref, dw_ref, g_ref, b_ref, pw_ref, sc_ref, pool_ref, ps_ref, cat_ref, u_s, ch_s, p0, p1, p2, p3, u_ph):
        i = pl.program_id(0)
        hm = (i > 0).astype(F32)
        u_s[0:HALO, :] = zh_ref[:, 0:256] * _sigmoid(zh_ref[:, 256:512]) * hm
        u_s[HALO:e, :] = z_ref[:, 0:256] * _sigmoid(z_ref[:, 256:512])
        _phase_copies(u_s, u_ph, e)
        y = jnp.zeros((TB, DG), F32)
        for k in range(CONF_K):
            y = y + dw_ref[k:k + 1, :] * _tap(u_s, u_ph, HALO - (CONF_K - 1) + k, TB)
        _, _, yn = _layer_norm_parts(y, g_ref[...], b_ref[...])
        s = yn * _sigmoid(yn)
        cat_ref[:, 0:256] = _dot(s.astype(BF16), pw_ref[...])
        ch_s[0:HALO, :] = zh_ref[:, 1024:1280] * zh_ref[:, 512:768] * hm
        ch_s[HALO:e, :] = z_ref[:, 1024:1280] * z_ref[:, 512:768]
        cv = jnp.zeros((TB, DG), F32)
        for k in range(SC_K):
            cv = cv + sc_ref[k:k + 1, :] * ch_s[pl.ds(HALO - (SC_K - 1) + k, TB), :]
        cat_ref[:, 256:512] = z_ref[:, 768:1024] * cv
        p0[0:HALO, :] = zh_ref[:, 1280:1536] * hm
        p0[HALO:e, :] = z_ref[:, 1280:1536]
        s2, s4, s8, s16 = _pool_window_sums(p0, p1, p2, p3)
        cnt, lane = _pool_counts(i * TB, TB)
        dlt = _lane_group_select(lane, s2, s4, s8, s16) / cnt - z_ref[:, 1280:1536]
        cat_ref[:, 512:768] = _dot(dlt.astype(BF16), pool_ref[...]) * ps_ref[...]

    scr = [pltpu.VMEM((e, DG), F32) for _ in range(6)] + [pltpu.VMEM((7, e, DG), F32)]
    return _pcall(
        body, name="branch_fwd", grid=(t_len // TB,),
        in_specs=[_tok(1536), _halo_prev(1536), _full((32, DG)), _full((1, DG)), _full((1, DG)), _full((DG, DG)),
                  _full((8, DG)), _full((DG, DG)), _full((1, DG))],
        out_specs=_tok(768), out_shape=_sds((t_len, 768), F32), scratch_shapes=scr, compiler_params=_cparams(1),
    )(zc, zc, w_dw, ln_g, ln_b, w_pw, w_sc, w_pool, pool_scale)


def _head_masks(rows):
    lane = lax.broadcasted_iota(jnp.int32, (rows, LANES), 1)
    return lane, (lane < HEAD_DIM, lane >= HEAD_DIM)


def _keep_lanes(mask, x):
    return jnp.where(mask, x.astype(F32), 0.0).astype(BF16)


def _with_exchange(refs, n_in, n_out, n_x, per_dest, first, last):
    ins, x_src = refs[:n_in], refs[n_in:n_in + n_x]
    outs, x_dst = refs[n_in + n_x:n_in + n_x + n_out], refs[n_in + n_x + n_out:n_in + 2 * n_x + n_out]
    begin = finish = None
    if n_x:
        start, wait = _exchange_ops(x_src, x_dst, [per_dest] * n_x, *refs[n_in + 2 * n_x + n_out:])

        def begin():
            pl.when(first)(start)

        def finish():
            pl.when(last)(wait)

    return ins, outs, begin, finish


def _attn_fwd(qkv, c, ct, bcast=()):
    t_len = qkv.shape[0]
    n_t = t_len // TB
    n_x = len(bcast)

    def body(*refs):
        i = pl.program_id(0)
        (q_ref, k_ref, v_ref, c_ref, ct_ref), (o_ref, lset_ref), begin, finish = _with_exchange(
            refs, 5, 2, n_x, False, i == 0, i == n_t - 1)
        if begin:
            begin()
        lane, halves = _head_masks(TB)
        crow = c_ref[...]
        causal = lax.broadcasted_iota(jnp.int32, (TB, TB), 0) >= lax.broadcasted_iota(jnp.int32, (TB, TB), 1)
        lse_out = jnp.zeros((TB, LANES), F32)
        for g in range(2):
            cols = slice(g * LANES, (g + 1) * LANES)
            qg = q_ref[:, cols]
            qms = [_keep_lanes(halves[hh], qg) for hh in range(2)]
            cqs = [jnp.sum(jnp.where(lane == 2 * g + hh, crow, 0.0), axis=1, keepdims=True) for hh in range(2)]

            def block(j, carry, masked):
                off = pl.multiple_of(j * TB, TB)
                kj = k_ref[pl.ds(off, TB), cols]
                vj = v_ref[pl.ds(off, TB), cols]
                new = []
                for hh in range(2):
                    m, l, acc = carry[hh]
                    s = _dot_nt(qms[hh], kj) + (cqs[hh] - ct_ref[2 * g + hh:2 * g + hh + 1, pl.ds(off, TB)])
                    if masked:
                        s = jnp.where(causal, s, -jnp.inf)
                    m_new = jnp.maximum(m, jnp.max(s, axis=1, keepdims=True))
                    alpha = jnp.exp(m - m_new)
                    p = jnp.exp(s - m_new)
                    l = alpha * l + jnp.sum(p, axis=1, keepdims=True)
                    acc = alpha * acc + _dot(p.astype(BF16), vj)
                    new.append((m_new, l, acc))
                return tuple(new)

            init = tuple((jnp.full((TB, 1), -jnp.inf, F32), jnp.zeros((TB, 1), F32), jnp.zeros((TB, LANES), F32))
                         for _ in range(2))
            carry = lax.fori_loop(0, i, lambda j, cr: block(j, cr, False), init)
            (m0, l0, acc0), (m1, l1, acc1) = block(i, carry, True)
            o_ref[:, cols] = jnp.where(halves[0], acc0 / l0, acc1 / l1)
            lse_out = jnp.where(lane == 2 * g, m0 + jnp.log(l0), lse_out)
            lse_out = jnp.where(lane == 2 * g + 1, m1 + jnp.log(l1), lse_out)
        lset_ref[...] = _transpose_lanes8(lse_out)
        if finish:
            finish()

    return _pcall(
        body, name="attn_fwd_gather" if n_x else "attn_fwd", grid=(n_t,),
        in_specs=[_tokcol(DG, 0), pl.BlockSpec((t_len, DG), lambda i: (0, 1)), pl.BlockSpec((t_len, DG), lambda i: (0, 2)),
                  _tok(LANES), _full((8, t_len))] + [_HBM] * n_x,
        out_specs=[_tok(DG), pl.BlockSpec((8, TB), lambda i: (0, i))] + [_HBM] * n_x,
        out_shape=[_sds((t_len, DG), F32), _sds((8, t_len), F32)] + _exchange_shapes(bcast, [False] * n_x),
        scratch_shapes=_exchange_scratch(n_x) if n_x else [],
        compiler_params=_cparams(1),
    )(qkv, qkv, qkv, c, ct, *bcast)


def _mix_projection(cat_ref, o_ref, w):
    return (_dot(cat_ref[:, 0:256].astype(BF16), w[0:256]) + _dot(o_ref[...].astype(BF16), w[256:512])
            + _dot(cat_ref[:, 256:768].astype(BF16), w[512:1024]))


def _mixout_fwd(h, cat3, o, w_rows, g):
    t_len = h.shape[0]

    def body(h_ref, cat_ref, o_ref, w_ref, g_ref, h1_ref):
        mix = _mix_projection(cat_ref, o_ref, w_ref[...].reshape(D, D))
        h1_ref[...] = h_ref[...] + _rms(mix, g_ref[...])

    return _pcall(
        body, name="mixout_fwd", grid=(t_len // TB,),
        in_specs=[_tok(D), _tok(768), _tok(DG), pl.BlockSpec((N_DEV, 128, D), lambda i: (0, 4, 0)), _full((1, D))],
        out_specs=_tok(D), out_shape=_sds((t_len, D), F32), compiler_params=_cparams(1),
    )(h, cat3, o, w_rows, g)


def _mlp_fwd(h, g_pre, w_up, w_rows, g_post):
    t_len = h.shape[0]

    def body(h_ref, g1_ref, up_ref, dn_ref, g2_ref, u_ref, ff_ref, h2_ref):
        hn = _rms(h_ref[...], g1_ref[...]).astype(BF16)
        ff = jnp.zeros((TB, D), F32)
        for j in range(N_DEV):
            u = _dot(hn, up_ref[j])
            u_ref[:, j * FF_BLK:(j + 1) * FF_BLK] = u.astype(BF16)
            r = jnp.maximum(u, 0.0)
            ff = ff + _dot((r * r).astype(BF16), dn_ref[j])
        ff_ref[...] = ff
        h2_ref[...] = h_ref[...] + _rms(ff, g2_ref[...])

    return _pcall(
        body, name="mlp_fwd", grid=(t_len // TB,),
        in_specs=[_tok(D), _full((1, D)), _resident((N_DEV, D, FF_BLK)), _resident((N_DEV, FF_BLK, D)), _full((1, D))],
        out_specs=[_tok(D_FF), _tok(D), _tok(D)],
        out_shape=[_sds((t_len, D_FF), BF16), _sds((t_len, D), F32), _sds((t_len, D), F32)],
        compiler_params=_cparams(1),
    )(h, g_pre, w_up, w_rows, g_post)


def _ple_fwd(h, p_all, layer, g_pre, w_rows, w_proj, g_post):
    t_len = h.shape[0]
    n_t = t_len // TB

    def body(h_ref, p_ref, g1_ref, wg_ref, wp_ref, g2_ref, h3_ref):
        pp, gate = _ple_parts(h_ref, p_ref, g1_ref, wg_ref, wp_ref)
        h3_ref[...] = h_ref[...] + _rms(pp * gate, g2_ref[...])

    return _pcall(
        body, name="ple_fwd", grid=(n_t,),
        in_specs=[_tok(D), pl.BlockSpec((TB, D_PLE), lambda i: (layer * n_t + i, 0)), _full((1, D)),
                  pl.BlockSpec((N_DEV, 128, D), lambda i: (0, 5, 0)), _full((D_PLE, D)), _full((1, D))],
        out_specs=_tok(D), out_shape=_sds((t_len, D), F32), compiler_params=_cparams(1),
    )(h, p_all, g_pre, w_rows, w_proj, g_post)


def _ple_parts(h_ref, p_ref, g1_ref, wg_ref, wp_ref):
    hn = _rms(h_ref[...], g1_ref[...]).astype(BF16)
    gate = _sigmoid(_dot(hn, wg_ref[...].reshape(D, D)))
    return _dot(p_ref[...].astype(BF16), wp_ref[...]), gate


def _loss_bwd(h, target):
    t_len = h.shape[0]

    def body(h_ref, t_ref, dh_ref, loss_ref):
        @pl.when(pl.program_id(0) == 0)
        def _():
            loss_ref[...] = jnp.zeros_like(loss_ref)

        d = h_ref[...] - t_ref[...]
        dh_ref[...] = d * (1.0 / D)
        loss_ref[...] += 0.5 * jnp.sum(jnp.mean(d * d, axis=-1, keepdims=True), axis=0, keepdims=True)

    return _pcall(
        body, name="loss_bwd", grid=(t_len // TB,), in_specs=[_tok(D), _tok(D)],
        out_specs=[_tok(D), _full((8, LANES))], out_shape=[_sds((t_len, D), F32), _sds((8, LANES), F32)],
        compiler_params=_cparams(1),
    )(h, target)


def _acc_init(refs):
    @pl.when(pl.program_id(0) == 0)
    def _():
        for r in refs:
            r[...] = jnp.zeros_like(r)


def _ple_bwd(dh3, h2, p_all, layer, g_post, g_pre, w_rows, w_proj):
    t_len = dh3.shape[0]
    n_t = t_len // TB

    def body(dh_ref, h_ref, p_ref, g2_ref, g1_ref, wg_ref, wp_ref, dh2_ref, dpp_ref, dpre_ref, hn_ref, dg2_ref, dg1_ref):
        _acc_init([dg2_ref, dg1_ref])
        dh = dh_ref[...]
        pp, gate = _ple_parts(h_ref, p_ref, g1_ref, wg_ref, wp_ref)
        de, dg2, _ = _rms_bwd(pp * gate, g2_ref[...], dh)
        dg2_ref[...] += dg2
        dpp_ref[...] = (de * gate).astype(BF16)
        dpre = (de * pp * gate * (1.0 - gate)).astype(BF16)
        dpre_ref[...] = dpre
        dhn = _dot_nt(dpre, wg_ref[...].reshape(D, D))
        dx, dg1, hn = _rms_bwd(h_ref[...], g1_ref[...], dhn)
        dg1_ref[...] += dg1
        hn_ref[...] = hn.astype(BF16)
        dh2_ref[...] = dh + dx

    return _pcall(
        body, name="ple_bwd", grid=(n_t,),
        in_specs=[_tok(D), _tok(D), pl.BlockSpec((TB, D_PLE), lambda i: (layer * n_t + i, 0)), _full((1, D)), _full((1, D)),
                  pl.BlockSpec((N_DEV, 128, D), lambda i: (0, 5, 0)), _full((D_PLE, D))],
        out_specs=[_tok(D)] * 4 + [_full((1, D))] * 2,
        out_shape=[_sds((t_len, D), F32)] + [_sds((t_len, D), BF16)] * 3 + [_sds((1, D), F32)] * 2,
        compiler_params=_cparams(1),
    )(dh3, h2, p_all, g_post, g_pre, w_rows, w_proj)


def _mlp_bwd(dh2, h1, u, ff, g_post, g_pre, w_up, w_rows):
    t_len = dh2.shape[0]

    def body(dh_ref, h_ref, u_ref, ff_ref, g2_ref, g1_ref, up_ref, dn_ref,
             dh1_ref, a2_ref, du_ref, dff_ref, hn_ref, dg2_ref, dg1_ref, dff_s, acc_s):
        i, j = pl.program_id(0), pl.program_id(1)

        @pl.when((i == 0) & (j == 0))
        def _():
            dg2_ref[...] = jnp.zeros_like(dg2_ref)
            dg1_ref[...] = jnp.zeros_like(dg1_ref)

        @pl.when(j == 0)
        def _():
            dff, dg2, _ = _rms_bwd(ff_ref[...], g2_ref[...], dh_ref[...])
            dg2_ref[...] += dg2
            dff_s[...] = dff.astype(BF16)
            dff_ref[...] = dff.astype(BF16)
            acc_s[...] = jnp.zeros_like(acc_s)

        acc = acc_s[...]
        for b in range(MLP_BWD_BLOCKS):
            cols = slice(b * FF_BLK, (b + 1) * FF_BLK)
            r = jnp.maximum(u_ref[:, cols].astype(F32), 0.0)
            a2_ref[:, cols] = (r * r).astype(BF16)
            du = (_dot_nt(dff_s[...], dn_ref[MLP_BWD_BLOCKS * j + b]) * (2.0 * r)).astype(BF16)
            du_ref[:, cols] = du
            acc = acc + _dot_nt(du, up_ref[MLP_BWD_BLOCKS * j + b])
        acc_s[...] = acc

        @pl.when(j == N_DEV // MLP_BWD_BLOCKS - 1)
        def _():
            dx, dg1, hn = _rms_bwd(h_ref[...], g1_ref[...], acc_s[...])
            dg1_ref[...] += dg1
            hn_ref[...] = hn.astype(BF16)
            dh1_ref[...] = dh_ref[...] + dx

    tok2 = pl.BlockSpec((TB, D), lambda i, j: (i, 0))
    vec2 = pl.BlockSpec((1, D), lambda i, j: (0, 0))
    blk2 = pl.BlockSpec((TB, MLP_BWD_BLOCKS * FF_BLK), lambda i, j: (i, j))
    return _pcall(
        body, name="mlp_bwd", grid=(t_len // TB, N_DEV // MLP_BWD_BLOCKS),
        in_specs=[tok2, tok2, blk2, tok2, vec2, vec2, _resident((N_DEV, D, FF_BLK)), _resident((N_DEV, FF_BLK, D))],
        out_specs=[tok2, blk2, blk2, tok2, tok2, vec2, vec2],
        out_shape=[_sds((t_len, D), F32), _sds((t_len, D_FF), BF16), _sds((t_len, D_FF), BF16), _sds((t_len, D), BF16),
                   _sds((t_len, D), BF16), _sds((1, D), F32), _sds((1, D), F32)],
        scratch_shapes=[pltpu.VMEM((TB, D), BF16), pltpu.VMEM((TB, D), F32)], compiler_params=_cparams(2),
    )(dh2, h1, u, ff, g_post, g_pre, w_up, w_rows)


def _mixout_bwd(dh1, cat3, o, g, w_rows):
    t_len = dh1.shape[0]

    def body(dh_ref, cat_ref, o_ref, g_ref, w_ref, dcat_ref, dmix_ref, catb_ref, dg_ref):
        _acc_init([dg_ref])
        w = w_ref[...].reshape(D, D)
        dmix, dg, _ = _rms_bwd(_mix_projection(cat_ref, o_ref, w), g_ref[...], dh_ref[...])
        dg_ref[...] += dg
        dmix = dmix.astype(BF16)
        dmix_ref[...] = dmix
        dcat_ref[...] = _dot_nt(dmix, w)
        catb_ref[:, 0:256] = cat_ref[:, 0:256].astype(BF16)
        catb_ref[:, 256:512] = o_ref[...].astype(BF16)
        catb_ref[:, 512:1024] = cat_ref[:, 256:768].astype(BF16)

    return _pcall(
        body, name="mixout_bwd", grid=(t_len // TB,),
        in_specs=[_tok(D), _tok(768), _tok(DG), _full((1, D)), pl.BlockSpec((N_DEV, 128, D), lambda i: (0, 4, 0))],
        out_specs=[_tok(D), _tok(D), _tok(D), _full((1, D))],
        out_shape=[_sds((t_len, D), F32), _sds((t_len, D), BF16), _sds((t_len, D), BF16), _sds((1, D), F32)],
        compiler_params=_cparams(1),
    )(dh1, cat3, o, g, w_rows)


def _attn_bwd_dsum(qkv, dcat, c, ct, lset):
    t_len = qkv.shape[0]

    def body(q_ref, do_ref, k_ref, v_ref, c_ref, ct_ref, lset_ref, dt_ref, dob_ref):
        i = pl.program_id(0)
        lane, halves = _head_masks(TB)
        causal_t = lax.broadcasted_iota(jnp.int32, (TB, TB), 1) >= lax.broadcasted_iota(jnp.int32, (TB, TB), 0)
        sub = lax.broadcasted_iota(jnp.int32, (8, TB), 0)
        dob_ref[...] = do_ref[...].astype(BF16)
        out = jnp.zeros((8, TB), F32)
        for g in range(2):
            cols = slice(g * LANES, (g + 1) * LANES)
            qi = q_ref[:, cols]
            doi = do_ref[:, cols].astype(BF16)
            doms = [_keep_lanes(halves[hh], doi) for hh in range(2)]
            cqs = [ct_ref[2 * g + hh:2 * g + hh + 1, :] for hh in range(2)]
            lses = [lset_ref[2 * g + hh:2 * g + hh + 1, :] for hh in range(2)]

            def block(j, accs, masked):
                off = pl.multiple_of(j * TB, TB)
                kj = k_ref[pl.ds(off, TB), cols]
                vj = v_ref[pl.ds(off, TB), cols]
                cj = c_ref[pl.ds(off, TB), :]
                new = []
                for hh in range(2):
                    ck = jnp.sum(jnp.where(lane == 2 * g + hh, cj, 0.0), axis=1, keepdims=True)
                    st = _dot_nt(_keep_lanes(halves[hh], kj), qi) + (cqs[hh] - ck)
                    if masked:
                        st = jnp.where(causal_t, st, -jnp.inf)
                    pt = jnp.exp(st - lses[hh])
                    new.append(accs[hh] + jnp.sum(pt * _dot_nt(vj, doms[hh]), axis=0, keepdims=True))
                return tuple(new)

            init = (jnp.zeros((1, TB), F32), jnp.zeros((1, TB), F32))
            accs = block(i, lax.fori_loop(0, i, lambda j, cr: block(j, cr, False), init), True)
            out = jnp.where(sub == 2 * g, accs[0], out)
            out = jnp.where(sub == 2 * g + 1, accs[1], out)
        dt_ref[...] = out

    row8 = pl.BlockSpec((8, TB), lambda i: (0, i))
    return _pcall(
        body, name="attn_bwd_dsum", grid=(t_len // TB,),
        in_specs=[_tokcol(DG, 0), _tokcol(DG, 1), pl.BlockSpec((t_len, DG), lambda i: (0, 1)),
                  pl.BlockSpec((t_len, DG), lambda i: (0, 2)), _full((t_len, LANES)), row8, row8],
        out_specs=[row8, _tok(DG)], out_shape=[_sds((8, t_len), F32), _sds((t_len, DG), BF16)],
        compiler_params=_cparams(1),
    )(qkv, dcat, qkv, qkv, c, ct, lset)


def _attn_bwd(qkv, dob, c, ct, lset, dt, xchg=()):
    t_len = qkv.shape[0]
    n_q = t_len // TB
    n_x = len(xchg)

    def body(*refs):
        j = pl.program_id(0)
        ins, outs, begin, finish = _with_exchange(refs, 8, 4, n_x, True, j == 0, j == n_q - 1)
        q_ref, dob_ref, k_ref, v_ref, c_ref, ct_ref, lset_ref, dt_ref = ins
        dq_ref, dk_ref, dv_ref, dc_ref = outs
        if begin:
            begin()

        @pl.when(j == 0)
        def _():
            dq_ref[...] = jnp.zeros_like(dq_ref)

        lane, halves = _head_masks(TB)
        crow = c_ref[...]
        causal_t = lax.broadcasted_iota(jnp.int32, (TB, TB), 1) >= lax.broadcasted_iota(jnp.int32, (TB, TB), 0)
        dc_out = jnp.zeros((TB, LANES), F32)
        for g in range(2):
            cols = slice(g * LANES, (g + 1) * LANES)
            kg, vg = k_ref[:, cols], v_ref[:, cols]
            kms = [_keep_lanes(halves[hh], kg) for hh in range(2)]
            cks = [jnp.sum(jnp.where(lane == 2 * g + hh, crow, 0.0), axis=1, keepdims=True) for hh in range(2)]

            def block(i, carry, masked):
                dk, dv, dcs = carry
                off = pl.multiple_of(i * TB, TB)
                qi = q_ref[pl.ds(off, TB), cols]
                doi = dob_ref[pl.ds(off, TB), cols]
                dq_add = jnp.zeros((TB, LANES), F32)
                dcs_new = []
                for hh in range(2):
                    h = 2 * g + hh
                    dom = _keep_lanes(halves[hh], doi)
                    st = _dot_nt(kms[hh], qi) + (ct_ref[h:h + 1, pl.ds(off, TB)] - cks[hh])
                    if masked:
                        st = jnp.where(causal_t, st, -jnp.inf)
                    pt = jnp.exp(st - lset_ref[h:h + 1, pl.ds(off, TB)])
                    dv = dv + _dot(pt.astype(BF16), dom)
                    dst = pt * (_dot_nt(vg, dom) - dt_ref[h:h + 1, pl.ds(off, TB)])
                    dsb = dst.astype(BF16)
                    dk = dk + _dot(dsb, _keep_lanes(halves[hh], qi))
                    dcs_new.append(dcs[hh] + jnp.sum(dst, axis=1, keepdims=True))
                    dq_add = dq_add + _dot_tn(dsb, kms[hh])
                dq_ref[pl.ds(off, TB), cols] += dq_add
                return dk, dv, tuple(dcs_new)

            init = (jnp.zeros((TB, LANES), F32), jnp.zeros((TB, LANES), F32),
                    (jnp.zeros((TB, 1), F32), jnp.zeros((TB, 1), F32)))
            carry = block(j, init, True)
            dk, dv, dcs = lax.fori_loop(j + 1, n_q, lambda i, cr: block(i, cr, False), carry)
            dk_ref[:, cols] = dk
            dv_ref[:, cols] = dv
            dc_out = jnp.where(lane == 2 * g, -dcs[0], dc_out)
            dc_out = jnp.where(lane == 2 * g + 1, -dcs[1], dc_out)
        dc_ref[...] = dc_out
        if finish:
            finish()

    return _pcall(
        body, name="attn_bwd_exchange" if n_x else "attn_bwd", grid=(n_q,),
        in_specs=[pl.BlockSpec((t_len, DG), lambda i: (0, 0)), _full((t_len, DG)), _tokcol(DG, 1), _tokcol(DG, 2),
                  _tok(LANES), _full((8, t_len)), _full((8, t_len)), _full((8, t_len))] + [_HBM] * n_x,
        out_specs=[_full((t_len, DG)), _tok(DG), _tok(DG), _tok(LANES)] + [_HBM] * n_x,
        out_shape=[_sds((t_len, DG), F32), _sds((t_len, DG), F32), _sds((t_len, DG), F32), _sds((t_len, LANES), F32)]
        + _exchange_shapes(xchg, [True] * n_x),
        scratch_shapes=_exchange_scratch(n_x) if n_x else [],
        compiler_params=_cparams(1),
    )(qkv, dob, qkv, qkv, c, ct, lset, dt, *xchg)


def _forget_bwd(dc, fl, b_row):
    t_len = dc.shape[0]
    n_t = t_len // TB
    rev = pl.BlockSpec((TB, LANES), lambda i: (n_t - 1 - i, 0))

    def body(dc_ref, fl_ref, b_ref, dfl_ref, db_ref, carry):
        @pl.when(pl.program_id(0) == 0)
        def _():
            carry[...] = jnp.zeros_like(carry)
            db_ref[...] = jnp.zeros_like(db_ref)

        r = lax.broadcasted_iota(jnp.int32, (TB, TB), 0)
        s = lax.broadcasted_iota(jnp.int32, (TB, TB), 1)
        dc = dc_ref[...]
        dl = _dot_exact((r <= s).astype(F32), dc) + carry[0:1, :]
        carry[...] += jnp.sum(dc, axis=0, keepdims=True)
        dfl = dl * _sigmoid(-(fl_ref[...] + b_ref[...]))
        dfl_ref[...] = dfl
        db_ref[...] += jnp.sum(dfl, axis=0, keepdims=True)

    return _pcall(
        body, name="forget_bwd", grid=(n_t,), in_specs=[rev, rev, _full((1, LANES))],
        out_specs=[rev, _full((1, LANES))], out_shape=[_sds((t_len, LANES), F32), _sds((1, LANES), F32)],
        scratch_shapes=[pltpu.VMEM((8, LANES), F32)], compiler_params=_cparams(1),
    )(dc, fl, b_row)


def _branch_bwd(zc, dcat, dq, dk, dv, w_dw, ln_g, ln_b, w_pw, w_sc, w_pool, pool_scale):
    t_len = zc.shape[0]
    n_t = t_len // TB
    e2 = HALO + TB + HALO
    e1 = TB + HALO

    def body(z_ref, zp_ref, zn_ref, dcf_ref, dcfn_ref, dsp_ref, dspn_ref, dq_ref, dk_ref, dv_ref,
             dw_ref, g_ref, b_ref, pw_ref, sc_ref, pool_ref, ps_ref,
             dz_ref, ddw_ref, dg_ref, db_ref, dpw_ref, dsc_ref, dpool_ref, dps_ref,
             u_s, dy_s, ch_s, dcv_s, p0, p1, p2, p3, g0, g1, g2, g3, u_ph, dy_ph):
        i = pl.program_id(0)
        _acc_init([ddw_ref, dg_ref, db_ref, dpw_ref, dsc_ref, dpool_ref, dps_ref])
        hm = (i > 0).astype(F32)
        nm = (i < n_t - 1).astype(F32)

        sig_b = _sigmoid(z_ref[:, 256:512])
        a = z_ref[:, 0:256]
        u_s[0:HALO, :] = zp_ref[:, 0:256] * _sigmoid(zp_ref[:, 256:512]) * hm
        u_s[HALO:HALO + TB, :] = a * sig_b
        u_s[HALO + TB:e2, :] = zn_ref[:, 0:256] * _sigmoid(zn_ref[:, 256:512])
        _phase_copies(u_s, u_ph, e2)
        y = jnp.zeros((e1, DG), F32)
        for k in range(CONF_K):
            y = y + dw_ref[k:k + 1, :] * _tap(u_s, u_ph, HALO - (CONF_K - 1) + k, e1)
        n, r, yn = _layer_norm_parts(y, g_ref[...], b_ref[...])
        sg = _sigmoid(yn)
        dyc = jnp.concatenate([dcf_ref[...], dcfn_ref[...] * nm], axis=0)
        ds = _dot_nt(dyc.astype(BF16), pw_ref[...])
        dyn = ds * sg * (1.0 + yn * (1.0 - sg))
        dg_ref[...] += jnp.sum((dyn * n)[0:TB], axis=0, keepdims=True)
        db_ref[...] += jnp.sum(dyn[0:TB], axis=0, keepdims=True)
        dn = dyn * g_ref[...]
        dyv = r * (dn - jnp.mean(dn, axis=-1, keepdims=True) - n * jnp.mean(dn * n, axis=-1, keepdims=True))
        dpw_ref[...] += _dot_tn((yn * sg)[0:TB].astype(BF16), dcf_ref[...].astype(BF16))
        dy_s[...] = dyv
        _phase_copies(dy_s, dy_ph, e1)
        du = jnp.zeros((TB, DG), F32)
        dyv_t = dyv[0:TB]
        for k in range(CONF_K):
            du = du + dw_ref[k:k + 1, :] * _tap(dy_s, dy_ph, CONF_K - 1 - k, TB)
            ddw_ref[k:k + 1, :] += jnp.sum(dyv_t * _tap(u_s, u_ph, HALO - (CONF_K - 1) + k, TB), axis=0, keepdims=True)
        dz_ref[:, 0:256] = (du * sig_b).astype(BF16)
        dz_ref[:, 256:512] = (du * a * sig_b * (1.0 - sig_b)).astype(BF16)

        dz_ref[:, 512:768] = (dq_ref[...] * SCALE).astype(BF16)
        dz_ref[:, 768:1024] = dk_ref[...].astype(BF16)
        dz_ref[:, 1024:1280] = dv_ref[...].astype(BF16)

        sc_h, sc_b, sc_c = z_ref[:, 512:768], z_ref[:, 768:1024], z_ref[:, 1024:1280]
        ch_s[0:HALO, :] = zp_ref[:, 1024:1280] * zp_ref[:, 512:768] * hm
        ch_s[HALO:HALO + TB, :] = sc_c * sc_h
        ch_s[HALO + TB:e2, :] = zn_ref[:, 1024:1280] * zn_ref[:, 512:768]
        cv = jnp.zeros((TB, DG), F32)
        for k in range(SC_K):
            cv = cv + sc_ref[k:k + 1, :] * ch_s[pl.ds(HALO - (SC_K - 1) + k, TB), :]
        dy_sc = dsp_ref[:, 0:256]
        dcv_t = dy_sc * sc_b
        dcv_s[0:TB, :] = dcv_t
        dcv_s[TB:e1, :] = dspn_ref[:, 0:256] * nm * zn_ref[:, 768:1024]
        dch = jnp.zeros((TB, DG), F32)
        for k in range(SC_K):
            dch = dch + sc_ref[k:k + 1, :] * dcv_s[pl.ds(SC_K - 1 - k, TB), :]
            dsc_ref[k:k + 1, :] += jnp.sum(dcv_t * ch_s[pl.ds(HALO - (SC_K - 1) + k, TB), :], axis=0, keepdims=True)
        dz_ref[:, 1280:1536] = (dch * sc_c).astype(BF16)
        dz_ref[:, 1536:1792] = (dy_sc * cv).astype(BF16)
        dz_ref[:, 1792:2048] = (dch * sc_h).astype(BF16)

        v_t = z_ref[:, 1280:1536]
        p0[0:HALO, :] = zp_ref[:, 1280:1536] * hm
        p0[HALO:HALO + TB, :] = v_t
        s2, s4, s8, s16 = _pool_window_sums(p0, p1, p2, p3)
        cnt, lane = _pool_counts(i * TB, e1)
        dlt = (_lane_group_select(lane[0:TB], s2, s4, s8, s16) / cnt[0:TB] - v_t).astype(BF16)
        dyp_t = dsp_ref[:, 256:512]
        dps_ref[...] += jnp.sum(dyp_t * _dot(dlt, pool_ref[...]), axis=0, keepdims=True)
        dpre = (jnp.concatenate([dyp_t, dspn_ref[:, 256:512] * nm], axis=0) * ps_ref[...]).astype(BF16)
        dpool_ref[...] += _dot_tn(dlt, dpre[0:TB])
        dd = _dot_nt(dpre, pool_ref[...])
        g0[...] = dd / cnt
        g1[0:TB + 24, :] = g0[pl.ds(0, TB + 24), :] + g0[pl.ds(1, TB + 24), :]
        g2[0:TB + 16, :] = g1[pl.ds(0, TB + 16), :] + g1[pl.ds(2, TB + 16), :]
        g3[0:TB + 8, :] = g2[pl.ds(0, TB + 8), :] + g2[pl.ds(4, TB + 8), :]
        f16 = g3[pl.ds(0, TB), :] + g3[pl.ds(8, TB), :]
        fwd_sum = _lane_group_select(lane[0:TB], g1[pl.ds(0, TB), :], g2[pl.ds(0, TB), :], g3[pl.ds(0, TB), :], f16)
        dz_ref[:, 2048:2304] = (fwd_sum - dd[0:TB]).astype(BF16)

    vec = _full((1, DG))
    mat = _full((DG, DG))
    scr = ([pltpu.VMEM((e2, DG), F32), pltpu.VMEM((e1, DG), F32), pltpu.VMEM((e2, DG), F32), pltpu.VMEM((e1, DG), F32)]
           + [pltpu.VMEM((HALO + TB, DG), F32)] * 4 + [pltpu.VMEM((e1, DG), F32)] * 4
           + [pltpu.VMEM((7, e2, DG), F32), pltpu.VMEM((7, e1, DG), F32)])
    return _pcall(
        body, name="branch_bwd", grid=(n_t,),
        in_specs=[_tok(1536), _halo_prev(1536), _halo_next(1536, t_len),
                  _tokcol(DG, 0), _halo_next(DG, t_len, 0), _tokcol(512, 1), _halo_next(512, t_len, 1),
                  _tok(DG), _tok(DG), _tok(DG),
                  _full((32, DG)), vec, vec, mat, _full((8, DG)), mat, vec],
        out_specs=[_tok(W_MAIN), _full((32, DG)), vec, vec, mat, _full((8, DG)), mat, vec],
        out_shape=[_sds((t_len, W_MAIN), BF16), _sds((32, DG), F32), _sds((1, DG), F32), _sds((1, DG), F32),
                   _sds((DG, DG), F32), _sds((8, DG), F32), _sds((DG, DG), F32), _sds((1, DG), F32)],
        scratch_shapes=scr, compiler_params=_cparams(1),
    )(zc, zc, zc, dcat, dcat, dcat, dcat, dq, dk, dv, w_dw, ln_g, ln_b, w_pw, w_sc, w_pool, pool_scale)


def _mixin_bwd(dh, h, g, dz, dfl, win, wf):
    t_len = dh.shape[0]

    def body(dh_ref, h_ref, g_ref, dz_ref, dfl_ref, win_ref, wf_ref, dh0_ref, xn_ref, dg_ref):
        _acc_init([dg_ref])
        dxn = _dot_nt(dz_ref[...], win_ref[...].reshape(D, W_MAIN)) + _dot_nt(dfl_ref[...].astype(BF16), wf_ref[...])
        dx, dg, xn = _rms_bwd(h_ref[...], g_ref[...], dxn)
        dg_ref[...] += dg
        xn_ref[...] = xn.astype(BF16)
        dh0_ref[...] = dh_ref[...] + dx

    return _pcall(
        body, name="mixin_bwd", grid=(t_len // TB,),
        in_specs=[_tok(D), _tok(D), _full((1, D)), _tok(W_MAIN), _tok(LANES), _full((N_DEV, D // N_DEV, W_MAIN)),
                  _full((D, LANES))],
        out_specs=[_tok(D), _tok(D), _full((1, D))],
        out_shape=[_sds((t_len, D), F32), _sds((t_len, D), BF16), _sds((1, D), F32)],
        compiler_params=_cparams(1),
    )(dh, h, g, dz, dfl, win, wf)


def _matmul_tn(name, a, b, tm, tn, out_dtype, block_major=False, a_section=0):
    t_len, n = b.shape
    m = a.shape[1]
    tk = min(t_len, 1024)
    n_k = t_len // tk

    def body(a_ref, b_ref, o_ref, acc):
        k = pl.program_id(2)

        @pl.when(k == 0)
        def _():
            acc[...] = jnp.zeros_like(acc)

        acc[...] += _dot_tn(a_ref[...].astype(BF16), b_ref[...].astype(BF16))

        @pl.when(k == n_k - 1)
        def _():
            if block_major:
                for blk in range(tn // FF_BLK):
                    o_ref[blk] = acc[:, blk * FF_BLK:(blk + 1) * FF_BLK].astype(out_dtype)
            else:
                o_ref[...] = acc[...].astype(out_dtype)

    if block_major:
        out_spec = pl.BlockSpec((tn // FF_BLK, tm, FF_BLK), lambda i, j, k: (j, i, 0))
        out_shape = _sds((n // FF_BLK, m, FF_BLK), out_dtype)
    else:
        out_spec = pl.BlockSpec((tm, tn), lambda i, j, k: (i, j))
        out_shape = _sds((m, n), out_dtype)
    return _pcall(
        body, name=name, grid=(m // tm, n // tn, n_k),
        in_specs=[pl.BlockSpec((tk, tm), lambda i, j, k: (a_section * n_k + k, i)), pl.BlockSpec((tk, tn), lambda i, j, k: (k, j))],
        out_specs=out_spec, out_shape=out_shape, scratch_shapes=[pltpu.VMEM((tm, tn), F32)], compiler_params=_cparams(3),
    )(a, b)


_HBM = pl.BlockSpec(memory_space=pltpu.HBM)


def _mesh_place():
    return lax.axis_index("x"), lax.axis_index("y"), lax.axis_index("c")


def _allgather(name, srcs):
    n = len(srcs)

    def body(*refs):
        src, dst = refs[:n], refs[n:2 * n]
        send_sems, recv_sems, local_sems = refs[2 * n:]
        x, y, c = _mesh_place()
        me, sibling = (x, y, c), (x, y, 1 - c)
        chips = [(1 - x, y), (x, 1 - y), (1 - x, 1 - y)]

        def slot(px, py, pc):
            return 4 * px + 2 * py + pc

        def copy(t, k, block, to, from_src=False):
            return pltpu.make_async_remote_copy(
                src_ref=src[t] if from_src else dst[t].at[slot(*block)], dst_ref=dst[t].at[slot(*block)],
                send_sem=send_sems.at[t, k], recv_sem=recv_sems.at[t, k], device_id=to, device_id_type=MESH_ID)

        mine = [pltpu.make_async_copy(src[t], dst[t].at[slot(*me)], local_sems.at[t]) for t in range(n)]
        for cp in mine:
            cp.start()
        started = []
        for t in range(n):
            started.append(copy(t, 0, me, sibling, from_src=True))
            started += [copy(t, 1 + j, me, (*chip, c), from_src=True) for j, chip in enumerate(chips)]
        for cp in started:
            cp.start()
        for j, chip in enumerate(chips):
            for t in range(n):
                copy(t, 1 + j, (*chip, c), me).wait_recv()
                fwd = copy(t, 4 + j, (*chip, c), sibling)
                fwd.start()
                started.append(fwd)
        for t in range(n):
            copy(t, 0, sibling, me).wait_recv()
            for j, chip in enumerate(chips):
                copy(t, 4 + j, (*chip, 1 - c), me).wait_recv()
        for cp in started:
            cp.wait_send()
        for cp in mine:
            cp.wait()

    return _pcall(
        body, name=name, in_specs=[_HBM] * n, out_specs=[_HBM] * n,
        out_shape=[_sds((N_DEV,) + s.shape, s.dtype) for s in srcs],
        scratch_shapes=[pltpu.SemaphoreType.DMA((n, 7)), pltpu.SemaphoreType.DMA((n, 7)), pltpu.SemaphoreType.DMA((n,))],
    )(*srcs)


def _exchange_ops(src, dst, per_dest, send_sems, recv_sems, local_sems):
    n = len(src)
    x, y, c = _mesh_place()
    me_slot = 4 * x + 2 * y + c
    peers = []
    for r in range(1, N_DEV):
        px, py, pc = x ^ ((r >> 2) & 1), y ^ ((r >> 1) & 1), c ^ (r & 1)
        peers.append(((px, py, pc), 4 * px + 2 * py + pc))

    def piece(t, dest_slot):
        return src[t].at[dest_slot] if per_dest[t] else src[t]

    def local(t):
        return pltpu.make_async_copy(piece(t, me_slot), dst[t].at[me_slot], local_sems.at[t])

    def remote(t, r, landing_slot):
        peer, peer_slot = peers[r]
        return pltpu.make_async_remote_copy(
            src_ref=piece(t, peer_slot), dst_ref=dst[t].at[landing_slot], send_sem=send_sems.at[t, r],
            recv_sem=recv_sems.at[t, r], device_id=peer, device_id_type=MESH_ID)

    def start():
        for t in range(n):
            local(t).start()
        for r in range(N_DEV - 1):
            for t in range(n):
                remote(t, r, me_slot).start()

    def wait():
        for r in range(N_DEV - 1):
            for t in range(n):
                remote(t, r, peers[r][1]).wait_recv()
        for r in range(N_DEV - 1):
            for t in range(n):
                remote(t, r, me_slot).wait_send()
        for t in range(n):
            local(t).wait()

    return start, wait


def _exchange_shapes(srcs, per_dest):
    return [_sds((N_DEV,) + tuple(s.shape[1:] if pd else s.shape), s.dtype) for s, pd in zip(srcs, per_dest)]


def _exchange_scratch(n):
    return [pltpu.SemaphoreType.DMA((n, N_DEV - 1)), pltpu.SemaphoreType.DMA((n, N_DEV - 1)), pltpu.SemaphoreType.DMA((n,))]


def _exchange(name, srcs, per_dest):
    n = len(srcs)

    def body(*refs):
        start, wait = _exchange_ops(refs[:n], refs[n:2 * n], per_dest, *refs[2 * n:])
        start()
        wait()

    return _pcall(
        body, name=name, in_specs=[_HBM] * n, out_specs=[_HBM] * n, out_shape=_exchange_shapes(srcs, per_dest),
        scratch_shapes=_exchange_scratch(n),
    )(*srcs)


def _adam_math(w, g, m, v):
    m = ADAM_B1 * m + (1.0 - ADAM_B1) * g
    v = ADAM_B2 * v + (1.0 - ADAM_B2) * (g * g)
    m_hat = m / (1.0 - ADAM_B1 ** ADAM_STEP)
    v_hat = v / (1.0 - ADAM_B2 ** ADAM_STEP)
    delta = -ADAM_LR * (m_hat / (jnp.sqrt(v_hat) + ADAM_EPS) + ADAM_WD * w)
    return delta, m, v


def _adam_rows(name, parts, w, m, v, row_tile):
    n_l, rows, cols = w.shape

    def body(*refs):
        p_refs = refs[:n_l]
        w_ref, m_ref, v_ref, g_out, d_out, m_out, v_out = refs[n_l:]
        layer = pl.program_id(0)
        for k in range(n_l):
            @pl.when(layer == k)
            def _(p_ref=p_refs[k]):
                g = p_ref[0].astype(F32)
                for s in range(1, N_DEV):
                    g = g + p_ref[s].astype(F32)
                delta, m_new, v_new = _adam_math(w_ref[...], g, m_ref[...], v_ref[...])
                g_out[...] = g
                d_out[...] = delta
                m_out[...] = m_new
                v_out[...] = v_new

    def part_spec(k):
        return pl.BlockSpec((N_DEV, row_tile, cols),
                            lambda l, i: (0, jnp.where(l == k, i, 0), 0))

    blk = pl.BlockSpec((None, row_tile, cols), lambda l, i: (l, i, 0))
    return _pcall(
        body, name=name, grid=(n_l, rows // row_tile),
        in_specs=[part_spec(k) for k in range(n_l)] + [blk, blk, blk],
        out_specs=[blk] * 4, out_shape=[_sds(w.shape, F32)] * 4, compiler_params=_cparams(2),
    )(*parts, w, m, v)


def _adam_packed(name, parts, w, m, v):
    def body(p_ref, w_ref, m_ref, v_ref, g_out, d_out, m_out, v_out):
        g = p_ref[0]
        for s in range(1, N_DEV):
            g = g + p_ref[s]
        delta, m_new, v_new = _adam_math(w_ref[...], g, m_ref[...], v_ref[...])
        g_out[...] = g
        d_out[...] = delta
        m_out[...] = m_new
        v_out[...] = v_new

    return _pcall(
        body, name=name, grid=(1,), in_specs=[_full(parts.shape), _full(w.shape), _full(w.shape), _full(w.shape)],
        out_specs=[_full(w.shape)] * 4, out_shape=[_sds(w.shape, F32)] * 4, compiler_params=_cparams(1),
    )(parts, w, m, v)


def _pack_rows(flat_parts, lead=()):
    flat = jnp.concatenate(flat_parts, axis=-1)
    n = flat.shape[-1]
    rows = -(-n // LANES)
    rows = -(-rows // 8) * 8
    flat = jnp.pad(flat, [(0, 0)] * len(lead) + [(0, rows * LANES - n)])
    return flat.reshape(lead + (rows, LANES))


def _unpack_rows(packed, shapes, lead=()):
    flat = packed.reshape(lead + (-1,))
    out, off = [], 0
    for shp in shapes:
        size = 1
        for s in shp:
            size *= s
        out.append(flat[..., off:off + size].reshape(lead + tuple(shp)))
        off += size
    return out


_SMALL_SHARD_SHAPES = [(N_LAYERS, 128, 4), (N_LAYERS, 32, DG), (N_LAYERS, D_PLE, 128), (N_LAYERS, CONF_K, 32), (N_LAYERS, SC_K, 32)]
_REP_SHAPES = [(N_LAYERS, D)] * 6 + [(N_LAYERS, DG)] * 3 + [(N_LAYERS, N_HEADS), (N_LAYERS, 4, 64, 64)]


def _small_full_to_shards(fcol, pw, proj, dw, sc):
    return [
        fcol.reshape(N_LAYERS, N_DEV, 128, 4).transpose(1, 0, 2, 3),
        pw.reshape(N_LAYERS, N_DEV, 32, DG).transpose(1, 0, 2, 3),
        proj.reshape(N_LAYERS, D_PLE, N_DEV, 128).transpose(2, 0, 1, 3),
        dw.reshape(N_LAYERS, CONF_K, N_DEV, 32).transpose(2, 0, 1, 3),
        sc.reshape(N_LAYERS, SC_K, N_DEV, 32).transpose(2, 0, 1, 3),
    ]


def _small_shards_to_full(fcol, pw, proj, dw, sc):
    return [
        fcol.transpose(1, 0, 2, 3).reshape(N_LAYERS, D, 4),
        pw.transpose(1, 0, 2, 3).reshape(N_LAYERS, DG, DG),
        proj.transpose(1, 2, 0, 3).reshape(N_LAYERS, D_PLE, D),
        dw.transpose(1, 2, 0, 3).reshape(N_LAYERS, CONF_K, DG),
        sc.transpose(1, 2, 0, 3).reshape(N_LAYERS, SC_K, DG),
    ]


def _pad_rows(a, rows):
    return jnp.pad(a, ((0, rows - a.shape[0]), (0, 0)))


def _block_diag4(w):
    z = jnp.zeros((64, 64), w.dtype)
    return jnp.concatenate([jnp.concatenate([w[g] if k == g else z for k in range(4)], axis=1) for g in range(4)], axis=0)


def kernel(x, p, g_mix_pre, w_in, b_forget, w_conf_dw, conf_ln_g, conf_ln_b, w_conf_pw, w_sc, w_pool, pool_scale, w_out, g_mix_post, g_mlp_pre, w_up, w_down, g_mlp_post, g_ple_pre, w_ple_gate, w_ple_proj, g_ple_post, loss_target, m_g_mix_pre, m_w_in, m_b_forget, m_w_conf_dw, m_conf_ln_g, m_conf_ln_b, m_w_conf_pw, m_w_sc, m_w_pool, m_pool_scale, m_w_out, m_g_mix_post, m_g_mlp_pre, m_w_up, m_w_down, m_g_mlp_post, m_g_ple_pre, m_w_ple_gate, m_w_ple_proj, m_g_ple_post, v_g_mix_pre, v_w_in, v_b_forget, v_w_conf_dw, v_conf_ln_g, v_conf_ln_b, v_w_conf_pw, v_w_sc, v_w_pool, v_pool_scale, v_w_out, v_g_mix_post, v_g_mlp_pre, v_w_up, v_w_down, v_g_mlp_post, v_g_ple_pre, v_w_ple_gate, v_w_ple_proj, v_g_ple_post):
    n_l = N_LAYERS
    t_len = x.shape[1]
    assert t_len % TB == 0 and x.shape[0] == 1 and x.shape[2] == D

    def main_cols(a):
        return jnp.concatenate([a[..., :F_LO], a[..., F_HI:]], axis=-1)

    def fcols(a):
        return a[..., F_LO:F_HI]

    def rows_pack(down, out, gate):
        return jnp.concatenate([down, out, gate], axis=1)

    rows_b = rows_pack(w_down, w_out, w_ple_gate).astype(BF16)
    win_b = main_cols(w_in).astype(BF16)
    wup_b = w_up.astype(BF16)
    small_local = _pack_rows([a.reshape(-1) for a in (fcols(w_in), w_conf_pw, w_ple_proj, w_conf_dw, w_sc)])
    rows_g, win_g, wup_g = [None] * n_l, [None] * n_l, [None] * n_l
    rows_g[0], win_g[0], wup_g[0], small_all = _allgather("weight_allgather", [rows_b[0], win_b[0], wup_b[0], small_local])
    small_g = _unpack_rows(small_all, _SMALL_SHARD_SHAPES, lead=(N_DEV,))
    fcol_f, pw_f, proj_f, dw_f, sc_f = _small_shards_to_full(*small_g)
    wf_b = jnp.pad(fcol_f, ((0, 0), (0, 0), (0, LANES - 4))).astype(BF16)
    pw_b, proj_b = pw_f.astype(BF16), proj_f.astype(BF16)
    dw_pad = jnp.pad(dw_f, ((0, 0), (0, 32 - CONF_K), (0, 0)))
    sc_pad = jnp.pad(sc_f, ((0, 0), (0, 8 - SC_K), (0, 0)))
    pool_bd = jnp.stack([_block_diag4(w_pool[l]) for l in range(n_l)]).astype(BF16)
    b_row = jnp.pad(b_forget, ((0, 0), (0, LANES - N_HEADS)))[:, None, :]

    def vec(a, l):
        return a[l][None, :]

    p_all = p.reshape(n_l * t_len, D_PLE)
    h = x[0]
    saved = []
    for l in range(n_l):
        zc, qkv, fl = _mixin_fwd(h, vec(g_mix_pre, l), win_g[l], wf_b[l])
        c, ct = _cumsum_fwd(fl, b_row[l])
        cat3 = _branch_fwd(zc, dw_pad[l], vec(conf_ln_g, l), vec(conf_ln_b, l), pw_b[l], sc_pad[l], pool_bd[l], vec(pool_scale, l))
        if l + 1 < n_l:
            o, lset, rows_g[l + 1], win_g[l + 1], wup_g[l + 1] = _attn_fwd(qkv, c, ct, [rows_b[l + 1], win_b[l + 1], wup_b[l + 1]])
        else:
            o, lset = _attn_fwd(qkv, c, ct)
        h1 = _mixout_fwd(h, cat3, o, rows_g[l], vec(g_mix_post, l))
        u, ff, h2 = _mlp_fwd(h1, vec(g_mlp_pre, l), wup_g[l], rows_g[l], vec(g_mlp_post, l))
        h3 = _ple_fwd(h2, p_all, l, vec(g_ple_pre, l), rows_g[l], proj_b[l], vec(g_ple_post, l))
        saved.append(dict(h0=h, zc=zc, qkv=qkv, fl=fl, c=c, ct=ct, cat3=cat3, o=o, lset=lset, h1=h1, u=u, ff=ff, h2=h2))
        h = h3

    dh, loss_part = _loss_bwd(h, loss_target[0])
    loss = lax.psum(loss_part[0, 0], ("x", "y", "c"))

    d_win = [None] * n_l
    r_down, r_out, r_gate, r_wup, r_win = ([None] * n_l for _ in range(5))
    small_grads = {k: [None] * n_l for k in ("fcol", "pw", "proj", "dw", "sc")}
    rep_grads = {k: [None] * n_l for k in ("g_mix_pre", "g_mix_post", "g_mlp_pre", "g_mlp_post", "g_ple_pre", "g_ple_post",
                                           "ln_g", "ln_b", "pool_scale", "b_forget", "w_pool")}
    for l in reversed(range(n_l)):
        s = saved[l]
        dh, dpp_b, dpre_b, hn3_b, dg_ple_post, dg_ple_pre = _ple_bwd(
            dh, s["h2"], p_all, l, vec(g_ple_post, l), vec(g_ple_pre, l), rows_g[l], proj_b[l])
        small_grads["proj"][l] = _matmul_tn("wgrad_proj", p_all, dpp_b, D_PLE, D, F32, a_section=l)
        d_gate = _matmul_tn("wgrad_gate", hn3_b, dpre_b, D, D, BF16).reshape(N_DEV, 128, D)
        dh, a2_b, du_b, dff_b, hn2_b, dg_mlp_post, dg_mlp_pre = _mlp_bwd(
            dh, s["h1"], s["u"], s["ff"], vec(g_mlp_post, l), vec(g_mlp_pre, l), wup_g[l], rows_g[l])
        d_down = _matmul_tn("wgrad_down", a2_b, dff_b, 2 * D, D, BF16).reshape(N_DEV, FF_BLK, D)
        d_wup = _matmul_tn("wgrad_up", hn2_b, du_b, D, 2 * FF_BLK, BF16, block_major=True)
        dcat, dmix_b, cat_b, dg_mix_post = _mixout_bwd(dh, s["cat3"], s["o"], vec(g_mix_post, l), rows_g[l])
        d_out = _matmul_tn("wgrad_out", cat_b, dmix_b, D, D, BF16).reshape(N_DEV, 128, D)
        dt, dob = _attn_bwd_dsum(s["qkv"], dcat, s["c"], s["ct"], s["lset"])
        riders = [d_down, d_out, d_gate, d_wup] + ([d_win[l + 1]] if l + 1 < n_l else [])
        dq, dk, dv, dc, r_down[l], r_out[l], r_gate[l], r_wup[l], *landed = _attn_bwd(
            s["qkv"], dob, s["c"], s["ct"], s["lset"], dt, riders)
        if landed:
            r_win[l + 1] = landed[0]
        dfl, db_f = _forget_bwd(dc, s["fl"], b_row[l])
        dz_b, ddw, dln_g, dln_b, dpw, dsc, dpool, dps = _branch_bwd(
            s["zc"], dcat, dq, dk, dv, dw_pad[l], vec(conf_ln_g, l), vec(conf_ln_b, l), pw_b[l], sc_pad[l], pool_bd[l],
            vec(pool_scale, l))
        dh, xn_b, dg_mix_pre = _mixin_bwd(dh, s["h0"], vec(g_mix_pre, l), dz_b, dfl, win_g[l], wf_b[l])
        d_win[l] = _matmul_tn("wgrad_in", xn_b, dz_b, D, W_MAIN // 2, BF16).reshape(N_DEV, 128, W_MAIN)
        small_grads["fcol"][l] = _matmul_tn("wgrad_fcol", xn_b, dfl, D, LANES, F32)[:, 0:4]
        small_grads["pw"][l], small_grads["dw"][l], small_grads["sc"][l] = dpw, ddw[0:CONF_K], dsc[0:SC_K]
        rep_grads["g_mix_pre"][l], rep_grads["g_mix_post"][l] = dg_mix_pre[0], dg_mix_post[0]
        rep_grads["g_mlp_pre"][l], rep_grads["g_mlp_post"][l] = dg_mlp_pre[0], dg_mlp_post[0]
        rep_grads["g_ple_pre"][l], rep_grads["g_ple_post"][l] = dg_ple_pre[0], dg_ple_post[0]
        rep_grads["ln_g"][l], rep_grads["ln_b"][l], rep_grads["pool_scale"][l] = dln_g[0], dln_b[0], dps[0]
        rep_grads["b_forget"][l] = db_f[0, 0:N_HEADS]
        rep_grads["w_pool"][l] = jnp.stack([dpool[64 * g:64 * g + 64, 64 * g:64 * g + 64] for g in range(4)])
    grad_x = dh[None]

    small_part = _pack_rows(
        [a.reshape(N_DEV, -1) for a in _small_full_to_shards(*[jnp.stack(small_grads[k]) for k in ("fcol", "pw", "proj", "dw", "sc")])],
        lead=(N_DEV,))
    rep_order = ("g_mix_pre", "g_mix_post", "g_mlp_pre", "g_mlp_post", "g_ple_pre", "g_ple_post", "ln_g", "ln_b",
                 "pool_scale", "b_forget", "w_pool")
    rep_part = _pack_rows([jnp.stack(rep_grads[k]).reshape(-1) for k in rep_order])
    r_win[0], r_small, r_rep = _exchange("grad_exchange", [d_win[0], small_part, rep_part], [True, True, False])

    res = {}
    res["w_down"] = _adam_rows("adam_down", r_down, w_down, m_w_down, v_w_down, 128)
    res["w_out"] = _adam_rows("adam_out", r_out, w_out, m_w_out, v_w_out, 128)
    res["w_ple_gate"] = _adam_rows("adam_gate", r_gate, w_ple_gate, m_w_ple_gate, v_w_ple_gate, 128)
    res["w_up"] = _adam_rows("adam_up", r_wup, w_up, m_w_up, v_w_up, 256)
    win_main = _adam_rows("adam_in", r_win, main_cols(w_in), main_cols(m_w_in), main_cols(v_w_in), 128)

    small_w = [(fcols(w_in), w_conf_pw, w_ple_proj, w_conf_dw, w_sc), (fcols(m_w_in), m_w_conf_pw, m_w_ple_proj, m_w_conf_dw, m_w_sc),
               (fcols(v_w_in), v_w_conf_pw, v_w_ple_proj, v_w_conf_dw, v_w_sc)]
    small_packed = [_pack_rows([a.reshape(-1) for a in grp]) for grp in small_w]
    small_res = [_unpack_rows(a, _SMALL_SHARD_SHAPES) for a in _adam_packed("adam_small", r_small, *small_packed)]
    rep_w = [(g_mix_pre, g_mix_post, g_mlp_pre, g_mlp_post, g_ple_pre, g_ple_post, conf_ln_g, conf_ln_b, pool_scale, b_forget, w_pool),
             (m_g_mix_pre, m_g_mix_post, m_g_mlp_pre, m_g_mlp_post, m_g_ple_pre, m_g_ple_post, m_conf_ln_g, m_conf_ln_b, m_pool_scale,
              m_b_forget, m_w_pool),
             (v_g_mix_pre, v_g_mix_post, v_g_mlp_pre, v_g_mlp_post, v_g_ple_pre, v_g_ple_post, v_conf_ln_g, v_conf_ln_b, v_pool_scale,
              v_b_forget, v_w_pool)]
    rep_packed = [_pack_rows([a.reshape(-1) for a in grp]) for grp in rep_w]
    rep_res = [_unpack_rows(a, _REP_SHAPES) for a in _adam_packed("adam_replicated", r_rep, *rep_packed)]

    for kind in range(4):
        fc, pw, proj, dwc, scc = small_res[kind]
        main = win_main[kind]
        (rg_mix_pre, rg_mix_post, rg_mlp_pre, rg_mlp_post, rg_ple_pre, rg_ple_post, r_ln_g, r_ln_b, r_ps, r_bf, r_wpool) = rep_res[kind]
        res.setdefault("by_kind", []).append(dict(
            g_mix_pre=rg_mix_pre, w_in=jnp.concatenate([main[..., :F_LO], fc, main[..., F_LO:]], axis=-1), b_forget=r_bf,
            w_conf_dw=dwc, conf_ln_g=r_ln_g, conf_ln_b=r_ln_b, w_conf_pw=pw, w_sc=scc, w_pool=r_wpool, pool_scale=r_ps,
            w_out=res["w_out"][kind], g_mix_post=rg_mix_post, g_mlp_pre=rg_mlp_pre, w_up=res["w_up"][kind],
            w_down=res["w_down"][kind], g_mlp_post=rg_mlp_post, g_ple_pre=rg_ple_pre, w_ple_gate=res["w_ple_gate"][kind],
            w_ple_proj=proj, g_ple_post=rg_ple_post))
    names = ("g_mix_pre", "w_in", "b_forget", "w_conf_dw", "conf_ln_g", "conf_ln_b", "w_conf_pw", "w_sc", "w_pool", "pool_scale",
             "w_out", "g_mix_post", "g_mlp_pre", "w_up", "w_down", "g_mlp_post", "g_ple_pre", "w_ple_gate", "w_ple_proj", "g_ple_post")
    outs = [loss, grad_x]
    for kind in range(4):
        outs += [res["by_kind"][kind][nm] for nm in names]
    return tuple(outs)
```

```python
import jax
import jax.numpy as jnp
from jax import lax
from jax.experimental import pallas as pl
from jax.experimental.pallas import tpu as pltpu

F32, BF16 = jnp.float32, jnp.bfloat16

D = 1024
DG = 256
N_HEADS = 4
HEAD_DIM = 64
CONF_K = 31
SC_K = 3
POOL_WINDOWS = (2, 4, 8, 16)
D_FF = 4096
D_PLE = 256
N_LAYERS = 4
N_DEV = 8
EPS = 1e-6
SCALE = HEAD_DIM ** -0.5
W_MAIN = 2304
F_LO, F_HI = 1280, 1284

ADAM_LR, ADAM_B1, ADAM_B2, ADAM_EPS, ADAM_WD, ADAM_STEP = 0.001, 0.9, 0.999, 1e-08, 0.01, 10

TB = 512
HALO = 32
LANES = 128
FF_BLK = D_FF // N_DEV
MLP_BWD_BLOCKS = 2
VMEM_LIMIT = 56 * 1024 * 1024

NT_DIMS = (((1,), (1,)), ((), ()))
TN_DIMS = (((0,), (0,)), ((), ()))
MESH_ID = pl.DeviceIdType.MESH


def _pcall(body, **kw):
    return pl.pallas_call(body, **kw)


def _cparams(n_axes):
    return pltpu.CompilerParams(dimension_semantics=("arbitrary",) * n_axes, vmem_limit_bytes=VMEM_LIMIT)


def _sds(shape, dtype):
    return jax.ShapeDtypeStruct(shape, dtype)


def _tok(width, tb=TB):
    return pl.BlockSpec((tb, width), lambda i: (i, 0))


def _tokcol(width, col):
    return pl.BlockSpec((TB, width), lambda i: (i, col))


def _full(shape):
    zeros = (0,) * len(shape)
    return pl.BlockSpec(shape, lambda *_: zeros)


def _resident(shape):
    zeros = (0,) * len(shape)
    return pl.BlockSpec(shape, lambda *_: zeros, pipeline_mode=pl.Buffered(1))


def _halo_prev(width, col=0):
    return pl.BlockSpec((HALO, width), lambda i: (jnp.maximum(i * (TB // HALO) - 1, 0), col))


def _halo_next(width, n_rows, col=0):
    last = n_rows // HALO - 1
    return pl.BlockSpec((HALO, width), lambda i: (jnp.minimum((i + 1) * (TB // HALO), last), col))


def _dot(a, b):
    return jnp.dot(a, b, preferred_element_type=F32)


def _dot_nt(a, b):
    return lax.dot_general(a, b, NT_DIMS, preferred_element_type=F32)


def _dot_tn(a, b):
    return lax.dot_general(a, b, TN_DIMS, preferred_element_type=F32)


def _dot_exact(a, b):
    return jnp.dot(a, b, precision=lax.Precision.HIGHEST, preferred_element_type=F32)


def _rms(x, g):
    r = lax.rsqrt(jnp.mean(x * x, axis=-1, keepdims=True) + EPS)
    return x * r * g


def _rms_bwd(x, g, dy):
    r = lax.rsqrt(jnp.mean(x * x, axis=-1, keepdims=True) + EPS)
    n = x * r
    dg = jnp.sum(dy * n, axis=0, keepdims=True)
    dn = dy * g
    dx = r * (dn - n * jnp.mean(dn * n, axis=-1, keepdims=True))
    return dx, dg, n * g


def _sigmoid(x):
    return jax.nn.sigmoid(x)


def _log_sigmoid(x):
    return jnp.minimum(x, 0.0) - jnp.log(1.0 + jnp.exp(-jnp.abs(x)))


def _lane_group_select(lane, v2, v4, v8, v16):
    return jnp.where(lane < 64, v2, jnp.where(lane < 128, v4, jnp.where(lane < 192, v8, v16)))


def _pool_counts(t0, rows):
    lane = lax.broadcasted_iota(jnp.int32, (rows, DG), 1)
    t = lax.broadcasted_iota(jnp.int32, (rows, DG), 0) + t0
    win = _lane_group_select(lane, 2, 4, 8, 16)
    return jnp.minimum(t + 1, win).astype(F32), lane


def _mixin_fwd(h, g, win, wf):
    t_len = h.shape[0]

    def body(h_ref, g_ref, win_ref, wf_ref, zc_ref, qkv_ref, qt_ref, fl_ref):
        xn = _rms(h_ref[...], g_ref[...]).astype(BF16)
        z = _dot(xn, win_ref[...].reshape(D, W_MAIN))
        zc_ref[:, 0:512] = z[:, 0:512]
        zc_ref[:, 512:1536] = z[:, 1280:2304]
        q = z[:, 512:768] * SCALE
        qkv_ref[:, 0:256] = q.astype(BF16)
        qkv_ref[:, 256:768] = z[:, 768:1280].astype(BF16)
        qt_ref[...] = q.T.astype(BF16)
        fl_ref[...] = _dot(xn, wf_ref[...])

    return _pcall(
        body, name="mixin_fwd", grid=(t_len // TB,),
        in_specs=[_tok(D), _full((1, D)), _full((N_DEV, D // N_DEV, W_MAIN)), _full((D, LANES))],
        out_specs=[_tok(1536), _tok(768), pl.BlockSpec((DG, TB), lambda i: (0, i)), _tok(LANES)],
        out_shape=[_sds((t_len, 1536), F32), _sds((t_len, 768), BF16), _sds((DG, t_len), BF16), _sds((t_len, LANES), F32)],
        compiler_params=_cparams(1),
    )(h, g, win, wf)


def _transpose_lanes8(x):
    eye = (lax.broadcasted_iota(jnp.int32, (8, LANES), 0) == lax.broadcasted_iota(jnp.int32, (8, LANES), 1)).astype(F32)
    return lax.dot_general(eye, x, NT_DIMS, precision=lax.Precision.HIGHEST, preferred_element_type=F32)


def _cumsum_fwd(fl, b_row):
    t_len = fl.shape[0]

    def body(fl_ref, b_ref, c_ref, ct_ref, carry):
        @pl.when(pl.program_id(0) == 0)
        def _():
            carry[...] = jnp.zeros_like(carry)

        r = lax.broadcasted_iota(jnp.int32, (TB, TB), 0)
        s = lax.broadcasted_iota(jnp.int32, (TB, TB), 1)
        lf = _log_sigmoid(fl_ref[...] + b_ref[...])
        c = _dot_exact((r >= s).astype(F32), lf) + carry[0:1, :]
        c_ref[...] = c
        ct_ref[...] = _transpose_lanes8(c)
        carry[...] += jnp.sum(lf, axis=0, keepdims=True)

    return _pcall(
        body, name="cumsum_fwd", grid=(t_len // TB,),
        in_specs=[_tok(LANES), _full((1, LANES))],
        out_specs=[_tok(LANES), pl.BlockSpec((8, TB), lambda i: (0, i))],
        out_shape=[_sds((t_len, LANES), F32), _sds((8, t_len), F32)],
        scratch_shapes=[pltpu.VMEM((8, LANES), F32)],
        compiler_params=_cparams(1),
    )(fl, b_row)


def _layer_norm_parts(y, g, b):
    mu = jnp.mean(y, axis=-1, keepdims=True)
    yc = y - mu
    r = lax.rsqrt(jnp.mean(yc * yc, axis=-1, keepdims=True) + EPS)
    n = yc * r
    return n, r, n * g + b


def _phase_copies(src, dst, rows):
    for p in range(1, 8):
        dst[p - 1, 0:rows - 8, :] = src[pl.ds(p, rows - 8), :]


def _tap(src, copies, off, n):
    p = off % 8
    return src[pl.ds(off, n), :] if p == 0 else copies[p - 1, pl.ds(off - p, n), :]


def _pool_window_sums(p0, p1, p2, p3):
    e = HALO + TB
    p1[8:e, :] = p0[pl.ds(8, e - 8), :] + p0[pl.ds(7, e - 8), :]
    p2[16:e, :] = p1[pl.ds(16, e - 16), :] + p1[pl.ds(14, e - 16), :]
    p3[24:e, :] = p2[pl.ds(24, e - 24), :] + p2[pl.ds(20, e - 24), :]
    s16 = p3[pl.ds(HALO, TB), :] + p3[pl.ds(HALO - 8, TB), :]
    return p1[pl.ds(HALO, TB), :], p2[pl.ds(HALO, TB), :], p3[pl.ds(HALO, TB), :], s16


def _branch_fwd(zc, w_dw, ln_g, ln_b, w_pw, w_sc, w_pool, pool_scale):
    t_len = zc.shape[0]
    e = HALO + TB

    def body(z_ref, zh_ref, dw_ref, g_ref, b_ref, pw_ref, sc_ref, pool_ref, ps_ref, cat_ref, u_s, ch_s, p0, p1, p2, p3, u_ph):
        i = pl.program_id(0)
        hm = (i > 0).astype(F32)
        u_s[0:HALO, :] = zh_ref[:, 0:256] * _sigmoid(zh_ref[:, 256:512]) * hm
        u_s[HALO:e, :] = z_ref[:, 0:256] * _sigmoid(z_ref[:, 256:512])
        _phase_copies(u_s, u_ph, e)
        y = jnp.zeros((TB, DG), F32)
        for k in range(CONF_K):
            y = y + dw_ref[k:k + 1, :] * _tap(u_s, u_ph, HALO - (CONF_K - 1) + k, TB)
        _, _, yn = _layer_norm_parts(y, g_ref[...], b_ref[...])
        s = yn * _sigmoid(yn)
        cat_ref[:, 0:256] = _dot(s.astype(BF16), pw_ref[...])
        ch_s[0:HALO, :] = zh_ref[:, 1024:1280] * zh_ref[:, 512:768] * hm
        ch_s[HALO:e, :] = z_ref[:, 1024:1280] * z_ref[:, 512:768]
        cv = jnp.zeros((TB, DG), F32)
        for k in range(SC_K):
            cv = cv + sc_ref[k:k + 1, :] * ch_s[pl.ds(HALO - (SC_K - 1) + k, TB), :]
        cat_ref[:, 256:512] = z_ref[:, 768:1024] * cv
        p0[0:HALO, :] = zh_ref[:, 1280:1536] * hm
        p0[HALO:e, :] = z_ref[:, 1280:1536]
        s2, s4, s8, s16 = _pool_window_sums(p0, p1, p2, p3)
        cnt, lane = _pool_counts(i * TB, TB)
        dlt = _lane_group_select(lane, s2, s4, s8, s16) / cnt - z_ref[:, 1280:1536]
        cat_ref[:, 512:768] = _dot(dlt.astype(BF16), pool_ref[...]) * ps_ref[...]

    scr = [pltpu.VMEM((e, DG), F32) for _ in range(6)] + [pltpu.VMEM((7, e, DG), F32)]
    return _pcall(
        body, name="branch_fwd", grid=(t_len // TB,),
        in_specs=[_tok(1536), _halo_prev(1536), _full((32, DG)), _full((1, DG)), _full((1, DG)), _full((DG, DG)),
                  _full((8, DG)), _full((DG, DG)), _full((1, DG))],
        out_specs=_tok(768), out_shape=_sds((t_len, 768), F32), scratch_shapes=scr, compiler_params=_cparams(1),
    )(zc, zc, w_dw, ln_g, ln_b, w_pw, w_sc, w_pool, pool_scale)


def _head_masks(rows):
    lane = lax.broadcasted_iota(jnp.int32, (rows, LANES), 1)
    return lane, (lane < HEAD_DIM, lane >= HEAD_DIM)


def _keep_lanes(mask, x):
    return jnp.where(mask, x.astype(F32), 0.0).astype(BF16)


def _with_exchange(refs, n_in, n_out, n_x, per_dest, first, last):
    ins, x_src = refs[:n_in], refs[n_in:n_in + n_x]
    outs, x_dst = refs[n_in + n_x:n_in + n_x + n_out], refs[n_in + n_x + n_out:n_in + 2 * n_x + n_out]
    begin = finish = None
    if n_x:
        start, wait = _exchange_ops(x_src, x_dst, [per_dest] * n_x, *refs[n_in + 2 * n_x + n_out:])

        def begin():
            pl.when(first)(start)

        def finish():
            pl.when(last)(wait)

    return ins, outs, begin, finish


def _attn_fwd(qkv, c, ct, bcast=()):
    t_len = qkv.shape[0]
    n_t = t_len // TB
    n_x = len(bcast)

    def body(*refs):
        i = pl.program_id(0)
        (q_ref, k_ref, v_ref, c_ref, ct_ref), (o_ref, lset_ref), begin, finish = _with_exchange(
            refs, 5, 2, n_x, False, i == 0, i == n_t - 1)
        if begin:
            begin()
        lane, halves = _head_masks(TB)
        crow = c_ref[...]
        causal = lax.broadcasted_iota(jnp.int32, (TB, TB), 0) >= lax.broadcasted_iota(jnp.int32, (TB, TB), 1)
        lse_out = jnp.zeros((TB, LANES), F32)
        for g in range(2):
            cols = slice(g * LANES, (g + 1) * LANES)
            qg = q_ref[:, cols]
            qms = [_keep_lanes(halves[hh], qg) for hh in range(2)]
            cqs = [jnp.sum(jnp.where(lane == 2 * g + hh, crow, 0.0), axis=1, keepdims=True) for hh in range(2)]

            def block(j, carry, masked):
                off = pl.multiple_of(j * TB, TB)
                kj = k_ref[pl.ds(off, TB), cols]
                vj = v_ref[pl.ds(off, TB), cols]
                new = []
                for hh in range(2):
                    m, l, acc = carry[hh]
                    s = _dot_nt(qms[hh], kj) + (cqs[hh] - ct_ref[2 * g + hh:2 * g + hh + 1, pl.ds(off, TB)])
                    if masked:
                        s = jnp.where(causal, s, -jnp.inf)
                    m_new = jnp.maximum(m, jnp.max(s, axis=1, keepdims=True))
                    alpha = jnp.exp(m - m_new)
                    p = jnp.exp(s - m_new)
                    l = alpha * l + jnp.sum(p, axis=1, keepdims=True)
                    acc = alpha * acc + _dot(p.astype(BF16), vj)
                    new.append((m_new, l, acc))
                return tuple(new)

            init = tuple((jnp.full((TB, 1), -jnp.inf, F32), jnp.zeros((TB, 1), F32), jnp.zeros((TB, LANES), F32))
                         for _ in range(2))
            carry = lax.fori_loop(0, i, lambda j, cr: block(j, cr, False), init)
            (m0, l0, acc0), (m1, l1, acc1) = block(i, carry, True)
            o_ref[:, cols] = jnp.where(halves[0], acc0 / l0, acc1 / l1)
            lse_out = jnp.where(lane == 2 * g, m0 + jnp.log(l0), lse_out)
            lse_out = jnp.where(lane == 2 * g + 1, m1 + jnp.log(l1), lse_out)
        lset_ref[...] = _transpose_lanes8(lse_out)
        if finish:
            finish()

    return _pcall(
        body, name="attn_fwd_gather" if n_x else "attn_fwd", grid=(n_t,),
        in_specs=[_tokcol(DG, 0), pl.BlockSpec((t_len, DG), lambda i: (0, 1)), pl.BlockSpec((t_len, DG), lambda i: (0, 2)),
                  _tok(LANES), _full((8, t_len))] + [_HBM] * n_x,
        out_specs=[_tok(DG), pl.BlockSpec((8, TB), lambda i: (0, i))] + [_HBM] * n_x,
        out_shape=[_sds((t_len, DG), F32), _sds((8, t_len), F32)] + _exchange_shapes(bcast, [False] * n_x),
        scratch_shapes=_exchange_scratch(n_x) if n_x else [],
        compiler_params=_cparams(1),
    )(qkv, qkv, qkv, c, ct, *bcast)


def _mix_projection(cat_ref, o_ref, w):
    return (_dot(cat_ref[:, 0:256].astype(BF16), w[0:256]) + _dot(o_ref[...].astype(BF16), w[256:512])
            + _dot(cat_ref[:, 256:768].astype(BF16), w[512:1024]))


def _mixout_fwd(h, cat3, o, w_rows, g):
    t_len = h.shape[0]

    def body(h_ref, cat_ref, o_ref, w_ref, g_ref, h1_ref):
        mix = _mix_projection(cat_ref, o_ref, w_ref[...].reshape(D, D))
        h1_ref[...] = h_ref[...] + _rms(mix, g_ref[...])

    return _pcall(
        body, name="mixout_fwd", grid=(t_len // TB,),
        in_specs=[_tok(D), _tok(768), _tok(DG), pl.BlockSpec((N_DEV, 128, D), lambda i: (0, 4, 0)), _full((1, D))],
        out_specs=_tok(D), out_shape=_sds((t_len, D), F32), compiler_params=_cparams(1),
    )(h, cat3, o, w_rows, g)


def _mlp_fwd(h, g_pre, w_up, w_rows, g_post):
    t_len = h.shape[0]

    def body(h_ref, g1_ref, up_ref, dn_ref, g2_ref, u_ref, ff_ref, h2_ref):
        hn = _rms(h_ref[...], g1_ref[...]).astype(BF16)
        ff = jnp.zeros((TB, D), F32)
        for j in range(N_DEV):
            u = _dot(hn, up_ref[j])
            u_ref[:, j * FF_BLK:(j + 1) * FF_BLK] = u.astype(BF16)
            r = jnp.maximum(u, 0.0)
            ff = ff + _dot((r * r).astype(BF16), dn_ref[j])
        ff_ref[...] = ff
        h2_ref[...] = h_ref[...] + _rms(ff, g2_ref[...])

    return _pcall(
        body, name="mlp_fwd", grid=(t_len // TB,),
        in_specs=[_tok(D), _full((1, D)), _resident((N_DEV, D, FF_BLK)), _resident((N_DEV, FF_BLK, D)), _full((1, D))],
        out_specs=[_tok(D_FF), _tok(D), _tok(D)],
        out_shape=[_sds((t_len, D_FF), BF16), _sds((t_len, D), F32), _sds((t_len, D), F32)],
        compiler_params=_cparams(1),
    )(h, g_pre, w_up, w_rows, g_post)


def _ple_fwd(h, p_all, layer, g_pre, w_rows, w_proj, g_post):
    t_len = h.shape[0]
    n_t = t_len // TB

    def body(h_ref, p_ref, g1_ref, wg_ref, wp_ref, g2_ref, h3_ref):
        pp, gate = _ple_parts(h_ref, p_ref, g1_ref, wg_ref, wp_ref)
        h3_ref[...] = h_ref[...] + _rms(pp * gate, g2_ref[...])

    return _pcall(
        body, name="ple_fwd", grid=(n_t,),
        in_specs=[_tok(D), pl.BlockSpec((TB, D_PLE), lambda i: (layer * n_t + i, 0)), _full((1, D)),
                  pl.BlockSpec((N_DEV, 128, D), lambda i: (0, 5, 0)), _full((D_PLE, D)), _full((1, D))],
        out_specs=_tok(D), out_shape=_sds((t_len, D), F32), compiler_params=_cparams(1),
    )(h, p_all, g_pre, w_rows, w_proj, g_post)


def _ple_parts(h_ref, p_ref, g1_ref, wg_ref, wp_ref):
    hn = _rms(h_ref[...], g1_ref[...]).astype(BF16)
    gate = _sigmoid(_dot(hn, wg_ref[...].reshape(D, D)))
    return _dot(p_ref[...].astype(BF16), wp_ref[...]), gate


def _loss_bwd(h, target):
    t_len = h.shape[0]

    def body(h_ref, t_ref, dh_ref, loss_ref):
        @pl.when(pl.program_id(0) == 0)
        def _():
            loss_ref[...] = jnp.zeros_like(loss_ref)

        d = h_ref[...] - t_ref[...]
        dh_ref[...] = d * (1.0 / D)
        loss_ref[...] += 0.5 * jnp.sum(jnp.mean(d * d, axis=-1, keepdims=True), axis=0, keepdims=True)

    return _pcall(
        body, name="loss_bwd", grid=(t_len // TB,), in_specs=[_tok(D), _tok(D)],
        out_specs=[_tok(D), _full((8, LANES))], out_shape=[_sds((t_len, D), F32), _sds((8, LANES), F32)],
        compiler_params=_cparams(1),
    )(h, target)


def _acc_init(refs):
    @pl.when(pl.program_id(0) == 0)
    def _():
        for r in refs:
            r[...] = jnp.zeros_like(r)


def _ple_bwd(dh3, h2, p_all, layer, g_post, g_pre, w_rows, w_proj):
    t_len = dh3.shape[0]
    n_t = t_len // TB

    def body(dh_ref, h_ref, p_ref, g2_ref, g1_ref, wg_ref, wp_ref, dh2_ref, dpp_ref, dpre_ref, hn_ref, dg2_ref, dg1_ref):
        _acc_init([dg2_ref, dg1_ref])
        dh = dh_ref[...]
        pp, gate = _ple_parts(h_ref, p_ref, g1_ref, wg_ref, wp_ref)
        de, dg2, _ = _rms_bwd(pp * gate, g2_ref[...], dh)
        dg2_ref[...] += dg2
        dpp_ref[...] = (de * gate).astype(BF16)
        dpre = (de * pp * gate * (1.0 - gate)).astype(BF16)
        dpre_ref[...] = dpre
        dhn = _dot_nt(dpre, wg_ref[...].reshape(D, D))
        dx, dg1, hn = _rms_bwd(h_ref[...], g1_ref[...], dhn)
        dg1_ref[...] += dg1
        hn_ref[...] = hn.astype(BF16)
        dh2_ref[...] = dh + dx

    return _pcall(
        body, name="ple_bwd", grid=(n_t,),
        in_specs=[_tok(D), _tok(D), pl.BlockSpec((TB, D_PLE), lambda i: (layer * n_t + i, 0)), _full((1, D)), _full((1, D)),
                  pl.BlockSpec((N_DEV, 128, D), lambda i: (0, 5, 0)), _full((D_PLE, D))],
        out_specs=[_tok(D)] * 4 + [_full((1, D))] * 2,
        out_shape=[_sds((t_len, D), F32)] + [_sds((t_len, D), BF16)] * 3 + [_sds((1, D), F32)] * 2,
        compiler_params=_cparams(1),
    )(dh3, h2, p_all, g_post, g_pre, w_rows, w_proj)


def _mlp_bwd(dh2, h1, u, ff, g_post, g_pre, w_up, w_rows):
    t_len = dh2.shape[0]

    def body(dh_ref, h_ref, u_ref, ff_ref, g2_ref, g1_ref, up_ref, dn_ref,
             dh1_ref, a2_ref, du_ref, dff_ref, hn_ref, dg2_ref, dg1_ref, dff_s, acc_s):
        i, j = pl.program_id(0), pl.program_id(1)

        @pl.when((i == 0) & (j == 0))
        def _():
            dg2_ref[...] = jnp.zeros_like(dg2_ref)
            dg1_ref[...] = jnp.zeros_like(dg1_ref)

        @pl.when(j == 0)
        def _():
            dff, dg2, _ = _rms_bwd(ff_ref[...], g2_ref[...], dh_ref[...])
            dg2_ref[...] += dg2
            dff_s[...] = dff.astype(BF16)
            dff_ref[...] = dff.astype(BF16)
            acc_s[...] = jnp.zeros_like(acc_s)

        acc = acc_s[...]
        for b in range(MLP_BWD_BLOCKS):
            cols = slice(b * FF_BLK, (b + 1) * FF_BLK)
            r = jnp.maximum(u_ref[:, cols].astype(F32), 0.0)
            a2_ref[:, cols] = (r * r).astype(BF16)
            du = (_dot_nt(dff_s[...], dn_ref[MLP_BWD_BLOCKS * j + b]) * (2.0 * r)).astype(BF16)
            du_ref[:, cols] = du
            acc = acc + _dot_nt(du, up_ref[MLP_BWD_BLOCKS * j + b])
        acc_s[...] = acc

        @pl.when(j == N_DEV // MLP_BWD_BLOCKS - 1)
        def _():
            dx, dg1, hn = _rms_bwd(h_ref[...], g1_ref[...], acc_s[...])
            dg1_ref[...] += dg1
            hn_ref[...] = hn.astype(BF16)
            dh1_ref[...] = dh_ref[...] + dx

    tok2 = pl.BlockSpec((TB, D), lambda i, j: (i, 0))
    vec2 = pl.BlockSpec((1, D), lambda i, j: (0, 0))
    blk2 = pl.BlockSpec((TB, MLP_BWD_BLOCKS * FF_BLK), lambda i, j: (i, j))
    return _pcall(
        body, name="mlp_bwd", grid=(t_len // TB, N_DEV // MLP_BWD_BLOCKS),
        in_specs=[tok2, tok2, blk2, tok2, vec2, vec2, _resident((N_DEV, D, FF_BLK)), _resident((N_DEV, FF_BLK, D))],
        out_specs=[tok2, blk2, blk2, tok2, tok2, vec2, vec2],
        out_shape=[_sds((t_len, D), F32), _sds((t_len, D_FF), BF16), _sds((t_len, D_FF), BF16), _sds((t_len, D), BF16),
                   _sds((t_len, D), BF16), _sds((1, D), F32), _sds((1, D), F32)],
        scratch_shapes=[pltpu.VMEM((TB, D), BF16), pltpu.VMEM((TB, D), F32)], compiler_params=_cparams(2),
    )(dh2, h1, u, ff, g_post, g_pre, w_up, w_rows)


def _mixout_bwd(dh1, cat3, o, g, w_rows):
    t_len = dh1.shape[0]

    def body(dh_ref, cat_ref, o_ref, g_ref, w_ref, dcat_ref, dmix_ref, catb_ref, dg_ref):
        _acc_init([dg_ref])
        w = w_ref[...].reshape(D, D)
        dmix, dg, _ = _rms_bwd(_mix_projection(cat_ref, o_ref, w), g_ref[...], dh_ref[...])
        dg_ref[...] += dg
        dmix = dmix.astype(BF16)
        dmix_ref[...] = dmix
        dcat_ref[...] = _dot_nt(dmix, w)
        catb_ref[:, 0:256] = cat_ref[:, 0:256].astype(BF16)
        catb_ref[:, 256:512] = o_ref[...].astype(BF16)
        catb_ref[:, 512:1024] = cat_ref[:, 256:768].astype(BF16)

    return _pcall(
        body, name="mixout_bwd", grid=(t_len // TB,),
        in_specs=[_tok(D), _tok(768), _tok(DG), _full((1, D)), pl.BlockSpec((N_DEV, 128, D), lambda i: (0, 4, 0))],
        out_specs=[_tok(D), _tok(D), _tok(D), _full((1, D))],
        out_shape=[_sds((t_len, D), F32), _sds((t_len, D), BF16), _sds((t_len, D), BF16), _sds((1, D), F32)],
        compiler_params=_cparams(1),
    )(dh1, cat3, o, g, w_rows)


def _attn_bwd_dsum(qkv, dcat, c, ct, lset):
    t_len = qkv.shape[0]

    def body(q_ref, do_ref, k_ref, v_ref, c_ref, ct_ref, lset_ref, dt_ref, dob_ref, dobt_ref):
        i = pl.program_id(0)
        lane, halves = _head_masks(TB)
        causal_t = lax.broadcasted_iota(jnp.int32, (TB, TB), 1) >= lax.broadcasted_iota(jnp.int32, (TB, TB), 0)
        sub = lax.broadcasted_iota(jnp.int32, (8, TB), 0)
        dob_ref[...] = do_ref[...].astype(BF16)
        dobt_ref[...] = do_ref[...].T.astype(BF16)
        out = jnp.zeros((8, TB), F32)
        for g in range(2):
            cols = slice(g * LANES, (g + 1) * LANES)
            qi = q_ref[:, cols]
            doi = do_ref[:, cols].astype(BF16)
            doms = [_keep_lanes(halves[hh], doi) for hh in range(2)]
            cqs = [ct_ref[2 * g + hh:2 * g + hh + 1, :] for hh in range(2)]
            lses = [lset_ref[2 * g + hh:2 * g + hh + 1, :] for hh in range(2)]

            def block(j, accs, masked):
                off = pl.multiple_of(j * TB, TB)
                kj = k_ref[pl.ds(off, TB), cols]
                vj = v_ref[pl.ds(off, TB), cols]
                cj = c_ref[pl.ds(off, TB), :]
                new = []
                for hh in range(2):
                    ck = jnp.sum(jnp.where(lane == 2 * g + hh, cj, 0.0), axis=1, keepdims=True)
                    st = _dot_nt(_keep_lanes(halves[hh], kj), qi) + (cqs[hh] - ck)
                    if masked:
                        st = jnp.where(causal_t, st, -jnp.inf)
                    pt = jnp.exp(st - lses[hh])
                    new.append(accs[hh] + jnp.sum(pt * _dot_nt(vj, doms[hh]), axis=0, keepdims=True))
                return tuple(new)

            init = (jnp.zeros((1, TB), F32), jnp.zeros((1, TB), F32))
            accs = block(i, lax.fori_loop(0, i, lambda j, cr: block(j, cr, False), init), True)
            out = jnp.where(sub == 2 * g, accs[0], out)
            out = jnp.where(sub == 2 * g + 1, accs[1], out)
        dt_ref[...] = out

    row8 = pl.BlockSpec((8, TB), lambda i: (0, i))
    return _pcall(
        body, name="attn_bwd_dsum", grid=(t_len // TB,),
        in_specs=[_tokcol(DG, 0), _tokcol(DG, 1), pl.BlockSpec((t_len, DG), lambda i: (0, 1)),
                  pl.BlockSpec((t_len, DG), lambda i: (0, 2)), _full((t_len, LANES)), row8, row8],
        out_specs=[row8, _tok(DG), pl.BlockSpec((DG, TB), lambda i: (0, i))],
        out_shape=[_sds((8, t_len), F32), _sds((t_len, DG), BF16), _sds((DG, t_len), BF16)],
        compiler_params=_cparams(1),
    )(qkv, dcat, qkv, qkv, c, ct, lset)


def _attn_bwd(qkv, dob, qt, dobt, c, ct, lset, dt, xchg=()):
    t_len = qkv.shape[0]
    n_q = t_len // TB
    n_x = len(xchg)

    def body(*refs):
        j = pl.program_id(0)
        ins, outs, begin, finish = _with_exchange(refs, 10, 4, n_x, True, j == 0, j == n_q - 1)
        q_ref, dob_ref, qt_ref, dobt_ref, k_ref, v_ref, c_ref, ct_ref, lset_ref, dt_ref = ins
        dq_ref, dk_ref, dv_ref, dc_ref = outs
        if begin:
            begin()

        @pl.when(j == 0)
        def _():
            dq_ref[...] = jnp.zeros_like(dq_ref)

        lane, halves = _head_masks(TB)
        feat = lax.broadcasted_iota(jnp.int32, (LANES, TB), 0)
        halves_t = (feat < HEAD_DIM, feat >= HEAD_DIM)
        crow = c_ref[...]
        causal_t = lax.broadcasted_iota(jnp.int32, (TB, TB), 1) >= lax.broadcasted_iota(jnp.int32, (TB, TB), 0)
        dc_out = jnp.zeros((TB, LANES), F32)
        for g in range(2):
            cols = slice(g * LANES, (g + 1) * LANES)
            kg, vg = k_ref[:, cols], v_ref[:, cols]
            kms = [_keep_lanes(halves[hh], kg) for hh in range(2)]
            cks = [jnp.sum(jnp.where(lane == 2 * g + hh, crow, 0.0), axis=1, keepdims=True) for hh in range(2)]

            def block(i, carry, masked):
                dk_t, dv_t, dcs = carry
                off = pl.multiple_of(i * TB, TB)
                qi = q_ref[pl.ds(off, TB), cols]
                doi = dob_ref[pl.ds(off, TB), cols]
                qi_t = qt_ref[cols, pl.ds(off, TB)]
                doi_t = dobt_ref[cols, pl.ds(off, TB)]
                dq_add = jnp.zeros((LANES, TB), F32)
                dcs_new = []
                for hh in range(2):
                    h = 2 * g + hh
                    dom = _keep_lanes(halves[hh], doi)
                    st = _dot_nt(kms[hh], qi) + (ct_ref[h:h + 1, pl.ds(off, TB)] - cks[hh])
                    if masked:
                        st = jnp.where(causal_t, st, -jnp.inf)
                    pt = jnp.exp(st - lset_ref[h:h + 1, pl.ds(off, TB)])
                    dv_t = dv_t + _dot_nt(_keep_lanes(halves_t[hh], doi_t), pt.astype(BF16))
                    dst = pt * (_dot_nt(vg, dom) - dt_ref[h:h + 1, pl.ds(off, TB)])
                    dsb = dst.astype(BF16)
                    dk_t = dk_t + _dot_nt(_keep_lanes(halves_t[hh], qi_t), dsb)
                    dcs_new.append(dcs[hh] + jnp.sum(dst, axis=1, keepdims=True))
                    dq_add = dq_add + _dot_tn(kms[hh], dsb)
                dq_ref[cols, pl.ds(off, TB)] += dq_add
                return dk_t, dv_t, tuple(dcs_new)

            init = (jnp.zeros((LANES, TB), F32), jnp.zeros((LANES, TB), F32),
                    (jnp.zeros((TB, 1), F32), jnp.zeros((TB, 1), F32)))
            carry = block(j, init, True)
            dk_t, dv_t, dcs = lax.fori_loop(j + 1, n_q, lambda i, cr: block(i, cr, False), carry)
            dk_ref[:, cols] = dk_t.T
            dv_ref[:, cols] = dv_t.T
            dc_out = jnp.where(lane == 2 * g, -dcs[0], dc_out)
            dc_out = jnp.where(lane == 2 * g + 1, -dcs[1], dc_out)
        dc_ref[...] = dc_out
        if finish:
            finish()

    return _pcall(
        body, name="attn_bwd_exchange" if n_x else "attn_bwd", grid=(n_q,),
        in_specs=[_resident((t_len, DG)), _resident((t_len, DG)), _resident((DG, t_len)), _resident((DG, t_len)),
                  _tokcol(DG, 1), _tokcol(DG, 2), _tok(LANES), _full((8, t_len)), _full((8, t_len)), _full((8, t_len))]
        + [_HBM] * n_x,
        out_specs=[_full((DG, t_len)), _tok(DG), _tok(DG), _tok(LANES)] + [_HBM] * n_x,
        out_shape=[_sds((DG, t_len), F32), _sds((t_len, DG), F32), _sds((t_len, DG), F32), _sds((t_len, LANES), F32)]
        + _exchange_shapes(xchg, [True] * n_x),
        scratch_shapes=_exchange_scratch(n_x) if n_x else [],
        compiler_params=_cparams(1),
    )(qkv, dob, qt, dobt, qkv, qkv, c, ct, lset, dt, *xchg)


def _forget_bwd(dc, fl, b_row):
    t_len = dc.shape[0]
    n_t = t_len // TB
    rev = pl.BlockSpec((TB, LANES), lambda i: (n_t - 1 - i, 0))

    def body(dc_ref, fl_ref, b_ref, dfl_ref, db_ref, carry):
        @pl.when(pl.program_id(0) == 0)
        def _():
            carry[...] = jnp.zeros_like(carry)
            db_ref[...] = jnp.zeros_like(db_ref)

        r = lax.broadcasted_iota(jnp.int32, (TB, TB), 0)
        s = lax.broadcasted_iota(jnp.int32, (TB, TB), 1)
        dc = dc_ref[...]
        dl = _dot_exact((r <= s).astype(F32), dc) + carry[0:1, :]
        carry[...] += jnp.sum(dc, axis=0, keepdims=True)
        dfl = dl * _sigmoid(-(fl_ref[...] + b_ref[...]))
        dfl_ref[...] = dfl
        db_ref[...] += jnp.sum(dfl, axis=0, keepdims=True)

    return _pcall(
        body, name="forget_bwd", grid=(n_t,), in_specs=[rev, rev, _full((1, LANES))],
        out_specs=[rev, _full((1, LANES))], out_shape=[_sds((t_len, LANES), F32), _sds((1, LANES), F32)],
        scratch_shapes=[pltpu.VMEM((8, LANES), F32)], compiler_params=_cparams(1),
    )(dc, fl, b_row)


def _branch_bwd(zc, dcat, dq, dk, dv, w_dw, ln_g, ln_b, w_pw, w_sc, w_pool, pool_scale):
    t_len = zc.shape[0]
    n_t = t_len // TB
    e2 = HALO + TB + HALO
    e1 = TB + HALO

    def body(z_ref, zp_ref, zn_ref, dcf_ref, dcfn_ref, dsp_ref, dspn_ref, dq_ref, dk_ref, dv_ref,
             dw_ref, g_ref, b_ref, pw_ref, sc_ref, pool_ref, ps_ref,
             dz_ref, ddw_ref, dg_ref, db_ref, dpw_ref, dsc_ref, dpool_ref, dps_ref,
             u_s, dy_s, ch_s, dcv_s, p0, p1, p2, p3, g0, g1, g2, g3, u_ph, dy_ph):
        i = pl.program_id(0)
        _acc_init([ddw_ref, dg_ref, db_ref, dpw_ref, dsc_ref, dpool_ref, dps_ref])
        hm = (i > 0).astype(F32)
        nm = (i < n_t - 1).astype(F32)

        sig_b = _sigmoid(z_ref[:, 256:512])
        a = z_ref[:, 0:256]
        u_s[0:HALO, :] = zp_ref[:, 0:256] * _sigmoid(zp_ref[:, 256:512]) * hm
        u_s[HALO:HALO + TB, :] = a * sig_b
        u_s[HALO + TB:e2, :] = zn_ref[:, 0:256] * _sigmoid(zn_ref[:, 256:512])
        _phase_copies(u_s, u_ph, e2)
        y = jnp.zeros((e1, DG), F32)
        for k in range(CONF_K):
            y = y + dw_ref[k:k + 1, :] * _tap(u_s, u_ph, HALO - (CONF_K - 1) + k, e1)
        n, r, yn = _layer_norm_parts(y, g_ref[...], b_ref[...])
        sg = _sigmoid(yn)
        dyc = jnp.concatenate([dcf_ref[...], dcfn_ref[...] * nm], axis=0)
        ds = _dot_nt(dyc.astype(BF16), pw_ref[...])
        dyn = ds * sg * (1.0 + yn * (1.0 - sg))
        dg_ref[...] += jnp.sum((dyn * n)[0:TB], axis=0, keepdims=True)
        db_ref[...] += jnp.sum(dyn[0:TB], axis=0, keepdims=True)
        dn = dyn * g_ref[...]
        dyv = r * (dn - jnp.mean(dn, axis=-1, keepdims=True) - n * jnp.mean(dn * n, axis=-1, keepdims=True))
        dpw_ref[...] += _dot_tn((yn * sg)[0:TB].astype(BF16), dcf_ref[...].astype(BF16))
        dy_s[...] = dyv
        _phase_copies(dy_s, dy_ph, e1)
        du = jnp.zeros((TB, DG), F32)
        dyv_t = dyv[0:TB]
        for k in range(CONF_K):
            du = du + dw_ref[k:k + 1, :] * _tap(dy_s, dy_ph, CONF_K - 1 - k, TB)
            ddw_ref[k:k + 1, :] += jnp.sum(dyv_t * _tap(u_s, u_ph, HALO - (CONF_K - 1) + k, TB), axis=0, keepdims=True)
        dz_ref[:, 0:256] = (du * sig_b).astype(BF16)
        dz_ref[:, 256:512] = (du * a * sig_b * (1.0 - sig_b)).astype(BF16)

        dz_ref[:, 512:768] = (dq_ref[...].T * SCALE).astype(BF16)
        dz_ref[:, 768:1024] = dk_ref[...].astype(BF16)
        dz_ref[:, 1024:1280] = dv_ref[...].astype(BF16)

        sc_h, sc_b, sc_c = z_ref[:, 512:768], z_ref[:, 768:1024], z_ref[:, 1024:1280]
        ch_s[0:HALO, :] = zp_ref[:, 1024:1280] * zp_ref[:, 512:768] * hm
        ch_s[HALO:HALO + TB, :] = sc_c * sc_h
        ch_s[HALO + TB:e2, :] = zn_ref[:, 1024:1280] * zn_ref[:, 512:768]
        cv = jnp.zeros((TB, DG), F32)
        for k in range(SC_K):
            cv = cv + sc_ref[k:k + 1, :] * ch_s[pl.ds(HALO - (SC_K - 1) + k, TB), :]
        dy_sc = dsp_ref[:, 0:256]
        dcv_t = dy_sc * sc_b
        dcv_s[0:TB, :] = dcv_t
        dcv_s[TB:e1, :] = dspn_ref[:, 0:256] * nm * zn_ref[:, 768:1024]
        dch = jnp.zeros((TB, DG), F32)
        for k in range(SC_K):
            dch = dch + sc_ref[k:k + 1, :] * dcv_s[pl.ds(SC_K - 1 - k, TB), :]
            dsc_ref[k:k + 1, :] += jnp.sum(dcv_t * ch_s[pl.ds(HALO - (SC_K - 1) + k, TB), :], axis=0, keepdims=True)
        dz_ref[:, 1280:1536] = (dch * sc_c).astype(BF16)
        dz_ref[:, 1536:1792] = (dy_sc * cv).astype(BF16)
        dz_ref[:, 1792:2048] = (dch * sc_h).astype(BF16)

        v_t = z_ref[:, 1280:1536]
        p0[0:HALO, :] = zp_ref[:, 1280:1536] * hm
        p0[HALO:HALO + TB, :] = v_t
        s2, s4, s8, s16 = _pool_window_sums(p0, p1, p2, p3)
        cnt, lane = _pool_counts(i * TB, e1)
        dlt = (_lane_group_select(lane[0:TB], s2, s4, s8, s16) / cnt[0:TB] - v_t).astype(BF16)
        dyp_t = dsp_ref[:, 256:512]
        dps_ref[...] += jnp.sum(dyp_t * _dot(dlt, pool_ref[...]), axis=0, keepdims=True)
        dpre = (jnp.concatenate([dyp_t, dspn_ref[:, 256:512] * nm], axis=0) * ps_ref[...]).astype(BF16)
        dpool_ref[...] += _dot_tn(dlt, dpre[0:TB])
        dd = _dot_nt(dpre, pool_ref[...])
        g0[...] = dd / cnt
        g1[0:TB + 24, :] = g0[pl.ds(0, TB + 24), :] + g0[pl.ds(1, TB + 24), :]
        g2[0:TB + 16, :] = g1[pl.ds(0, TB + 16), :] + g1[pl.ds(2, TB + 16), :]
        g3[0:TB + 8, :] = g2[pl.ds(0, TB + 8), :] + g2[pl.ds(4, TB + 8), :]
        f16 = g3[pl.ds(0, TB), :] + g3[pl.ds(8, TB), :]
        fwd_sum = _lane_group_select(lane[0:TB], g1[pl.ds(0, TB), :], g2[pl.ds(0, TB), :], g3[pl.ds(0, TB), :], f16)
        dz_ref[:, 2048:2304] = (fwd_sum - dd[0:TB]).astype(BF16)

    vec = _full((1, DG))
    mat = _full((DG, DG))
    scr = ([pltpu.VMEM((e2, DG), F32), pltpu.VMEM((e1, DG), F32), pltpu.VMEM((e2, DG), F32), pltpu.VMEM((e1, DG), F32)]
           + [pltpu.VMEM((HALO + TB, DG), F32)] * 4 + [pltpu.VMEM((e1, DG), F32)] * 4
           + [pltpu.VMEM((7, e2, DG), F32), pltpu.VMEM((7, e1, DG), F32)])
    return _pcall(
        body, name="branch_bwd", grid=(n_t,),
        in_specs=[_tok(1536), _halo_prev(1536), _halo_next(1536, t_len),
                  _tokcol(DG, 0), _halo_next(DG, t_len, 0), _tokcol(512, 1), _halo_next(512, t_len, 1),
                  pl.BlockSpec((DG, TB), lambda i: (0, i)), _tok(DG), _tok(DG),
                  _full((32, DG)), vec, vec, mat, _full((8, DG)), mat, vec],
        out_specs=[_tok(W_MAIN), _full((32, DG)), vec, vec, mat, _full((8, DG)), mat, vec],
        out_shape=[_sds((t_len, W_MAIN), BF16), _sds((32, DG), F32), _sds((1, DG), F32), _sds((1, DG), F32),
                   _sds((DG, DG), F32), _sds((8, DG), F32), _sds((DG, DG), F32), _sds((1, DG), F32)],
        scratch_shapes=scr, compiler_params=_cparams(1),
    )(zc, zc, zc, dcat, dcat, dcat, dcat, dq, dk, dv, w_dw, ln_g, ln_b, w_pw, w_sc, w_pool, pool_scale)


def _mixin_bwd(dh, h, g, dz, dfl, win, wf):
    t_len = dh.shape[0]

    def body(dh_ref, h_ref, g_ref, dz_ref, dfl_ref, win_ref, wf_ref, dh0_ref, xn_ref, dg_ref):
        _acc_init([dg_ref])
        dxn = _dot_nt(dz_ref[...], win_ref[...].reshape(D, W_MAIN)) + _dot_nt(dfl_ref[...].astype(BF16), wf_ref[...])
        dx, dg, xn = _rms_bwd(h_ref[...], g_ref[...], dxn)
        dg_ref[...] += dg
        xn_ref[...] = xn.astype(BF16)
        dh0_ref[...] = dh_ref[...] + dx

    return _pcall(
        body, name="mixin_bwd", grid=(t_len // TB,),
        in_specs=[_tok(D), _tok(D), _full((1, D)), _tok(W_MAIN), _tok(LANES), _full((N_DEV, D // N_DEV, W_MAIN)),
                  _full((D, LANES))],
        out_specs=[_tok(D), _tok(D), _full((1, D))],
        out_shape=[_sds((t_len, D), F32), _sds((t_len, D), BF16), _sds((1, D), F32)],
        compiler_params=_cparams(1),
    )(dh, h, g, dz, dfl, win, wf)


def _matmul_tn(name, a, b, tm, tn, out_dtype, block_major=False, a_section=0):
    t_len, n = b.shape
    m = a.shape[1]
    tk = min(t_len, 1024)
    n_k = t_len // tk

    def body(a_ref, b_ref, o_ref, acc):
        k = pl.program_id(2)

        @pl.when(k == 0)
        def _():
            acc[...] = jnp.zeros_like(acc)

        acc[...] += _dot_tn(a_ref[...].astype(BF16), b_ref[...].astype(BF16))

        @pl.when(k == n_k - 1)
        def _():
            if block_major:
                for blk in range(tn // FF_BLK):
                    o_ref[blk] = acc[:, blk * FF_BLK:(blk + 1) * FF_BLK].astype(out_dtype)
            else:
                o_ref[...] = acc[...].astype(out_dtype)

    if block_major:
        out_spec = pl.BlockSpec((tn // FF_BLK, tm, FF_BLK), lambda i, j, k: (j, i, 0))
        out_shape = _sds((n // FF_BLK, m, FF_BLK), out_dtype)
    else:
        out_spec = pl.BlockSpec((tm, tn), lambda i, j, k: (i, j))
        out_shape = _sds((m, n), out_dtype)
    return _pcall(
        body, name=name, grid=(m // tm, n // tn, n_k),
        in_specs=[pl.BlockSpec((tk, tm), lambda i, j, k: (a_section * n_k + k, i)), pl.BlockSpec((tk, tn), lambda i, j, k: (k, j))],
        out_specs=out_spec, out_shape=out_shape, scratch_shapes=[pltpu.VMEM((tm, tn), F32)], compiler_params=_cparams(3),
    )(a, b)


_HBM = pl.BlockSpec(memory_space=pltpu.HBM)


def _mesh_place():
    return lax.axis_index("x"), lax.axis_index("y"), lax.axis_index("c")


def _allgather(name, srcs):
    n = len(srcs)

    def body(*refs):
        src, dst = refs[:n], refs[n:2 * n]
        send_sems, recv_sems, local_sems = refs[2 * n:]
        x, y, c = _mesh_place()
        me, sibling = (x, y, c), (x, y, 1 - c)
        chips = [(1 - x, y), (x, 1 - y), (1 - x, 1 - y)]

        def slot(px, py, pc):
            return 4 * px + 2 * py + pc

        def copy(t, k, block, to, from_src=False):
            return pltpu.make_async_remote_copy(
                src_ref=src[t] if from_src else dst[t].at[slot(*block)], dst_ref=dst[t].at[slot(*block)],
                send_sem=send_sems.at[t, k], recv_sem=recv_sems.at[t, k], device_id=to, device_id_type=MESH_ID)

        mine = [pltpu.make_async_copy(src[t], dst[t].at[slot(*me)], local_sems.at[t]) for t in range(n)]
        for cp in mine:
            cp.start()
        started = []
        for t in range(n):
            started.append(copy(t, 0, me, sibling, from_src=True))
            started += [copy(t, 1 + j, me, (*chip, c), from_src=True) for j, chip in enumerate(chips)]
        for cp in started:
            cp.start()
        for j, chip in enumerate(chips):
            for t in range(n):
                copy(t, 1 + j, (*chip, c), me).wait_recv()
                fwd = copy(t, 4 + j, (*chip, c), sibling)
                fwd.start()
                started.append(fwd)
        for t in range(n):
            copy(t, 0, sibling, me).wait_recv()
            for j, chip in enumerate(chips):
                copy(t, 4 + j, (*chip, 1 - c), me).wait_recv()
        for cp in started:
            cp.wait_send()
        for cp in mine:
            cp.wait()

    return _pcall(
        body, name=name, in_specs=[_HBM] * n, out_specs=[_HBM] * n,
        out_shape=[_sds((N_DEV,) + s.shape, s.dtype) for s in srcs],
        scratch_shapes=[pltpu.SemaphoreType.DMA((n, 7)), pltpu.SemaphoreType.DMA((n, 7)), pltpu.SemaphoreType.DMA((n,))],
    )(*srcs)


def _exchange_ops(src, dst, per_dest, send_sems, recv_sems, local_sems):
    n = len(src)
    x, y, c = _mesh_place()
    me_slot = 4 * x + 2 * y + c
    peers = []
    for r in range(1, N_DEV):
        px, py, pc = x ^ ((r >> 2) & 1), y ^ ((r >> 1) & 1), c ^ (r & 1)
        peers.append(((px, py, pc), 4 * px + 2 * py + pc))

    def piece(t, dest_slot):
        return src[t].at[dest_slot] if per_dest[t] else src[t]

    def local(t):
        return pltpu.make_async_copy(piece(t, me_slot), dst[t].at[me_slot], local_sems.at[t])

    def remote(t, r, landing_slot):
        peer, peer_slot = peers[r]
        return pltpu.make_async_remote_copy(
            src_ref=piece(t, peer_slot), dst_ref=dst[t].at[landing_slot], send_sem=send_sems.at[t, r],
            recv_sem=recv_sems.at[t, r], device_id=peer, device_id_type=MESH_ID)

    def start():
        for t in range(n):
            local(t).start()
        for r in range(N_DEV - 1):
            for t in range(n):
                remote(t, r, me_slot).start()

    def wait():
        for r in range(N_DEV - 1):
            for t in range(n):
                remote(t, r, peers[r][1]).wait_recv()
        for r in range(N_DEV - 1):
            for t in range(n):
                remote(t, r, me_slot).wait_send()
        for t in range(n):
            local(t).wait()

    return start, wait


def _exchange_shapes(srcs, per_dest):
    return [_sds((N_DEV,) + tuple(s.shape[1:] if pd else s.shape), s.dtype) for s, pd in zip(srcs, per_dest)]


def _exchange_scratch(n):
    return [pltpu.SemaphoreType.DMA((n, N_DEV - 1)), pltpu.SemaphoreType.DMA((n, N_DEV - 1)), pltpu.SemaphoreType.DMA((n,))]


def _exchange(name, srcs, per_dest):
    n = len(srcs)

    def body(*refs):
        start, wait = _exchange_ops(refs[:n], refs[n:2 * n], per_dest, *refs[2 * n:])
        start()
        wait()

    return _pcall(
        body, name=name, in_specs=[_HBM] * n, out_specs=[_HBM] * n, out_shape=_exchange_shapes(srcs, per_dest),
        scratch_shapes=_exchange_scratch(n),
    )(*srcs)


def _adam_math(w, g, m, v):
    m = ADAM_B1 * m + (1.0 - ADAM_B1) * g
    v = ADAM_B2 * v + (1.0 - ADAM_B2) * (g * g)
    m_hat = m / (1.0 - ADAM_B1 ** ADAM_STEP)
    v_hat = v / (1.0 - ADAM_B2 ** ADAM_STEP)
    delta = -ADAM_LR * (m_hat / (jnp.sqrt(v_hat) + ADAM_EPS) + ADAM_WD * w)
    return delta, m, v


def _adam_rows(name, parts, w, m, v, row_tile):
    n_l, rows, cols = w.shape

    def body(*refs):
        p_refs = refs[:n_l]
        w_ref, m_ref, v_ref, g_out, d_out, m_out, v_out = refs[n_l:]
        layer = pl.program_id(0)
        for k in range(n_l):
            @pl.when(layer == k)
            def _(p_ref=p_refs[k]):
                g = p_ref[0].astype(F32)
                for s in range(1, N_DEV):
                    g = g + p_ref[s].astype(F32)
                delta, m_new, v_new = _adam_math(w_ref[...], g, m_ref[...], v_ref[...])
                g_out[...] = g
                d_out[...] = delta
                m_out[...] = m_new
                v_out[...] = v_new

    def part_spec(k):
        return pl.BlockSpec((N_DEV, row_tile, cols),
                            lambda l, i: (0, jnp.where(l == k, i, 0), 0))

    blk = pl.BlockSpec((None, row_tile, cols), lambda l, i: (l, i, 0))
    return _pcall(
        body, name=name, grid=(n_l, rows // row_tile),
        in_specs=[part_spec(k) for k in range(n_l)] + [blk, blk, blk],
        out_specs=[blk] * 4, out_shape=[_sds(w.shape, F32)] * 4, compiler_params=_cparams(2),
    )(*parts, w, m, v)


def _adam_packed(name, parts, w, m, v):
    def body(p_ref, w_ref, m_ref, v_ref, g_out, d_out, m_out, v_out):
        g = p_ref[0]
        for s in range(1, N_DEV):
            g = g + p_ref[s]
        delta, m_new, v_new = _adam_math(w_ref[...], g, m_ref[...], v_ref[...])
        g_out[...] = g
        d_out[...] = delta
        m_out[...] = m_new
        v_out[...] = v_new

    return _pcall(
        body, name=name, grid=(1,), in_specs=[_full(parts.shape), _full(w.shape), _full(w.shape), _full(w.shape)],
        out_specs=[_full(w.shape)] * 4, out_shape=[_sds(w.shape, F32)] * 4, compiler_params=_cparams(1),
    )(parts, w, m, v)


def _pack_rows(flat_parts, lead=()):
    flat = jnp.concatenate(flat_parts, axis=-1)
    n = flat.shape[-1]
    rows = -(-n // LANES)
    rows = -(-rows // 8) * 8
    flat = jnp.pad(flat, [(0, 0)] * len(lead) + [(0, rows * LANES - n)])
    return flat.reshape(lead + (rows, LANES))


def _unpack_rows(packed, shapes, lead=()):
    flat = packed.reshape(lead + (-1,))
    out, off = [], 0
    for shp in shapes:
        size = 1
        for s in shp:
            size *= s
        out.append(flat[..., off:off + size].reshape(lead + tuple(shp)))
        off += size
    return out


_SMALL_SHARD_SHAPES = [(N_LAYERS, 128, 4), (N_LAYERS, 32, DG), (N_LAYERS, D_PLE, 128), (N_LAYERS, CONF_K, 32), (N_LAYERS, SC_K, 32)]
_REP_SHAPES = [(N_LAYERS, D)] * 6 + [(N_LAYERS, DG)] * 3 + [(N_LAYERS, N_HEADS), (N_LAYERS, 4, 64, 64)]


def _small_full_to_shards(fcol, pw, proj, dw, sc):
    return [
        fcol.reshape(N_LAYERS, N_DEV, 128, 4).transpose(1, 0, 2, 3),
        pw.reshape(N_LAYERS, N_DEV, 32, DG).transpose(1, 0, 2, 3),
        proj.reshape(N_LAYERS, D_PLE, N_DEV, 128).transpose(2, 0, 1, 3),
        dw.reshape(N_LAYERS, CONF_K, N_DEV, 32).transpose(2, 0, 1, 3),
        sc.reshape(N_LAYERS, SC_K, N_DEV, 32).transpose(2, 0, 1, 3),
    ]


def _small_shards_to_full(fcol, pw, proj, dw, sc):
    return [
        fcol.transpose(1, 0, 2, 3).reshape(N_LAYERS, D, 4),
        pw.transpose(1, 0, 2, 3).reshape(N_LAYERS, DG, DG),
        proj.transpose(1, 2, 0, 3).reshape(N_LAYERS, D_PLE, D),
        dw.transpose(1, 2, 0, 3).reshape(N_LAYERS, CONF_K, DG),
        sc.transpose(1, 2, 0, 3).reshape(N_LAYERS, SC_K, DG),
    ]


def _pad_rows(a, rows):
    return jnp.pad(a, ((0, rows - a.shape[0]), (0, 0)))


def _block_diag4(w):
    z = jnp.zeros((64, 64), w.dtype)
    return jnp.concatenate([jnp.concatenate([w[g] if k == g else z for k in range(4)], axis=1) for g in range(4)], axis=0)


def kernel(x, p, g_mix_pre, w_in, b_forget, w_conf_dw, conf_ln_g, conf_ln_b, w_conf_pw, w_sc, w_pool, pool_scale, w_out, g_mix_post, g_mlp_pre, w_up, w_down, g_mlp_post, g_ple_pre, w_ple_gate, w_ple_proj, g_ple_post, loss_target, m_g_mix_pre, m_w_in, m_b_forget, m_w_conf_dw, m_conf_ln_g, m_conf_ln_b, m_w_conf_pw, m_w_sc, m_w_pool, m_pool_scale, m_w_out, m_g_mix_post, m_g_mlp_pre, m_w_up, m_w_down, m_g_mlp_post, m_g_ple_pre, m_w_ple_gate, m_w_ple_proj, m_g_ple_post, v_g_mix_pre, v_w_in, v_b_forget, v_w_conf_dw, v_conf_ln_g, v_conf_ln_b, v_w_conf_pw, v_w_sc, v_w_pool, v_pool_scale, v_w_out, v_g_mix_post, v_g_mlp_pre, v_w_up, v_w_down, v_g_mlp_post, v_g_ple_pre, v_w_ple_gate, v_w_ple_proj, v_g_ple_post):
    n_l = N_LAYERS
    t_len = x.shape[1]
    assert t_len % TB == 0 and x.shape[0] == 1 and x.shape[2] == D

    def main_cols(a):
        return jnp.concatenate([a[..., :F_LO], a[..., F_HI:]], axis=-1)

    def fcols(a):
        return a[..., F_LO:F_HI]

    def rows_pack(down, out, gate):
        return jnp.concatenate([down, out, gate], axis=1)

    rows_b = rows_pack(w_down, w_out, w_ple_gate).astype(BF16)
    win_b = main_cols(w_in).astype(BF16)
    wup_b = w_up.astype(BF16)
    small_local = _pack_rows([a.reshape(-1) for a in (fcols(w_in), w_conf_pw, w_ple_proj, w_conf_dw, w_sc)])
    rows_g, win_g, wup_g = [None] * n_l, [None] * n_l, [None] * n_l
    rows_g[0], win_g[0], wup_g[0], small_all = _allgather("weight_allgather", [rows_b[0], win_b[0], wup_b[0], small_local])
    small_g = _unpack_rows(small_all, _SMALL_SHARD_SHAPES, lead=(N_DEV,))
    fcol_f, pw_f, proj_f, dw_f, sc_f = _small_shards_to_full(*small_g)
    wf_b = jnp.pad(fcol_f, ((0, 0), (0, 0), (0, LANES - 4))).astype(BF16)
    pw_b, proj_b = pw_f.astype(BF16), proj_f.astype(BF16)
    dw_pad = jnp.pad(dw_f, ((0, 0), (0, 32 - CONF_K), (0, 0)))
    sc_pad = jnp.pad(sc_f, ((0, 0), (0, 8 - SC_K), (0, 0)))
    pool_bd = jnp.stack([_block_diag4(w_pool[l]) for l in range(n_l)]).astype(BF16)
    b_row = jnp.pad(b_forget, ((0, 0), (0, LANES - N_HEADS)))[:, None, :]

    def vec(a, l):
        return a[l][None, :]

    p_all = p.reshape(n_l * t_len, D_PLE)
    h = x[0]
    saved = []
    for l in range(n_l):
        zc, qkv, qt, fl = _mixin_fwd(h, vec(g_mix_pre, l), win_g[l], wf_b[l])
        c, ct = _cumsum_fwd(fl, b_row[l])
        cat3 = _branch_fwd(zc, dw_pad[l], vec(conf_ln_g, l), vec(conf_ln_b, l), pw_b[l], sc_pad[l], pool_bd[l], vec(pool_scale, l))
        if l + 1 < n_l:
            o, lset, rows_g[l + 1], win_g[l + 1], wup_g[l + 1] = _attn_fwd(qkv, c, ct, [rows_b[l + 1], win_b[l + 1], wup_b[l + 1]])
        else:
            o, lset = _attn_fwd(qkv, c, ct)
        h1 = _mixout_fwd(h, cat3, o, rows_g[l], vec(g_mix_post, l))
        u, ff, h2 = _mlp_fwd(h1, vec(g_mlp_pre, l), wup_g[l], rows_g[l], vec(g_mlp_post, l))
        h3 = _ple_fwd(h2, p_all, l, vec(g_ple_pre, l), rows_g[l], proj_b[l], vec(g_ple_post, l))
        saved.append(dict(h0=h, zc=zc, qkv=qkv, qt=qt, fl=fl, c=c, ct=ct, cat3=cat3, o=o, lset=lset, h1=h1, u=u, ff=ff, h2=h2))
        h = h3

    dh, loss_part = _loss_bwd(h, loss_target[0])
    loss = lax.psum(loss_part[0, 0], ("x", "y", "c"))

    d_win = [None] * n_l
    r_down, r_out, r_gate, r_wup, r_win = ([None] * n_l for _ in range(5))
    small_grads = {k: [None] * n_l for k in ("fcol", "pw", "proj", "dw", "sc")}
    rep_grads = {k: [None] * n_l for k in ("g_mix_pre", "g_mix_post", "g_mlp_pre", "g_mlp_post", "g_ple_pre", "g_ple_post",
                                           "ln_g", "ln_b", "pool_scale", "b_forget", "w_pool")}
    for l in reversed(range(n_l)):
        s = saved[l]
        dh, dpp_b, dpre_b, hn3_b, dg_ple_post, dg_ple_pre = _ple_bwd(
            dh, s["h2"], p_all, l, vec(g_ple_post, l), vec(g_ple_pre, l), rows_g[l], proj_b[l])
        small_grads["proj"][l] = _matmul_tn("wgrad_proj", p_all, dpp_b, D_PLE, D, F32, a_section=l)
        d_gate = _matmul_tn("wgrad_gate", hn3_b, dpre_b, D, D, BF16).reshape(N_DEV, 128, D)
        dh, a2_b, du_b, dff_b, hn2_b, dg_mlp_post, dg_mlp_pre = _mlp_bwd(
            dh, s["h1"], s["u"], s["ff"], vec(g_mlp_post, l), vec(g_mlp_pre, l), wup_g[l], rows_g[l])
        d_down = _matmul_tn("wgrad_down", a2_b, dff_b, 2 * D, D, BF16).reshape(N_DEV, FF_BLK, D)
        d_wup = _matmul_tn("wgrad_up", hn2_b, du_b, D, 2 * FF_BLK, BF16, block_major=True)
        dcat, dmix_b, cat_b, dg_mix_post = _mixout_bwd(dh, s["cat3"], s["o"], vec(g_mix_post, l), rows_g[l])
        d_out = _matmul_tn("wgrad_out", cat_b, dmix_b, D, D, BF16).reshape(N_DEV, 128, D)
        dt, dob, dobt = _attn_bwd_dsum(s["qkv"], dcat, s["c"], s["ct"], s["lset"])
        riders = [d_down, d_out, d_gate, d_wup] + ([d_win[l + 1]] if l + 1 < n_l else [])
        dq, dk, dv, dc, r_down[l], r_out[l], r_gate[l], r_wup[l], *landed = _attn_bwd(
            s["qkv"], dob, s["qt"], dobt, s["c"], s["ct"], s["lset"], dt, riders)
        if landed:
            r_win[l + 1] = landed[0]
        dfl, db_f = _forget_bwd(dc, s["fl"], b_row[l])
        dz_b, ddw, dln_g, dln_b, dpw, dsc, dpool, dps = _branch_bwd(
            s["zc"], dcat, dq, dk, dv, dw_pad[l], vec(conf_ln_g, l), vec(conf_ln_b, l), pw_b[l], sc_pad[l], pool_bd[l],
            vec(pool_scale, l))
        dh, xn_b, dg_mix_pre = _mixin_bwd(dh, s["h0"], vec(g_mix_pre, l), dz_b, dfl, win_g[l], wf_b[l])
        d_win[l] = _matmul_tn("wgrad_in", xn_b, dz_b, D, W_MAIN // 2, BF16).reshape(N_DEV, 128, W_MAIN)
        small_grads["fcol"][l] = _matmul_tn("wgrad_fcol", xn_b, dfl, D, LANES, F32)[:, 0:4]
        small_grads["pw"][l], small_grads["dw"][l], small_grads["sc"][l] = dpw, ddw[0:CONF_K], dsc[0:SC_K]
        rep_grads["g_mix_pre"][l], rep_grads["g_mix_post"][l] = dg_mix_pre[0], dg_mix_post[0]
        rep_grads["g_mlp_pre"][l], rep_grads["g_mlp_post"][l] = dg_mlp_pre[0], dg_mlp_post[0]
        rep_grads["g_ple_pre"][l], rep_grads["g_ple_post"][l] = dg_ple_pre[0], dg_ple_post[0]
        rep_grads["ln_g"][l], rep_grads["ln_b"][l], rep_grads["pool_scale"][l] = dln_g[0], dln_b[0], dps[0]
        rep_grads["b_forget"][l] = db_f[0, 0:N_HEADS]
        rep_grads["w_pool"][l] = jnp.stack([dpool[64 * g:64 * g + 64, 64 * g:64 * g + 64] for g in range(4)])
    grad_x = dh[None]

    small_part = _pack_rows(
        [a.reshape(N_DEV, -1) for a in _small_full_to_shards(*[jnp.stack(small_grads[k]) for k in ("fcol", "pw", "proj", "dw", "sc")])],
        lead=(N_DEV,))
    rep_order = ("g_mix_pre", "g_mix_post", "g_mlp_pre", "g_mlp_post", "g_ple_pre", "g_ple_post", "ln_g", "ln_b",
                 "pool_scale", "b_forget", "w_pool")
    rep_part = _pack_rows([jnp.stack(rep_grads[k]).reshape(-1) for k in rep_order])
    r_win[0], r_small, r_rep = _exchange("grad_exchange", [d_win[0], small_part, rep_part], [True, True, False])

    res = {}
    res["w_down"] = _adam_rows("adam_down", r_down, w_down, m_w_down, v_w_down, 128)
    res["w_out"] = _adam_rows("adam_out", r_out, w_out, m_w_out, v_w_out, 128)
    res["w_ple_gate"] = _adam_rows("adam_gate", r_gate, w_ple_gate, m_w_ple_gate, v_w_ple_gate, 128)
    res["w_up"] = _adam_rows("adam_up", r_wup, w_up, m_w_up, v_w_up, 256)
    win_main = _adam_rows("adam_in", r_win, main_cols(w_in), main_cols(m_w_in), main_cols(v_w_in), 128)

    small_w = [(fcols(w_in), w_conf_pw, w_ple_proj, w_conf_dw, w_sc), (fcols(m_w_in), m_w_conf_pw, m_w_ple_proj, m_w_conf_dw, m_w_sc),
               (fcols(v_w_in), v_w_conf_pw, v_w_ple_proj, v_w_conf_dw, v_w_sc)]
    small_packed = [_pack_rows([a.reshape(-1) for a in grp]) for grp in small_w]
    small_res = [_unpack_rows(a, _SMALL_SHARD_SHAPES) for a in _adam_packed("adam_small", r_small, *small_packed)]
    rep_w = [(g_mix_pre, g_mix_post, g_mlp_pre, g_mlp_post, g_ple_pre, g_ple_post, conf_ln_g, conf_ln_b, pool_scale, b_forget, w_pool),
             (m_g_mix_pre, m_g_mix_post, m_g_mlp_pre, m_g_mlp_post, m_g_ple_pre, m_g_ple_post, m_conf_ln_g, m_conf_ln_b, m_pool_scale,
              m_b_forget, m_w_pool),
             (v_g_mix_pre, v_g_mix_post, v_g_mlp_pre, v_g_mlp_post, v_g_ple_pre, v_g_ple_post, v_conf_ln_g, v_conf_ln_b, v_pool_scale,
              v_b_forget, v_w_pool)]
    rep_packed = [_pack_rows([a.reshape(-1) for a in grp]) for grp in rep_w]
    rep_res = [_unpack_rows(a, _REP_SHAPES) for a in _adam_packed("adam_replicated", r_rep, *rep_packed)]

    for kind in range(4):
        fc, pw, proj, dwc, scc = small_res[kind]
        main = win_main[kind]
        (rg_mix_pre, rg_mix_post, rg_mlp_pre, rg_mlp_post, rg_ple_pre, rg_ple_post, r_ln_g, r_ln_b, r_ps, r_bf, r_wpool) = rep_res[kind]
        res.setdefault("by_kind", []).append(dict(
            g_mix_pre=rg_mix_pre, w_in=jnp.concatenate([main[..., :F_LO], fc, main[..., F_LO:]], axis=-1), b_forget=r_bf,
            w_conf_dw=dwc, conf_ln_g=r_ln_g, conf_ln_b=r_ln_b, w_conf_pw=pw, w_sc=scc, w_pool=r_wpool, pool_scale=r_ps,
            w_out=res["w_out"][kind], g_mix_post=rg_mix_post, g_mlp_pre=rg_mlp_pre, w_up=res["w_up"][kind],
            w_down=res["w_down"][kind], g_mlp_post=rg_mlp_post, g_ple_pre=rg_ple_pre, w_ple_gate=res["w_ple_gate"][kind],
            w_ple_proj=proj, g_ple_post=rg_ple_post))
    names = ("g_mix_pre", "w_in", "b_forget", "w_conf_dw", "conf_ln_g", "conf_ln_b", "w_conf_pw", "w_sc", "w_pool", "pool_scale",
             "w_out", "g_mix_post", "g_mlp_pre", "w_up", "w_down", "g_mlp_post", "g_ple_pre", "w_ple_gate", "w_ple_proj", "g_ple_post")
    outs = [loss, grad_x]
    for kind in range(4):
        outs += [res["by_kind"][kind][nm] for nm in names]
    return tuple(outs)
```

```python
import jax
import jax.numpy as jnp
from jax import lax
from jax.experimental import pallas as pl
from jax.experimental.pallas import tpu as pltpu

F32, BF16 = jnp.float32, jnp.bfloat16

D = 1024
DG = 256
N_HEADS = 4
HEAD_DIM = 64
CONF_K = 31
SC_K = 3
POOL_WINDOWS = (2, 4, 8, 16)
D_FF = 4096
D_PLE = 256
N_LAYERS = 4
N_DEV = 8
EPS = 1e-6
SCALE = HEAD_DIM ** -0.5
W_MAIN = 2304
F_LO, F_HI = 1280, 1284

ADAM_LR, ADAM_B1, ADAM_B2, ADAM_EPS, ADAM_WD, ADAM_STEP = 0.001, 0.9, 0.999, 1e-08, 0.01, 10

TB = 512
HALO = 32
LANES = 128
FF_BLK = D_FF // N_DEV
MLP_BWD_BLOCKS = 2
VMEM_LIMIT = 56 * 1024 * 1024

NT_DIMS = (((1,), (1,)), ((), ()))
TN_DIMS = (((0,), (0,)), ((), ()))
MESH_ID = pl.DeviceIdType.MESH


def _pcall(body, **kw):
    return pl.pallas_call(body, **kw)


def _cparams(n_axes):
    return pltpu.CompilerParams(dimension_semantics=("arbitrary",) * n_axes, vmem_limit_bytes=VMEM_LIMIT)


def _sds(shape, dtype):
    return jax.ShapeDtypeStruct(shape, dtype)


def _tok(width, tb=TB):
    return pl.BlockSpec((tb, width), lambda i: (i, 0))


def _tokcol(width, col):
    return pl.BlockSpec((TB, width), lambda i: (i, col))


def _full(shape):
    zeros = (0,) * len(shape)
    return pl.BlockSpec(shape, lambda *_: zeros)


def _resident(shape):
    zeros = (0,) * len(shape)
    return pl.BlockSpec(shape, lambda *_: zeros, pipeline_mode=pl.Buffered(1))


def _halo_prev(width, col=0):
    return pl.BlockSpec((HALO, width), lambda i: (jnp.maximum(i * (TB // HALO) - 1, 0), col))


def _halo_next(width, n_rows, col=0):
    last = n_rows // HALO - 1
    return pl.BlockSpec((HALO, width), lambda i: (jnp.minimum((i + 1) * (TB // HALO), last), col))


def _dot(a, b):
    return jnp.dot(a, b, preferred_element_type=F32)


def _dot_nt(a, b):
    return lax.dot_general(a, b, NT_DIMS, preferred_element_type=F32)


def _dot_tn(a, b):
    return lax.dot_general(a, b, TN_DIMS, preferred_element_type=F32)


def _dot_exact(a, b):
    return jnp.dot(a, b, precision=lax.Precision.HIGHEST, preferred_element_type=F32)


def _rms(x, g):
    r = lax.rsqrt(jnp.mean(x * x, axis=-1, keepdims=True) + EPS)
    return x * r * g


def _rms_bwd(x, g, dy):
    r = lax.rsqrt(jnp.mean(x * x, axis=-1, keepdims=True) + EPS)
    n = x * r
    dg = jnp.sum(dy * n, axis=0, keepdims=True)
    dn = dy * g
    dx = r * (dn - n * jnp.mean(dn * n, axis=-1, keepdims=True))
    return dx, dg, n * g


def _sigmoid(x):
    return jax.nn.sigmoid(x)


def _log_sigmoid(x):
    return jnp.minimum(x, 0.0) - jnp.log(1.0 + jnp.exp(-jnp.abs(x)))


def _lane_group_select(lane, v2, v4, v8, v16):
    return jnp.where(lane < 64, v2, jnp.where(lane < 128, v4, jnp.where(lane < 192, v8, v16)))


def _pool_counts(t0, rows):
    lane = lax.broadcasted_iota(jnp.int32, (rows, DG), 1)
    t = lax.broadcasted_iota(jnp.int32, (rows, DG), 0) + t0
    win = _lane_group_select(lane, 2, 4, 8, 16)
    return jnp.minimum(t + 1, win).astype(F32), lane


def _transpose_lanes8(x):
    eye = (lax.broadcasted_iota(jnp.int32, (8, LANES), 0) == lax.broadcasted_iota(jnp.int32, (8, LANES), 1)).astype(F32)
    return lax.dot_general(eye, x, NT_DIMS, precision=lax.Precision.HIGHEST, preferred_element_type=F32)


def _mixin_fwd(h, g, win, wf, b_row):
    t_len = h.shape[0]

    def body(h_ref, g_ref, win_ref, wf_ref, b_ref, zc_ref, qkv_ref, fl_ref, c_ref, ct_ref, carry):
        @pl.when(pl.program_id(0) == 0)
        def _():
            carry[...] = jnp.zeros_like(carry)

        xn = _rms(h_ref[...], g_ref[...]).astype(BF16)
        z = _dot(xn, win_ref[...].reshape(D, W_MAIN))
        zc_ref[:, 0:512] = z[:, 0:512]
        zc_ref[:, 512:1536] = z[:, 1280:2304]
        qkv_ref[:, 0:256] = (z[:, 512:768] * SCALE).astype(BF16)
        qkv_ref[:, 256:768] = z[:, 768:1280].astype(BF16)
        fl = _dot(xn, wf_ref[...])
        fl_ref[...] = fl
        r = lax.broadcasted_iota(jnp.int32, (TB, TB), 0)
        s = lax.broadcasted_iota(jnp.int32, (TB, TB), 1)
        lf = _log_sigmoid(fl + b_ref[...])
        c = _dot_exact((r >= s).astype(F32), lf) + carry[0:1, :]
        c_ref[...] = c
        ct_ref[...] = _transpose_lanes8(c)
        carry[...] += jnp.sum(lf, axis=0, keepdims=True)

    return _pcall(
        body, name="mixin_fwd", grid=(t_len // TB,),
        in_specs=[_tok(D), _full((1, D)), _full((N_DEV, D // N_DEV, W_MAIN)), _full((D, LANES)), _full((1, LANES))],
        out_specs=[_tok(1536), _tok(768), _tok(LANES), _tok(LANES), pl.BlockSpec((8, TB), lambda i: (0, i))],
        out_shape=[_sds((t_len, 1536), F32), _sds((t_len, 768), BF16), _sds((t_len, LANES), F32),
                   _sds((t_len, LANES), F32), _sds((8, t_len), F32)],
        scratch_shapes=[pltpu.VMEM((8, LANES), F32)],
        compiler_params=_cparams(1),
    )(h, g, win, wf, b_row)


def _layer_norm_parts(y, g, b):
    mu = jnp.mean(y, axis=-1, keepdims=True)
    yc = y - mu
    r = lax.rsqrt(jnp.mean(yc * yc, axis=-1, keepdims=True) + EPS)
    n = yc * r
    return n, r, n * g + b


def _phase_copies(src, dst, rows):
    for p in range(1, 8):
        dst[p - 1, 0:rows - 8, :] = src[pl.ds(p, rows - 8), :]


def _tap(src, copies, off, n):
    p = off % 8
    return src[pl.ds(off, n), :] if p == 0 else copies[p - 1, pl.ds(off - p, n), :]


def _pool_window_sums(p0, p1, p2, p3):
    e = HALO + TB
    p1[8:e, :] = p0[pl.ds(8, e - 8), :] + p0[pl.ds(7, e - 8), :]
    p2[16:e, :] = p1[pl.ds(16, e - 16), :] + p1[pl.ds(14, e - 16), :]
    p3[24:e, :] = p2[pl.ds(24, e - 24), :] + p2[pl.ds(20, e - 24), :]
    s16 = p3[pl.ds(HALO, TB), :] + p3[pl.ds(HALO - 8, TB), :]
    return p1[pl.ds(HALO, TB), :], p2[pl.ds(HALO, TB), :], p3[pl.ds(HALO, TB), :], s16


def _branch_fwd(zc, w_dw, ln_g, ln_b, w_pw, w_sc, w_pool, pool_scale):
    t_len = zc.shape[0]
    e = HALO + TB

    def body(z_ref, zh_ref, dw_ref, g_ref, b_ref, pw_ref, sc_ref, pool_ref, ps_ref, cat_ref, u_s, ch_s, p0, p1, p2, p3, u_ph):
        i = pl.program_id(0)
        hm = (i > 0).astype(F32)
        u_s[0:HALO, :] = zh_ref[:, 0:256] * _sigmoid(zh_ref[:, 256:512]) * hm
        u_s[HALO:e, :] = z_ref[:, 0:256] * _sigmoid(z_ref[:, 256:512])
        _phase_copies(u_s, u_ph, e)
        y = jnp.zeros((TB, DG), F32)
        for k in range(CONF_K):
            y = y + dw_ref[k:k + 1, :] * _tap(u_s, u_ph, HALO - (CONF_K - 1) + k, TB)
        _, _, yn = _layer_norm_parts(y, g_ref[...], b_ref[...])
        s = yn * _sigmoid(yn)
        cat_ref[:, 0:256] = _dot(s.astype(BF16), pw_ref[...])
        ch_s[0:HALO, :] = zh_ref[:, 1024:1280] * zh_ref[:, 512:768] * hm
        ch_s[HALO:e, :] = z_ref[:, 1024:1280] * z_ref[:, 512:768]
        cv = jnp.zeros((TB, DG), F32)
        for k in range(SC_K):
            cv = cv + sc_ref[k:k + 1, :] * ch_s[pl.ds(HALO - (SC_K - 1) + k, TB), :]
        cat_ref[:, 256:512] = z_ref[:, 768:1024] * cv
        p0[0:HALO, :] = zh_ref[:, 1280:1536] * hm
        p0[HALO:e, :] = z_ref[:, 1280:1536]
        s2, s4, s8, s16 = _pool_window_sums(p0, p1, p2, p3)
        cnt, lane = _pool_counts(i * TB, TB)
        dlt = _lane_group_select(lane, s2, s4, s8, s16) / cnt - z_ref[:, 1280:1536]
        cat_ref[:, 512:768] = _dot(dlt.astype(BF16), pool_ref[...]) * ps_ref[...]

    scr = [pltpu.VMEM((e, DG), F32) for _ in range(6)] + [pltpu.VMEM((7, e, DG), F32)]
    return _pcall(
        body, name="branch_fwd", grid=(t_len // TB,),
        in_specs=[_tok(1536), _halo_prev(1536), _full((32, DG)), _full((1, DG)), _full((1, DG)), _full((DG, DG)),
                  _full((8, DG)), _full((DG, DG)), _full((1, DG))],
        out_specs=_tok(768), out_shape=_sds((t_len, 768), F32), scratch_shapes=scr, compiler_params=_cparams(1),
    )(zc, zc, w_dw, ln_g, ln_b, w_pw, w_sc, w_pool, pool_scale)


def _head_masks(rows):
    lane = lax.broadcasted_iota(jnp.int32, (rows, LANES), 1)
    return lane, (lane < HEAD_DIM, lane >= HEAD_DIM)


def _keep_lanes(mask, x):
    return jnp.where(mask, x.astype(F32), 0.0).astype(BF16)


def _with_exchange(refs, n_in, n_out, n_x, per_dest, first, last):
    ins, x_src = refs[:n_in], refs[n_in:n_in + n_x]
    outs, x_dst = refs[n_in + n_x:n_in + n_x + n_out], refs[n_in + n_x + n_out:n_in + 2 * n_x + n_out]
    begin = finish = None
    if n_x:
        start, wait = _exchange_ops(x_src, x_dst, [per_dest] * n_x, *refs[n_in + 2 * n_x + n_out:])

        def begin():
            pl.when(first)(start)

        def finish():
            pl.when(last)(wait)

    return ins, outs, begin, finish


def _attn_fwd(qkv, c, ct, bcast=()):
    t_len = qkv.shape[0]
    n_t = t_len // TB
    n_x = len(bcast)

    def body(*refs):
        i = pl.program_id(0)
        (q_ref, k_ref, v_ref, c_ref, ct_ref), (o_ref, lset_ref), begin, finish = _with_exchange(
            refs, 5, 2, n_x, False, i == 0, i == n_t - 1)
        if begin:
            begin()
        lane, halves = _head_masks(TB)
        crow = c_ref[...]
        causal = lax.broadcasted_iota(jnp.int32, (TB, TB), 0) >= lax.broadcasted_iota(jnp.int32, (TB, TB), 1)
        lse_out = jnp.zeros((TB, LANES), F32)
        for g in range(2):
            cols = slice(g * LANES, (g + 1) * LANES)
            qg = q_ref[:, cols]
            qms = [_keep_lanes(halves[hh], qg) for hh in range(2)]
            cqs = [jnp.sum(jnp.where(lane == 2 * g + hh, crow, 0.0), axis=1, keepdims=True) for hh in range(2)]

            def block(j, carry, masked):
                off = pl.multiple_of(j * TB, TB)
                kj = k_ref[pl.ds(off, TB), cols]
                vj = v_ref[pl.ds(off, TB), cols]
                new = []
                for hh in range(2):
                    m, l, acc = carry[hh]
                    s = _dot_nt(qms[hh], kj) + (cqs[hh] - ct_ref[2 * g + hh:2 * g + hh + 1, pl.ds(off, TB)])
                    if masked:
                        s = jnp.where(causal, s, -jnp.inf)
                    m_new = jnp.maximum(m, jnp.max(s, axis=1, keepdims=True))
                    alpha = jnp.exp(m - m_new)
                    p = jnp.exp(s - m_new)
                    l = alpha * l + jnp.sum(p, axis=1, keepdims=True)
                    acc = alpha * acc + _dot(p.astype(BF16), vj)
                    new.append((m_new, l, acc))
                return tuple(new)

            init = tuple((jnp.full((TB, 1), -jnp.inf, F32), jnp.zeros((TB, 1), F32), jnp.zeros((TB, LANES), F32))
                         for _ in range(2))
            carry = lax.fori_loop(0, i, lambda j, cr: block(j, cr, False), init)
            (m0, l0, acc0), (m1, l1, acc1) = block(i, carry, True)
            o_ref[:, cols] = jnp.where(halves[0], acc0 / l0, acc1 / l1)
            lse_out = jnp.where(lane == 2 * g, m0 + jnp.log(l0), lse_out)
            lse_out = jnp.where(lane == 2 * g + 1, m1 + jnp.log(l1), lse_out)
        lset_ref[...] = _transpose_lanes8(lse_out)
        if finish:
            finish()

    return _pcall(
        body, name="attn_fwd_gather" if n_x else "attn_fwd", grid=(n_t,),
        in_specs=[_tokcol(DG, 0), pl.BlockSpec((t_len, DG), lambda i: (0, 1)), pl.BlockSpec((t_len, DG), lambda i: (0, 2)),
                  _tok(LANES), _full((8, t_len))] + [_HBM] * n_x,
        out_specs=[_tok(DG), pl.BlockSpec((8, TB), lambda i: (0, i))] + [_HBM] * n_x,
        out_shape=[_sds((t_len, DG), F32), _sds((8, t_len), F32)] + _exchange_shapes(bcast, [False] * n_x),
        scratch_shapes=_exchange_scratch(n_x) if n_x else [],
        compiler_params=_cparams(1),
    )(qkv, qkv, qkv, c, ct, *bcast)


def _mix_projection(cat_ref, o_ref, w):
    return (_dot(cat_ref[:, 0:256].astype(BF16), w[0:256]) + _dot(o_ref[...].astype(BF16), w[256:512])
            + _dot(cat_ref[:, 256:768].astype(BF16), w[512:1024]))


def _mixout_fwd(h, cat3, o, w_rows, g):
    t_len = h.shape[0]

    def body(h_ref, cat_ref, o_ref, w_ref, g_ref, h1_ref):
        mix = _mix_projection(cat_ref, o_ref, w_ref[...].reshape(D, D))
        h1_ref[...] = h_ref[...] + _rms(mix, g_ref[...])

    return _pcall(
        body, name="mixout_fwd", grid=(t_len // TB,),
        in_specs=[_tok(D), _tok(768), _tok(DG), pl.BlockSpec((N_DEV, 128, D), lambda i: (0, 4, 0)), _full((1, D))],
        out_specs=_tok(D), out_shape=_sds((t_len, D), F32), compiler_params=_cparams(1),
    )(h, cat3, o, w_rows, g)


def _mlp_fwd(h, g_pre, w_up, w_rows, g_post):
    t_len = h.shape[0]

    def body(h_ref, g1_ref, up_ref, dn_ref, g2_ref, u_ref, ff_ref, h2_ref):
        hn = _rms(h_ref[...], g1_ref[...]).astype(BF16)
        ff = jnp.zeros((TB, D), F32)
        for j in range(N_DEV):
            u = _dot(hn, up_ref[j])
            u_ref[:, j * FF_BLK:(j + 1) * FF_BLK] = u.astype(BF16)
            r = jnp.maximum(u, 0.0)
            ff = ff + _dot((r * r).astype(BF16), dn_ref[j])
        ff_ref[...] = ff
        h2_ref[...] = h_ref[...] + _rms(ff, g2_ref[...])

    return _pcall(
        body, name="mlp_fwd", grid=(t_len // TB,),
        in_specs=[_tok(D), _full((1, D)), _resident((N_DEV, D, FF_BLK)), _resident((N_DEV, FF_BLK, D)), _full((1, D))],
        out_specs=[_tok(D_FF), _tok(D), _tok(D)],
        out_shape=[_sds((t_len, D_FF), BF16), _sds((t_len, D), F32), _sds((t_len, D), F32)],
        compiler_params=_cparams(1),
    )(h, g_pre, w_up, w_rows, g_post)


def _ple_fwd(h, p_all, layer, g_pre, w_rows, w_proj, g_post):
    t_len = h.shape[0]
    n_t = t_len // TB

    def body(h_ref, p_ref, g1_ref, wg_ref, wp_ref, g2_ref, h3_ref):
        pp, gate = _ple_parts(h_ref, p_ref, g1_ref, wg_ref, wp_ref)
        h3_ref[...] = h_ref[...] + _rms(pp * gate, g2_ref[...])

    return _pcall(
        body, name="ple_fwd", grid=(n_t,),
        in_specs=[_tok(D), pl.BlockSpec((TB, D_PLE), lambda i: (layer * n_t + i, 0)), _full((1, D)),
                  pl.BlockSpec((N_DEV, 128, D), lambda i: (0, 5, 0)), _full((D_PLE, D)), _full((1, D))],
        out_specs=_tok(D), out_shape=_sds((t_len, D), F32), compiler_params=_cparams(1),
    )(h, p_all, g_pre, w_rows, w_proj, g_post)


def _ple_parts(h_ref, p_ref, g1_ref, wg_ref, wp_ref):
    hn = _rms(h_ref[...], g1_ref[...]).astype(BF16)
    gate = _sigmoid(_dot(hn, wg_ref[...].reshape(D, D)))
    return _dot(p_ref[...].astype(BF16), wp_ref[...]), gate


def _loss_bwd(h, target):
    t_len = h.shape[0]

    def body(h_ref, t_ref, dh_ref, loss_ref):
        @pl.when(pl.program_id(0) == 0)
        def _():
            loss_ref[...] = jnp.zeros_like(loss_ref)

        d = h_ref[...] - t_ref[...]
        dh_ref[...] = d * (1.0 / D)
        loss_ref[...] += 0.5 * jnp.sum(jnp.mean(d * d, axis=-1, keepdims=True), axis=0, keepdims=True)

    return _pcall(
        body, name="loss_bwd", grid=(t_len // TB,), in_specs=[_tok(D), _tok(D)],
        out_specs=[_tok(D), _full((8, LANES))], out_shape=[_sds((t_len, D), F32), _sds((8, LANES), F32)],
        compiler_params=_cparams(1),
    )(h, target)


def _acc_init(refs):
    @pl.when(pl.program_id(0) == 0)
    def _():
        for r in refs:
            r[...] = jnp.zeros_like(r)


def _ple_bwd(dh3, h2, p_all, layer, g_post, g_pre, w_rows, w_proj):
    t_len = dh3.shape[0]
    n_t = t_len // TB

    def body(dh_ref, h_ref, p_ref, g2_ref, g1_ref, wg_ref, wp_ref, dh2_ref, dpp_ref, dpre_ref, hn_ref, dg2_ref, dg1_ref):
        _acc_init([dg2_ref, dg1_ref])
        dh = dh_ref[...]
        pp, gate = _ple_parts(h_ref, p_ref, g1_ref, wg_ref, wp_ref)
        de, dg2, _ = _rms_bwd(pp * gate, g2_ref[...], dh)
        dg2_ref[...] += dg2
        dpp_ref[...] = (de * gate).astype(BF16)
        dpre = (de * pp * gate * (1.0 - gate)).astype(BF16)
        dpre_ref[...] = dpre
        dhn = _dot_nt(dpre, wg_ref[...].reshape(D, D))
        dx, dg1, hn = _rms_bwd(h_ref[...], g1_ref[...], dhn)
        dg1_ref[...] += dg1
        hn_ref[...] = hn.astype(BF16)
        dh2_ref[...] = dh + dx

    return _pcall(
        body, name="ple_bwd", grid=(n_t,),
        in_specs=[_tok(D), _tok(D), pl.BlockSpec((TB, D_PLE), lambda i: (layer * n_t + i, 0)), _full((1, D)), _full((1, D)),
                  pl.BlockSpec((N_DEV, 128, D), lambda i: (0, 5, 0)), _full((D_PLE, D))],
        out_specs=[_tok(D)] * 4 + [_full((1, D))] * 2,
        out_shape=[_sds((t_len, D), F32)] + [_sds((t_len, D), BF16)] * 3 + [_sds((1, D), F32)] * 2,
        compiler_params=_cparams(1),
    )(dh3, h2, p_all, g_post, g_pre, w_rows, w_proj)


def _mlp_bwd(dh2, h1, u, ff, g_post, g_pre, w_up, w_rows):
    t_len = dh2.shape[0]

    def body(dh_ref, h_ref, u_ref, ff_ref, g2_ref, g1_ref, up_ref, dn_ref,
             dh1_ref, a2_ref, du_ref, dff_ref, hn_ref, dg2_ref, dg1_ref, dff_s, acc_s):
        i, j = pl.program_id(0), pl.program_id(1)

        @pl.when((i == 0) & (j == 0))
        def _():
            dg2_ref[...] = jnp.zeros_like(dg2_ref)
            dg1_ref[...] = jnp.zeros_like(dg1_ref)

        @pl.when(j == 0)
        def _():
            dff, dg2, _ = _rms_bwd(ff_ref[...], g2_ref[...], dh_ref[...])
            dg2_ref[...] += dg2
            dff_s[...] = dff.astype(BF16)
            dff_ref[...] = dff.astype(BF16)
            acc_s[...] = jnp.zeros_like(acc_s)

        acc = acc_s[...]
        for b in range(MLP_BWD_BLOCKS):
            cols = slice(b * FF_BLK, (b + 1) * FF_BLK)
            r = jnp.maximum(u_ref[:, cols].astype(F32), 0.0)
            a2_ref[:, cols] = (r * r).astype(BF16)
            du = (_dot_nt(dff_s[...], dn_ref[MLP_BWD_BLOCKS * j + b]) * (2.0 * r)).astype(BF16)
            du_ref[:, cols] = du
            acc = acc + _dot_nt(du, up_ref[MLP_BWD_BLOCKS * j + b])
        acc_s[...] = acc

        @pl.when(j == N_DEV // MLP_BWD_BLOCKS - 1)
        def _():
            dx, dg1, hn = _rms_bwd(h_ref[...], g1_ref[...], acc_s[...])
            dg1_ref[...] += dg1
            hn_ref[...] = hn.astype(BF16)
            dh1_ref[...] = dh_ref[...] + dx

    tok2 = pl.BlockSpec((TB, D), lambda i, j: (i, 0))
    vec2 = pl.BlockSpec((1, D), lambda i, j: (0, 0))
    blk2 = pl.BlockSpec((TB, MLP_BWD_BLOCKS * FF_BLK), lambda i, j: (i, j))
    return _pcall(
        body, name="mlp_bwd", grid=(t_len // TB, N_DEV // MLP_BWD_BLOCKS),
        in_specs=[tok2, tok2, blk2, tok2, vec2, vec2, _resident((N_DEV, D, FF_BLK)), _resident((N_DEV, FF_BLK, D))],
        out_specs=[tok2, blk2, blk2, tok2, tok2, vec2, vec2],
        out_shape=[_sds((t_len, D), F32), _sds((t_len, D_FF), BF16), _sds((t_len, D_FF), BF16), _sds((t_len, D), BF16),
                   _sds((t_len, D), BF16), _sds((1, D), F32), _sds((1, D), F32)],
        scratch_shapes=[pltpu.VMEM((TB, D), BF16), pltpu.VMEM((TB, D), F32)], compiler_params=_cparams(2),
    )(dh2, h1, u, ff, g_post, g_pre, w_up, w_rows)


def _mixout_bwd(dh1, cat3, o, g, w_rows):
    t_len = dh1.shape[0]

    def body(dh_ref, cat_ref, o_ref, g_ref, w_ref, dcat_ref, dmix_ref, catb_ref, dg_ref):
        _acc_init([dg_ref])
        w = w_ref[...].reshape(D, D)
        dmix, dg, _ = _rms_bwd(_mix_projection(cat_ref, o_ref, w), g_ref[...], dh_ref[...])
        dg_ref[...] += dg
        dmix = dmix.astype(BF16)
        dmix_ref[...] = dmix
        dcat_ref[...] = _dot_nt(dmix, w)
        catb_ref[:, 0:256] = cat_ref[:, 0:256].astype(BF16)
        catb_ref[:, 256:512] = o_ref[...].astype(BF16)
        catb_ref[:, 512:1024] = cat_ref[:, 256:768].astype(BF16)

    return _pcall(
        body, name="mixout_bwd", grid=(t_len // TB,),
        in_specs=[_tok(D), _tok(768), _tok(DG), _full((1, D)), pl.BlockSpec((N_DEV, 128, D), lambda i: (0, 4, 0))],
        out_specs=[_tok(D), _tok(D), _tok(D), _full((1, D))],
        out_shape=[_sds((t_len, D), F32), _sds((t_len, D), BF16), _sds((t_len, D), BF16), _sds((1, D), F32)],
        compiler_params=_cparams(1),
    )(dh1, cat3, o, g, w_rows)


def _attn_bwd_dsum(qkv, dcat, c, ct, lset):
    t_len = qkv.shape[0]

    def body(q_ref, do_ref, k_ref, v_ref, c_ref, ct_ref, lset_ref, dt_ref, dob_ref):
        i = pl.program_id(0)
        lane, halves = _head_masks(TB)
        causal_t = lax.broadcasted_iota(jnp.int32, (TB, TB), 1) >= lax.broadcasted_iota(jnp.int32, (TB, TB), 0)
        sub = lax.broadcasted_iota(jnp.int32, (8, TB), 0)
        dob_ref[...] = do_ref[...].astype(BF16)
        out = jnp.zeros((8, TB), F32)
        for g in range(2):
            cols = slice(g * LANES, (g + 1) * LANES)
            qi = q_ref[:, cols]
            doi = do_ref[:, cols].astype(BF16)
            doms = [_keep_lanes(halves[hh], doi) for hh in range(2)]
            cqs = [ct_ref[2 * g + hh:2 * g + hh + 1, :] for hh in range(2)]
            lses = [lset_ref[2 * g + hh:2 * g + hh + 1, :] for hh in range(2)]

            def block(j, accs, masked):
                off = pl.multiple_of(j * TB, TB)
                kj = k_ref[pl.ds(off, TB), cols]
                vj = v_ref[pl.ds(off, TB), cols]
                cj = c_ref[pl.ds(off, TB), :]
                new = []
                for hh in range(2):
                    ck = jnp.sum(jnp.where(lane == 2 * g + hh, cj, 0.0), axis=1, keepdims=True)
                    st = _dot_nt(_keep_lanes(halves[hh], kj), qi) + (cqs[hh] - ck)
                    if masked:
                        st = jnp.where(causal_t, st, -jnp.inf)
                    pt = jnp.exp(st - lses[hh])
                    new.append(accs[hh] + jnp.sum(pt * _dot_nt(vj, doms[hh]), axis=0, keepdims=True))
                return tuple(new)

            init = (jnp.zeros((1, TB), F32), jnp.zeros((1, TB), F32))
            accs = block(i, lax.fori_loop(0, i, lambda j, cr: block(j, cr, False), init), True)
            out = jnp.where(sub == 2 * g, accs[0], out)
            out = jnp.where(sub == 2 * g + 1, accs[1], out)
        dt_ref[...] = out

    row8 = pl.BlockSpec((8, TB), lambda i: (0, i))
    return _pcall(
        body, name="attn_bwd_dsum", grid=(t_len // TB,),
        in_specs=[_tokcol(DG, 0), _tokcol(DG, 1), pl.BlockSpec((t_len, DG), lambda i: (0, 1)),
                  pl.BlockSpec((t_len, DG), lambda i: (0, 2)), _full((t_len, LANES)), row8, row8],
        out_specs=[row8, _tok(DG)], out_shape=[_sds((8, t_len), F32), _sds((t_len, DG), BF16)],
        compiler_params=_cparams(1),
    )(qkv, dcat, qkv, qkv, c, ct, lset)


def _attn_bwd(qkv, dob, c, ct, lset, dt, xchg=()):
    t_len = qkv.shape[0]
    n_q = t_len // TB
    n_x = len(xchg)

    def body(*refs):
        j = pl.program_id(0)
        ins, outs, begin, finish = _with_exchange(refs, 8, 4, n_x, True, j == 0, j == n_q - 1)
        q_ref, dob_ref, k_ref, v_ref, c_ref, ct_ref, lset_ref, dt_ref = ins
        dq_ref, dk_ref, dv_ref, dc_ref = outs
        if begin:
            begin()

        @pl.when(j == 0)
        def _():
            dq_ref[...] = jnp.zeros_like(dq_ref)

        lane, halves = _head_masks(TB)
        crow = c_ref[...]
        causal_t = lax.broadcasted_iota(jnp.int32, (TB, TB), 1) >= lax.broadcasted_iota(jnp.int32, (TB, TB), 0)
        dc_out = jnp.zeros((TB, LANES), F32)
        for g in range(2):
            cols = slice(g * LANES, (g + 1) * LANES)
            kg, vg = k_ref[:, cols], v_ref[:, cols]
            kms = [_keep_lanes(halves[hh], kg) for hh in range(2)]
            cks = [jnp.sum(jnp.where(lane == 2 * g + hh, crow, 0.0), axis=1, keepdims=True) for hh in range(2)]

            def block(i, carry, masked):
                dk, dv, dcs = carry
                off = pl.multiple_of(i * TB, TB)
                qi = q_ref[pl.ds(off, TB), cols]
                doi = dob_ref[pl.ds(off, TB), cols]
                dq_add = jnp.zeros((TB, LANES), F32)
                dcs_new = []
                for hh in range(2):
                    h = 2 * g + hh
                    dom = _keep_lanes(halves[hh], doi)
                    st = _dot_nt(kms[hh], qi) + (ct_ref[h:h + 1, pl.ds(off, TB)] - cks[hh])
                    if masked:
                        st = jnp.where(causal_t, st, -jnp.inf)
                    pt = jnp.exp(st - lset_ref[h:h + 1, pl.ds(off, TB)])
                    dv = dv + _dot(pt.astype(BF16), dom)
                    dst = pt * (_dot_nt(vg, dom) - dt_ref[h:h + 1, pl.ds(off, TB)])
                    dsb = dst.astype(BF16)
                    dk = dk + _dot(dsb, _keep_lanes(halves[hh], qi))
                    dcs_new.append(dcs[hh] + jnp.sum(dst, axis=1, keepdims=True))
                    dq_add = dq_add + _dot_tn(dsb, kms[hh])
                dq_ref[pl.ds(off, TB), cols] += dq_add
                return dk, dv, tuple(dcs_new)

            init = (jnp.zeros((TB, LANES), F32), jnp.zeros((TB, LANES), F32),
                    (jnp.zeros((TB, 1), F32), jnp.zeros((TB, 1), F32)))
            carry = block(j, init, True)
            dk, dv, dcs = lax.fori_loop(j + 1, n_q, lambda i, cr: block(i, cr, False), carry)
            dk_ref[:, cols] = dk
            dv_ref[:, cols] = dv
            dc_out = jnp.where(lane == 2 * g, -dcs[0], dc_out)
            dc_out = jnp.where(lane == 2 * g + 1, -dcs[1], dc_out)
        dc_ref[...] = dc_out
        if finish:
            finish()

    return _pcall(
        body, name="attn_bwd_exchange" if n_x else "attn_bwd", grid=(n_q,),
        in_specs=[pl.BlockSpec((t_len, DG), lambda i: (0, 0)), _full((t_len, DG)), _tokcol(DG, 1), _tokcol(DG, 2),
                  _tok(LANES), _full((8, t_len)), _full((8, t_len)), _full((8, t_len))] + [_HBM] * n_x,
        out_specs=[_full((t_len, DG)), _tok(DG), _tok(DG), _tok(LANES)] + [_HBM] * n_x,
        out_shape=[_sds((t_len, DG), F32), _sds((t_len, DG), F32), _sds((t_len, DG), F32), _sds((t_len, LANES), F32)]
        + _exchange_shapes(xchg, [True] * n_x),
        scratch_shapes=_exchange_scratch(n_x) if n_x else [],
        compiler_params=_cparams(1),
    )(qkv, dob, qkv, qkv, c, ct, lset, dt, *xchg)


def _forget_bwd(dc, fl, b_row):
    t_len = dc.shape[0]
    n_t = t_len // TB
    rev = pl.BlockSpec((TB, LANES), lambda i: (n_t - 1 - i, 0))

    def body(dc_ref, fl_ref, b_ref, dfl_ref, db_ref, carry):
        @pl.when(pl.program_id(0) == 0)
        def _():
            carry[...] = jnp.zeros_like(carry)
            db_ref[...] = jnp.zeros_like(db_ref)

        r = lax.broadcasted_iota(jnp.int32, (TB, TB), 0)
        s = lax.broadcasted_iota(jnp.int32, (TB, TB), 1)
        dc = dc_ref[...]
        dl = _dot_exact((r <= s).astype(F32), dc) + carry[0:1, :]
        carry[...] += jnp.sum(dc, axis=0, keepdims=True)
        dfl = dl * _sigmoid(-(fl_ref[...] + b_ref[...]))
        dfl_ref[...] = dfl
        db_ref[...] += jnp.sum(dfl, axis=0, keepdims=True)

    return _pcall(
        body, name="forget_bwd", grid=(n_t,), in_specs=[rev, rev, _full((1, LANES))],
        out_specs=[rev, _full((1, LANES))], out_shape=[_sds((t_len, LANES), F32), _sds((1, LANES), F32)],
        scratch_shapes=[pltpu.VMEM((8, LANES), F32)], compiler_params=_cparams(1),
    )(dc, fl, b_row)


def _branch_bwd(zc, dcat, dq, dk, dv, w_dw, ln_g, ln_b, w_pw, w_sc, w_pool, pool_scale):
    t_len = zc.shape[0]
    n_t = t_len // TB
    e2 = HALO + TB + HALO
    e1 = TB + HALO

    def body(z_ref, zp_ref, zn_ref, dcf_ref, dcfn_ref, dsp_ref, dspn_ref, dq_ref, dk_ref, dv_ref,
             dw_ref, g_ref, b_ref, pw_ref, sc_ref, pool_ref, ps_ref,
             dz_ref, ddw_ref, dg_ref, db_ref, dpw_ref, dsc_ref, dpool_ref, dps_ref,
             u_s, dy_s, ch_s, dcv_s, p0, p1, p2, p3, g0, g1, g2, g3, u_ph, dy_ph):
        i = pl.program_id(0)
        _acc_init([ddw_ref, dg_ref, db_ref, dpw_ref, dsc_ref, dpool_ref, dps_ref])
        hm = (i > 0).astype(F32)
        nm = (i < n_t - 1).astype(F32)

        sig_b = _sigmoid(z_ref[:, 256:512])
        a = z_ref[:, 0:256]
        u_s[0:HALO, :] = zp_ref[:, 0:256] * _sigmoid(zp_ref[:, 256:512]) * hm
        u_s[HALO:HALO + TB, :] = a * sig_b
        u_s[HALO + TB:e2, :] = zn_ref[:, 0:256] * _sigmoid(zn_ref[:, 256:512])
        _phase_copies(u_s, u_ph, e2)
        y = jnp.zeros((e1, DG), F32)
        for k in range(CONF_K):
            y = y + dw_ref[k:k + 1, :] * _tap(u_s, u_ph, HALO - (CONF_K - 1) + k, e1)
        n, r, yn = _layer_norm_parts(y, g_ref[...], b_ref[...])
        sg = _sigmoid(yn)
        dyc = jnp.concatenate([dcf_ref[...], dcfn_ref[...] * nm], axis=0)
        ds = _dot_nt(dyc.astype(BF16), pw_ref[...])
        dyn = ds * sg * (1.0 + yn * (1.0 - sg))
        dg_ref[...] += jnp.sum((dyn * n)[0:TB], axis=0, keepdims=True)
        db_ref[...] += jnp.sum(dyn[0:TB], axis=0, keepdims=True)
        dn = dyn * g_ref[...]
        dyv = r * (dn - jnp.mean(dn, axis=-1, keepdims=True) - n * jnp.mean(dn * n, axis=-1, keepdims=True))
        dpw_ref[...] += _dot_tn((yn * sg)[0:TB].astype(BF16), dcf_ref[...].astype(BF16))
        dy_s[...] = dyv
        _phase_copies(dy_s, dy_ph, e1)
        du = jnp.zeros((TB, DG), F32)
        dyv_t = dyv[0:TB]
        for k in range(CONF_K):
            du = du + dw_ref[k:k + 1, :] * _tap(dy_s, dy_ph, CONF_K - 1 - k, TB)
            ddw_ref[k:k + 1, :] += jnp.sum(dyv_t * _tap(u_s, u_ph, HALO - (CONF_K - 1) + k, TB), axis=0, keepdims=True)
        dz_ref[:, 0:256] = (du * sig_b).astype(BF16)
        dz_ref[:, 256:512] = (du * a * sig_b * (1.0 - sig_b)).astype(BF16)

        dz_ref[:, 512:768] = (dq_ref[...] * SCALE).astype(BF16)
        dz_ref[:, 768:1024] = dk_ref[...].astype(BF16)
        dz_ref[:, 1024:1280] = dv_ref[...].astype(BF16)

        sc_h, sc_b, sc_c = z_ref[:, 512:768], z_ref[:, 768:1024], z_ref[:, 1024:1280]
        ch_s[0:HALO, :] = zp_ref[:, 1024:1280] * zp_ref[:, 512:768] * hm
        ch_s[HALO:HALO + TB, :] = sc_c * sc_h
        ch_s[HALO + TB:e2, :] = zn_ref[:, 1024:1280] * zn_ref[:, 512:768]
        cv = jnp.zeros((TB, DG), F32)
        for k in range(SC_K):
            cv = cv + sc_ref[k:k + 1, :] * ch_s[pl.ds(HALO - (SC_K - 1) + k, TB), :]
        dy_sc = dsp_ref[:, 0:256]
        dcv_t = dy_sc * sc_b
        dcv_s[0:TB, :] = dcv_t
        dcv_s[TB:e1, :] = dspn_ref[:, 0:256] * nm * zn_ref[:, 768:1024]
        dch = jnp.zeros((TB, DG), F32)
        for k in range(SC_K):
            dch = dch + sc_ref[k:k + 1, :] * dcv_s[pl.ds(SC_K - 1 - k, TB), :]
            dsc_ref[k:k + 1, :] += jnp.sum(dcv_t * ch_s[pl.ds(HALO - (SC_K - 1) + k, TB), :], axis=0, keepdims=True)
        dz_ref[:, 1280:1536] = (dch * sc_c).astype(BF16)
        dz_ref[:, 1536:1792] = (dy_sc * cv).astype(BF16)
        dz_ref[:, 1792:2048] = (dch * sc_h).astype(BF16)

        v_t = z_ref[:, 1280:1536]
        p0[0:HALO, :] = zp_ref[:, 1280:1536] * hm
        p0[HALO:HALO + TB, :] = v_t
        s2, s4, s8, s16 = _pool_window_sums(p0, p1, p2, p3)
        cnt, lane = _pool_counts(i * TB, e1)
        dlt = (_lane_group_select(lane[0:TB], s2, s4, s8, s16) / cnt[0:TB] - v_t).astype(BF16)
        dyp_t = dsp_ref[:, 256:512]
        dps_ref[...] += jnp.sum(dyp_t * _dot(dlt, pool_ref[...]), axis=0, keepdims=True)
        dpre = (jnp.concatenate([dyp_t, dspn_ref[:, 256:512] * nm], axis=0) * ps_ref[...]).astype(BF16)
        dpool_ref[...] += _dot_tn(dlt, dpre[0:TB])
        dd = _dot_nt(dpre, pool_ref[...])
        g0[...] = dd / cnt
        g1[0:TB + 24, :] = g0[pl.ds(0, TB + 24), :] + g0[pl.ds(1, TB + 24), :]
        g2[0:TB + 16, :] = g1[pl.ds(0, TB + 16), :] + g1[pl.ds(2, TB + 16), :]
        g3[0:TB + 8, :] = g2[pl.ds(0, TB + 8), :] + g2[pl.ds(4, TB + 8), :]
        f16 = g3[pl.ds(0, TB), :] + g3[pl.ds(8, TB), :]
        fwd_sum = _lane_group_select(lane[0:TB], g1[pl.ds(0, TB), :], g2[pl.ds(0, TB), :], g3[pl.ds(0, TB), :], f16)
        dz_ref[:, 2048:2304] = (fwd_sum - dd[0:TB]).astype(BF16)

    vec = _full((1, DG))
    mat = _full((DG, DG))
    scr = ([pltpu.VMEM((e2, DG), F32), pltpu.VMEM((e1, DG), F32), pltpu.VMEM((e2, DG), F32), pltpu.VMEM((e1, DG), F32)]
           + [pltpu.VMEM((HALO + TB, DG), F32)] * 4 + [pltpu.VMEM((e1, DG), F32)] * 4
           + [pltpu.VMEM((7, e2, DG), F32), pltpu.VMEM((7, e1, DG), F32)])
    return _pcall(
        body, name="branch_bwd", grid=(n_t,),
        in_specs=[_tok(1536), _halo_prev(1536), _halo_next(1536, t_len),
                  _tokcol(DG, 0), _halo_next(DG, t_len, 0), _tokcol(512, 1), _halo_next(512, t_len, 1),
                  _tok(DG), _tok(DG), _tok(DG),
                  _full((32, DG)), vec, vec, mat, _full((8, DG)), mat, vec],
        out_specs=[_tok(W_MAIN), _full((32, DG)), vec, vec, mat, _full((8, DG)), mat, vec],
        out_shape=[_sds((t_len, W_MAIN), BF16), _sds((32, DG), F32), _sds((1, DG), F32), _sds((1, DG), F32),
                   _sds((DG, DG), F32), _sds((8, DG), F32), _sds((DG, DG), F32), _sds((1, DG), F32)],
        scratch_shapes=scr, compiler_params=_cparams(1),
    )(zc, zc, zc, dcat, dcat, dcat, dcat, dq, dk, dv, w_dw, ln_g, ln_b, w_pw, w_sc, w_pool, pool_scale)


def _mixin_bwd(dh, h, g, dz, dfl, win, wf):
    t_len = dh.shape[0]

    def body(dh_ref, h_ref, g_ref, dz_ref, dfl_ref, win_ref, wf_ref, dh0_ref, xn_ref, dg_ref):
        _acc_init([dg_ref])
        dxn = _dot_nt(dz_ref[...], win_ref[...].reshape(D, W_MAIN)) + _dot_nt(dfl_ref[...].astype(BF16), wf_ref[...])
        dx, dg, xn = _rms_bwd(h_ref[...], g_ref[...], dxn)
        dg_ref[...] += dg
        xn_ref[...] = xn.astype(BF16)
        dh0_ref[...] = dh_ref[...] + dx

    return _pcall(
        body, name="mixin_bwd", grid=(t_len // TB,),
        in_specs=[_tok(D), _tok(D), _full((1, D)), _tok(W_MAIN), _tok(LANES), _full((N_DEV, D // N_DEV, W_MAIN)),
                  _full((D, LANES))],
        out_specs=[_tok(D), _tok(D), _full((1, D))],
        out_shape=[_sds((t_len, D), F32), _sds((t_len, D), BF16), _sds((1, D), F32)],
        compiler_params=_cparams(1),
    )(dh, h, g, dz, dfl, win, wf)


def _matmul_tn(name, a, b, tm, tn, out_dtype, block_major=False, a_section=0):
    t_len, n = b.shape
    m = a.shape[1]
    tk = min(t_len, 1024)
    n_k = t_len // tk

    def body(a_ref, b_ref, o_ref, acc):
        k = pl.program_id(2)

        @pl.when(k == 0)
        def _():
            acc[...] = jnp.zeros_like(acc)

        acc[...] += _dot_tn(a_ref[...].astype(BF16), b_ref[...].astype(BF16))

        @pl.when(k == n_k - 1)
        def _():
            if block_major:
                for blk in range(tn // FF_BLK):
                    o_ref[blk] = acc[:, blk * FF_BLK:(blk + 1) * FF_BLK].astype(out_dtype)
            else:
                o_ref[...] = acc[...].astype(out_dtype)

    if block_major:
        out_spec = pl.BlockSpec((tn // FF_BLK, tm, FF_BLK), lambda i, j, k: (j, i, 0))
        out_shape = _sds((n // FF_BLK, m, FF_BLK), out_dtype)
    else:
        out_spec = pl.BlockSpec((tm, tn), lambda i, j, k: (i, j))
        out_shape = _sds((m, n), out_dtype)
    return _pcall(
        body, name=name, grid=(m // tm, n // tn, n_k),
        in_specs=[pl.BlockSpec((tk, tm), lambda i, j, k: (a_section * n_k + k, i)), pl.BlockSpec((tk, tn), lambda i, j, k: (k, j))],
        out_specs=out_spec, out_shape=out_shape, scratch_shapes=[pltpu.VMEM((tm, tn), F32)], compiler_params=_cparams(3),
    )(a, b)


_HBM = pl.BlockSpec(memory_space=pltpu.HBM)


def _mesh_place():
    return lax.axis_index("x"), lax.axis_index("y"), lax.axis_index("c")


def _allgather(name, srcs):
    n = len(srcs)

    def body(*refs):
        src, dst = refs[:n], refs[n:2 * n]
        send_sems, recv_sems, local_sems = refs[2 * n:]
        x, y, c = _mesh_place()
        me, sibling = (x, y, c), (x, y, 1 - c)
        chips = [(1 - x, y), (x, 1 - y), (1 - x, 1 - y)]

        def slot(px, py, pc):
            return 4 * px + 2 * py + pc

        def copy(t, k, block, to, from_src=False):
            return pltpu.make_async_remote_copy(
                src_ref=src[t] if from_src else dst[t].at[slot(*block)], dst_ref=dst[t].at[slot(*block)],
                send_sem=send_sems.at[t, k], recv_sem=recv_sems.at[t, k], device_id=to, device_id_type=MESH_ID)

        mine = [pltpu.make_async_copy(src[t], dst[t].at[slot(*me)], local_sems.at[t]) for t in range(n)]
        for cp in mine:
            cp.start()
        started = []
        for t in range(n):
            started.append(copy(t, 0, me, sibling, from_src=True))
            started += [copy(t, 1 + j, me, (*chip, c), from_src=True) for j, chip in enumerate(chips)]
        for cp in started:
            cp.start()
        for j, chip in enumerate(chips):
            for t in range(n):
                copy(t, 1 + j, (*chip, c), me).wait_recv()
                fwd = copy(t, 4 + j, (*chip, c), sibling)
                fwd.start()
                started.append(fwd)
        for t in range(n):
            copy(t, 0, sibling, me).wait_recv()
            for j, chip in enumerate(chips):
                copy(t, 4 + j, (*chip, 1 - c), me).wait_recv()
        for cp in started:
            cp.wait_send()
        for cp in mine:
            cp.wait()

    return _pcall(
        body, name=name, in_specs=[_HBM] * n, out_specs=[_HBM] * n,
        out_shape=[_sds((N_DEV,) + s.shape, s.dtype) for s in srcs],
        scratch_shapes=[pltpu.SemaphoreType.DMA((n, 7)), pltpu.SemaphoreType.DMA((n, 7)), pltpu.SemaphoreType.DMA((n,))],
    )(*srcs)


def _exchange_ops(src, dst, per_dest, send_sems, recv_sems, local_sems):
    n = len(src)
    x, y, c = _mesh_place()
    me_slot = 4 * x + 2 * y + c
    peers = []
    for r in range(1, N_DEV):
        px, py, pc = x ^ ((r >> 2) & 1), y ^ ((r >> 1) & 1), c ^ (r & 1)
        peers.append(((px, py, pc), 4 * px + 2 * py + pc))

    def piece(t, dest_slot):
        return src[t].at[dest_slot] if per_dest[t] else src[t]

    def local(t):
        return pltpu.make_async_copy(piece(t, me_slot), dst[t].at[me_slot], local_sems.at[t])

    def remote(t, r, landing_slot):
        peer, peer_slot = peers[r]
        return pltpu.make_async_remote_copy(
            src_ref=piece(t, peer_slot), dst_ref=dst[t].at[landing_slot], send_sem=send_sems.at[t, r],
            recv_sem=recv_sems.at[t, r], device_id=peer, device_id_type=MESH_ID)

    def start():
        for t in range(n):
            local(t).start()
        for r in range(N_DEV - 1):
            for t in range(n):
                remote(t, r, me_slot).start()

    def wait():
        for r in range(N_DEV - 1):
            for t in range(n):
                remote(t, r, peers[r][1]).wait_recv()
        for r in range(N_DEV - 1):
            for t in range(n):
                remote(t, r, me_slot).wait_send()
        for t in range(n):
            local(t).wait()

    return start, wait


def _exchange_shapes(srcs, per_dest):
    return [_sds((N_DEV,) + tuple(s.shape[1:] if pd else s.shape), s.dtype) for s, pd in zip(srcs, per_dest)]


def _exchange_scratch(n):
    return [pltpu.SemaphoreType.DMA((n, N_DEV - 1)), pltpu.SemaphoreType.DMA((n, N_DEV - 1)), pltpu.SemaphoreType.DMA((n,))]


def _exchange(name, srcs, per_dest):
    n = len(srcs)

    def body(*refs):
        start, wait = _exchange_ops(refs[:n], refs[n:2 * n], per_dest, *refs[2 * n:])
        start()
        wait()

    return _pcall(
        body, name=name, in_specs=[_HBM] * n, out_specs=[_HBM] * n, out_shape=_exchange_shapes(srcs, per_dest),
        scratch_shapes=_exchange_scratch(n),
    )(*srcs)


def _adam_math(w, g, m, v):
    m = ADAM_B1 * m + (1.0 - ADAM_B1) * g
    v = ADAM_B2 * v + (1.0 - ADAM_B2) * (g * g)
    m_hat = m / (1.0 - ADAM_B1 ** ADAM_STEP)
    v_hat = v / (1.0 - ADAM_B2 ** ADAM_STEP)
    delta = -ADAM_LR * (m_hat / (jnp.sqrt(v_hat) + ADAM_EPS) + ADAM_WD * w)
    return delta, m, v


def _adam_rows(name, parts, w, m, v, row_tile):
    n_l, rows, cols = w.shape

    def body(*refs):
        p_refs = refs[:n_l]
        w_ref, m_ref, v_ref, g_out, d_out, m_out, v_out = refs[n_l:]
        layer = pl.program_id(0)
        for k in range(n_l):
            @pl.when(layer == k)
            def _(p_ref=p_refs[k]):
                g = p_ref[0].astype(F32)
                for s in range(1, N_DEV):
                    g = g + p_ref[s].astype(F32)
                delta, m_new, v_new = _adam_math(w_ref[...], g, m_ref[...], v_ref[...])
                g_out[...] = g
                d_out[...] = delta
                m_out[...] = m_new
                v_out[...] = v_new

    def part_spec(k):
        return pl.BlockSpec((N_DEV, row_tile, cols),
                            lambda l, i: (0, jnp.where(l == k, i, 0), 0))

    blk = pl.BlockSpec((None, row_tile, cols), lambda l, i: (l, i, 0))
    return _pcall(
        body, name=name, grid=(n_l, rows // row_tile),
        in_specs=[part_spec(k) for k in range(n_l)] + [blk, blk, blk],
        out_specs=[blk] * 4, out_shape=[_sds(w.shape, F32)] * 4, compiler_params=_cparams(2),
    )(*parts, w, m, v)


def _adam_packed(name, parts, w, m, v):
    def body(p_ref, w_ref, m_ref, v_ref, g_out, d_out, m_out, v_out):
        g = p_ref[0]
        for s in range(1, N_DEV):
            g = g + p_ref[s]
        delta, m_new, v_new = _adam_math(w_ref[...], g, m_ref[...], v_ref[...])
        g_out[...] = g
        d_out[...] = delta
        m_out[...] = m_new
        v_out[...] = v_new

    return _pcall(
        body, name=name, grid=(1,), in_specs=[_full(parts.shape), _full(w.shape), _full(w.shape), _full(w.shape)],
        out_specs=[_full(w.shape)] * 4, out_shape=[_sds(w.shape, F32)] * 4, compiler_params=_cparams(1),
    )(parts, w, m, v)


def _pack_rows(flat_parts, lead=()):
    flat = jnp.concatenate(flat_parts, axis=-1)
    n = flat.shape[-1]
    rows = -(-n // LANES)
    rows = -(-rows // 8) * 8
    flat = jnp.pad(flat, [(0, 0)] * len(lead) + [(0, rows * LANES - n)])
    return flat.reshape(lead + (rows, LANES))


def _unpack_rows(packed, shapes, lead=()):
    flat = packed.reshape(lead + (-1,))
    out, off = [], 0
    for shp in shapes:
        size = 1
        for s in shp:
            size *= s
        out.append(flat[..., off:off + size].reshape(lead + tuple(shp)))
        off += size
    return out


_SMALL_SHARD_SHAPES = [(N_LAYERS, 128, 4), (N_LAYERS, 32, DG), (N_LAYERS, D_PLE, 128), (N_LAYERS, CONF_K, 32), (N_LAYERS, SC_K, 32)]
_REP_SHAPES = [(N_LAYERS, D)] * 6 + [(N_LAYERS, DG)] * 3 + [(N_LAYERS, N_HEADS), (N_LAYERS, 4, 64, 64)]


def _small_full_to_shards(fcol, pw, proj, dw, sc):
    return [
        fcol.reshape(N_LAYERS, N_DEV, 128, 4).transpose(1, 0, 2, 3),
        pw.reshape(N_LAYERS, N_DEV, 32, DG).transpose(1, 0, 2, 3),
        proj.reshape(N_LAYERS, D_PLE, N_DEV, 128).transpose(2, 0, 1, 3),
        dw.reshape(N_LAYERS, CONF_K, N_DEV, 32).transpose(2, 0, 1, 3),
        sc.reshape(N_LAYERS, SC_K, N_DEV, 32).transpose(2, 0, 1, 3),
    ]


def _small_shards_to_full(fcol, pw, proj, dw, sc):
    return [
        fcol.transpose(1, 0, 2, 3).reshape(N_LAYERS, D, 4),
        pw.transpose(1, 0, 2, 3).reshape(N_LAYERS, DG, DG),
        proj.transpose(1, 2, 0, 3).reshape(N_LAYERS, D_PLE, D),
        dw.transpose(1, 2, 0, 3).reshape(N_LAYERS, CONF_K, DG),
        sc.transpose(1, 2, 0, 3).reshape(N_LAYERS, SC_K, DG),
    ]


def _pad_rows(a, rows):
    return jnp.pad(a, ((0, rows - a.shape[0]), (0, 0)))


def _block_diag4(w):
    z = jnp.zeros((64, 64), w.dtype)
    return jnp.concatenate([jnp.concatenate([w[g] if k == g else z for k in range(4)], axis=1) for g in range(4)], axis=0)


def kernel(x, p, g_mix_pre, w_in, b_forget, w_conf_dw, conf_ln_g, conf_ln_b, w_conf_pw, w_sc, w_pool, pool_scale, w_out, g_mix_post, g_mlp_pre, w_up, w_down, g_mlp_post, g_ple_pre, w_ple_gate, w_ple_proj, g_ple_post, loss_target, m_g_mix_pre, m_w_in, m_b_forget, m_w_conf_dw, m_conf_ln_g, m_conf_ln_b, m_w_conf_pw, m_w_sc, m_w_pool, m_pool_scale, m_w_out, m_g_mix_post, m_g_mlp_pre, m_w_up, m_w_down, m_g_mlp_post, m_g_ple_pre, m_w_ple_gate, m_w_ple_proj, m_g_ple_post, v_g_mix_pre, v_w_in, v_b_forget, v_w_conf_dw, v_conf_ln_g, v_conf_ln_b, v_w_conf_pw, v_w_sc, v_w_pool, v_pool_scale, v_w_out, v_g_mix_post, v_g_mlp_pre, v_w_up, v_w_down, v_g_mlp_post, v_g_ple_pre, v_w_ple_gate, v_w_ple_proj, v_g_ple_post):
    n_l = N_LAYERS
    t_len = x.shape[1]
    assert t_len % TB == 0 and x.shape[0] == 1 and x.shape[2] == D

    def main_cols(a):
        return jnp.concatenate([a[..., :F_LO], a[..., F_HI:]], axis=-1)

    def fcols(a):
        return a[..., F_LO:F_HI]

    def rows_pack(down, out, gate):
        return jnp.concatenate([down, out, gate], axis=1)

    rows_b = rows_pack(w_down, w_out, w_ple_gate).astype(BF16)
    win_b = main_cols(w_in).astype(BF16)
    wup_b = w_up.astype(BF16)
    small_local = _pack_rows([a.reshape(-1) for a in (fcols(w_in), w_conf_pw, w_ple_proj, w_conf_dw, w_sc)])
    rows_g, win_g, wup_g = [None] * n_l, [None] * n_l, [None] * n_l
    rows_g[0], win_g[0], wup_g[0], small_all = _allgather("weight_allgather", [rows_b[0], win_b[0], wup_b[0], small_local])
    small_g = _unpack_rows(small_all, _SMALL_SHARD_SHAPES, lead=(N_DEV,))
    fcol_f, pw_f, proj_f, dw_f, sc_f = _small_shards_to_full(*small_g)
    wf_b = jnp.pad(fcol_f, ((0, 0), (0, 0), (0, LANES - 4))).astype(BF16)
    pw_b, proj_b = pw_f.astype(BF16), proj_f.astype(BF16)
    dw_pad = jnp.pad(dw_f, ((0, 0), (0, 32 - CONF_K), (0, 0)))
    sc_pad = jnp.pad(sc_f, ((0, 0), (0, 8 - SC_K), (0, 0)))
    pool_bd = jnp.stack([_block_diag4(w_pool[l]) for l in range(n_l)]).astype(BF16)
    b_row = jnp.pad(b_forget, ((0, 0), (0, LANES - N_HEADS)))[:, None, :]

    def vec(a, l):
        return a[l][None, :]

    p_all = p.reshape(n_l * t_len, D_PLE)
    h = x[0]
    saved = []
    for l in range(n_l):
        zc, qkv, fl, c, ct = _mixin_fwd(h, vec(g_mix_pre, l), win_g[l], wf_b[l], b_row[l])
        cat3 = _branch_fwd(zc, dw_pad[l], vec(conf_ln_g, l), vec(conf_ln_b, l), pw_b[l], sc_pad[l], pool_bd[l], vec(pool_scale, l))
        if l + 1 < n_l:
            o, lset, rows_g[l + 1], win_g[l + 1], wup_g[l + 1] = _attn_fwd(qkv, c, ct, [rows_b[l + 1], win_b[l + 1], wup_b[l + 1]])
        else:
            o, lset = _attn_fwd(qkv, c, ct)
        h1 = _mixout_fwd(h, cat3, o, rows_g[l], vec(g_mix_post, l))
        u, ff, h2 = _mlp_fwd(h1, vec(g_mlp_pre, l), wup_g[l], rows_g[l], vec(g_mlp_post, l))
        h3 = _ple_fwd(h2, p_all, l, vec(g_ple_pre, l), rows_g[l], proj_b[l], vec(g_ple_post, l))
        saved.append(dict(h0=h, zc=zc, qkv=qkv, fl=fl, c=c, ct=ct, cat3=cat3, o=o, lset=lset, h1=h1, u=u, ff=ff, h2=h2))
        h = h3

    dh, loss_part = _loss_bwd(h, loss_target[0])
    loss = lax.psum(loss_part[0, 0], ("x", "y", "c"))

    d_win = [None] * n_l
    r_down, r_out, r_gate, r_wup, r_win = ([None] * n_l for _ in range(5))
    small_grads = {k: [None] * n_l for k in ("fcol", "pw", "proj", "dw", "sc")}
    rep_grads = {k: [None] * n_l for k in ("g_mix_pre", "g_mix_post", "g_mlp_pre", "g_mlp_post", "g_ple_pre", "g_ple_post",
                                           "ln_g", "ln_b", "pool_scale", "b_forget", "w_pool")}
    for l in reversed(range(n_l)):
        s = saved[l]
        dh, dpp_b, dpre_b, hn3_b, dg_ple_post, dg_ple_pre = _ple_bwd(
            dh, s["h2"], p_all, l, vec(g_ple_post, l), vec(g_ple_pre, l), rows_g[l], proj_b[l])
        small_grads["proj"][l] = _matmul_tn("wgrad_proj", p_all, dpp_b, D_PLE, D, F32, a_section=l)
        d_gate = _matmul_tn("wgrad_gate", hn3_b, dpre_b, D, D, BF16).reshape(N_DEV, 128, D)
        dh, a2_b, du_b, dff_b, hn2_b, dg_mlp_post, dg_mlp_pre = _mlp_bwd(
            dh, s["h1"], s["u"], s["ff"], vec(g_mlp_post, l), vec(g_mlp_pre, l), wup_g[l], rows_g[l])
        d_down = _matmul_tn("wgrad_down", a2_b, dff_b, 2 * D, D, BF16).reshape(N_DEV, FF_BLK, D)
        d_wup = _matmul_tn("wgrad_up", hn2_b, du_b, D, 2 * FF_BLK, BF16, block_major=True)
        dcat, dmix_b, cat_b, dg_mix_post = _mixout_bwd(dh, s["cat3"], s["o"], vec(g_mix_post, l), rows_g[l])
        d_out = _matmul_tn("wgrad_out", cat_b, dmix_b, D, D, BF16).reshape(N_DEV, 128, D)
        dt, dob = _attn_bwd_dsum(s["qkv"], dcat, s["c"], s["ct"], s["lset"])
        riders = [d_down, d_out, d_gate, d_wup] + ([d_win[l + 1]] if l + 1 < n_l else [])
        dq, dk, dv, dc, r_down[l], r_out[l], r_gate[l], r_wup[l], *landed = _attn_bwd(
            s["qkv"], dob, s["c"], s["ct"], s["lset"], dt, riders)
        if landed:
            r_win[l + 1] = landed[0]
        dfl, db_f = _forget_bwd(dc, s["fl"], b_row[l])
        dz_b, ddw, dln_g, dln_b, dpw, dsc, dpool, dps = _branch_bwd(
            s["zc"], dcat, dq, dk, dv, dw_pad[l], vec(conf_ln_g, l), vec(conf_ln_b, l), pw_b[l], sc_pad[l], pool_bd[l],
            vec(pool_scale, l))
        dh, xn_b, dg_mix_pre = _mixin_bwd(dh, s["h0"], vec(g_mix_pre, l), dz_b, dfl, win_g[l], wf_b[l])
        d_win[l] = _matmul_tn("wgrad_in", xn_b, dz_b, D, W_MAIN // 2, BF16).reshape(N_DEV, 128, W_MAIN)
        small_grads["fcol"][l] = _matmul_tn("wgrad_fcol", xn_b, dfl, D, LANES, F32)[:, 0:4]
        small_grads["pw"][l], small_grads["dw"][l], small_grads["sc"][l] = dpw, ddw[0:CONF_K], dsc[0:SC_K]
        rep_grads["g_mix_pre"][l], rep_grads["g_mix_post"][l] = dg_mix_pre[0], dg_mix_post[0]
        rep_grads["g_mlp_pre"][l], rep_grads["g_mlp_post"][l] = dg_mlp_pre[0], dg_mlp_post[0]
        rep_grads["g_ple_pre"][l], rep_grads["g_ple_post"][l] = dg_ple_pre[0], dg_ple_post[0]
        rep_grads["ln_g"][l], rep_grads["ln_b"][l], rep_grads["pool_scale"][l] = dln_g[0], dln_b[0], dps[0]
        rep_grads["b_forget"][l] = db_f[0, 0:N_HEADS]
        rep_grads["w_pool"][l] = jnp.stack([dpool[64 * g:64 * g + 64, 64 * g:64 * g + 64] for g in range(4)])
    grad_x = dh[None]

    small_part = _pack_rows(
        [a.reshape(N_DEV, -1) for a in _small_full_to_shards(*[jnp.stack(small_grads[k]) for k in ("fcol", "pw", "proj", "dw", "sc")])],
        lead=(N_DEV,))
    rep_order = ("g_mix_pre", "g_mix_post", "g_mlp_pre", "g_mlp_post", "g_ple_pre", "g_ple_post", "ln_g", "ln_b",
                 "pool_scale", "b_forget", "w_pool")
    rep_part = _pack_rows([jnp.stack(rep_grads[k]).reshape(-1) for k in rep_order])
    r_win[0], r_small, r_rep = _exchange("grad_exchange", [d_win[0], small_part, rep_part], [True, True, False])

    res = {}
    res["w_down"] = _adam_rows("adam_down", r_down, w_down, m_w_down, v_w_down, 128)
    res["w_out"] = _adam_rows("adam_out", r_out, w_out, m_w_out, v_w_out, 128)
    res["w_ple_gate"] = _adam_rows("adam_gate", r_gate, w_ple_gate, m_w_ple_gate, v_w_ple_gate, 128)
    res["w_up"] = _adam_rows("adam_up", r_wup, w_up, m_w_up, v_w_up, 256)
    win_main = _adam_rows("adam_in", r_win, main_cols(w_in), main_cols(m_w_in), main_cols(v_w_in), 128)

    small_w = [(fcols(w_in), w_conf_pw, w_ple_proj, w_conf_dw, w_sc), (fcols(m_w_in), m_w_conf_pw, m_w_ple_proj, m_w_conf_dw, m_w_sc),
               (fcols(v_w_in), v_w_conf_pw, v_w_ple_proj, v_w_conf_dw, v_w_sc)]
    small_packed = [_pack_rows([a.reshape(-1) for a in grp]) for grp in small_w]
    small_res = [_unpack_rows(a, _SMALL_SHARD_SHAPES) for a in _adam_packed("adam_small", r_small, *small_packed)]
    rep_w = [(g_mix_pre, g_mix_post, g_mlp_pre, g_mlp_post, g_ple_pre, g_ple_post, conf_ln_g, conf_ln_b, pool_scale, b_forget, w_pool),
             (m_g_mix_pre, m_g_mix_post, m_g_mlp_pre, m_g_mlp_post, m_g_ple_pre, m_g_ple_post, m_conf_ln_g, m_conf_ln_b, m_pool_scale,
              m_b_forget, m_w_pool),
             (v_g_mix_pre, v_g_mix_post, v_g_mlp_pre, v_g_mlp_post, v_g_ple_pre, v_g_ple_post, v_conf_ln_g, v_conf_ln_b, v_pool_scale,
              v_b_forget, v_w_pool)]
    rep_packed = [_pack_rows([a.reshape(-1) for a in grp]) for grp in rep_w]
    rep_res = [_unpack_rows(a, _REP_SHAPES) for a in _adam_packed("adam_replicated", r_rep, *rep_packed)]

    for kind in range(4):
        fc, pw, proj, dwc, scc = small_res[kind]
        main = win_main[kind]
        (rg_mix_pre, rg_mix_post, rg_mlp_pre, rg_mlp_post, rg_ple_pre, rg_ple_post, r_ln_g, r_ln_b, r_ps, r_bf, r_wpool) = rep_res[kind]
        res.setdefault("by_kind", []).append(dict(
            g_mix_pre=rg_mix_pre, w_in=jnp.concatenate([main[..., :F_LO], fc, main[..., F_LO:]], axis=-1), b_forget=r_bf,
            w_conf_dw=dwc, conf_ln_g=r_ln_g, conf_ln_b=r_ln_b, w_conf_pw=pw, w_sc=scc, w_pool=r_wpool, pool_scale=r_ps,
            w_out=res["w_out"][kind], g_mix_post=rg_mix_post, g_mlp_pre=rg_mlp_pre, w_up=res["w_up"][kind],
            w_down=res["w_down"][kind], g_mlp_post=rg_mlp_post, g_ple_pre=rg_ple_pre, w_ple_gate=res["w_ple_gate"][kind],
            w_ple_proj=proj, g_ple_post=rg_ple_post))
    names = ("g_mix_pre", "w_in", "b_forget", "w_conf_dw", "conf_ln_g", "conf_ln_b", "w_conf_pw", "w_sc", "w_pool", "pool_scale",
             "w_out", "g_mix_post", "g_mlp_pre", "w_up", "w_down", "g_mlp_post", "g_ple_pre", "w_ple_gate", "w_ple_proj", "g_ple_post")
    outs = [loss, grad_x]
    for kind in range(4):
        outs += [res["by_kind"][kind][nm] for nm in names]
    return tuple(outs)
```

```python
import jax
import jax.numpy as jnp
from jax import lax
from jax.experimental import pallas as pl
from jax.experimental.pallas import tpu as pltpu

F32, BF16 = jnp.float32, jnp.bfloat16

D = 1024
DG = 256
N_HEADS = 4
HEAD_DIM = 64
CONF_K = 31
SC_K = 3
POOL_WINDOWS = (2, 4, 8, 16)
D_FF = 4096
D_PLE = 256
N_LAYERS = 4
N_DEV = 8
EPS = 1e-6
SCALE = HEAD_DIM ** -0.5
W_MAIN = 2304
F_LO, F_HI = 1280, 1284

ADAM_LR, ADAM_B1, ADAM_B2, ADAM_EPS, ADAM_WD, ADAM_STEP = 0.001, 0.9, 0.999, 1e-08, 0.01, 10

TB = 512
HALO = 32
LANES = 128
FF_BLK = D_FF // N_DEV
MLP_BWD_BLOCKS = 2
VMEM_LIMIT = 56 * 1024 * 1024

NT_DIMS = (((1,), (1,)), ((), ()))
TN_DIMS = (((0,), (0,)), ((), ()))
MESH_ID = pl.DeviceIdType.MESH


def _pcall(body, **kw):
    return pl.pallas_call(body, **kw)


def _cparams(n_axes):
    return pltpu.CompilerParams(dimension_semantics=("arbitrary",) * n_axes, vmem_limit_bytes=VMEM_LIMIT)


def _sds(shape, dtype):
    return jax.ShapeDtypeStruct(shape, dtype)


def _tok(width, tb=TB):
    return pl.BlockSpec((tb, width), lambda i: (i, 0))


def _tokcol(width, col):
    return pl.BlockSpec((TB, width), lambda i: (i, col))


def _full(shape):
    zeros = (0,) * len(shape)
    return pl.BlockSpec(shape, lambda *_: zeros)


def _resident(shape):
    zeros = (0,) * len(shape)
    return pl.BlockSpec(shape, lambda *_: zeros, pipeline_mode=pl.Buffered(1))


def _halo_prev(width, col=0):
    return pl.BlockSpec((HALO, width), lambda i: (jnp.maximum(i * (TB // HALO) - 1, 0), col))


def _halo_next(width, n_rows, col=0):
    last = n_rows // HALO - 1
    return pl.BlockSpec((HALO, width), lambda i: (jnp.minimum((i + 1) * (TB // HALO), last), col))


def _dot(a, b):
    return jnp.dot(a, b, preferred_element_type=F32)


def _dot_nt(a, b):
    return lax.dot_general(a, b, NT_DIMS, preferred_element_type=F32)


def _dot_tn(a, b):
    return lax.dot_general(a, b, TN_DIMS, preferred_element_type=F32)


def _dot_exact(a, b):
    return jnp.dot(a, b, precision=lax.Precision.HIGHEST, preferred_element_type=F32)


def _rms(x, g):
    r = lax.rsqrt(jnp.mean(x * x, axis=-1, keepdims=True) + EPS)
    return x * r * g


def _rms_bwd(x, g, dy):
    r = lax.rsqrt(jnp.mean(x * x, axis=-1, keepdims=True) + EPS)
    n = x * r
    dg = jnp.sum(dy * n, axis=0, keepdims=True)
    dn = dy * g
    dx = r * (dn - n * jnp.mean(dn * n, axis=-1, keepdims=True))
    return dx, dg, n * g


def _sigmoid(x):
    return jax.nn.sigmoid(x)


def _log_sigmoid(x):
    return jnp.minimum(x, 0.0) - jnp.log(1.0 + jnp.exp(-jnp.abs(x)))


def _lane_group_select(lane, v2, v4, v8, v16):
    return jnp.where(lane < 64, v2, jnp.where(lane < 128, v4, jnp.where(lane < 192, v8, v16)))


def _pool_counts(t0, rows):
    lane = lax.broadcasted_iota(jnp.int32, (rows, DG), 1)
    t = lax.broadcasted_iota(jnp.int32, (rows, DG), 0) + t0
    win = _lane_group_select(lane, 2, 4, 8, 16)
    return jnp.minimum(t + 1, win).astype(F32), lane


def _mixin_fwd(h, g, win, wf):
    t_len = h.shape[0]

    def body(h_ref, g_ref, win_ref, wf_ref, zc_ref, qkv_ref, fl_ref):
        xn = _rms(h_ref[...], g_ref[...]).astype(BF16)
        z = _dot(xn, win_ref[...].reshape(D, W_MAIN))
        zc_ref[:, 0:512] = z[:, 0:512]
        zc_ref[:, 512:1536] = z[:, 1280:2304]
        qkv_ref[:, 0:256] = (z[:, 512:768] * SCALE).astype(BF16)
        qkv_ref[:, 256:768] = z[:, 768:1280].astype(BF16)
        fl_ref[...] = _dot(xn, wf_ref[...])

    return _pcall(
        body, name="mixin_fwd", grid=(t_len // TB,),
        in_specs=[_tok(D), _full((1, D)), _full((N_DEV, D // N_DEV, W_MAIN)), _full((D, LANES))],
        out_specs=[_tok(1536), _tok(768), _tok(LANES)],
        out_shape=[_sds((t_len, 1536), F32), _sds((t_len, 768), BF16), _sds((t_len, LANES), F32)],
        compiler_params=_cparams(1),
    )(h, g, win, wf)


def _transpose_lanes8(x):
    eye = (lax.broadcasted_iota(jnp.int32, (8, LANES), 0) == lax.broadcasted_iota(jnp.int32, (8, LANES), 1)).astype(F32)
    return lax.dot_general(eye, x, NT_DIMS, precision=lax.Precision.HIGHEST, preferred_element_type=F32)


def _cumsum_fwd(fl, b_row):
    t_len = fl.shape[0]

    def body(fl_ref, b_ref, c_ref, ct_ref, carry):
        @pl.when(pl.program_id(0) == 0)
        def _():
            carry[...] = jnp.zeros_like(carry)

        r = lax.broadcasted_iota(jnp.int32, (TB, TB), 0)
        s = lax.broadcasted_iota(jnp.int32, (TB, TB), 1)
        lf = _log_sigmoid(fl_ref[...] + b_ref[...])
        c = _dot_exact((r >= s).astype(F32), lf) + carry[0:1, :]
        c_ref[...] = c
        ct_ref[...] = _transpose_lanes8(c)
        carry[...] += jnp.sum(lf, axis=0, keepdims=True)

    return _pcall(
        body, name="cumsum_fwd", grid=(t_len // TB,),
        in_specs=[_tok(LANES), _full((1, LANES))],
        out_specs=[_tok(LANES), pl.BlockSpec((8, TB), lambda i: (0, i))],
        out_shape=[_sds((t_len, LANES), F32), _sds((8, t_len), F32)],
        scratch_shapes=[pltpu.VMEM((8, LANES), F32)],
        compiler_params=_cparams(1),
    )(fl, b_row)


def _layer_norm_parts(y, g, b):
    mu = jnp.mean(y, axis=-1, keepdims=True)
    yc = y - mu
    r = lax.rsqrt(jnp.mean(yc * yc, axis=-1, keepdims=True) + EPS)
    n = yc * r
    return n, r, n * g + b


def _phase_copies(src, dst, rows):
    for p in range(1, 8):
        dst[p - 1, 0:rows - 8, :] = src[pl.ds(p, rows - 8), :]


def _tap(src, copies, off, n):
    p = off % 8
    return src[pl.ds(off, n), :] if p == 0 else copies[p - 1, pl.ds(off - p, n), :]


def _pool_window_sums(p0, p1, p2, p3):
    e = HALO + TB
    p1[8:e, :] = p0[pl.ds(8, e - 8), :] + p0[pl.ds(7, e - 8), :]
    p2[16:e, :] = p1[pl.ds(16, e - 16), :] + p1[pl.ds(14, e - 16), :]
    p3[24:e, :] = p2[pl.ds(24, e - 24), :] + p2[pl.ds(20, e - 24), :]
    s16 = p3[pl.ds(HALO, TB), :] + p3[pl.ds(HALO - 8, TB), :]
    return p1[pl.ds(HALO, TB), :], p2[pl.ds(HALO, TB), :], p3[pl.ds(HALO, TB), :], s16


def _branch_fwd(zc, w_dw, ln_g, ln_b, w_pw, w_sc, w_pool, pool_scale):
    t_len = zc.shape[0]
    e = HALO + TB

    def body(z_ref, zh_ref, dw_ref, g_ref, b_ref, pw_ref, sc_ref, pool_ref, ps_ref, cat_ref, u_s, ch_s, p0, p1, p2, p3, u_ph):
        i = pl.program_id(0)
        hm = (i > 0).astype(F32)
        u_s[0:HALO, :] = zh_ref[:, 0:256] * _sigmoid(zh_ref[:, 256:512]) * hm
        u_s[HALO:e, :] = z_ref[:, 0:256] * _sigmoid(z_ref[:, 256:512])
        _phase_copies(u_s, u_ph, e)
        y = jnp.zeros((TB, DG), F32)
        for k in range(CONF_K):
            y = y + dw_ref[k:k + 1, :] * _tap(u_s, u_ph, HALO - (CONF_K - 1) + k, TB)
        _, _, yn = _layer_norm_parts(y, g_ref[...], b_ref[...])
        s = yn * _sigmoid(yn)
        cat_ref[:, 0:256] = _dot(s.astype(BF16), pw_ref[...])
        ch_s[0:HALO, :] = zh_ref[:, 1024:1280] * zh_ref[:, 512:768] * hm
        ch_s[HALO:e, :] = z_ref[:, 1024:1280] * z_ref[:, 512:768]
        cv = jnp.zeros((TB, DG), F32)
        for k in range(SC_K):
            cv = cv + sc_ref[k:k + 1, :] * ch_s[pl.ds(HALO - (SC_K - 1) + k, TB), :]
        cat_ref[:, 256:512] = z_ref[:, 768:1024] * cv
        p0[0:HALO, :] = zh_ref[:, 1280:1536] * hm
        p0[HALO:e, :] = z_ref[:, 1280:1536]
        s2, s4, s8, s16 = _pool_window_sums(p0, p1, p2, p3)
        cnt, lane = _pool_counts(i * TB, TB)
        dlt = _lane_group_select(lane, s2, s4, s8, s16) / cnt - z_ref[:, 1280:1536]
        cat_ref[:, 512:768] = _dot(dlt.astype(BF16), pool_ref[...]) * ps_ref[...]

    scr = [pltpu.VMEM((e, DG), F32) for _ in range(6)] + [pltpu.VMEM((7, e, DG), F32)]
    return _pcall(
        body, name="branch_fwd", grid=(t_len // TB,),
        in_specs=[_tok(1536), _halo_prev(1536), _full((32, DG)), _full((1, DG)), _full((1, DG)), _full((DG, DG)),
                  _full((8, DG)), _full((DG, DG)), _full((1, DG))],
        out_specs=_tok(768), out_shape=_sds((t_len, 768), F32), scratch_shapes=scr, compiler_params=_cparams(1),
    )(zc, zc, w_dw, ln_g, ln_b, w_pw, w_sc, w_pool, pool_scale)


def _head_masks(rows):
    lane = lax.broadcasted_iota(jnp.int32, (rows, LANES), 1)
    return lane, (lane < HEAD_DIM, lane >= HEAD_DIM)


def _keep_lanes(mask, x):
    return jnp.where(mask, x.astype(F32), 0.0).astype(BF16)


def _with_exchange(refs, n_in, n_out, n_x, per_dest, first, last):
    ins, x_src = refs[:n_in], refs[n_in:n_in + n_x]
    outs, x_dst = refs[n_in + n_x:n_in + n_x + n_out], refs[n_in + n_x + n_out:n_in + 2 * n_x + n_out]
    begin = finish = None
    if n_x:
        start, wait = _exchange_ops(x_src, x_dst, [per_dest] * n_x, *refs[n_in + 2 * n_x + n_out:])

        def begin():
            pl.when(first)(start)

        def finish():
            pl.when(last)(wait)

    return ins, outs, begin, finish


def _attn_fwd(qkv, c, ct, bcast=()):
    t_len = qkv.shape[0]
    n_t = t_len // TB
    n_x = len(bcast)

    def body(*refs):
        i = pl.program_id(0)
        (q_ref, k_ref, v_ref, c_ref, ct_ref), (o_ref, lset_ref), begin, finish = _with_exchange(
            refs, 5, 2, n_x, False, i == 0, i == n_t - 1)
        if begin:
            begin()
        lane, halves = _head_masks(TB)
        crow = c_ref[...]
        causal = lax.broadcasted_iota(jnp.int32, (TB, TB), 0) >= lax.broadcasted_iota(jnp.int32, (TB, TB), 1)
        lse_out = jnp.zeros((TB, LANES), F32)
        for g in range(2):
            cols = slice(g * LANES, (g + 1) * LANES)
            qg = q_ref[:, cols]
            qms = [_keep_lanes(halves[hh], qg) for hh in range(2)]
            cqs = [jnp.sum(jnp.where(lane == 2 * g + hh, crow, 0.0), axis=1, keepdims=True) for hh in range(2)]

            def block(j, carry, masked):
                off = pl.multiple_of(j * TB, TB)
                kj = k_ref[pl.ds(off, TB), cols]
                vj = v_ref[pl.ds(off, TB), cols]
                new = []
                for hh in range(2):
                    m, l, acc = carry[hh]
                    s = _dot_nt(qms[hh], kj) + (cqs[hh] - ct_ref[2 * g + hh:2 * g + hh + 1, pl.ds(off, TB)])
                    if masked:
                        s = jnp.where(causal, s, -jnp.inf)
                    m_new = jnp.maximum(m, jnp.max(s, axis=1, keepdims=True))
                    alpha = jnp.exp(m - m_new)
                    p = jnp.exp(s - m_new)
                    l = alpha * l + jnp.sum(p, axis=1, keepdims=True)
                    acc = alpha * acc + _dot(p.astype(BF16), vj)
                    new.append((m_new, l, acc))
                return tuple(new)

            init = tuple((jnp.full((TB, 1), -jnp.inf, F32), jnp.zeros((TB, 1), F32), jnp.zeros((TB, LANES), F32))
                         for _ in range(2))
            carry = lax.fori_loop(0, i, lambda j, cr: block(j, cr, False), init)
            (m0, l0, acc0), (m1, l1, acc1) = block(i, carry, True)
            o_ref[:, cols] = jnp.where(halves[0], acc0 / l0, acc1 / l1)
            lse_out = jnp.where(lane == 2 * g, m0 + jnp.log(l0), lse_out)
            lse_out = jnp.where(lane == 2 * g + 1, m1 + jnp.log(l1), lse_out)
        lset_ref[...] = _transpose_lanes8(lse_out)
        if finish:
            finish()

    return _pcall(
        body, name="attn_fwd_gather" if n_x else "attn_fwd", grid=(n_t,),
        in_specs=[_tokcol(DG, 0), pl.BlockSpec((t_len, DG), lambda i: (0, 1)), pl.BlockSpec((t_len, DG), lambda i: (0, 2)),
                  _tok(LANES), _full((8, t_len))] + [_HBM] * n_x,
        out_specs=[_tok(DG), pl.BlockSpec((8, TB), lambda i: (0, i))] + [_HBM] * n_x,
        out_shape=[_sds((t_len, DG), F32), _sds((8, t_len), F32)] + _exchange_shapes(bcast, [False] * n_x),
        scratch_shapes=_exchange_scratch(n_x) if n_x else [],
        compiler_params=_cparams(1),
    )(qkv, qkv, qkv, c, ct, *bcast)


def _mix_projection(cat_ref, o_ref, w):
    return (_dot(cat_ref[:, 0:256].astype(BF16), w[0:256]) + _dot(o_ref[...].astype(BF16), w[256:512])
            + _dot(cat_ref[:, 256:768].astype(BF16), w[512:1024]))


def _mixout_fwd(h, cat3, o, w_rows, g):
    t_len = h.shape[0]

    def body(h_ref, cat_ref, o_ref, w_ref, g_ref, h1_ref):
        mix = _mix_projection(cat_ref, o_ref, w_ref[...].reshape(D, D))
        h1_ref[...] = h_ref[...] + _rms(mix, g_ref[...])

    return _pcall(
        body, name="mixout_fwd", grid=(t_len // TB,),
        in_specs=[_tok(D), _tok(768), _tok(DG), pl.BlockSpec((N_DEV, 128, D), lambda i: (0, 4, 0)), _full((1, D))],
        out_specs=_tok(D), out_shape=_sds((t_len, D), F32), compiler_params=_cparams(1),
    )(h, cat3, o, w_rows, g)


def _mlp_fwd(h, g_pre, w_up, w_rows, g_post):
    t_len = h.shape[0]

    def body(h_ref, g1_ref, up_ref, dn_ref, g2_ref, u_ref, ff_ref, h2_ref):
        hn = _rms(h_ref[...], g1_ref[...]).astype(BF16)
        ff = jnp.zeros((TB, D), F32)
        for j in range(N_DEV):
            u = _dot(hn, up_ref[j])
            u_ref[:, j * FF_BLK:(j + 1) * FF_BLK] = u.astype(BF16)
            r = jnp.maximum(u, 0.0)
            ff = ff + _dot((r * r).astype(BF16), dn_ref[j])
        ff_ref[...] = ff
        h2_ref[...] = h_ref[...] + _rms(ff, g2_ref[...])

    return _pcall(
        body, name="mlp_fwd", grid=(t_len // TB,),
        in_specs=[_tok(D), _full((1, D)), _resident((N_DEV, D, FF_BLK)), _resident((N_DEV, FF_BLK, D)), _full((1, D))],
        out_specs=[_tok(D_FF), _tok(D), _tok(D)],
        out_shape=[_sds((t_len, D_FF), BF16), _sds((t_len, D), F32), _sds((t_len, D), F32)],
        compiler_params=_cparams(1),
    )(h, g_pre, w_up, w_rows, g_post)


def _ple_fwd(h, p_all, layer, g_pre, w_rows, w_proj, g_post):
    t_len = h.shape[0]
    n_t = t_len // TB

    def body(h_ref, p_ref, g1_ref, wg_ref, wp_ref, g2_ref, h3_ref):
        pp, gate = _ple_parts(h_ref, p_ref, g1_ref, wg_ref, wp_ref)
        h3_ref[...] = h_ref[...] + _rms(pp * gate, g2_ref[...])

    return _pcall(
        body, name="ple_fwd", grid=(n_t,),
        in_specs=[_tok(D), pl.BlockSpec((TB, D_PLE), lambda i: (layer * n_t + i, 0)), _full((1, D)),
                  pl.BlockSpec((N_DEV, 128, D), lambda i: (0, 5, 0)), _full((D_PLE, D)), _full((1, D))],
        out_specs=_tok(D), out_shape=_sds((t_len, D), F32), compiler_params=_cparams(1),
    )(h, p_all, g_pre, w_rows, w_proj, g_post)


def _ple_parts(h_ref, p_ref, g1_ref, wg_ref, wp_ref):
    hn = _rms(h_ref[...], g1_ref[...]).astype(BF16)
    gate = _sigmoid(_dot(hn, wg_ref[...].reshape(D, D)))
    return _dot(p_ref[...].astype(BF16), wp_ref[...]), gate


def _loss_bwd(h, target):
    t_len = h.shape[0]

    def body(h_ref, t_ref, dh_ref, loss_ref):
        @pl.when(pl.program_id(0) == 0)
        def _():
            loss_ref[...] = jnp.zeros_like(loss_ref)

        d = h_ref[...] - t_ref[...]
        dh_ref[...] = d * (1.0 / D)
        loss_ref[...] += 0.5 * jnp.sum(jnp.mean(d * d, axis=-1, keepdims=True), axis=0, keepdims=True)

    return _pcall(
        body, name="loss_bwd", grid=(t_len // TB,), in_specs=[_tok(D), _tok(D)],
        out_specs=[_tok(D), _full((8, LANES))], out_shape=[_sds((t_len, D), F32), _sds((8, LANES), F32)],
        compiler_params=_cparams(1),
    )(h, target)


def _acc_init(refs):
    @pl.when(pl.program_id(0) == 0)
    def _():
        for r in refs:
            r[...] = jnp.zeros_like(r)


def _ple_bwd(dh3, h2, p_all, layer, g_post, g_pre, w_rows, w_proj):
    t_len = dh3.shape[0]
    n_t = t_len // TB

    def body(dh_ref, h_ref, p_ref, g2_ref, g1_ref, wg_ref, wp_ref, dh2_ref, dpp_ref, dpre_ref, hn_ref, dg2_ref, dg1_ref):
        _acc_init([dg2_ref, dg1_ref])
        dh = dh_ref[...]
        pp, gate = _ple_parts(h_ref, p_ref, g1_ref, wg_ref, wp_ref)
        de, dg2, _ = _rms_bwd(pp * gate, g2_ref[...], dh)
        dg2_ref[...] += dg2
        dpp_ref[...] = (de * gate).astype(BF16)
        dpre = (de * pp * gate * (1.0 - gate)).astype(BF16)
        dpre_ref[...] = dpre
        dhn = _dot_nt(dpre, wg_ref[...].reshape(D, D))
        dx, dg1, hn = _rms_bwd(h_ref[...], g1_ref[...], dhn)
        dg1_ref[...] += dg1
        hn_ref[...] = hn.astype(BF16)
        dh2_ref[...] = dh + dx

    return _pcall(
        body, name="ple_bwd", grid=(n_t,),
        in_specs=[_tok(D), _tok(D), pl.BlockSpec((TB, D_PLE), lambda i: (layer * n_t + i, 0)), _full((1, D)), _full((1, D)),
                  pl.BlockSpec((N_DEV, 128, D), lambda i: (0, 5, 0)), _full((D_PLE, D))],
        out_specs=[_tok(D)] * 4 + [_full((1, D))] * 2,
        out_shape=[_sds((t_len, D), F32)] + [_sds((t_len, D), BF16)] * 3 + [_sds((1, D), F32)] * 2,
        compiler_params=_cparams(1),
    )(dh3, h2, p_all, g_post, g_pre, w_rows, w_proj)


def _mlp_bwd(dh2, h1, u, ff, g_post, g_pre, w_up, w_rows):
    t_len = dh2.shape[0]

    def body(dh_ref, h_ref, u_ref, ff_ref, g2_ref, g1_ref, up_ref, dn_ref,
             dh1_ref, a2_ref, du_ref, dff_ref, hn_ref, dg2_ref, dg1_ref, dff_s, acc_s):
        i, j = pl.program_id(0), pl.program_id(1)

        @pl.when((i == 0) & (j == 0))
        def _():
            dg2_ref[...] = jnp.zeros_like(dg2_ref)
            dg1_ref[...] = jnp.zeros_like(dg1_ref)

        @pl.when(j == 0)
        def _():
            dff, dg2, _ = _rms_bwd(ff_ref[...], g2_ref[...], dh_ref[...])
            dg2_ref[...] += dg2
            dff_s[...] = dff.astype(BF16)
            dff_ref[...] = dff.astype(BF16)
            acc_s[...] = jnp.zeros_like(acc_s)

        acc = acc_s[...]
        for b in range(MLP_BWD_BLOCKS):
            cols = slice(b * FF_BLK, (b + 1) * FF_BLK)
            r = jnp.maximum(u_ref[:, cols].astype(F32), 0.0)
            a2_ref[:, cols] = (r * r).astype(BF16)
            du = (_dot_nt(dff_s[...], dn_ref[MLP_BWD_BLOCKS * j + b]) * (2.0 * r)).astype(BF16)
            du_ref[:, cols] = du
            acc = acc + _dot_nt(du, up_ref[MLP_BWD_BLOCKS * j + b])
        acc_s[...] = acc

        @pl.when(j == N_DEV // MLP_BWD_BLOCKS - 1)
        def _():
            dx, dg1, hn = _rms_bwd(h_ref[...], g1_ref[...], acc_s[...])
            dg1_ref[...] += dg1
            hn_ref[...] = hn.astype(BF16)
            dh1_ref[...] = dh_ref[...] + dx

    tok2 = pl.BlockSpec((TB, D), lambda i, j: (i, 0))
    vec2 = pl.BlockSpec((1, D), lambda i, j: (0, 0))
    blk2 = pl.BlockSpec((TB, MLP_BWD_BLOCKS * FF_BLK), lambda i, j: (i, j))
    return _pcall(
        body, name="mlp_bwd", grid=(t_len // TB, N_DEV // MLP_BWD_BLOCKS),
        in_specs=[tok2, tok2, blk2, tok2, vec2, vec2, _resident((N_DEV, D, FF_BLK)), _resident((N_DEV, FF_BLK, D))],
        out_specs=[tok2, blk2, blk2, tok2, tok2, vec2, vec2],
        out_shape=[_sds((t_len, D), F32), _sds((t_len, D_FF), BF16), _sds((t_len, D_FF), BF16), _sds((t_len, D), BF16),
                   _sds((t_len, D), BF16), _sds((1, D), F32), _sds((1, D), F32)],
        scratch_shapes=[pltpu.VMEM((TB, D), BF16), pltpu.VMEM((TB, D), F32)], compiler_params=_cparams(2),
    )(dh2, h1, u, ff, g_post, g_pre, w_up, w_rows)


def _mixout_bwd(dh1, cat3, o, g, w_rows):
    t_len = dh1.shape[0]

    def body(dh_ref, cat_ref, o_ref, g_ref, w_ref, dcat_ref, dmix_ref, catb_ref, dg_ref):
        _acc_init([dg_ref])
        w = w_ref[...].reshape(D, D)
        dmix, dg, _ = _rms_bwd(_mix_projection(cat_ref, o_ref, w), g_ref[...], dh_ref[...])
        dg_ref[...] += dg
        dmix = dmix.astype(BF16)
        dmix_ref[...] = dmix
        dcat_ref[...] = _dot_nt(dmix, w)
        catb_ref[:, 0:256] = cat_ref[:, 0:256].astype(BF16)
        catb_ref[:, 256:512] = o_ref[...].astype(BF16)
        catb_ref[:, 512:1024] = cat_ref[:, 256:768].astype(BF16)

    return _pcall(
        body, name="mixout_bwd", grid=(t_len // TB,),
        in_specs=[_tok(D), _tok(768), _tok(DG), _full((1, D)), pl.BlockSpec((N_DEV, 128, D), lambda i: (0, 4, 0))],
        out_specs=[_tok(D), _tok(D), _tok(D), _full((1, D))],
        out_shape=[_sds((t_len, D), F32), _sds((t_len, D), BF16), _sds((t_len, D), BF16), _sds((1, D), F32)],
        compiler_params=_cparams(1),
    )(dh1, cat3, o, g, w_rows)


def _attn_bwd_dsum(qkv, dcat, c, ct, lset):
    t_len = qkv.shape[0]

    def body(q_ref, do_ref, k_ref, v_ref, c_ref, ct_ref, lset_ref, dt_ref, dob_ref):
        i = pl.program_id(0)
        lane, halves = _head_masks(TB)
        causal_t = lax.broadcasted_iota(jnp.int32, (TB, TB), 1) >= lax.broadcasted_iota(jnp.int32, (TB, TB), 0)
        sub = lax.broadcasted_iota(jnp.int32, (8, TB), 0)
        dob_ref[...] = do_ref[...].astype(BF16)
        out = jnp.zeros((8, TB), F32)
        for g in range(2):
            cols = slice(g * LANES, (g + 1) * LANES)
            qi = q_ref[:, cols]
            doi = do_ref[:, cols].astype(BF16)
            doms = [_keep_lanes(halves[hh], doi) for hh in range(2)]
            cqs = [ct_ref[2 * g + hh:2 * g + hh + 1, :] for hh in range(2)]
            lses = [lset_ref[2 * g + hh:2 * g + hh + 1, :] for hh in range(2)]

            def block(j, accs, masked):
                off = pl.multiple_of(j * TB, TB)
                kj = k_ref[pl.ds(off, TB), cols]
                vj = v_ref[pl.ds(off, TB), cols]
                cj = c_ref[pl.ds(off, TB), :]
                new = []
                for hh in range(2):
                    ck = jnp.sum(jnp.where(lane == 2 * g + hh, cj, 0.0), axis=1, keepdims=True)
                    st = _dot_nt(_keep_lanes(halves[hh], kj), qi) + (cqs[hh] - ck)
                    if masked:
                        st = jnp.where(causal_t, st, -jnp.inf)
                    pt = jnp.exp(st - lses[hh])
                    new.append(accs[hh] + jnp.sum(pt * _dot_nt(vj, doms[hh]), axis=0, keepdims=True))
                return tuple(new)

            init = (jnp.zeros((1, TB), F32), jnp.zeros((1, TB), F32))
            accs = block(i, lax.fori_loop(0, i, lambda j, cr: block(j, cr, False), init), True)
            out = jnp.where(sub == 2 * g, accs[0], out)
            out = jnp.where(sub == 2 * g + 1, accs[1], out)
        dt_ref[...] = out

    row8 = pl.BlockSpec((8, TB), lambda i: (0, i))
    return _pcall(
        body, name="attn_bwd_dsum", grid=(t_len // TB,),
        in_specs=[_tokcol(DG, 0), _tokcol(DG, 1), pl.BlockSpec((t_len, DG), lambda i: (0, 1)),
                  pl.BlockSpec((t_len, DG), lambda i: (0, 2)), _full((t_len, LANES)), row8, row8],
        out_specs=[row8, _tok(DG)], out_shape=[_sds((8, t_len), F32), _sds((t_len, DG), BF16)],
        compiler_params=_cparams(1),
    )(qkv, dcat, qkv, qkv, c, ct, lset)


def _attn_bwd(qkv, dob, c, ct, lset, dt, xchg=()):
    t_len = qkv.shape[0]
    n_q = t_len // TB
    n_x = len(xchg)

    def body(*refs):
        j = pl.program_id(0)
        ins, outs, begin, finish = _with_exchange(refs, 8, 4, n_x, True, j == 0, j == n_q - 1)
        q_ref, dob_ref, k_ref, v_ref, c_ref, ct_ref, lset_ref, dt_ref = ins
        dq_ref, dk_ref, dv_ref, dc_ref = outs
        if begin:
            begin()

        @pl.when(j == 0)
        def _():
            dq_ref[...] = jnp.zeros_like(dq_ref)

        lane, halves = _head_masks(TB)
        crow = c_ref[...]
        causal_t = lax.broadcasted_iota(jnp.int32, (TB, TB), 1) >= lax.broadcasted_iota(jnp.int32, (TB, TB), 0)
        dc_out = jnp.zeros((TB, LANES), F32)
        for g in range(2):
            cols = slice(g * LANES, (g + 1) * LANES)
            kg, vg = k_ref[:, cols], v_ref[:, cols]
            kms = [_keep_lanes(halves[hh], kg) for hh in range(2)]
            cks = [jnp.sum(jnp.where(lane == 2 * g + hh, crow, 0.0), axis=1, keepdims=True) for hh in range(2)]

            def block(i, carry, masked):
                dk, dv, dcs = carry
                off = pl.multiple_of(i * TB, TB)
                qi = q_ref[pl.ds(off, TB), cols]
                doi = dob_ref[pl.ds(off, TB), cols]
                dq_add = jnp.zeros((TB, LANES), F32)
                dcs_new = []
                for hh in range(2):
                    h = 2 * g + hh
                    dom = _keep_lanes(halves[hh], doi)
                    st = _dot_nt(kms[hh], qi) + (ct_ref[h:h + 1, pl.ds(off, TB)] - cks[hh])
                    if masked:
                        st = jnp.where(causal_t, st, -jnp.inf)
                    pt = jnp.exp(st - lset_ref[h:h + 1, pl.ds(off, TB)])
                    dv = dv + _dot(pt.astype(BF16), dom)
                    dst = pt * (_dot_nt(vg, dom) - dt_ref[h:h + 1, pl.ds(off, TB)])
                    dsb = dst.astype(BF16)
                    dk = dk + _dot(dsb, _keep_lanes(halves[hh], qi))
                    dcs_new.append(dcs[hh] + jnp.sum(dst, axis=1, keepdims=True))
                    dq_add = dq_add + _dot_tn(dsb, kms[hh])
                dq_ref[pl.ds(off, TB), cols] += dq_add
                return dk, dv, tuple(dcs_new)

            init = (jnp.zeros((TB, LANES), F32), jnp.zeros((TB, LANES), F32),
                    (jnp.zeros((TB, 1), F32), jnp.zeros((TB, 1), F32)))
            carry = block(j, init, True)
            dk, dv, dcs = lax.fori_loop(j + 1, n_q, lambda i, cr: block(i, cr, False), carry)
            dk_ref[:, cols] = dk
            dv_ref[:, cols] = dv
            dc_out = jnp.where(lane == 2 * g, -dcs[0], dc_out)
            dc_out = jnp.where(lane == 2 * g + 1, -dcs[1], dc_out)
        dc_ref[...] = dc_out
        if finish:
            finish()

    return _pcall(
        body, name="attn_bwd_exchange" if n_x else "attn_bwd", grid=(n_q,),
        in_specs=[pl.BlockSpec((t_len, DG), lambda i: (0, 0)), _full((t_len, DG)), _tokcol(DG, 1), _tokcol(DG, 2),
                  _tok(LANES), _full((8, t_len)), _full((8, t_len)), _full((8, t_len))] + [_HBM] * n_x,
        out_specs=[_full((t_len, DG)), _tok(DG), _tok(DG), _tok(LANES)] + [_HBM] * n_x,
        out_shape=[_sds((t_len, DG), F32), _sds((t_len, DG), F32), _sds((t_len, DG), F32), _sds((t_len, LANES), F32)]
        + _exchange_shapes(xchg, [True] * n_x),
        scratch_shapes=_exchange_scratch(n_x) if n_x else [],
        compiler_params=_cparams(1),
    )(qkv, dob, qkv, qkv, c, ct, lset, dt, *xchg)


def _forget_bwd(dc, fl, b_row):
    t_len = dc.shape[0]
    n_t = t_len // TB
    rev = pl.BlockSpec((TB, LANES), lambda i: (n_t - 1 - i, 0))

    def body(dc_ref, fl_ref, b_ref, dfl_ref, db_ref, carry):
        @pl.when(pl.program_id(0) == 0)
        def _():
            carry[...] = jnp.zeros_like(carry)
            db_ref[...] = jnp.zeros_like(db_ref)

        r = lax.broadcasted_iota(jnp.int32, (TB, TB), 0)
        s = lax.broadcasted_iota(jnp.int32, (TB, TB), 1)
        dc = dc_ref[...]
        dl = _dot_exact((r <= s).astype(F32), dc) + carry[0:1, :]
        carry[...] += jnp.sum(dc, axis=0, keepdims=True)
        dfl = dl * _sigmoid(-(fl_ref[...] + b_ref[...]))
        dfl_ref[...] = dfl
        db_ref[...] += jnp.sum(dfl, axis=0, keepdims=True)

    return _pcall(
        body, name="forget_bwd", grid=(n_t,), in_specs=[rev, rev, _full((1, LANES))],
        out_specs=[rev, _full((1, LANES))], out_shape=[_sds((t_len, LANES), F32), _sds((1, LANES), F32)],
        scratch_shapes=[pltpu.VMEM((8, LANES), F32)], compiler_params=_cparams(1),
    )(dc, fl, b_row)


def _branch_bwd(zc, dcat, dq, dk, dv, w_dw, ln_g, ln_b, w_pw, w_sc, w_pool, pool_scale):
    t_len = zc.shape[0]
    n_t = t_len // TB
    e2 = HALO + TB + HALO
    e1 = TB + HALO

    def body(z_ref, zp_ref, zn_ref, dcf_ref, dcfn_ref, dsp_ref, dspn_ref, dq_ref, dk_ref, dv_ref,
             dw_ref, g_ref, b_ref, pw_ref, sc_ref, pool_ref, ps_ref,
             dz_ref, ddw_ref, dg_ref, db_ref, dpw_ref, dsc_ref, dpool_ref, dps_ref,
             u_s, dy_s, ch_s, dcv_s, p0, p1, p2, p3, g0, g1, g2, g3, u_ph, dy_ph):
        i = pl.program_id(0)
        _acc_init([ddw_ref, dg_ref, db_ref, dpw_ref, dsc_ref, dpool_ref, dps_ref])
        hm = (i > 0).astype(F32)
        nm = (i < n_t - 1).astype(F32)

        sig_b = _sigmoid(z_ref[:, 256:512])
        a = z_ref[:, 0:256]
        u_s[0:HALO, :] = zp_ref[:, 0:256] * _sigmoid(zp_ref[:, 256:512]) * hm
        u_s[HALO:HALO + TB, :] = a * sig_b
        u_s[HALO + TB:e2, :] = zn_ref[:, 0:256] * _sigmoid(zn_ref[:, 256:512])
        _phase_copies(u_s, u_ph, e2)
        y = jnp.zeros((e1, DG), F32)
        for k in range(CONF_K):
            y = y + dw_ref[k:k + 1, :] * _tap(u_s, u_ph, HALO - (CONF_K - 1) + k, e1)
        n, r, yn = _layer_norm_parts(y, g_ref[...], b_ref[...])
        sg = _sigmoid(yn)
        dyc = jnp.concatenate([dcf_ref[...], dcfn_ref[...] * nm], axis=0)
        ds = _dot_nt(dyc.astype(BF16), pw_ref[...])
        dyn = ds * sg * (1.0 + yn * (1.0 - sg))
        dg_ref[...] += jnp.sum((dyn * n)[0:TB], axis=0, keepdims=True)
        db_ref[...] += jnp.sum(dyn[0:TB], axis=0, keepdims=True)
        dn = dyn * g_ref[...]
        dyv = r * (dn - jnp.mean(dn, axis=-1, keepdims=True) - n * jnp.mean(dn * n, axis=-1, keepdims=True))
        dpw_ref[...] += _dot_tn((yn * sg)[0:TB].astype(BF16), dcf_ref[...].astype(BF16))
        dy_s[...] = dyv
        _phase_copies(dy_s, dy_ph, e1)
        du = jnp.zeros((TB, DG), F32)
        dyv_t = dyv[0:TB]
        for k in range(CONF_K):
            du = du + dw_ref[k:k + 1, :] * _tap(dy_s, dy_ph, CONF_K - 1 - k, TB)
            ddw_ref[k:k + 1, :] += jnp.sum(dyv_t * _tap(u_s, u_ph, HALO - (CONF_K - 1) + k, TB), axis=0, keepdims=True)
        dz_ref[:, 0:256] = (du * sig_b).astype(BF16)
        dz_ref[:, 256:512] = (du * a * sig_b * (1.0 - sig_b)).astype(BF16)

        dz_ref[:, 512:768] = (dq_ref[...] * SCALE).astype(BF16)
        dz_ref[:, 768:1024] = dk_ref[...].astype(BF16)
        dz_ref[:, 1024:1280] = dv_ref[...].astype(BF16)

        sc_h, sc_b, sc_c = z_ref[:, 512:768], z_ref[:, 768:1024], z_ref[:, 1024:1280]
        ch_s[0:HALO, :] = zp_ref[:, 1024:1280] * zp_ref[:, 512:768] * hm
        ch_s[HALO:HALO + TB, :] = sc_c * sc_h
        ch_s[HALO + TB:e2, :] = zn_ref[:, 1024:1280] * zn_ref[:, 512:768]
        cv = jnp.zeros((TB, DG), F32)
        for k in range(SC_K):
            cv = cv + sc_ref[k:k + 1, :] * ch_s[pl.ds(HALO - (SC_K - 1) + k, TB), :]
        dy_sc = dsp_ref[:, 0:256]
        dcv_t = dy_sc * sc_b
        dcv_s[0:TB, :] = dcv_t
        dcv_s[TB:e1, :] = dspn_ref[:, 0:256] * nm * zn_ref[:, 768:1024]
        dch = jnp.zeros((TB, DG), F32)
        for k in range(SC_K):
            dch = dch + sc_ref[k:k + 1, :] * dcv_s[pl.ds(SC_K - 1 - k, TB), :]
            dsc_ref[k:k + 1, :] += jnp.sum(dcv_t * ch_s[pl.ds(HALO - (SC_K - 1) + k, TB), :], axis=0, keepdims=True)
        dz_ref[:, 1280:1536] = (dch * sc_c).astype(BF16)
        dz_ref[:, 1536:1792] = (dy_sc * cv).astype(BF16)
        dz_ref[:, 1792:2048] = (dch * sc_h).astype(BF16)

        v_t = z_ref[:, 1280:1536]
        p0[0:HALO, :] = zp_ref[:, 1280:1536] * hm
        p0[HALO:HALO + TB, :] = v_t
        s2, s4, s8, s16 = _pool_window_sums(p0, p1, p2, p3)
        cnt, lane = _pool_counts(i * TB, e1)
        dlt = (_lane_group_select(lane[0:TB], s2, s4, s8, s16) / cnt[0:TB] - v_t).astype(BF16)
        dyp_t = dsp_ref[:, 256:512]
        dps_ref[...] += jnp.sum(dyp_t * _dot(dlt, pool_ref[...]), axis=0, keepdims=True)
        dpre = (jnp.concatenate([dyp_t, dspn_ref[:, 256:512] * nm], axis=0) * ps_ref[...]).astype(BF16)
        dpool_ref[...] += _dot_tn(dlt, dpre[0:TB])
        dd = _dot_nt(dpre, pool_ref[...])
        g0[...] = dd / cnt
        g1[0:TB + 24, :] = g0[pl.ds(0, TB + 24), :] + g0[pl.ds(1, TB + 24), :]
        g2[0:TB + 16, :] = g1[pl.ds(0, TB + 16), :] + g1[pl.ds(2, TB + 16), :]
        g3[0:TB + 8, :] = g2[pl.ds(0, TB + 8), :] + g2[pl.ds(4, TB + 8), :]
        f16 = g3[pl.ds(0, TB), :] + g3[pl.ds(8, TB), :]
        fwd_sum = _lane_group_select(lane[0:TB], g1[pl.ds(0, TB), :], g2[pl.ds(0, TB), :], g3[pl.ds(0, TB), :], f16)
        dz_ref[:, 2048:2304] = (fwd_sum - dd[0:TB]).astype(BF16)

    vec = _full((1, DG))
    mat = _full((DG, DG))
    scr = ([pltpu.VMEM((e2, DG), F32), pltpu.VMEM((e1, DG), F32), pltpu.VMEM((e2, DG), F32), pltpu.VMEM((e1, DG), F32)]
           + [pltpu.VMEM((HALO + TB, DG), F32)] * 4 + [pltpu.VMEM((e1, DG), F32)] * 4
           + [pltpu.VMEM((7, e2, DG), F32), pltpu.VMEM((7, e1, DG), F32)])
    return _pcall(
        body, name="branch_bwd", grid=(n_t,),
        in_specs=[_tok(1536), _halo_prev(1536), _halo_next(1536, t_len),
                  _tokcol(DG, 0), _halo_next(DG, t_len, 0), _tokcol(512, 1), _halo_next(512, t_len, 1),
                  _tok(DG), _tok(DG), _tok(DG),
                  _full((32, DG)), vec, vec, mat, _full((8, DG)), mat, vec],
        out_specs=[_tok(W_MAIN), _full((32, DG)), vec, vec, mat, _full((8, DG)), mat, vec],
        out_shape=[_sds((t_len, W_MAIN), BF16), _sds((32, DG), F32), _sds((1, DG), F32), _sds((1, DG), F32),
                   _sds((DG, DG), F32), _sds((8, DG), F32), _sds((DG, DG), F32), _sds((1, DG), F32)],
        scratch_shapes=scr, compiler_params=_cparams(1),
    )(zc, zc, zc, dcat, dcat, dcat, dcat, dq, dk, dv, w_dw, ln_g, ln_b, w_pw, w_sc, w_pool, pool_scale)


def _mixin_bwd(dh, h, g, dz, dfl, win, wf):
    t_len = dh.shape[0]

    def body(dh_ref, h_ref, g_ref, dz_ref, dfl_ref, win_ref, wf_ref, dh0_ref, xn_ref, dg_ref):
        _acc_init([dg_ref])
        dxn = _dot_nt(dz_ref[...], win_ref[...].reshape(D, W_MAIN)) + _dot_nt(dfl_ref[...].astype(BF16), wf_ref[...])
        dx, dg, xn = _rms_bwd(h_ref[...], g_ref[...], dxn)
        dg_ref[...] += dg
        xn_ref[...] = xn.astype(BF16)
        dh0_ref[...] = dh_ref[...] + dx

    return _pcall(
        body, name="mixin_bwd", grid=(t_len // TB,),
        in_specs=[_tok(D), _tok(D), _full((1, D)), _tok(W_MAIN), _tok(LANES), _full((N_DEV, D // N_DEV, W_MAIN)),
                  _full((D, LANES))],
        out_specs=[_tok(D), _tok(D), _full((1, D))],
        out_shape=[_sds((t_len, D), F32), _sds((t_len, D), BF16), _sds((1, D), F32)],
        compiler_params=_cparams(1),
    )(dh, h, g, dz, dfl, win, wf)


def _matmul_tn(name, a, b, tm, tn, out_dtype, block_major=False, a_section=0):
    t_len, n = b.shape
    m = a.shape[1]
    tk = min(t_len, 1024)
    n_k = t_len // tk

    def body(a_ref, b_ref, o_ref, acc):
        k = pl.program_id(2)

        @pl.when(k == 0)
        def _():
            acc[...] = jnp.zeros_like(acc)

        acc[...] += _dot_tn(a_ref[...].astype(BF16), b_ref[...].astype(BF16))

        @pl.when(k == n_k - 1)
        def _():
            if block_major:
                for blk in range(tn // FF_BLK):
                    o_ref[blk] = acc[:, blk * FF_BLK:(blk + 1) * FF_BLK].astype(out_dtype)
            else:
                o_ref[...] = acc[...].astype(out_dtype)

    if block_major:
        out_spec = pl.BlockSpec((tn // FF_BLK, tm, FF_BLK), lambda i, j, k: (j, i, 0))
        out_shape = _sds((n // FF_BLK, m, FF_BLK), out_dtype)
    else:
        out_spec = pl.BlockSpec((tm, tn), lambda i, j, k: (i, j))
        out_shape = _sds((m, n), out_dtype)
    return _pcall(
        body, name=name, grid=(m // tm, n // tn, n_k),
        in_specs=[pl.BlockSpec((tk, tm), lambda i, j, k: (a_section * n_k + k, i)), pl.BlockSpec((tk, tn), lambda i, j, k: (k, j))],
        out_specs=out_spec, out_shape=out_shape, scratch_shapes=[pltpu.VMEM((tm, tn), F32)], compiler_params=_cparams(3),
    )(a, b)


_HBM = pl.BlockSpec(memory_space=pltpu.HBM)


def _mesh_place():
    return lax.axis_index("x"), lax.axis_index("y"), lax.axis_index("c")


def _allgather(name, srcs):
    n = len(srcs)

    def body(*refs):
        src, dst = refs[:n], refs[n:2 * n]
        send_sems, recv_sems, local_sems = refs[2 * n:]
        x, y, c = _mesh_place()
        me, sibling = (x, y, c), (x, y, 1 - c)
        chips = [(1 - x, y), (x, 1 - y), (1 - x, 1 - y)]

        def slot(px, py, pc):
            return 4 * px + 2 * py + pc

        def copy(t, k, block, to, from_src=False):
            return pltpu.make_async_remote_copy(
                src_ref=src[t] if from_src else dst[t].at[slot(*block)], dst_ref=dst[t].at[slot(*block)],
                send_sem=send_sems.at[t, k], recv_sem=recv_sems.at[t, k], device_id=to, device_id_type=MESH_ID)

        mine = [pltpu.make_async_copy(src[t], dst[t].at[slot(*me)], local_sems.at[t]) for t in range(n)]
        for cp in mine:
            cp.start()
        started = []
        for t in range(n):
            started.append(copy(t, 0, me, sibling, from_src=True))
            started += [copy(t, 1 + j, me, (*chip, c), from_src=True) for j, chip in enumerate(chips)]
        for cp in started:
            cp.start()
        for j, chip in enumerate(chips):
            for t in range(n):
                copy(t, 1 + j, (*chip, c), me).wait_recv()
                fwd = copy(t, 4 + j, (*chip, c), sibling)
                fwd.start()
                started.append(fwd)
        for t in range(n):
            copy(t, 0, sibling, me).wait_recv()
            for j, chip in enumerate(chips):
                copy(t, 4 + j, (*chip, 1 - c), me).wait_recv()
        for cp in started:
            cp.wait_send()
        for cp in mine:
            cp.wait()

    return _pcall(
        body, name=name, in_specs=[_HBM] * n, out_specs=[_HBM] * n,
        out_shape=[_sds((N_DEV,) + s.shape, s.dtype) for s in srcs],
        scratch_shapes=[pltpu.SemaphoreType.DMA((n, 7)), pltpu.SemaphoreType.DMA((n, 7)), pltpu.SemaphoreType.DMA((n,))],
    )(*srcs)


def _exchange_ops(src, dst, per_dest, send_sems, recv_sems, local_sems):
    n = len(src)
    x, y, c = _mesh_place()
    me_slot = 4 * x + 2 * y + c
    peers = []
    for r in range(1, N_DEV):
        px, py, pc = x ^ ((r >> 2) & 1), y ^ ((r >> 1) & 1), c ^ (r & 1)
        peers.append(((px, py, pc), 4 * px + 2 * py + pc))

    def piece(t, dest_slot):
        return src[t].at[dest_slot] if per_dest[t] else src[t]

    def local(t):
        return pltpu.make_async_copy(piece(t, me_slot), dst[t].at[me_slot], local_sems.at[t])

    def remote(t, r, landing_slot):
        peer, peer_slot = peers[r]
        return pltpu.make_async_remote_copy(
            src_ref=piece(t, peer_slot), dst_ref=dst[t].at[landing_slot], send_sem=send_sems.at[t, r],
            recv_sem=recv_sems.at[t, r], device_id=peer, device_id_type=MESH_ID)

    def start():
        for t in range(n):
            local(t).start()
        for r in range(N_DEV - 1):
            for t in range(n):
                remote(t, r, me_slot).start()

    def wait():
        for r in range(N_DEV - 1):
            for t in range(n):
                remote(t, r, peers[r][1]).wait_recv()
        for r in range(N_DEV - 1):
            for t in range(n):
                remote(t, r, me_slot).wait_send()
        for t in range(n):
            local(t).wait()

    return start, wait


def _exchange_shapes(srcs, per_dest):
    return [_sds((N_DEV,) + tuple(s.shape[1:] if pd else s.shape), s.dtype) for s, pd in zip(srcs, per_dest)]


def _exchange_scratch(n):
    return [pltpu.SemaphoreType.DMA((n, N_DEV - 1)), pltpu.SemaphoreType.DMA((n, N_DEV - 1)), pltpu.SemaphoreType.DMA((n,))]


def _exchange(name, srcs, per_dest):
    n = len(srcs)

    def body(*refs):
        start, wait = _exchange_ops(refs[:n], refs[n:2 * n], per_dest, *refs[2 * n:])
        start()
        wait()

    return _pcall(
        body, name=name, in_specs=[_HBM] * n, out_specs=[_HBM] * n, out_shape=_exchange_shapes(srcs, per_dest),
        scratch_shapes=_exchange_scratch(n),
    )(*srcs)


def _adam_math(w, g, m, v):
    m = ADAM_B1 * m + (1.0 - ADAM_B1) * g
    v = ADAM_B2 * v + (1.0 - ADAM_B2) * (g * g)
    m_hat = m / (1.0 - ADAM_B1 ** ADAM_STEP)
    v_hat = v / (1.0 - ADAM_B2 ** ADAM_STEP)
    delta = -ADAM_LR * (m_hat / (jnp.sqrt(v_hat) + ADAM_EPS) + ADAM_WD * w)
    return delta, m, v


def _adam_rows(name, parts, w, m, v, row_tile):
    n_l, rows, cols = w.shape

    def body(*refs):
        p_refs = refs[:n_l]
        w_ref, m_ref, v_ref, g_out, d_out, m_out, v_out = refs[n_l:]
        layer = pl.program_id(0)
        for k in range(n_l):
            @pl.when(layer == k)
            def _(p_ref=p_refs[k]):
                g = p_ref[0].astype(F32)
                for s in range(1, N_DEV):
                    g = g + p_ref[s].astype(F32)
                delta, m_new, v_new = _adam_math(w_ref[...], g, m_ref[...], v_ref[...])
                g_out[...] = g
                d_out[...] = delta
                m_out[...] = m_new
                v_out[...] = v_new

    def part_spec(k):
        return pl.BlockSpec((N_DEV, row_tile, cols),
                            lambda l, i: (0, jnp.where(l == k, i, 0), 0))

    blk = pl.BlockSpec((None, row_tile, cols), lambda l, i: (l, i, 0))
    return _pcall(
        body, name=name, grid=(n_l, rows // row_tile),
        in_specs=[part_spec(k) for k in range(n_l)] + [blk, blk, blk],
        out_specs=[blk] * 4, out_shape=[_sds(w.shape, F32)] * 4, compiler_params=_cparams(2),
    )(*parts, w, m, v)


def _adam_packed(name, parts, w, m, v):
    def body(p_ref, w_ref, m_ref, v_ref, g_out, d_out, m_out, v_out):
        g = p_ref[0]
        for s in range(1, N_DEV):
            g = g + p_ref[s]
        delta, m_new, v_new = _adam_math(w_ref[...], g, m_ref[...], v_ref[...])
        g_out[...] = g
        d_out[...] = delta
        m_out[...] = m_new
        v_out[...] = v_new

    return _pcall(
        body, name=name, grid=(1,), in_specs=[_full(parts.shape), _full(w.shape), _full(w.shape), _full(w.shape)],
        out_specs=[_full(w.shape)] * 4, out_shape=[_sds(w.shape, F32)] * 4, compiler_params=_cparams(1),
    )(parts, w, m, v)


def _pack_rows(flat_parts, lead=()):
    flat = jnp.concatenate(flat_parts, axis=-1)
    n = flat.shape[-1]
    rows = -(-n // LANES)
    rows = -(-rows // 8) * 8
    flat = jnp.pad(flat, [(0, 0)] * len(lead) + [(0, rows * LANES - n)])
    return flat.reshape(lead + (rows, LANES))


def _unpack_rows(packed, shapes, lead=()):
    flat = packed.reshape(lead + (-1,))
    out, off = [], 0
    for shp in shapes:
        size = 1
        for s in shp:
            size *= s
        out.append(flat[..., off:off + size].reshape(lead + tuple(shp)))
        off += size
    return out


_SMALL_SHARD_SHAPES = [(N_LAYERS, 128, 4), (N_LAYERS, 32, DG), (N_LAYERS, D_PLE, 128), (N_LAYERS, CONF_K, 32), (N_LAYERS, SC_K, 32)]
_REP_SHAPES = [(N_LAYERS, D)] * 6 + [(N_LAYERS, DG)] * 3 + [(N_LAYERS, N_HEADS), (N_LAYERS, 4, 64, 64)]


def _small_full_to_shards(fcol, pw, proj, dw, sc):
    return [
        fcol.reshape(N_LAYERS, N_DEV, 128, 4).transpose(1, 0, 2, 3),
        pw.reshape(N_LAYERS, N_DEV, 32, DG).transpose(1, 0, 2, 3),
        proj.reshape(N_LAYERS, D_PLE, N_DEV, 128).transpose(2, 0, 1, 3),
        dw.reshape(N_LAYERS, CONF_K, N_DEV, 32).transpose(2, 0, 1, 3),
        sc.reshape(N_LAYERS, SC_K, N_DEV, 32).transpose(2, 0, 1, 3),
    ]


def _small_shards_to_full(fcol, pw, proj, dw, sc):
    return [
        fcol.transpose(1, 0, 2, 3).reshape(N_LAYERS, D, 4),
        pw.transpose(1, 0, 2, 3).reshape(N_LAYERS, DG, DG),
        proj.transpose(1, 2, 0, 3).reshape(N_LAYERS, D_PLE, D),
        dw.transpose(1, 2, 0, 3).reshape(N_LAYERS, CONF_K, DG),
        sc.transpose(1, 2, 0, 3).reshape(N_LAYERS, SC_K, DG),
    ]


def _pad_rows(a, rows):
    return jnp.pad(a, ((0, rows - a.shape[0]), (0, 0)))


def _block_diag4(w):
    z = jnp.zeros((64, 64), w.dtype)
    return jnp.concatenate([jnp.concatenate([w[g] if k == g else z for k in range(4)], axis=1) for g in range(4)], axis=0)


def kernel(x, p, g_mix_pre, w_in, b_forget, w_conf_dw, conf_ln_g, conf_ln_b, w_conf_pw, w_sc, w_pool, pool_scale, w_out, g_mix_post, g_mlp_pre, w_up, w_down, g_mlp_post, g_ple_pre, w_ple_gate, w_ple_proj, g_ple_post, loss_target, m_g_mix_pre, m_w_in, m_b_forget, m_w_conf_dw, m_conf_ln_g, m_conf_ln_b, m_w_conf_pw, m_w_sc, m_w_pool, m_pool_scale, m_w_out, m_g_mix_post, m_g_mlp_pre, m_w_up, m_w_down, m_g_mlp_post, m_g_ple_pre, m_w_ple_gate, m_w_ple_proj, m_g_ple_post, v_g_mix_pre, v_w_in, v_b_forget, v_w_conf_dw, v_conf_ln_g, v_conf_ln_b, v_w_conf_pw, v_w_sc, v_w_pool, v_pool_scale, v_w_out, v_g_mix_post, v_g_mlp_pre, v_w_up, v_w_down, v_g_mlp_post, v_g_ple_pre, v_w_ple_gate, v_w_ple_proj, v_g_ple_post):
    n_l = N_LAYERS
    t_len = x.shape[1]
    assert t_len % TB == 0 and x.shape[0] == 1 and x.shape[2] == D

    def main_cols(a):
        return jnp.concatenate([a[..., :F_LO], a[..., F_HI:]], axis=-1)

    def fcols(a):
        return a[..., F_LO:F_HI]

    def rows_pack(down, out, gate):
        return jnp.concatenate([down, out, gate], axis=1)

    rows_b = rows_pack(w_down, w_out, w_ple_gate).astype(BF16)
    win_b = main_cols(w_in).astype(BF16)
    wup_b = w_up.astype(BF16)
    small_local = _pack_rows([a.reshape(-1) for a in (fcols(w_in), w_conf_pw, w_ple_proj, w_conf_dw, w_sc)])
    rows_g, win_g, wup_g = [None] * n_l, [None] * n_l, [None] * n_l
    win_g[0], small_all = _allgather("weight_allgather", [win_b[0], small_local])
    small_g = _unpack_rows(small_all, _SMALL_SHARD_SHAPES, lead=(N_DEV,))
    fcol_f, pw_f, proj_f, dw_f, sc_f = _small_shards_to_full(*small_g)
    wf_b = jnp.pad(fcol_f, ((0, 0), (0, 0), (0, LANES - 4))).astype(BF16)
    pw_b, proj_b = pw_f.astype(BF16), proj_f.astype(BF16)
    dw_pad = jnp.pad(dw_f, ((0, 0), (0, 32 - CONF_K), (0, 0)))
    sc_pad = jnp.pad(sc_f, ((0, 0), (0, 8 - SC_K), (0, 0)))
    pool_bd = jnp.stack([_block_diag4(w_pool[l]) for l in range(n_l)]).astype(BF16)
    b_row = jnp.pad(b_forget, ((0, 0), (0, LANES - N_HEADS)))[:, None, :]

    def vec(a, l):
        return a[l][None, :]

    p_all = p.reshape(n_l * t_len, D_PLE)
    h = x[0]
    saved = []
    for l in range(n_l):
        zc, qkv, fl = _mixin_fwd(h, vec(g_mix_pre, l), win_g[l], wf_b[l])
        c, ct = _cumsum_fwd(fl, b_row[l])
        cat3 = _branch_fwd(zc, dw_pad[l], vec(conf_ln_g, l), vec(conf_ln_b, l), pw_b[l], sc_pad[l], pool_bd[l], vec(pool_scale, l))
        riders = [rows_b[l], wup_b[l]] + ([win_b[l + 1]] if l + 1 < n_l else [])
        o, lset, rows_g[l], wup_g[l], *landed = _attn_fwd(qkv, c, ct, riders)
        if landed:
            win_g[l + 1] = landed[0]
        h1 = _mixout_fwd(h, cat3, o, rows_g[l], vec(g_mix_post, l))
        u, ff, h2 = _mlp_fwd(h1, vec(g_mlp_pre, l), wup_g[l], rows_g[l], vec(g_mlp_post, l))
        h3 = _ple_fwd(h2, p_all, l, vec(g_ple_pre, l), rows_g[l], proj_b[l], vec(g_ple_post, l))
        saved.append(dict(h0=h, zc=zc, qkv=qkv, fl=fl, c=c, ct=ct, cat3=cat3, o=o, lset=lset, h1=h1, u=u, ff=ff, h2=h2))
        h = h3

    dh, loss_part = _loss_bwd(h, loss_target[0])
    loss = lax.psum(loss_part[0, 0], ("x", "y", "c"))

    d_win = [None] * n_l
    r_down, r_out, r_gate, r_wup, r_win = ([None] * n_l for _ in range(5))
    small_grads = {k: [None] * n_l for k in ("fcol", "pw", "proj", "dw", "sc")}
    rep_grads = {k: [None] * n_l for k in ("g_mix_pre", "g_mix_post", "g_mlp_pre", "g_mlp_post", "g_ple_pre", "g_ple_post",
                                           "ln_g", "ln_b", "pool_scale", "b_forget", "w_pool")}
    for l in reversed(range(n_l)):
        s = saved[l]
        dh, dpp_b, dpre_b, hn3_b, dg_ple_post, dg_ple_pre = _ple_bwd(
            dh, s["h2"], p_all, l, vec(g_ple_post, l), vec(g_ple_pre, l), rows_g[l], proj_b[l])
        small_grads["proj"][l] = _matmul_tn("wgrad_proj", p_all, dpp_b, D_PLE, D, F32, a_section=l)
        d_gate = _matmul_tn("wgrad_gate", hn3_b, dpre_b, D, D, BF16).reshape(N_DEV, 128, D)
        dh, a2_b, du_b, dff_b, hn2_b, dg_mlp_post, dg_mlp_pre = _mlp_bwd(
            dh, s["h1"], s["u"], s["ff"], vec(g_mlp_post, l), vec(g_mlp_pre, l), wup_g[l], rows_g[l])
        d_down = _matmul_tn("wgrad_down", a2_b, dff_b, 2 * D, D, BF16).reshape(N_DEV, FF_BLK, D)
        d_wup = _matmul_tn("wgrad_up", hn2_b, du_b, D, 2 * FF_BLK, BF16, block_major=True)
        dcat, dmix_b, cat_b, dg_mix_post = _mixout_bwd(dh, s["cat3"], s["o"], vec(g_mix_post, l), rows_g[l])
        d_out = _matmul_tn("wgrad_out", cat_b, dmix_b, D, D, BF16).reshape(N_DEV, 128, D)
        dt, dob = _attn_bwd_dsum(s["qkv"], dcat, s["c"], s["ct"], s["lset"])
        riders = [d_down, d_out, d_gate, d_wup] + ([d_win[l + 1]] if l + 1 < n_l else [])
        dq, dk, dv, dc, r_down[l], r_out[l], r_gate[l], r_wup[l], *landed = _attn_bwd(
            s["qkv"], dob, s["c"], s["ct"], s["lset"], dt, riders)
        if landed:
            r_win[l + 1] = landed[0]
        dfl, db_f = _forget_bwd(dc, s["fl"], b_row[l])
        dz_b, ddw, dln_g, dln_b, dpw, dsc, dpool, dps = _branch_bwd(
            s["zc"], dcat, dq, dk, dv, dw_pad[l], vec(conf_ln_g, l), vec(conf_ln_b, l), pw_b[l], sc_pad[l], pool_bd[l],
            vec(pool_scale, l))
        dh, xn_b, dg_mix_pre = _mixin_bwd(dh, s["h0"], vec(g_mix_pre, l), dz_b, dfl, win_g[l], wf_b[l])
        d_win[l] = _matmul_tn("wgrad_in", xn_b, dz_b, D, W_MAIN // 2, BF16).reshape(N_DEV, 128, W_MAIN)
        small_grads["fcol"][l] = _matmul_tn("wgrad_fcol", xn_b, dfl, D, LANES, F32)[:, 0:4]
        small_grads["pw"][l], small_grads["dw"][l], small_grads["sc"][l] = dpw, ddw[0:CONF_K], dsc[0:SC_K]
        rep_grads["g_mix_pre"][l], rep_grads["g_mix_post"][l] = dg_mix_pre[0], dg_mix_post[0]
        rep_grads["g_mlp_pre"][l], rep_grads["g_mlp_post"][l] = dg_mlp_pre[0], dg_mlp_post[0]
        rep_grads["g_ple_pre"][l], rep_grads["g_ple_post"][l] = dg_ple_pre[0], dg_ple_post[0]
        rep_grads["ln_g"][l], rep_grads["ln_b"][l], rep_grads["pool_scale"][l] = dln_g[0], dln_b[0], dps[0]
        rep_grads["b_forget"][l] = db_f[0, 0:N_HEADS]
        rep_grads["w_pool"][l] = jnp.stack([dpool[64 * g:64 * g + 64, 64 * g:64 * g + 64] for g in range(4)])
    grad_x = dh[None]

    small_part = _pack_rows(
        [a.reshape(N_DEV, -1) for a in _small_full_to_shards(*[jnp.stack(small_grads[k]) for k in ("fcol", "pw", "proj", "dw", "sc")])],
        lead=(N_DEV,))
    rep_order = ("g_mix_pre", "g_mix_post", "g_mlp_pre", "g_mlp_post", "g_ple_pre", "g_ple_post", "ln_g", "ln_b",
                 "pool_scale", "b_forget", "w_pool")
    rep_part = _pack_rows([jnp.stack(rep_grads[k]).reshape(-1) for k in rep_order])
    r_win[0], r_small, r_rep = _exchange("grad_exchange", [d_win[0], small_part, rep_part], [True, True, False])

    res = {}
    res["w_down"] = _adam_rows("adam_down", r_down, w_down, m_w_down, v_w_down, 128)
    res["w_out"] = _adam_rows("adam_out", r_out, w_out, m_w_out, v_w_out, 128)
    res["w_ple_gate"] = _adam_rows("adam_gate", r_gate, w_ple_gate, m_w_ple_gate, v_w_ple_gate, 128)
    res["w_up"] = _adam_rows("adam_up", r_wup, w_up, m_w_up, v_w_up, 256)
    win_main = _adam_rows("adam_in", r_win, main_cols(w_in), main_cols(m_w_in), main_cols(v_w_in), 128)

    small_w = [(fcols(w_in), w_conf_pw, w_ple_proj, w_conf_dw, w_sc), (fcols(m_w_in), m_w_conf_pw, m_w_ple_proj, m_w_conf_dw, m_w_sc),
               (fcols(v_w_in), v_w_conf_pw, v_w_ple_proj, v_w_conf_dw, v_w_sc)]
    small_packed = [_pack_rows([a.reshape(-1) for a in grp]) for grp in small_w]
    small_res = [_unpack_rows(a, _SMALL_SHARD_SHAPES) for a in _adam_packed("adam_small", r_small, *small_packed)]
    rep_w = [(g_mix_pre, g_mix_post, g_mlp_pre, g_mlp_post, g_ple_pre, g_ple_post, conf_ln_g, conf_ln_b, pool_scale, b_forget, w_pool),
             (m_g_mix_pre, m_g_mix_post, m_g_mlp_pre, m_g_mlp_post, m_g_ple_pre, m_g_ple_post, m_conf_ln_g, m_conf_ln_b, m_pool_scale,
              m_b_forget, m_w_pool),
             (v_g_mix_pre, v_g_mix_post, v_g_mlp_pre, v_g_mlp_post, v_g_ple_pre, v_g_ple_post, v_conf_ln_g, v_conf_ln_b, v_pool_scale,
              v_b_forget, v_w_pool)]
    rep_packed = [_pack_rows([a.reshape(-1) for a in grp]) for grp in rep_w]
    rep_res = [_unpack_rows(a, _REP_SHAPES) for a in _adam_packed("adam_replicated", r_rep, *rep_packed)]

    for kind in range(4):
        fc, pw, proj, dwc, scc = small_res[kind]
        main = win_main[kind]
        (rg_mix_pre, rg_mix_post, rg_mlp_pre, rg_mlp_post, rg_ple_pre, rg_ple_post, r_ln_g, r_ln_b, r_ps, r_bf, r_wpool) = rep_res[kind]
        res.setdefault("by_kind", []).append(dict(
            g_mix_pre=rg_mix_pre, w_in=jnp.concatenate([main[..., :F_LO], fc, main[..., F_LO:]], axis=-1), b_forget=r_bf,
            w_conf_dw=dwc, conf_ln_g=r_ln_g, conf_ln_b=r_ln_b, w_conf_pw=pw, w_sc=scc, w_pool=r_wpool, pool_scale=r_ps,
            w_out=res["w_out"][kind], g_mix_post=rg_mix_post, g_mlp_pre=rg_mlp_pre, w_up=res["w_up"][kind],
            w_down=res["w_down"][kind], g_mlp_post=rg_mlp_post, g_ple_pre=rg_ple_pre, w_ple_gate=res["w_ple_gate"][kind],
            w_ple_proj=proj, g_ple_post=rg_ple_post))
    names = ("g_mix_pre", "w_in", "b_forget", "w_conf_dw", "conf_ln_g", "conf_ln_b", "w_conf_pw", "w_sc", "w_pool", "pool_scale",
             "w_out", "g_mix_post", "g_mlp_pre", "w_up", "w_down", "g_mlp_post", "g_ple_pre", "w_ple_gate", "w_ple_proj", "g_ple_post")
    outs = [loss, grad_x]
    for kind in range(4):
        outs += [res["by_kind"][kind][nm] for nm in names]
    return tuple(outs)
```
